```python
import math
import jax, jax.numpy as jnp
from jax import lax
import numpy as np

D_MODEL = 1024
BATCH = 4
SEQ = 4096
DEPTH = 2

CHUNK = 64
Q_BLOCK = 128
A_HEADS = 8
A_HEAD_DIM = 64
IDX_HEADS = 4
IDX_DIM = 64
TOPK_MAX = 256
B_HEADS = 4
B_HEAD_DIM = 64
C_HEADS = 8
C_HEAD_DIM = 64
REL_BUCKETS = 32
REL_MAX_DIST = 128
D_FF = 2816
N_EXPERTS = 8
TOP_K_EXPERTS = 2
D_FF_EXPERT = 3584
EPS = 1e-6
SUBLN_EPS = 1e-5

A_WIDTH = A_HEADS * A_HEAD_DIM
B_WIDTH = B_HEADS * 2 * B_HEAD_DIM
C_WIDTH = C_HEADS * C_HEAD_DIM
BRANCH_WIDTHS = (A_WIDTH, B_WIDTH, C_WIDTH)
MIX_WIDTH = A_WIDTH + B_WIDTH + C_WIDTH
N_BRANCHES = 3
IN_SIZES = (A_WIDTH, A_WIDTH, A_WIDTH,
            IDX_HEADS * IDX_DIM, IDX_DIM, IDX_HEADS,
            B_WIDTH, B_WIDTH, B_WIDTH,
            C_WIDTH, C_WIDTH, C_WIDTH,
            N_BRANCHES * D_MODEL)
IN_WIDTH = sum(IN_SIZES)
N_DENSE = (DEPTH + 1) // 2
N_MOE = DEPTH // 2

kernel_name = "hybrid_dsa_diff_stickbreak_moe_block"


def rmsnorm(x, g, eps=EPS):
    x32 = x.astype(jnp.float32)
    y = x32 * lax.rsqrt(jnp.mean(x32 * x32, axis=-1, keepdims=True) + eps)
    return y.astype(x.dtype) * g


def rel_bucket(rel):
    half = REL_BUCKETS // 2
    max_exact = half // 2
    bucket = jnp.where(rel > 0, half, 0)
    n = jnp.abs(rel)
    nf = jnp.maximum(n, 1).astype(jnp.float32)
    large = max_exact + (jnp.log(nf / max_exact) / math.log(REL_MAX_DIST / max_exact)
                         * (half - max_exact)).astype(jnp.int32)
    large = jnp.minimum(large, half - 1)
    return bucket + jnp.where(n < max_exact, n, large)


def dsa_mixer(q, k, v, qi, ki, wi, bias_tab):
    bsz, s_len = q.shape[0], q.shape[1]
    topk = min(TOPK_MAX, s_len // 4)
    n_blk = s_len // CHUNK
    key_pos = jnp.arange(s_len)
    scale = A_HEAD_DIM ** -0.5

    def block(i):
        t0 = i * CHUNK
        qb = lax.dynamic_slice_in_dim(q, t0, CHUNK, axis=1)
        qib = lax.dynamic_slice_in_dim(qi, t0, CHUNK, axis=1)
        wib = lax.dynamic_slice_in_dim(wi, t0, CHUNK, axis=1)
        dots = jnp.einsum('bqhd,bsd->bqhs', qib, ki) * (IDX_DIM ** -0.5)
        score = jnp.einsum('bqh,bqhs->bqs', wib, jax.nn.relu(dots)).astype(jnp.float32)
        limit = t0 + CHUNK
        score = jnp.where((key_pos < limit)[None, None, :], score, -jnp.inf)
        _, sel = lax.top_k(score, topk)
        valid = sel < limit
        k_sel = jax.vmap(lambda kk, ii: kk[ii])(k, sel)
        v_sel = jax.vmap(lambda vv, ii: vv[ii])(v, sel)
        qpos = t0 + jnp.arange(CHUNK)
        bias = bias_tab[rel_bucket(sel - qpos[None, :, None])]
        logits = jnp.einsum('bqhd,bqkhd->bhqk', qb, k_sel).astype(jnp.float32) * scale
        logits = logits + jnp.transpose(bias, (0, 3, 1, 2)).astype(jnp.float32)
        logits = jnp.where(valid[:, None], logits, -jnp.inf)
        p = jax.nn.softmax(logits, axis=-1).astype(v.dtype)
        o = jnp.einsum('bhqk,bqkhd->bqhd', p, v_sel)
        return o.reshape(bsz, CHUNK, A_WIDTH)

    out = lax.map(block, jnp.arange(n_blk))
    return jnp.moveaxis(out, 0, 1).reshape(bsz, s_len, A_WIDTH)


def diff_mixer(q, k, v, lam, lam_init, subln_g, bias_tab):
    bsz, s_len = q.shape[0], q.shape[1]
    n_blk = s_len // Q_BLOCK
    key_pos = jnp.arange(s_len)
    scale = B_HEAD_DIM ** -0.5

    def block(i):
        t0 = i * Q_BLOCK
        qb = lax.dynamic_slice_in_dim(q, t0, Q_BLOCK, axis=1)
        qpos = t0 + jnp.arange(Q_BLOCK)
        logits = jnp.einsum('bqhmd,bshmd->bhmqs', qb, k).astype(jnp.float32) * scale
        bias = bias_tab[rel_bucket(key_pos[None, :] - qpos[:, None])]
        bias = jnp.transpose(bias, (2, 0, 1)).astype(jnp.float32)
        mask = (key_pos[None, :] // CHUNK) <= (qpos[:, None] // CHUNK)
        logits = jnp.where(mask, logits + bias[None, :, None], -jnp.inf)
        p = jax.nn.softmax(logits, axis=-1)
        attn = (p[:, :, 0] - lam * p[:, :, 1]).astype(v.dtype)
        o = jnp.einsum('bhqs,bshe->bqhe', attn, v)
        o = rmsnorm(o, subln_g, SUBLN_EPS) * (1.0 - lam_init)
        return o.reshape(bsz, Q_BLOCK, B_WIDTH)

    out = lax.map(block, jnp.arange(n_blk))
    return jnp.moveaxis(out, 0, 1).reshape(bsz, s_len, B_WIDTH)


def stickbreak_mixer(q, k, v):
    bsz, s_len = q.shape[0], q.shape[1]
    n_blk = s_len // Q_BLOCK
    key_pos = jnp.arange(s_len)
    scale = C_HEAD_DIM ** -0.5

    def block(i):
        t0 = i * Q_BLOCK
        qb = lax.dynamic_slice_in_dim(q, t0, Q_BLOCK, axis=1)
        qpos = t0 + jnp.arange(Q_BLOCK)
        z = jnp.einsum('bqhd,bshd->bhqs', qb, k).astype(jnp.float32) * scale
        strict = key_pos[None, :] < qpos[:, None]
        log_1mb = jnp.where(strict, jax.nn.log_sigmoid(-z), 0.0)
        between = lax.cumsum(log_1mb, axis=3, reverse=True) - log_1mb
        w = jnp.where(strict, jnp.exp(jax.nn.log_sigmoid(z) + between), 0.0)
        o = jnp.einsum('bhqs,bshd->bqhd', w.astype(v.dtype), v)
        return o.reshape(bsz, Q_BLOCK, C_WIDTH)

    out = lax.map(block, jnp.arange(n_blk))
    return jnp.moveaxis(out, 0, 1).reshape(bsz, s_len, C_WIDTH)


def swiglu(h, w1, w3, w2):
    return (jax.nn.silu(h @ w1) * (h @ w3)) @ w2


def moe_swiglu(h, w_r, b_r, w1, w3, w2):
    logits = (h @ w_r).astype(jnp.float32) + b_r
    top_val, top_idx = lax.top_k(logits, TOP_K_EXPERTS)
    top_w = jax.nn.softmax(top_val, axis=-1)
    combine = jnp.sum(jax.nn.one_hot(top_idx, N_EXPERTS, dtype=jnp.float32)
                      * top_w[..., None], axis=-2).astype(h.dtype)
    out = jnp.zeros_like(h)
    for e in range(N_EXPERTS):
        out = out + combine[..., e:e + 1] * swiglu(h, w1[e], w3[e], w2[e])
    return out


def setup_inputs(seed: int = 0) -> dict:
    key = jax.random.key(seed)
    ks = jax.random.split(key, 24)
    nrm = jax.random.normal
    f32 = jnp.float32
    w_br = jnp.concatenate(
        [nrm(ks[7 + j], (DEPTH, w, D_MODEL), f32) * w ** -0.5 for j, w in enumerate(BRANCH_WIDTHS)],
        axis=1)
    return {
        "x": nrm(ks[0], (BATCH, SEQ, D_MODEL), f32),
        "c": nrm(ks[1], (BATCH, D_MODEL), f32),
        "w_ada": nrm(ks[2], (DEPTH, D_MODEL, 6 * D_MODEL), f32) * 0.5 * D_MODEL ** -0.5,
        "b_ada": nrm(ks[3], (DEPTH, 6 * D_MODEL), f32) * 0.02,
        "norm1_g": 1.0 + 0.02 * nrm(ks[4], (DEPTH, D_MODEL), f32),
        "norm2_g": 1.0 + 0.02 * nrm(ks[5], (DEPTH, D_MODEL), f32),
        "w_in": nrm(ks[6], (DEPTH, D_MODEL, IN_WIDTH), f32) * D_MODEL ** -0.5,
        "w_br": w_br,
        "w_out": nrm(ks[10], (DEPTH, D_MODEL, D_MODEL), f32) * D_MODEL ** -0.5,
        "rel_bias": nrm(ks[11], (REL_BUCKETS, A_HEADS + B_HEADS), f32) * 0.5,
        "lam_params": nrm(ks[12], (DEPTH, 4, B_HEAD_DIM), f32) * 0.1,
        "subln_g": 1.0 + 0.02 * nrm(ks[13], (DEPTH, 2 * B_HEAD_DIM), f32),
        "ffn_w1": nrm(ks[14], (N_DENSE, D_MODEL, D_FF), f32) * D_MODEL ** -0.5,
        "ffn_w3": nrm(ks[15], (N_DENSE, D_MODEL, D_FF), f32) * D_MODEL ** -0.5,
        "ffn_w2": nrm(ks[16], (N_DENSE, D_FF, D_MODEL), f32) * D_FF ** -0.5,
        "router_w": nrm(ks[17], (N_MOE, D_MODEL, N_EXPERTS), f32) * D_MODEL ** -0.5,
        "router_b": nrm(ks[18], (N_MOE, N_EXPERTS), f32) * 0.01,
        "moe_w1": nrm(ks[19], (N_MOE, N_EXPERTS, D_MODEL, D_FF_EXPERT), f32) * D_MODEL ** -0.5,
        "moe_w3": nrm(ks[20], (N_MOE, N_EXPERTS, D_MODEL, D_FF_EXPERT), f32) * D_MODEL ** -0.5,
        "moe_w2": nrm(ks[21], (N_MOE, N_EXPERTS, D_FF_EXPERT, D_MODEL), f32) * D_FF_EXPERT ** -0.5,
        "final_g": 1.0 + 0.02 * nrm(ks[22], (D_MODEL,), f32),
    }


def reference(x, c, w_ada, b_ada, norm1_g, norm2_g, w_in, w_br, w_out, rel_bias,
              lam_params, subln_g, ffn_w1, ffn_w3, ffn_w2, router_w, router_b,
              moe_w1, moe_w3, moe_w2, final_g):
    bsz, s_len = x.shape[0], x.shape[1]
    in_offsets = [int(o) for o in np.cumsum(IN_SIZES)[:-1]]
    br_offsets = [int(o) for o in np.cumsum(BRANCH_WIDTHS)[:-1]]
    bias_a = rel_bias[:, :A_HEADS]
    bias_b = rel_bias[:, A_HEADS:]
    for l in range(DEPTH):
        mod = jax.nn.silu(c) @ w_ada[l] + b_ada[l]
        sh1, sc1, g1, sh2, sc2, g2 = jnp.split(mod[:, None, :], 6, axis=-1)
        h = rmsnorm(x, norm1_g[l]) * (1.0 + sc1) + sh1
        parts = jnp.split(h @ w_in[l], in_offsets, axis=-1)
        (qa, ka, va, qi, ki, wi, qb, kb, vb, qc, kc, vc, gl) = parts
        o_a = dsa_mixer(qa.reshape(bsz, s_len, A_HEADS, A_HEAD_DIM),
                        ka.reshape(bsz, s_len, A_HEADS, A_HEAD_DIM),
                        va.reshape(bsz, s_len, A_HEADS, A_HEAD_DIM),
                        qi.reshape(bsz, s_len, IDX_HEADS, IDX_DIM), ki,
                        wi * IDX_HEADS ** -0.5, bias_a)
        lam_init = 0.8 - 0.6 * math.exp(-0.3 * l)
        lp = lam_params[l].astype(jnp.float32)
        lam = jnp.exp(jnp.sum(lp[0] * lp[1])) - jnp.exp(jnp.sum(lp[2] * lp[3])) + lam_init
        o_b = diff_mixer(qb.reshape(bsz, s_len, B_HEADS, 2, B_HEAD_DIM),
                         kb.reshape(bsz, s_len, B_HEADS, 2, B_HEAD_DIM),
                         vb.reshape(bsz, s_len, B_HEADS, 2 * B_HEAD_DIM),
                         lam, lam_init, subln_g[l], bias_b)
        o_c = stickbreak_mixer(qc.reshape(bsz, s_len, C_HEADS, C_HEAD_DIM),
                               kc.reshape(bsz, s_len, C_HEADS, C_HEAD_DIM),
                               vc.reshape(bsz, s_len, C_HEADS, C_HEAD_DIM))
        gates = jax.nn.sigmoid(gl.reshape(bsz, s_len, N_BRANCHES, D_MODEL))
        w_rows = jnp.split(w_br[l], br_offsets, axis=0)
        merged = (gates[:, :, 0] * (o_a @ w_rows[0])
                  + gates[:, :, 1] * (o_b @ w_rows[1])
                  + gates[:, :, 2] * (o_c @ w_rows[2]))
        x = x + g1 * (merged @ w_out[l])
        h2 = rmsnorm(x, norm2_g[l]) * (1.0 + sc2) + sh2
        j = l // 2
        if l % 2 == 0:
            f = swiglu(h2, ffn_w1[j], ffn_w3[j], ffn_w2[j])
        else:
            f = moe_swiglu(h2, router_w[j], router_b[j], moe_w1[j], moe_w3[j], moe_w2[j])
        x = x + g2 * f
    return rmsnorm(x, final_g)
```

```python
import functools
import math

import jax
import jax.numpy as jnp
from jax import lax
from jax.experimental import pallas as pl
from jax.experimental.pallas import tpu as pltpu

F32 = jnp.float32
BF16 = jnp.bfloat16
I32 = jnp.int32

LANES = 128
VMEM_LIMIT_BYTES = 56 * 1024 * 1024

CHUNK = 64
A_HEADS = 8
IDX_HEADS = 4
IDX_DIM = 64
TOPK_MAX = 256
B_HEADS = 4
C_HEADS = 8
HEAD_DIM = 64
REL_BUCKETS = 32
N_EXPERTS = 8
EPS = 1e-6
SUBLN_EPS = 1e-5
NEG = -1e30
INT_MIN = -(2 ** 31)

KEY_BLOCK = 256
Q_TILE = 256

W_A = A_HEADS * HEAD_DIM
OFF_G = 0
OFF_QA, OFF_KA, OFF_VA = 3072, 3584, 4096
OFF_QI, OFF_KI, OFF_WI = 4608, 4864, 4992
OFF_QB, OFF_KB, OFF_VB = 5120, 5632, 6144
OFF_QC, OFF_KC, OFF_VC = 6656, 7168, 7680
PACKED = 8192

LOG_BUCKET_STEPS = (12, 16, 23, 32, 46, 64, 91)


def _nt_dot(a, b):
    return lax.dot_general(a, b, (((1,), (1,)), ((), ())), preferred_element_type=F32)


def _dot(a, b):
    return jnp.dot(a, b, preferred_element_type=F32)


def _cparams(*sem):
    return pltpu.CompilerParams(dimension_semantics=sem, vmem_limit_bytes=VMEM_LIMIT_BYTES)


def _split_heads(x):
    lane = lax.broadcasted_iota(I32, x.shape, 1)
    keep_a = jnp.where(lane < HEAD_DIM, 1.0, 0.0).astype(x.dtype)
    keep_b = jnp.where(lane < HEAD_DIM, 0.0, 1.0).astype(x.dtype)
    return jnp.concatenate([x * keep_a, x * keep_b], axis=0)


def _merge_heads(o, m):
    lane = lax.broadcasted_iota(I32, (m, LANES), 1)
    return jnp.where(lane < HEAD_DIM, o[:m], o[m:])


def _ada_kernel(c_ref, w_ref, b_ref, o_ref):
    c = c_ref[...]
    a = c * jax.nn.sigmoid(c)
    o_ref[0] = jnp.dot(a, w_ref[0], preferred_element_type=F32,
                       precision=lax.Precision.HIGHEST) + b_ref[0]


def _ada(c_pad, w_ada, b_ada):
    depth, d, n = w_ada.shape
    tn = 1024
    return pl.pallas_call(
        _ada_kernel,
        grid=(depth, n // tn),
        in_specs=[pl.BlockSpec(c_pad.shape, lambda l, j: (0, 0)),
                  pl.BlockSpec((1, d, tn), lambda l, j: (l, 0, j)),
                  pl.BlockSpec((1, 1, tn), lambda l, j: (l, 0, j))],
        out_specs=pl.BlockSpec((1, c_pad.shape[0], tn), lambda l, j: (l, 0, j)),
        out_shape=jax.ShapeDtypeStruct((depth, c_pad.shape[0], n), F32),
        compiler_params=_cparams("parallel", "parallel"),
        name="ada",
    )(c_pad, w_ada, b_ada.reshape(depth, 1, n))


def _rel_bias_tile(tab_ref, head, rows, d0, n_heads_total):
    r = lax.broadcasted_iota(I32, (rows, KEY_BLOCK), 0)
    c = lax.broadcasted_iota(I32, (rows, KEY_BLOCK), 1)
    d = c - r + d0
    n = jnp.abs(d)
    large = jnp.full(d.shape, REL_BUCKETS // 4, I32)
    for step in LOG_BUCKET_STEPS:
        large = large + jnp.where(n >= step, 1, 0)
    bucket = jnp.where(d > 0, REL_BUCKETS // 2, 0) + jnp.where(n < REL_BUCKETS // 4, n, large)
    out = jnp.zeros(d.shape, F32)
    for b in range(REL_BUCKETS):
        out = jnp.where(bucket == b, tab_ref[b * n_heads_total + head], out)
    return out


def _bias_kernel(tab_ref, dsa_ref, diff_ref):
    p = pl.program_id(0)
    n_heads = A_HEADS + B_HEADS
    far = -8 * KEY_BLOCK
    d0s = [far] + [-KEY_BLOCK - CHUNK * ph for ph in range(4)] + [-CHUNK * ph for ph in range(4)]
    for t, d0 in enumerate(d0s):
        dsa_ref[0, t, 0:CHUNK, :] = _rel_bias_tile(tab_ref, 2 * p, CHUNK, d0, n_heads)
        dsa_ref[0, t, CHUNK:2 * CHUNK, :] = _rel_bias_tile(tab_ref, 2 * p + 1, CHUNK, d0, n_heads)
    for t, d0 in enumerate([far, -KEY_BLOCK, 0]):
        tile = _rel_bias_tile(tab_ref, A_HEADS + p, Q_TILE, d0, n_heads)
        if t == 2:
            r = lax.broadcasted_iota(I32, tile.shape, 0)
            c = lax.broadcasted_iota(I32, tile.shape, 1)
            tile = jnp.where((c // CHUNK) <= (r // CHUNK), tile, NEG)
        diff_ref[0, t] = tile


def _bias_tiles(rel_bias):
    tab = rel_bias.reshape(-1)
    return pl.pallas_call(
        _bias_kernel,
        grid=(4,),
        in_specs=[pl.BlockSpec(memory_space=pltpu.SMEM)],
        out_specs=[pl.BlockSpec((1, 9, 2 * CHUNK, KEY_BLOCK), lambda p: (p, 0, 0, 0)),
                   pl.BlockSpec((1, 3, Q_TILE, KEY_BLOCK), lambda p: (p, 0, 0, 0))],
        out_shape=[jax.ShapeDtypeStruct((A_HEADS // 2, 9, 2 * CHUNK, KEY_BLOCK), F32),
                   jax.ShapeDtypeStruct((B_HEADS, 3, Q_TILE, KEY_BLOCK), F32)],
        compiler_params=_cparams("parallel"),
        name="rel_bias_tiles",
    )(tab)


def _modulated_norm(x, g, sc, sh):
    y = x * lax.rsqrt(jnp.mean(x * x, axis=-1, keepdims=True) + EPS)
    return y * g * (1.0 + sc) + sh


def _inproj_kernel(x_ref, g_ref, sc_ref, sh_ref, w_ref, o_ref, h_scr):
    @pl.when(pl.program_id(1) == 0)
    def _():
        h_scr[...] = _modulated_norm(x_ref[...], g_ref[...], sc_ref[0], sh_ref[0]).astype(BF16)

    o_ref[...] = _dot(h_scr[...], w_ref[...]).astype(o_ref.dtype)


def _inproj(x2, g, mod, w_packed, s_len):
    t, d = x2.shape
    n = w_packed.shape[1]
    tm, tn = 1024, 1024
    per_b = s_len // tm
    return pl.pallas_call(
        _inproj_kernel,
        grid=(t // tm, n // tn),
        in_specs=[pl.BlockSpec((tm, d), lambda i, j: (i, 0)),
                  pl.BlockSpec((1, d), lambda i, j: (0, 0)),
                  pl.BlockSpec((1, 1, d), lambda i, j: (i // per_b, 0, 1)),
                  pl.BlockSpec((1, 1, d), lambda i, j: (i // per_b, 0, 0)),
                  pl.BlockSpec((d, tn), lambda i, j: (0, j))],
        out_specs=pl.BlockSpec((tm, tn), lambda i, j: (i, j)),
        out_shape=jax.ShapeDtypeStruct((t, n), BF16),
        scratch_shapes=[pltpu.VMEM((tm, d), BF16)],
        compiler_params=_cparams("parallel", "arbitrary"),
        name="inproj",
    )(x2, g, mod, mod, w_packed)


def _dsa_kernel(qa_ref, qi_ref, wi_ref, ka_ref, va_ref, ki_ref, bias_ref, o_ref,
                key_scr, neg_scr, *, topk, idx_bits):
    i = pl.program_id(1)
    t0 = i * CHUNK
    limit = t0 + CHUNK
    per_block = KEY_BLOCK // CHUNK
    kbq = i // per_block
    ph = i % per_block
    nkb = kbq + 1
    kf = float(topk)

    def blk(kb):
        return pl.ds(pl.multiple_of(kb * KEY_BLOCK, KEY_BLOCK), KEY_BLOCK)

    def cols(kb):
        return kb * KEY_BLOCK + lax.broadcasted_iota(I32, (CHUNK, KEY_BLOCK), 1)

    qi = qi_ref[0]
    q_heads = [_split_heads(qi[:, pr * LANES:(pr + 1) * LANES]) for pr in range(IDX_HEADS // 2)]
    wi = wi_ref[0].astype(F32)
    w_cols = [wi[:, h:h + 1] for h in range(IDX_HEADS)]

    def score_body(kb, carry):
        kk = ki_ref[0, blk(kb), :]
        sc = jnp.zeros((CHUNK, KEY_BLOCK), F32)
        for pr in range(IDX_HEADS // 2):
            d = _nt_dot(q_heads[pr], kk)
            sc = sc + w_cols[2 * pr] * jnp.maximum(d[:CHUNK], 0.0)
            sc = sc + w_cols[2 * pr + 1] * jnp.maximum(d[CHUNK:], 0.0)
        sc = jnp.where(cols(kb) < limit, sc, -jnp.inf)
        bits = pltpu.bitcast(sc, I32)
        key_scr[kb] = bits ^ ((bits >> 31) & 0x7FFFFFFF)
        return carry

    lax.fori_loop(0, nkb, score_body, 0)

    def count(pred):
        def body(kb, acc):
            return acc + jnp.where(pred(key_scr[kb], cols(kb)), 1.0, 0.0)
        acc = lax.fori_loop(0, nkb, body, jnp.zeros((CHUNK, KEY_BLOCK), F32))
        return jnp.sum(acc, axis=-1, keepdims=True)

    thr = jnp.where(count(lambda key, col: key >= 0) >= kf,
                    jnp.zeros((CHUNK, 1), I32), jnp.full((CHUNK, 1), INT_MIN, I32))

    def thr_body(j, thr):
        cand = thr | lax.shift_left(jnp.int32(1), 30 - j)
        return jnp.where(count(lambda key, col: key >= cand) >= kf, cand, thr)

    thr = lax.fori_loop(0, 31, thr_body, thr)

    need = kf - count(lambda key, col: key > thr)

    def tie_body(j, last):
        cand = last | lax.shift_left(jnp.int32(1), idx_bits - 1 - j)
        taken = count(lambda key, col: (key == thr) & (col < cand))
        return jnp.where(taken < need, cand, last)

    last = lax.fori_loop(0, idx_bits, tie_body, jnp.zeros((CHUNK, 1), I32))

    def mask_body(kb, carry):
        key = key_scr[kb]
        col = cols(kb)
        tie_neg = jnp.where(key == thr, jnp.where(col <= last, 0.0, NEG), NEG)
        neg = jnp.where(col < limit, jnp.where(key > thr, 0.0, tie_neg), NEG)
        neg_scr[kb, 0:CHUNK, :] = neg
        neg_scr[kb, CHUNK:2 * CHUNK, :] = neg
        return carry

    lax.fori_loop(0, nkb, mask_body, 0)

    for pr in range(A_HEADS // 2):
        lanes = slice(pr * LANES, (pr + 1) * LANES)
        qs = _split_heads(qa_ref[0, :, lanes])

        def attn_body(kb, carry, lanes=lanes, qs=qs, pr=pr):
            m, l, acc = carry
            s = _nt_dot(qs, ka_ref[0, blk(kb), lanes])
            tile = jnp.where(kb == kbq, 1 + per_block + ph, jnp.where(kb == kbq - 1, 1 + ph, 0))
            s = s + bias_ref[pr, tile] + neg_scr[kb]
            m_new = jnp.maximum(m, jnp.max(s, axis=-1, keepdims=True))
            alpha = jnp.exp(m - m_new)
            p = jnp.exp(s - m_new)
            l = alpha * l + jnp.sum(p, axis=-1, keepdims=True)
            acc = alpha * acc + _dot(p.astype(BF16), va_ref[0, blk(kb), lanes])
            return m_new, l, acc

        init = (jnp.full((2 * CHUNK, 1), NEG, F32), jnp.zeros((2 * CHUNK, 1), F32),
                jnp.zeros((2 * CHUNK, LANES), F32))
        _, l, acc = lax.fori_loop(0, nkb, attn_body, init)
        o_ref[0, :, lanes] = _merge_heads(acc / l, CHUNK).astype(o_ref.dtype)


def _dsa(proj, bias_tiles, bsz, s_len):
    topk = min(TOPK_MAX, s_len // 4)
    idx_bits = max(1, (s_len - 1).bit_length())
    kern = functools.partial(_dsa_kernel, topk=topk, idx_bits=idx_bits)
    col = lambda off, width: off // width
    return pl.pallas_call(
        kern,
        grid=(bsz, s_len // CHUNK),
        in_specs=[pl.BlockSpec((1, CHUNK, W_A), lambda b, i: (b, i, col(OFF_QA, W_A))),
                  pl.BlockSpec((1, CHUNK, 256), lambda b, i: (b, i, col(OFF_QI, 256))),
                  pl.BlockSpec((1, CHUNK, LANES), lambda b, i: (b, i, col(OFF_WI, LANES))),
                  pl.BlockSpec((1, s_len, W_A), lambda b, i: (b, 0, col(OFF_KA, W_A))),
                  pl.BlockSpec((1, s_len, W_A), lambda b, i: (b, 0, col(OFF_VA, W_A))),
                  pl.BlockSpec((1, s_len, LANES), lambda b, i: (b, 0, col(OFF_KI, LANES))),
                  pl.BlockSpec(bias_tiles.shape, lambda b, i: (0, 0, 0, 0))],
        out_specs=pl.BlockSpec((1, CHUNK, W_A), lambda b, i: (b, i, 0)),
        out_shape=jax.ShapeDtypeStruct((bsz, s_len, W_A), BF16),
        scratch_shapes=[pltpu.VMEM((s_len // KEY_BLOCK, CHUNK, KEY_BLOCK), I32),
                        pltpu.VMEM((s_len // KEY_BLOCK, 2 * CHUNK, KEY_BLOCK), F32)],
        compiler_params=_cparams("parallel", "arbitrary"),
        name="dsa",
    )(proj, proj, proj, proj, proj, proj, bias_tiles)


def _diff_kernel(q_ref, k_ref, v_ref, bias_ref, lp_ref, g_ref, o_ref, *, lam_init):
    qb = pl.program_id(2)
    qs = _split_heads(q_ref[0])

    def body(kb, carry):
        m, l, acc = carry
        rows = pl.ds(pl.multiple_of(kb * KEY_BLOCK, KEY_BLOCK), KEY_BLOCK)
        s = _nt_dot(qs, k_ref[0, rows, :])
        tile = jnp.where(kb == qb, 2, jnp.where(kb == qb - 1, 1, 0))
        bias = bias_ref[0, tile]
        s = jnp.concatenate([s[:Q_TILE] + bias, s[Q_TILE:] + bias], axis=0)
        m_new = jnp.maximum(m, jnp.max(s, axis=-1, keepdims=True))
        alpha = jnp.exp(m - m_new)
        p = jnp.exp(s - m_new)
        l = alpha * l + jnp.sum(p, axis=-1, keepdims=True)
        acc = alpha * acc + _dot(p.astype(BF16), v_ref[0, rows, :])
        return m_new, l, acc

    init = (jnp.full((2 * Q_TILE, 1), NEG, F32), jnp.zeros((2 * Q_TILE, 1), F32),
            jnp.zeros((2 * Q_TILE, LANES), F32))
    _, l, acc = lax.fori_loop(0, qb + 1, body, init)
    o = acc / l
    lp = lp_ref[0]
    lam = (jnp.exp(jnp.sum(lp[0:1] * lp[1:2], axis=-1, keepdims=True))
           - jnp.exp(jnp.sum(lp[2:3] * lp[3:4], axis=-1, keepdims=True)) + lam_init)
    o = o[:Q_TILE] - lam * o[Q_TILE:]
    y = o * lax.rsqrt(jnp.mean(o * o, axis=-1, keepdims=True) + SUBLN_EPS)
    o_ref[0] = (y * g_ref[0] * (1.0 - lam_init)).astype(o_ref.dtype)


def _diff(proj, bias_tiles, lam_params, subln_g, layer, bsz, s_len, lam_init):
    kern = functools.partial(_diff_kernel, lam_init=lam_init)
    return pl.pallas_call(
        kern,
        grid=(bsz, B_HEADS, s_len // Q_TILE),
        in_specs=[pl.BlockSpec((1, Q_TILE, LANES), lambda b, h, i: (b, i, OFF_QB // LANES + h)),
                  pl.BlockSpec((1, s_len, LANES), lambda b, h, i: (b, 0, OFF_KB // LANES + h)),
                  pl.BlockSpec((1, s_len, LANES), lambda b, h, i: (b, 0, OFF_VB // LANES + h)),
                  pl.BlockSpec((1, 3, Q_TILE, KEY_BLOCK), lambda b, h, i: (h, 0, 0, 0)),
                  pl.BlockSpec((1, 4, HEAD_DIM), lambda b, h, i: (layer, 0, 0)),
                  pl.BlockSpec((1, 1, 2 * HEAD_DIM), lambda b, h, i: (layer, 0, 0))],
        out_specs=pl.BlockSpec((1, Q_TILE, LANES), lambda b, h, i: (b, i, h)),
        out_shape=jax.ShapeDtypeStruct((bsz, s_len, B_HEADS * 2 * HEAD_DIM), BF16),
        compiler_params=_cparams("parallel", "parallel", "arbitrary"),
        name="diff_attn",
    )(proj, proj, proj, bias_tiles, lam_params, subln_g.reshape(-1, 1, 2 * HEAD_DIM))


def _stick_kernel(q_ref, k_ref, v_ref, o_ref):
    qb = pl.program_id(2)
    qs = _split_heads(q_ref[0])
    jj = lax.broadcasted_iota(I32, (KEY_BLOCK, KEY_BLOCK), 0)
    ss = lax.broadcasted_iota(I32, (KEY_BLOCK, KEY_BLOCK), 1)
    later = jnp.where(jj > ss, 1.0, 0.0).astype(BF16)
    rr = lax.broadcasted_iota(I32, (2 * Q_TILE, KEY_BLOCK), 0) & (Q_TILE - 1)
    cc = lax.broadcasted_iota(I32, (2 * Q_TILE, KEY_BLOCK), 1)
    strict = cc < rr

    def block(kb, run, acc, diagonal):
        rows = pl.ds(pl.multiple_of(kb * KEY_BLOCK, KEY_BLOCK), KEY_BLOCK)
        z = _nt_dot(qs, k_ref[0, rows, :])
        soft = jnp.log1p(jnp.exp(-jnp.abs(z)))
        log_beta = jnp.minimum(z, 0.0) - soft
        log_1mb = -(jnp.maximum(z, 0.0) + soft)
        if diagonal:
            log_1mb = jnp.where(strict, log_1mb, 0.0)
        hi = log_1mb.astype(BF16)
        lo = (log_1mb - hi.astype(F32)).astype(BF16)
        between = _dot(hi, later) + _dot(lo, later) + run
        w = jnp.exp(log_beta + between)
        if diagonal:
            w = jnp.where(strict, w, 0.0)
        acc = acc + _dot(w.astype(BF16), v_ref[0, rows, :])
        run = run + jnp.sum(log_1mb, axis=-1, keepdims=True)
        return run, acc

    run, acc = block(qb, jnp.zeros((2 * Q_TILE, 1), F32), jnp.zeros((2 * Q_TILE, LANES), F32), True)

    def body(j, carry):
        return block(qb - 1 - j, carry[0], carry[1], False)

    run, acc = lax.fori_loop(0, qb, body, (run, acc))
    o_ref[0] = _merge_heads(acc, Q_TILE).astype(o_ref.dtype)


def _stick(proj, bsz, s_len):
    return pl.pallas_call(
        _stick_kernel,
        grid=(bsz, C_HEADS // 2, s_len // Q_TILE),
        in_specs=[pl.BlockSpec((1, Q_TILE, LANES), lambda b, h, i: (b, i, OFF_QC // LANES + h)),
                  pl.BlockSpec((1, s_len, LANES), lambda b, h, i: (b, 0, OFF_KC // LANES + h)),
                  pl.BlockSpec((1, s_len, LANES), lambda b, h, i: (b, 0, OFF_VC // LANES + h))],
        out_specs=pl.BlockSpec((1, Q_TILE, LANES), lambda b, h, i: (b, i, h)),
        out_shape=jax.ShapeDtypeStruct((bsz, s_len, C_HEADS * HEAD_DIM), BF16),
        compiler_params=_cparams("parallel", "parallel", "arbitrary"),
        name="stick_attn",
    )(proj, proj, proj)


def _merge_kernel(oa_ref, ob_ref, oc_ref, gate_ref, x_ref, g1_ref, wbr_ref, wout_ref, o_ref):
    d = x_ref.shape[-1]
    merged = jnp.zeros(x_ref.shape, F32)
    for j, o_br in enumerate((oa_ref, ob_ref, oc_ref)):
        w = o_br.shape[-1]
        gate = jax.nn.sigmoid(gate_ref[:, j * d:(j + 1) * d].astype(F32))
        merged = merged + gate * _dot(o_br[...], wbr_ref[j * w:(j + 1) * w, :])
    o_ref[...] = x_ref[...] + g1_ref[0] * _dot(merged.astype(BF16), wout_ref[...])


def _merge(o_a, o_b, o_c, proj, x2, mod, w_br, w_out, s_len):
    t, d = x2.shape
    tm = 512
    per_b = s_len // tm
    w = o_a.shape[-1]
    return pl.pallas_call(
        _merge_kernel,
        grid=(t // tm,),
        in_specs=[pl.BlockSpec((tm, w), lambda i: (i, 0)),
                  pl.BlockSpec((tm, w), lambda i: (i, 0)),
                  pl.BlockSpec((tm, w), lambda i: (i, 0)),
                  pl.BlockSpec((tm, 3 * d), lambda i: (i, OFF_G // (3 * d))),
                  pl.BlockSpec((tm, d), lambda i: (i, 0)),
                  pl.BlockSpec((1, 1, d), lambda i: (i // per_b, 0, 2)),
                  pl.BlockSpec(w_br.shape, lambda i: (0, 0)),
                  pl.BlockSpec(w_out.shape, lambda i: (0, 0))],
        out_specs=pl.BlockSpec((tm, d), lambda i: (i, 0)),
        out_shape=jax.ShapeDtypeStruct((t, d), F32),
        compiler_params=_cparams("parallel"),
        name="merge",
    )(o_a, o_b, o_c, proj, x2, mod, w_br, w_out)


def _finish(x, gate, acc, fg_ref, final):
    y = x + gate * acc
    if final:
        y = y * lax.rsqrt(jnp.mean(y * y, axis=-1, keepdims=True) + EPS) * fg_ref[...]
    return y


def _swiglu_partial(h, w1, w3, w2):
    a = _dot(h, w1)
    act = (a * jax.nn.sigmoid(a)) * _dot(h, w3)
    return _dot(act.astype(BF16), w2)


def _ffn_kernel(x_ref, g_ref, sc_ref, sh_ref, gate_ref, w1_ref, w3_ref, w2_ref, fg_ref, o_ref,
                h_scr, acc_scr, *, final):
    f = pl.program_id(1)

    @pl.when(f == 0)
    def _():
        h_scr[...] = _modulated_norm(x_ref[...], g_ref[...], sc_ref[0], sh_ref[0]).astype(BF16)
        acc_scr[...] = jnp.zeros_like(acc_scr)

    acc_scr[...] += _swiglu_partial(h_scr[...], w1_ref[...], w3_ref[...], w2_ref[...])

    @pl.when(f == pl.num_programs(1) - 1)
    def _():
        o_ref[...] = _finish(x_ref[...], gate_ref[0], acc_scr[...], fg_ref, final)


def _ffn(x2, g, mod, w1, w3, w2, final_g, s_len, final):
    t, d = x2.shape
    d_ff = w1.shape[1]
    tm, tf = 1024, 256
    per_b = s_len // tm
    kern = functools.partial(_ffn_kernel, final=final)
    return pl.pallas_call(
        kern,
        grid=(t // tm, d_ff // tf),
        in_specs=[pl.BlockSpec((tm, d), lambda i, f: (i, 0)),
                  pl.BlockSpec((1, d), lambda i, f: (0, 0)),
                  pl.BlockSpec((1, 1, d), lambda i, f: (i // per_b, 0, 4)),
                  pl.BlockSpec((1, 1, d), lambda i, f: (i // per_b, 0, 3)),
                  pl.BlockSpec((1, 1, d), lambda i, f: (i // per_b, 0, 5)),
                  pl.BlockSpec((d, tf), lambda i, f: (0, f)),
                  pl.BlockSpec((d, tf), lambda i, f: (0, f)),
                  pl.BlockSpec((tf, d), lambda i, f: (f, 0)),
                  pl.BlockSpec((1, d), lambda i, f: (0, 0))],
        out_specs=pl.BlockSpec((tm, d), lambda i, f: (i, 0)),
        out_shape=jax.ShapeDtypeStruct((t, d), F32),
        scratch_shapes=[pltpu.VMEM((tm, d), BF16), pltpu.VMEM((tm, d), F32)],
        compiler_params=_cparams("parallel", "arbitrary"),
        name="ffn",
    )(x2, g, mod, mod, mod, w1, w3, w2, final_g)


def _route(h, wr, br):
    h_hi = h.astype(BF16)
    h_lo = (h - h_hi.astype(F32)).astype(BF16)
    w_hi = wr.astype(BF16)
    w_lo = (wr - w_hi.astype(F32)).astype(BF16)
    logits = _dot(h_hi, w_hi) + _dot(h_hi, w_lo) + _dot(h_lo, w_hi) + br
    lane = lax.broadcasted_iota(I32, logits.shape, 1).astype(F32)
    lg = jnp.where(lane < N_EXPERTS, logits, -jnp.inf)
    m1 = jnp.max(lg, axis=-1, keepdims=True)
    i1 = jnp.min(jnp.where(lg == m1, lane, float(LANES)), axis=-1, keepdims=True)
    lg2 = jnp.where(lane == i1, -jnp.inf, lg)
    m2 = jnp.max(lg2, axis=-1, keepdims=True)
    i2 = jnp.min(jnp.where(lg2 == m2, lane, float(LANES)), axis=-1, keepdims=True)
    e2 = jnp.exp(m2 - m1)
    w_top = 1.0 / (1.0 + e2)
    return jnp.where(lane == i1, w_top, 0.0) + jnp.where(lane == i2, e2 * w_top, 0.0)


def _moe_kernel(x_ref, g_ref, sc_ref, sh_ref, gate_ref, wr_ref, br_ref, w1_ref, w3_ref, w2_ref,
                fg_ref, o_ref, h_scr, comb_scr, acc_scr, *, final):
    e = pl.program_id(1)
    f = pl.program_id(2)

    @pl.when((e == 0) & (f == 0))
    def _():
        h = _modulated_norm(x_ref[...], g_ref[...], sc_ref[0], sh_ref[0])
        h_scr[...] = h.astype(BF16)
        comb_scr[...] = _route(h, wr_ref[...], br_ref[...])
        acc_scr[...] = jnp.zeros_like(acc_scr)

    comb = comb_scr[...]
    lane = lax.broadcasted_iota(I32, comb.shape, 1)
    c_e = jnp.sum(jnp.where(lane == e, comb, 0.0), axis=-1, keepdims=True)
    acc_scr[...] += c_e * _swiglu_partial(h_scr[...], w1_ref[0], w3_ref[0], w2_ref[0])

    @pl.when((e == pl.num_programs(1) - 1) & (f == pl.num_programs(2) - 1))
    def _():
        o_ref[...] = _finish(x_ref[...], gate_ref[0], acc_scr[...], fg_ref, final)


def _moe(x2, g, mod, wr_pad, br_pad, w1, w3, w2, final_g, s_len, final):
    t, d = x2.shape
    n_e, _, d_ff = w1.shape
    tm, tf = 1024, 896
    per_b = s_len // tm
    kern = functools.partial(_moe_kernel, final=final)
    return pl.pallas_call(
        kern,
        grid=(t // tm, n_e, d_ff // tf),
        in_specs=[pl.BlockSpec((tm, d), lambda i, e, f: (i, 0)),
                  pl.BlockSpec((1, d), lambda i, e, f: (0, 0)),
                  pl.BlockSpec((1, 1, d), lambda i, e, f: (i // per_b, 0, 4)),
                  pl.BlockSpec((1, 1, d), lambda i, e, f: (i // per_b, 0, 3)),
                  pl.BlockSpec((1, 1, d), lambda i, e, f: (i // per_b, 0, 5)),
                  pl.BlockSpec((d, LANES), lambda i, e, f: (0, 0)),
                  pl.BlockSpec((1, LANES), lambda i, e, f: (0, 0)),
                  pl.BlockSpec((1, d, tf), lambda i, e, f: (e, 0, f)),
                  pl.BlockSpec((1, d, tf), lambda i, e, f: (e, 0, f)),
                  pl.BlockSpec((1, tf, d), lambda i, e, f: (e, f, 0)),
                  pl.BlockSpec((1, d), lambda i, e, f: (0, 0))],
        out_specs=pl.BlockSpec((tm, d), lambda i, e, f: (i, 0)),
        out_shape=jax.ShapeDtypeStruct((t, d), F32),
        scratch_shapes=[pltpu.VMEM((tm, d), BF16), pltpu.VMEM((tm, LANES), F32),
                        pltpu.VMEM((tm, d), F32)],
        compiler_params=_cparams("parallel", "arbitrary", "arbitrary"),
        name="moe",
    )(x2, g, mod, mod, mod, wr_pad, br_pad, w1, w3, w2, final_g)


def _pack_w_in(w):
    d = w.shape[0]
    sizes = (W_A, W_A, W_A, IDX_HEADS * IDX_DIM, IDX_DIM, IDX_HEADS,
             512, 512, 512, 512, 512, 512, 3 * d)
    offs = [0]
    for s in sizes:
        offs.append(offs[-1] + s)
    (qa, ka, va, qi, ki, wi, qb, kb, vb, qc, kc, vc, gl) = [
        w[:, offs[j]:offs[j + 1]] for j in range(len(sizes))]
    scale = HEAD_DIM ** -0.5
    pad_wi = jnp.zeros((d, LANES - IDX_HEADS), w.dtype)
    packed = jnp.concatenate(
        [gl, qa * scale, ka, va,
         qi * (IDX_DIM ** -0.5), ki, ki, wi * (IDX_HEADS ** -0.5), pad_wi,
         qb * scale, kb, vb, qc * scale, kc, vc], axis=1)
    assert packed.shape[1] == PACKED
    return packed.astype(BF16)


def kernel(x, c, w_ada, b_ada, norm1_g, norm2_g, w_in, w_br, w_out, rel_bias, lam_params,
           subln_g, ffn_w1, ffn_w3, ffn_w2, router_w, router_b, moe_w1, moe_w3, moe_w2, final_g):
    bsz, s_len, d = x.shape
    depth = w_ada.shape[0]
    assert s_len % 1024 == 0 and d == 1024 and OFF_G + 3 * d == OFF_QA

    c_pad = jnp.concatenate([c, jnp.zeros((8 - bsz % 8 if bsz % 8 else 0, d), c.dtype)], axis=0)
    mod_all = _ada(c_pad, w_ada, b_ada)
    dsa_tiles, diff_tiles = _bias_tiles(rel_bias)
    fg = final_g.reshape(1, d)

    x2 = x.reshape(bsz * s_len, d)
    for l in range(depth):
        mod = mod_all[l, :bsz].reshape(bsz, 1, 6 * d)
        proj = _inproj(x2, norm1_g[l].reshape(1, d), mod, _pack_w_in(w_in[l]), s_len)
        proj3 = proj.reshape(bsz, s_len, PACKED)
        lam_init = 0.8 - 0.6 * math.exp(-0.3 * l)
        o_a = _dsa(proj3, dsa_tiles, bsz, s_len)
        o_b = _diff(proj3, diff_tiles, lam_params, subln_g, l, bsz, s_len, lam_init)
        o_c = _stick(proj3, bsz, s_len)
        x2 = _merge(o_a.reshape(-1, o_a.shape[-1]), o_b.reshape(-1, o_b.shape[-1]),
                    o_c.reshape(-1, o_c.shape[-1]), proj, x2, mod,
                    w_br[l].astype(BF16), w_out[l].astype(BF16), s_len)
        g2 = norm2_g[l].reshape(1, d)
        final = l == depth - 1
        j = l // 2
        if l % 2 == 0:
            x2 = _ffn(x2, g2, mod, ffn_w1[j].astype(BF16), ffn_w3[j].astype(BF16),
                      ffn_w2[j].astype(BF16), fg, s_len, final)
        else:
            wr_pad = jnp.pad(router_w[j], ((0, 0), (0, LANES - N_EXPERTS)))
            br_pad = jnp.pad(router_b[j], (0, LANES - N_EXPERTS)).reshape(1, LANES)
            x2 = _moe(x2, g2, mod, wr_pad, br_pad, moe_w1[j].astype(BF16), moe_w3[j].astype(BF16),
                      moe_w2[j].astype(BF16), fg, s_len, final)
    return x2.reshape(bsz, s_len, d)
```

```python
import functools
import math

import jax
import jax.numpy as jnp
from jax import lax
from jax.experimental import pallas as pl
from jax.experimental.pallas import tpu as pltpu

F32 = jnp.float32
BF16 = jnp.bfloat16
I32 = jnp.int32

LANES = 128
VMEM_LIMIT_BYTES = 56 * 1024 * 1024

CHUNK = 64
A_HEADS = 8
IDX_HEADS = 4
IDX_DIM = 64
TOPK_MAX = 256
B_HEADS = 4
C_HEADS = 8
HEAD_DIM = 64
REL_BUCKETS = 32
FAR_BUCKET = REL_BUCKETS // 2 - 1
N_EXPERTS = 8
EPS = 1e-6
SUBLN_EPS = 1e-5
NEG = -1e30
INT_MIN = -(2 ** 31)

KEY_BLOCK = 256
Q_TILE = 256
GROUP = 4

W_A = A_HEADS * HEAD_DIM
OFF_G = 0
OFF_QA, OFF_KA, OFF_VA = 3072, 3584, 4096
OFF_QI, OFF_KI, OFF_WI = 4608, 4864, 4992
OFF_QB, OFF_KB, OFF_VB = 5120, 5632, 6144
OFF_QC, OFF_KC, OFF_VC = 6656, 7168, 7680
PACKED = 8192

LOG_BUCKET_STEPS = (12, 16, 23, 32, 46, 64, 91)


def _nt_dot(a, b):
    return lax.dot_general(a, b, (((1,), (1,)), ((), ())), preferred_element_type=F32)


def _dot(a, b):
    return jnp.dot(a, b, preferred_element_type=F32)


def _cparams(*sem):
    return pltpu.CompilerParams(dimension_semantics=sem, vmem_limit_bytes=VMEM_LIMIT_BYTES)


def _split_heads(x):
    lane = lax.broadcasted_iota(I32, x.shape, 1)
    keep_a = jnp.where(lane < HEAD_DIM, 1.0, 0.0).astype(x.dtype)
    keep_b = jnp.where(lane < HEAD_DIM, 0.0, 1.0).astype(x.dtype)
    return jnp.concatenate([x * keep_a, x * keep_b], axis=0)


def _merge_heads(o, m):
    lane = lax.broadcasted_iota(I32, (m, LANES), 1)
    return jnp.where(lane < HEAD_DIM, o[:m], o[m:])


def _wide(x):
    return jnp.concatenate([x, x], axis=1)


def _tall(x):
    return jnp.concatenate([x, x], axis=0)


def _key_rows(kb):
    return pl.ds(pl.multiple_of(kb * KEY_BLOCK, KEY_BLOCK), KEY_BLOCK)


def _chunk_causal():
    r = lax.broadcasted_iota(I32, (Q_TILE, KEY_BLOCK), 0)
    c = lax.broadcasted_iota(I32, (Q_TILE, KEY_BLOCK), 1)
    return (c // CHUNK) <= (r // CHUNK)


def _ada_kernel(c_ref, w_ref, b_ref, o_ref):
    c = c_ref[...]
    a = c * jax.nn.sigmoid(c)
    o_ref[0] = jnp.dot(a, w_ref[0], preferred_element_type=F32,
                       precision=lax.Precision.HIGHEST) + b_ref[0]


def _ada(c_pad, w_ada, b_ada):
    depth, d, n = w_ada.shape
    tn = 1024
    return pl.pallas_call(
        _ada_kernel,
        grid=(depth, n // tn),
        in_specs=[pl.BlockSpec(c_pad.shape, lambda l, j: (0, 0)),
                  pl.BlockSpec((1, d, tn), lambda l, j: (l, 0, j)),
                  pl.BlockSpec((1, 1, tn), lambda l, j: (l, 0, j))],
        out_specs=pl.BlockSpec((1, c_pad.shape[0], tn), lambda l, j: (l, 0, j)),
        out_shape=jax.ShapeDtypeStruct((depth, c_pad.shape[0], n), F32),
        compiler_params=_cparams("parallel", "parallel"),
        name="ada",
    )(c_pad, w_ada, b_ada.reshape(depth, 1, n))


def _rel_bias_tile(tab_ref, head, d0, n_heads_total):
    r = lax.broadcasted_iota(I32, (Q_TILE, KEY_BLOCK), 0)
    c = lax.broadcasted_iota(I32, (Q_TILE, KEY_BLOCK), 1)
    d = c - r + d0
    n = jnp.abs(d)
    large = jnp.full(d.shape, REL_BUCKETS // 4, I32)
    for step in LOG_BUCKET_STEPS:
        large = large + jnp.where(n >= step, 1, 0)
    bucket = jnp.where(d > 0, REL_BUCKETS // 2, 0) + jnp.where(n < REL_BUCKETS // 4, n, large)
    out = jnp.zeros(d.shape, F32)
    for b in range(REL_BUCKETS):
        out = jnp.where(bucket == b, tab_ref[b * n_heads_total + head], out)
    return out - tab_ref[FAR_BUCKET * n_heads_total + head]


def _bias_kernel(tab_ref, dsa_ref, diff_ref):
    p = pl.program_id(0)
    n_heads = A_HEADS + B_HEADS
    for t, d0 in enumerate((-KEY_BLOCK, 0)):
        dsa_ref[0, t, 0:Q_TILE, :] = _rel_bias_tile(tab_ref, 2 * p, d0, n_heads)
        dsa_ref[0, t, Q_TILE:2 * Q_TILE, :] = _rel_bias_tile(tab_ref, 2 * p + 1, d0, n_heads)
        tile = _rel_bias_tile(tab_ref, A_HEADS + p, d0, n_heads)
        if t == 1:
            tile = jnp.where(_chunk_causal(), tile, NEG)
        diff_ref[0, t] = tile


def _bias_tiles(rel_bias):
    tab = rel_bias.reshape(-1)
    return pl.pallas_call(
        _bias_kernel,
        grid=(4,),
        in_specs=[pl.BlockSpec(memory_space=pltpu.SMEM)],
        out_specs=[pl.BlockSpec((1, 2, 2 * Q_TILE, KEY_BLOCK), lambda p: (p, 0, 0, 0)),
                   pl.BlockSpec((1, 2, Q_TILE, KEY_BLOCK), lambda p: (p, 0, 0, 0))],
        out_shape=[jax.ShapeDtypeStruct((A_HEADS // 2, 2, 2 * Q_TILE, KEY_BLOCK), F32),
                   jax.ShapeDtypeStruct((B_HEADS, 2, Q_TILE, KEY_BLOCK), F32)],
        compiler_params=_cparams("parallel"),
        name="rel_bias_tiles",
    )(tab)


def _modulated_norm(x, g, sc, sh):
    y = x * lax.rsqrt(jnp.mean(x * x, axis=-1, keepdims=True) + EPS)
    return y * g * (1.0 + sc) + sh


def _inproj_kernel(x_ref, g_ref, sc_ref, sh_ref, w_ref, o_ref, h_scr):
    @pl.when(pl.program_id(1) == 0)
    def _():
        h_scr[...] = _modulated_norm(x_ref[...], g_ref[...], sc_ref[0], sh_ref[0]).astype(BF16)

    o_ref[...] = _dot(h_scr[...], w_ref[...]).astype(o_ref.dtype)


def _inproj(x2, g, mod, w_packed, s_len):
    t, d = x2.shape
    n = w_packed.shape[1]
    tm, tn = 1024, 1024
    per_b = s_len // tm
    return pl.pallas_call(
        _inproj_kernel,
        grid=(t // tm, n // tn),
        in_specs=[pl.BlockSpec((tm, d), lambda i, j: (i, 0)),
                  pl.BlockSpec((1, d), lambda i, j: (0, 0)),
                  pl.BlockSpec((1, 1, d), lambda i, j: (i // per_b, 0, 1)),
                  pl.BlockSpec((1, 1, d), lambda i, j: (i // per_b, 0, 0)),
                  pl.BlockSpec((d, tn), lambda i, j: (0, j))],
        out_specs=pl.BlockSpec((tm, tn), lambda i, j: (i, j)),
        out_shape=jax.ShapeDtypeStruct((t, n), BF16),
        scratch_shapes=[pltpu.VMEM((tm, d), BF16)],
        compiler_params=_cparams("parallel", "arbitrary"),
        name="inproj",
    )(x2, g, mod, mod, w_packed)


def _softmax_pv(qs, k_at, v_at, qb, add_far, add_near, s_scr, mx_scr, l_scr, acc_scr):
    n = qb + 1
    groups = n // GROUP
    rest = n - groups * GROUP
    grouped = groups > 0
    far_singles = jnp.where(grouped, rest, jnp.maximum(qb - 1, 0))
    near_singles_lo = jnp.where(grouped, n, jnp.maximum(qb - 1, 0))

    def pass1(blocks):
        mx = None
        for kb, add in blocks:
            s = add(kb, _nt_dot(qs, k_at(kb)))
            s_scr[kb] = s
            fold = jnp.maximum(s[:, :LANES], s[:, LANES:])
            mx = fold if mx is None else jnp.maximum(mx, fold)
        mx_scr[...] = jnp.maximum(mx_scr[...], mx)

    def loop(lo, hi, fn):
        def body(i, carry):
            fn(i)
            return carry
        lax.fori_loop(lo, hi, body, 0)

    mx_scr[...] = jnp.full(mx_scr.shape, NEG, F32)
    loop(0, far_singles, lambda kb: pass1([(kb, add_far)]))
    loop(0, groups - 1,
         lambda g: pass1([(rest + g * GROUP + u, add_far) for u in range(GROUP)]))

    @pl.when(grouped)
    def _():
        pass1([(qb - u, add_far) for u in range(GROUP - 1, 1, -1)]
              + [(qb - 1, lambda kb, s: add_near(kb, s, 0)), (qb, lambda kb, s: add_near(kb, s, 1))])

    loop(near_singles_lo, n, lambda kb: pass1([(kb, lambda kb, s: add_near(kb, s, kb - (qb - 1)))]))

    m = jnp.max(mx_scr[...], axis=-1, keepdims=True)
    mx_scr[...] = jnp.broadcast_to(m, mx_scr.shape)
    l_scr[...] = jnp.zeros_like(l_scr)
    acc_scr[...] = jnp.zeros_like(acc_scr)

    def pass2(kbs):
        mw = _wide(mx_scr[...])
        l_add = acc_add = None
        for kb in kbs:
            p = jnp.exp(s_scr[kb] - mw)
            fold = p[:, :LANES] + p[:, LANES:]
            pv = _dot(p.astype(BF16), v_at(kb))
            l_add = fold if l_add is None else l_add + fold
            acc_add = pv if acc_add is None else acc_add + pv
        l_scr[...] += l_add
        acc_scr[...] += acc_add

    loop(0, rest, lambda kb: pass2([kb]))
    loop(0, groups, lambda g: pass2([rest + g * GROUP + u for u in range(GROUP)]))
    return acc_scr[...] / jnp.sum(l_scr[...], axis=-1, keepdims=True)


def _softmax_scratch(s_len):
    return [pltpu.VMEM((s_len // KEY_BLOCK, 2 * Q_TILE, KEY_BLOCK), F32),
            pltpu.VMEM((2 * Q_TILE, LANES), F32),
            pltpu.VMEM((2 * Q_TILE, LANES), F32),
            pltpu.VMEM((2 * Q_TILE, LANES), F32)]


def _dsa_kernel(qa_ref, qi_ref, wi_ref, ka_ref, va_ref, ki_ref, bias_ref, o_ref,
                key_scr, neg_scr, w_scr, last_scr, s_scr, mx_scr, l_scr, acc_scr,
                *, topk, idx_bits, s_len):
    qb = pl.program_id(1)

    @pl.when(pl.program_id(2) == 0)
    def _():
        _dsa_select(qb, qi_ref, wi_ref, ki_ref, key_scr, neg_scr, w_scr, last_scr,
                    topk=topk, idx_bits=idx_bits, s_len=s_len)

    def add_far(kb, s):
        return s + _tall(neg_scr[kb])

    def add_near(kb, s, t):
        return s + _tall(neg_scr[kb]) + bias_ref[0, t]

    o = _softmax_pv(_split_heads(qa_ref[0]),
                    lambda kb: ka_ref[0, _key_rows(kb), :], lambda kb: va_ref[0, _key_rows(kb), :],
                    qb, add_far, add_near, s_scr, mx_scr, l_scr, acc_scr)
    o_ref[0] = _merge_heads(o, Q_TILE).astype(o_ref.dtype)


def _dsa_select(qb, qi_ref, wi_ref, ki_ref, key_scr, neg_scr, w_scr, last_scr,
                *, topk, idx_bits, s_len):
    nkb = qb + 1
    kf = float(topk)
    valid_diag = _chunk_causal()

    def cols(kb):
        return kb * KEY_BLOCK + lax.broadcasted_iota(I32, (Q_TILE, KEY_BLOCK), 1)

    qi = qi_ref[0]
    q_heads = [_split_heads(qi[:, pr * LANES:(pr + 1) * LANES]) for pr in range(IDX_HEADS // 2)]
    wi = wi_ref[0].astype(F32)
    for h in range(IDX_HEADS):
        w_scr[h] = jnp.broadcast_to(wi[:, h:h + 1], (Q_TILE, LANES))

    def score_block(kb, valid):
        kk = ki_ref[0, _key_rows(kb), :]
        sc = jnp.zeros((Q_TILE, KEY_BLOCK), F32)
        for pr in range(IDX_HEADS // 2):
            d = _nt_dot(q_heads[pr], kk)
            sc = sc + _wide(w_scr[2 * pr]) * jnp.maximum(d[:Q_TILE], 0.0)
            sc = sc + _wide(w_scr[2 * pr + 1]) * jnp.maximum(d[Q_TILE:], 0.0)
        if valid is not None:
            sc = jnp.where(valid, sc, -jnp.inf)
        bits = pltpu.bitcast(sc, I32)
        key_scr[kb] = bits ^ ((bits >> 31) & 0x7FFFFFFF)

    def score_body(kb, carry):
        score_block(kb, None)
        return carry

    lax.fori_loop(0, qb, score_body, 0)
    score_block(qb, valid_diag)

    lane_sum = jnp.ones((LANES, LANES), BF16)

    def count(pred):
        def body(kb, acc):
            hit = jnp.where(pred(key_scr[kb], kb), 1.0, 0.0)
            return acc + (hit[:, :LANES] + hit[:, LANES:])
        acc = lax.fori_loop(0, nkb, body, jnp.zeros((Q_TILE, LANES), F32))
        return _dot(acc.astype(BF16), lane_sum)

    thr = jnp.where(count(lambda key, kb: key >= 0) >= kf,
                    jnp.zeros((Q_TILE, LANES), I32), jnp.full((Q_TILE, LANES), INT_MIN, I32))

    def thr_body(j, thr):
        cand = thr | lax.shift_left(jnp.int32(1), 30 - j)
        cw = _wide(cand)
        return jnp.where(count(lambda key, kb: key >= cw) >= kf, cand, thr)

    thr = lax.fori_loop(0, 31, thr_body, thr)
    tw = _wide(thr)

    need = kf - count(lambda key, kb: key > tw)
    ties = count(lambda key, kb: key == tw)
    last_scr[...] = jnp.full(last_scr.shape, s_len, I32)

    @pl.when(jnp.max(ties - need) > 0.0)
    def _():
        def tie_body(j, last):
            cand = last | lax.shift_left(jnp.int32(1), idx_bits - 1 - j)
            cw = _wide(cand)
            taken = count(lambda key, kb: (key == tw) & (cols(kb) < cw))
            return jnp.where(taken < need, cand, last)

        last_scr[...] = lax.fori_loop(0, idx_bits, tie_body, jnp.zeros((Q_TILE, LANES), I32))

    lastw = _wide(last_scr[...])

    def mask_block(kb, valid):
        key = key_scr[kb]
        tie_neg = jnp.where(key == tw, jnp.where(cols(kb) <= lastw, 0.0, NEG), NEG)
        neg = jnp.where(key > tw, 0.0, tie_neg)
        if valid is not None:
            neg = jnp.where(valid, neg, NEG)
        neg_scr[kb] = neg

    def mask_body(kb, carry):
        mask_block(kb, None)
        return carry

    lax.fori_loop(0, qb, mask_body, 0)
    mask_block(qb, valid_diag)


def _dsa(proj, bias_tiles, bsz, s_len):
    topk = min(TOPK_MAX, s_len // 4)
    idx_bits = max(1, (s_len - 1).bit_length())
    n_blocks = s_len // KEY_BLOCK
    assert 2 * n_blocks <= 256
    kern = functools.partial(_dsa_kernel, topk=topk, idx_bits=idx_bits, s_len=s_len)
    lane_blk = lambda off: off // LANES
    return pl.pallas_call(
        kern,
        grid=(bsz, s_len // Q_TILE, A_HEADS // 2),
        in_specs=[pl.BlockSpec((1, Q_TILE, LANES), lambda b, i, p: (b, i, lane_blk(OFF_QA) + p)),
                  pl.BlockSpec((1, Q_TILE, 256), lambda b, i, p: (b, i, OFF_QI // 256)),
                  pl.BlockSpec((1, Q_TILE, LANES), lambda b, i, p: (b, i, lane_blk(OFF_WI))),
                  pl.BlockSpec((1, s_len, LANES), lambda b, i, p: (b, 0, lane_blk(OFF_KA) + p)),
                  pl.BlockSpec((1, s_len, LANES), lambda b, i, p: (b, 0, lane_blk(OFF_VA) + p)),
                  pl.BlockSpec((1, s_len, LANES), lambda b, i, p: (b, 0, lane_blk(OFF_KI))),
                  pl.BlockSpec((1, 2, 2 * Q_TILE, KEY_BLOCK), lambda b, i, p: (p, 0, 0, 0))],
        out_specs=pl.BlockSpec((1, Q_TILE, LANES), lambda b, i, p: (b, i, p)),
        out_shape=jax.ShapeDtypeStruct((bsz, s_len, W_A), BF16),
        scratch_shapes=[pltpu.VMEM((n_blocks, Q_TILE, KEY_BLOCK), I32),
                        pltpu.VMEM((n_blocks, Q_TILE, KEY_BLOCK), F32),
                        pltpu.VMEM((IDX_HEADS, Q_TILE, LANES), F32),
                        pltpu.VMEM((Q_TILE, LANES), I32)] + _softmax_scratch(s_len),
        compiler_params=_cparams("parallel", "arbitrary", "arbitrary"),
        name="dsa",
    )(proj, proj, proj, proj, proj, proj, bias_tiles)


def _diff_kernel(q_ref, k_ref, v_ref, bias_ref, lp_ref, g_ref, o_ref,
                 s_scr, mx_scr, l_scr, acc_scr, *, lam_init):
    qb = pl.program_id(2)
    qs = _split_heads(q_ref[0])

    def add_near(kb, s, t):
        return s + _tall(bias_ref[0, t])

    o = _softmax_pv(qs, lambda kb: k_ref[0, _key_rows(kb), :], lambda kb: v_ref[0, _key_rows(kb), :],
                    qb, lambda kb, s: s, add_near, s_scr, mx_scr, l_scr, acc_scr)
    lp = lp_ref[0]
    lam = (jnp.exp(jnp.sum(lp[0:1] * lp[1:2], axis=-1, keepdims=True))
           - jnp.exp(jnp.sum(lp[2:3] * lp[3:4], axis=-1, keepdims=True)) + lam_init)
    o = o[:Q_TILE] - lam * o[Q_TILE:]
    y = o * lax.rsqrt(jnp.mean(o * o, axis=-1, keepdims=True) + SUBLN_EPS)
    o_ref[0] = (y * g_ref[0] * (1.0 - lam_init)).astype(o_ref.dtype)


def _diff(proj, bias_tiles, lam_params, subln_g, layer, bsz, s_len, lam_init):
    kern = functools.partial(_diff_kernel, lam_init=lam_init)
    return pl.pallas_call(
        kern,
        grid=(bsz, B_HEADS, s_len // Q_TILE),
        in_specs=[pl.BlockSpec((1, Q_TILE, LANES), lambda b, h, i: (b, i, OFF_QB // LANES + h)),
                  pl.BlockSpec((1, s_len, LANES), lambda b, h, i: (b, 0, OFF_KB // LANES + h)),
                  pl.BlockSpec((1, s_len, LANES), lambda b, h, i: (b, 0, OFF_VB // LANES + h)),
                  pl.BlockSpec((1, 2, Q_TILE, KEY_BLOCK), lambda b, h, i: (h, 0, 0, 0)),
                  pl.BlockSpec((1, 4, HEAD_DIM), lambda b, h, i: (layer, 0, 0)),
                  pl.BlockSpec((1, 1, 2 * HEAD_DIM), lambda b, h, i: (layer, 0, 0))],
        out_specs=pl.BlockSpec((1, Q_TILE, LANES), lambda b, h, i: (b, i, h)),
        out_shape=jax.ShapeDtypeStruct((bsz, s_len, B_HEADS * 2 * HEAD_DIM), BF16),
        scratch_shapes=_softmax_scratch(s_len),
        compiler_params=_cparams("parallel", "parallel", "arbitrary"),
        name="diff_attn",
    )(proj, proj, proj, bias_tiles, lam_params, subln_g.reshape(-1, 1, 2 * HEAD_DIM))


def _stick_kernel(q_ref, k_ref, v_ref, o_ref, later_scr, run_scr, acc_scr):
    qb = pl.program_id(2)
    qs = _split_heads(q_ref[0])
    jj = lax.broadcasted_iota(I32, (KEY_BLOCK, KEY_BLOCK), 0)
    ss = lax.broadcasted_iota(I32, (KEY_BLOCK, KEY_BLOCK), 1)
    later_scr[...] = jnp.where(jj > ss, 1.0, 0.0).astype(BF16)
    run_scr[...] = jnp.zeros_like(run_scr)
    acc_scr[...] = jnp.zeros_like(acc_scr)

    def block(kb, diagonal):
        kk = k_ref[0, _key_rows(kb), :]
        vv = v_ref[0, _key_rows(kb), :]
        z = _nt_dot(qs, kk)
        soft = jnp.log(1.0 + jnp.exp(-jnp.abs(z)))
        log_beta = jnp.minimum(z, 0.0) - soft
        log_1mb = log_beta - z
        if diagonal:
            t = lax.broadcasted_iota(I32, z.shape, 0) & (Q_TILE - 1)
            strict = lax.broadcasted_iota(I32, z.shape, 1) < t
            log_1mb = jnp.where(strict, log_1mb, 0.0)
        hi = log_1mb.astype(BF16)
        lo = (log_1mb - hi.astype(F32)).astype(BF16)
        between = _dot(hi, later_scr[...]) + _dot(lo, later_scr[...])
        w = jnp.exp(log_beta + between + _wide(run_scr[...]))
        if diagonal:
            w = jnp.where(strict, w, 0.0)
        acc_scr[...] += _dot(w.astype(BF16), vv)
        total = between[:, 0:1] + log_1mb[:, 0:1]
        run_scr[...] += jnp.broadcast_to(total, run_scr.shape)

    block(qb, True)

    def body(j, carry):
        block(qb - 1 - j, False)
        return carry

    lax.fori_loop(0, qb, body, 0)
    o_ref[0] = _merge_heads(acc_scr[...], Q_TILE).astype(o_ref.dtype)


def _stick(proj, bsz, s_len):
    return pl.pallas_call(
        _stick_kernel,
        grid=(bsz, C_HEADS // 2, s_len // Q_TILE),
        in_specs=[pl.BlockSpec((1, Q_TILE, LANES), lambda b, h, i: (b, i, OFF_QC // LANES + h)),
                  pl.BlockSpec((1, s_len, LANES), lambda b, h, i: (b, 0, OFF_KC // LANES + h)),
                  pl.BlockSpec((1, s_len, LANES), lambda b, h, i: (b, 0, OFF_VC // LANES + h))],
        out_specs=pl.BlockSpec((1, Q_TILE, LANES), lambda b, h, i: (b, i, h)),
        out_shape=jax.ShapeDtypeStruct((bsz, s_len, C_HEADS * HEAD_DIM), BF16),
        scratch_shapes=[pltpu.VMEM((KEY_BLOCK, KEY_BLOCK), BF16),
                        pltpu.VMEM((2 * Q_TILE, LANES), F32),
                        pltpu.VMEM((2 * Q_TILE, LANES), F32)],
        compiler_params=_cparams("parallel", "parallel", "arbitrary"),
        name="stick_attn",
    )(proj, proj, proj)


def _merge_kernel(oa_ref, ob_ref, oc_ref, gate_ref, x_ref, g1_ref, wbr_ref, wout_ref, o_ref):
    d = x_ref.shape[-1]
    merged = jnp.zeros(x_ref.shape, F32)
    for j, o_br in enumerate((oa_ref, ob_ref, oc_ref)):
        w = o_br.shape[-1]
        gate = jax.nn.sigmoid(gate_ref[:, j * d:(j + 1) * d].astype(F32))
        merged = merged + gate * _dot(o_br[...], wbr_ref[j * w:(j + 1) * w, :])
    o_ref[...] = x_ref[...] + g1_ref[0] * _dot(merged.astype(BF16), wout_ref[...])


def _merge(o_a, o_b, o_c, proj, x2, mod, w_br, w_out, s_len):
    t, d = x2.shape
    tm = 512
    per_b = s_len // tm
    w = o_a.shape[-1]
    return pl.pallas_call(
        _merge_kernel,
        grid=(t // tm,),
        in_specs=[pl.BlockSpec((tm, w), lambda i: (i, 0)),
                  pl.BlockSpec((tm, w), lambda i: (i, 0)),
                  pl.BlockSpec((tm, w), lambda i: (i, 0)),
                  pl.BlockSpec((tm, 3 * d), lambda i: (i, OFF_G // (3 * d))),
                  pl.BlockSpec((tm, d), lambda i: (i, 0)),
                  pl.BlockSpec((1, 1, d), lambda i: (i // per_b, 0, 2)),
                  pl.BlockSpec(w_br.shape, lambda i: (0, 0)),
                  pl.BlockSpec(w_out.shape, lambda i: (0, 0))],
        out_specs=pl.BlockSpec((tm, d), lambda i: (i, 0)),
        out_shape=jax.ShapeDtypeStruct((t, d), F32),
        compiler_params=_cparams("parallel"),
        name="merge",
    )(o_a, o_b, o_c, proj, x2, mod, w_br, w_out)


def _finish(x, gate, acc, fg_ref, final):
    y = x + gate * acc
    if final:
        y = y * lax.rsqrt(jnp.mean(y * y, axis=-1, keepdims=True) + EPS) * fg_ref[...]
    return y


def _swiglu_partial(h, w1, w3, w2):
    a = _dot(h, w1)
    act = (a * jax.nn.sigmoid(a)) * _dot(h, w3)
    return _dot(act.astype(BF16), w2)


def _ffn_kernel(x_ref, g_ref, sc_ref, sh_ref, gate_ref, w1_ref, w3_ref, w2_ref, fg_ref, o_ref,
                h_scr, acc_scr, *, final):
    f = pl.program_id(1)

    @pl.when(f == 0)
    def _():
        h_scr[...] = _modulated_norm(x_ref[...], g_ref[...], sc_ref[0], sh_ref[0]).astype(BF16)
        acc_scr[...] = jnp.zeros_like(acc_scr)

    acc_scr[...] += _swiglu_partial(h_scr[...], w1_ref[...], w3_ref[...], w2_ref[...])

    @pl.when(f == pl.num_programs(1) - 1)
    def _():
        o_ref[...] = _finish(x_ref[...], gate_ref[0], acc_scr[...], fg_ref, final)


def _ffn(x2, g, mod, w1, w3, w2, final_g, s_len, final):
    t, d = x2.shape
    d_ff = w1.shape[1]
    tm, tf = 1024, 256
    per_b = s_len // tm
    kern = functools.partial(_ffn_kernel, final=final)
    return pl.pallas_call(
        kern,
        grid=(t // tm, d_ff // tf),
        in_specs=[pl.BlockSpec((tm, d), lambda i, f: (i, 0)),
                  pl.BlockSpec((1, d), lambda i, f: (0, 0)),
                  pl.BlockSpec((1, 1, d), lambda i, f: (i // per_b, 0, 4)),
                  pl.BlockSpec((1, 1, d), lambda i, f: (i // per_b, 0, 3)),
                  pl.BlockSpec((1, 1, d), lambda i, f: (i // per_b, 0, 5)),
                  pl.BlockSpec((d, tf), lambda i, f: (0, f)),
                  pl.BlockSpec((d, tf), lambda i, f: (0, f)),
                  pl.BlockSpec((tf, d), lambda i, f: (f, 0)),
                  pl.BlockSpec((1, d), lambda i, f: (0, 0))],
        out_specs=pl.BlockSpec((tm, d), lambda i, f: (i, 0)),
        out_shape=jax.ShapeDtypeStruct((t, d), F32),
        scratch_shapes=[pltpu.VMEM((tm, d), BF16), pltpu.VMEM((tm, d), F32)],
        compiler_params=_cparams("parallel", "arbitrary"),
        name="ffn",
    )(x2, g, mod, mod, mod, w1, w3, w2, final_g)


def _route(h, wr, br):
    h_hi = h.astype(BF16)
    h_lo = (h - h_hi.astype(F32)).astype(BF16)
    w_hi = wr.astype(BF16)
    w_lo = (wr - w_hi.astype(F32)).astype(BF16)
    logits = _dot(h_hi, w_hi) + _dot(h_hi, w_lo) + _dot(h_lo, w_hi) + br
    lane = lax.broadcasted_iota(I32, logits.shape, 1).astype(F32)
    lg = jnp.where(lane < N_EXPERTS, logits, -jnp.inf)
    m1 = jnp.max(lg, axis=-1, keepdims=True)
    i1 = jnp.min(jnp.where(lg == m1, lane, float(LANES)), axis=-1, keepdims=True)
    lg2 = jnp.where(lane == i1, -jnp.inf, lg)
    m2 = jnp.max(lg2, axis=-1, keepdims=True)
    i2 = jnp.min(jnp.where(lg2 == m2, lane, float(LANES)), axis=-1, keepdims=True)
    e2 = jnp.exp(m2 - m1)
    w_top = 1.0 / (1.0 + e2)
    return jnp.where(lane == i1, w_top, 0.0) + jnp.where(lane == i2, e2 * w_top, 0.0)


def _moe_kernel(x_ref, g_ref, sc_ref, sh_ref, gate_ref, wr_ref, br_ref, w1_ref, w3_ref, w2_ref,
                fg_ref, o_ref, h_scr, comb_scr, acc_scr, *, final):
    e = pl.program_id(1)
    f = pl.program_id(2)

    @pl.when((e == 0) & (f == 0))
    def _():
        h = _modulated_norm(x_ref[...], g_ref[...], sc_ref[0], sh_ref[0])
        h_scr[...] = h.astype(BF16)
        comb_scr[...] = _route(h, wr_ref[...], br_ref[...])
        acc_scr[...] = jnp.zeros_like(acc_scr)

    comb = comb_scr[...]
    lane = lax.broadcasted_iota(I32, comb.shape, 1)
    c_e = jnp.sum(jnp.where(lane == e, comb, 0.0), axis=-1, keepdims=True)
    acc_scr[...] += c_e * _swiglu_partial(h_scr[...], w1_ref[0], w3_ref[0], w2_ref[0])

    @pl.when((e == pl.num_programs(1) - 1) & (f == pl.num_programs(2) - 1))
    def _():
        o_ref[...] = _finish(x_ref[...], gate_ref[0], acc_scr[...], fg_ref, final)


def _moe(x2, g, mod, wr_pad, br_pad, w1, w3, w2, final_g, s_len, final):
    t, d = x2.shape
    n_e, _, d_ff = w1.shape
    tm, tf = 1024, 896
    per_b = s_len // tm
    kern = functools.partial(_moe_kernel, final=final)
    return pl.pallas_call(
        kern,
        grid=(t // tm, n_e, d_ff // tf),
        in_specs=[pl.BlockSpec((tm, d), lambda i, e, f: (i, 0)),
                  pl.BlockSpec((1, d), lambda i, e, f: (0, 0)),
                  pl.BlockSpec((1, 1, d), lambda i, e, f: (i // per_b, 0, 4)),
                  pl.BlockSpec((1, 1, d), lambda i, e, f: (i // per_b, 0, 3)),
                  pl.BlockSpec((1, 1, d), lambda i, e, f: (i // per_b, 0, 5)),
                  pl.BlockSpec((d, LANES), lambda i, e, f: (0, 0)),
                  pl.BlockSpec((1, LANES), lambda i, e, f: (0, 0)),
                  pl.BlockSpec((1, d, tf), lambda i, e, f: (e, 0, f)),
                  pl.BlockSpec((1, d, tf), lambda i, e, f: (e, 0, f)),
                  pl.BlockSpec((1, tf, d), lambda i, e, f: (e, f, 0)),
                  pl.BlockSpec((1, d), lambda i, e, f: (0, 0))],
        out_specs=pl.BlockSpec((tm, d), lambda i, e, f: (i, 0)),
        out_shape=jax.ShapeDtypeStruct((t, d), F32),
        scratch_shapes=[pltpu.VMEM((tm, d), BF16), pltpu.VMEM((tm, LANES), F32),
                        pltpu.VMEM((tm, d), F32)],
        compiler_params=_cparams("parallel", "arbitrary", "arbitrary"),
        name="moe",
    )(x2, g, mod, mod, mod, wr_pad, br_pad, w1, w3, w2, final_g)


def _pack_w_in(w):
    d = w.shape[0]
    sizes = (W_A, W_A, W_A, IDX_HEADS * IDX_DIM, IDX_DIM, IDX_HEADS,
             512, 512, 512, 512, 512, 512, 3 * d)
    offs = [0]
    for s in sizes:
        offs.append(offs[-1] + s)
    (qa, ka, va, qi, ki, wi, qb, kb, vb, qc, kc, vc, gl) = [
        w[:, offs[j]:offs[j + 1]] for j in range(len(sizes))]
    scale = HEAD_DIM ** -0.5
    pad_wi = jnp.zeros((d, LANES - IDX_HEADS), w.dtype)
    packed = jnp.concatenate(
        [gl, qa * scale, ka, va,
         qi * (IDX_DIM ** -0.5), ki, ki, wi * (IDX_HEADS ** -0.5), pad_wi,
         qb * scale, kb, vb, qc * scale, kc, vc], axis=1)
    assert packed.shape[1] == PACKED
    return packed.astype(BF16)


def kernel(x, c, w_ada, b_ada, norm1_g, norm2_g, w_in, w_br, w_out, rel_bias, lam_params,
           subln_g, ffn_w1, ffn_w3, ffn_w2, router_w, router_b, moe_w1, moe_w3, moe_w2, final_g):
    bsz, s_len, d = x.shape
    depth = w_ada.shape[0]
    assert s_len % 1024 == 0 and d == 1024 and OFF_G + 3 * d == OFF_QA

    c_pad = jnp.concatenate([c, jnp.zeros((8 - bsz % 8 if bsz % 8 else 0, d), c.dtype)], axis=0)
    mod_all = _ada(c_pad, w_ada, b_ada)
    dsa_tiles, diff_tiles = _bias_tiles(rel_bias)
    fg = final_g.reshape(1, d)

    x2 = x.reshape(bsz * s_len, d)
    for l in range(depth):
        mod = mod_all[l, :bsz].reshape(bsz, 1, 6 * d)
        proj = _inproj(x2, norm1_g[l].reshape(1, d), mod, _pack_w_in(w_in[l]), s_len)
        proj3 = proj.reshape(bsz, s_len, PACKED)
        lam_init = 0.8 - 0.6 * math.exp(-0.3 * l)
        o_a = _dsa(proj3, dsa_tiles, bsz, s_len)
        o_b = _diff(proj3, diff_tiles, lam_params, subln_g, l, bsz, s_len, lam_init)
        o_c = _stick(proj3, bsz, s_len)
        x2 = _merge(o_a.reshape(-1, o_a.shape[-1]), o_b.reshape(-1, o_b.shape[-1]),
                    o_c.reshape(-1, o_c.shape[-1]), proj, x2, mod,
                    w_br[l].astype(BF16), w_out[l].astype(BF16), s_len)
        g2 = norm2_g[l].reshape(1, d)
        final = l == depth - 1
        j = l // 2
        if l % 2 == 0:
            x2 = _ffn(x2, g2, mod, ffn_w1[j].astype(BF16), ffn_w3[j].astype(BF16),
                      ffn_w2[j].astype(BF16), fg, s_len, final)
        else:
            wr_pad = jnp.pad(router_w[j], ((0, 0), (0, LANES - N_EXPERTS)))
            br_pad = jnp.pad(router_b[j], (0, LANES - N_EXPERTS)).reshape(1, LANES)
            x2 = _moe(x2, g2, mod, wr_pad, br_pad, moe_w1[j].astype(BF16), moe_w3[j].astype(BF16),
                      moe_w2[j].astype(BF16), fg, s_len, final)
    return x2.reshape(bsz, s_len, d)
```

```python
import functools
import math

import jax
import jax.numpy as jnp
from jax import lax
from jax.experimental import pallas as pl
from jax.experimental.pallas import tpu as pltpu

F32 = jnp.float32
BF16 = jnp.bfloat16
I32 = jnp.int32

LANES = 128
VMEM_LIMIT_BYTES = 56 * 1024 * 1024

CHUNK = 64
A_HEADS = 8
IDX_HEADS = 4
IDX_DIM = 64
TOPK_MAX = 256
B_HEADS = 4
C_HEADS = 8
HEAD_DIM = 64
REL_BUCKETS = 32
FAR_BUCKET = REL_BUCKETS // 2 - 1
N_EXPERTS = 8
EPS = 1e-6
SUBLN_EPS = 1e-5
NEG = -1e30
INT_MIN = -(2 ** 31)

KEY_BLOCK = 256
Q_TILE = 256
GROUP = 4

W_A = A_HEADS * HEAD_DIM
OFF_G = 0
OFF_QA, OFF_KA, OFF_VA = 3072, 3584, 4096
OFF_QI, OFF_KI, OFF_WI = 4608, 4864, 4992
OFF_QB, OFF_KB, OFF_VB = 5120, 5632, 6144
OFF_QC, OFF_KC, OFF_VC = 6656, 7168, 7680
PACKED = 8192

LOG_BUCKET_STEPS = (12, 16, 23, 32, 46, 64, 91)


def _nt_dot(a, b):
    return lax.dot_general(a, b, (((1,), (1,)), ((), ())), preferred_element_type=F32)


def _dot(a, b):
    return jnp.dot(a, b, preferred_element_type=F32)


def _cparams(*sem):
    return pltpu.CompilerParams(dimension_semantics=sem, vmem_limit_bytes=VMEM_LIMIT_BYTES)


def _split_heads(x):
    lane = lax.broadcasted_iota(I32, x.shape, 1)
    keep_a = jnp.where(lane < HEAD_DIM, 1.0, 0.0).astype(x.dtype)
    keep_b = jnp.where(lane < HEAD_DIM, 0.0, 1.0).astype(x.dtype)
    return jnp.concatenate([x * keep_a, x * keep_b], axis=0)


def _merge_heads(o, m):
    lane = lax.broadcasted_iota(I32, (m, LANES), 1)
    return jnp.where(lane < HEAD_DIM, o[:m], o[m:])


def _wide(x):
    return jnp.concatenate([x, x], axis=1)


def _tall(x):
    return jnp.concatenate([x, x], axis=0)


def _key_rows(kb):
    return pl.ds(pl.multiple_of(kb * KEY_BLOCK, KEY_BLOCK), KEY_BLOCK)


def _chunk_causal():
    r = lax.broadcasted_iota(I32, (Q_TILE, KEY_BLOCK), 0)
    c = lax.broadcasted_iota(I32, (Q_TILE, KEY_BLOCK), 1)
    return (c // CHUNK) <= (r // CHUNK)


def _ada_kernel(c_ref, w_ref, b_ref, o_ref):
    c = c_ref[...]
    a = c * jax.nn.sigmoid(c)
    o_ref[0] = jnp.dot(a, w_ref[0], preferred_element_type=F32,
                       precision=lax.Precision.HIGHEST) + b_ref[0]


def _ada(c_pad, w_ada, b_ada):
    depth, d, n = w_ada.shape
    tn = 1024
    return pl.pallas_call(
        _ada_kernel,
        grid=(depth, n // tn),
        in_specs=[pl.BlockSpec(c_pad.shape, lambda l, j: (0, 0)),
                  pl.BlockSpec((1, d, tn), lambda l, j: (l, 0, j)),
                  pl.BlockSpec((1, 1, tn), lambda l, j: (l, 0, j))],
        out_specs=pl.BlockSpec((1, c_pad.shape[0], tn), lambda l, j: (l, 0, j)),
        out_shape=jax.ShapeDtypeStruct((depth, c_pad.shape[0], n), F32),
        compiler_params=_cparams("parallel", "parallel"),
        name="ada",
    )(c_pad, w_ada, b_ada.reshape(depth, 1, n))


def _rel_bias_tile(tab_ref, head, d0, n_heads_total):
    r = lax.broadcasted_iota(I32, (Q_TILE, KEY_BLOCK), 0)
    c = lax.broadcasted_iota(I32, (Q_TILE, KEY_BLOCK), 1)
    d = c - r + d0
    n = jnp.abs(d)
    large = jnp.full(d.shape, REL_BUCKETS // 4, I32)
    for step in LOG_BUCKET_STEPS:
        large = large + jnp.where(n >= step, 1, 0)
    bucket = jnp.where(d > 0, REL_BUCKETS // 2, 0) + jnp.where(n < REL_BUCKETS // 4, n, large)
    out = jnp.zeros(d.shape, F32)
    for b in range(REL_BUCKETS):
        out = jnp.where(bucket == b, tab_ref[b * n_heads_total + head], out)
    return out - tab_ref[FAR_BUCKET * n_heads_total + head]


def _bias_kernel(tab_ref, dsa_ref, diff_ref):
    p = pl.program_id(0)
    n_heads = A_HEADS + B_HEADS
    for t, d0 in enumerate((-KEY_BLOCK, 0)):
        dsa_ref[0, t, 0:Q_TILE, :] = _rel_bias_tile(tab_ref, 2 * p, d0, n_heads)
        dsa_ref[0, t, Q_TILE:2 * Q_TILE, :] = _rel_bias_tile(tab_ref, 2 * p + 1, d0, n_heads)
        tile = _rel_bias_tile(tab_ref, A_HEADS + p, d0, n_heads)
        if t == 1:
            tile = jnp.where(_chunk_causal(), tile, NEG)
        diff_ref[0, t] = tile


def _bias_tiles(rel_bias):
    tab = rel_bias.reshape(-1)
    return pl.pallas_call(
        _bias_kernel,
        grid=(4,),
        in_specs=[pl.BlockSpec(memory_space=pltpu.SMEM)],
        out_specs=[pl.BlockSpec((1, 2, 2 * Q_TILE, KEY_BLOCK), lambda p: (p, 0, 0, 0)),
                   pl.BlockSpec((1, 2, Q_TILE, KEY_BLOCK), lambda p: (p, 0, 0, 0))],
        out_shape=[jax.ShapeDtypeStruct((A_HEADS // 2, 2, 2 * Q_TILE, KEY_BLOCK), F32),
                   jax.ShapeDtypeStruct((B_HEADS, 2, Q_TILE, KEY_BLOCK), F32)],
        compiler_params=_cparams("parallel"),
        name="rel_bias_tiles",
    )(tab)


def _modulated_norm(x, g, sc, sh):
    y = x * lax.rsqrt(jnp.mean(x * x, axis=-1, keepdims=True) + EPS)
    return y * g * (1.0 + sc) + sh


def _inproj_kernel(x_ref, g_ref, sc_ref, sh_ref, w_ref, o_ref, h_scr):
    @pl.when(pl.program_id(1) == 0)
    def _():
        h_scr[...] = _modulated_norm(x_ref[...], g_ref[...], sc_ref[0], sh_ref[0]).astype(BF16)

    o_ref[...] = _dot(h_scr[...], w_ref[...]).astype(o_ref.dtype)


def _inproj(x2, g, mod, w_packed, s_len):
    t, d = x2.shape
    n = w_packed.shape[1]
    tm, tn = 1024, 1024
    per_b = s_len // tm
    return pl.pallas_call(
        _inproj_kernel,
        grid=(t // tm, n // tn),
        in_specs=[pl.BlockSpec((tm, d), lambda i, j: (i, 0)),
                  pl.BlockSpec((1, d), lambda i, j: (0, 0)),
                  pl.BlockSpec((1, 1, d), lambda i, j: (i // per_b, 0, 1)),
                  pl.BlockSpec((1, 1, d), lambda i, j: (i // per_b, 0, 0)),
                  pl.BlockSpec((d, tn), lambda i, j: (0, j))],
        out_specs=pl.BlockSpec((tm, tn), lambda i, j: (i, j)),
        out_shape=jax.ShapeDtypeStruct((t, n), BF16),
        scratch_shapes=[pltpu.VMEM((tm, d), BF16)],
        compiler_params=_cparams("parallel", "arbitrary"),
        name="inproj",
    )(x2, g, mod, mod, w_packed)


def _softmax_pv(qs, k_at, v_at, qb, add_far, add_near, s_scr, mx_scr, l_scr, acc_scr):
    n = qb + 1
    groups = n // GROUP
    rest = n - groups * GROUP
    grouped = groups > 0
    far_singles = jnp.where(grouped, rest, jnp.maximum(qb - 1, 0))
    near_singles_lo = jnp.where(grouped, n, jnp.maximum(qb - 1, 0))

    def pass1(blocks):
        mx = None
        for kb, add in blocks:
            s = add(kb, _nt_dot(qs, k_at(kb)))
            s_scr[kb] = s
            fold = jnp.maximum(s[:, :LANES], s[:, LANES:])
            mx = fold if mx is None else jnp.maximum(mx, fold)
        mx_scr[...] = jnp.maximum(mx_scr[...], mx)

    def loop(lo, hi, fn):
        def body(i, carry):
            fn(i)
            return carry
        lax.fori_loop(lo, hi, body, 0)

    mx_scr[...] = jnp.full(mx_scr.shape, NEG, F32)
    loop(0, far_singles, lambda kb: pass1([(kb, add_far)]))
    loop(0, groups - 1,
         lambda g: pass1([(rest + g * GROUP + u, add_far) for u in range(GROUP)]))

    @pl.when(grouped)
    def _():
        pass1([(qb - u, add_far) for u in range(GROUP - 1, 1, -1)]
              + [(qb - 1, lambda kb, s: add_near(kb, s, 0)), (qb, lambda kb, s: add_near(kb, s, 1))])

    loop(near_singles_lo, n, lambda kb: pass1([(kb, lambda kb, s: add_near(kb, s, kb - (qb - 1)))]))

    m = jnp.max(mx_scr[...], axis=-1, keepdims=True)
    mx_scr[...] = jnp.broadcast_to(m, mx_scr.shape)
    l_scr[...] = jnp.zeros_like(l_scr)
    acc_scr[...] = jnp.zeros_like(acc_scr)

    def pass2(kbs):
        mw = _wide(mx_scr[...])
        l_add = acc_add = None
        for kb in kbs:
            p = jnp.exp(s_scr[kb] - mw)
            fold = p[:, :LANES] + p[:, LANES:]
            pv = _dot(p.astype(BF16), v_at(kb))
            l_add = fold if l_add is None else l_add + fold
            acc_add = pv if acc_add is None else acc_add + pv
        l_scr[...] += l_add
        acc_scr[...] += acc_add

    loop(0, rest, lambda kb: pass2([kb]))
    loop(0, groups, lambda g: pass2([rest + g * GROUP + u for u in range(GROUP)]))
    return acc_scr[...] / jnp.sum(l_scr[...], axis=-1, keepdims=True)


def _softmax_scratch(s_len):
    return [pltpu.VMEM((s_len // KEY_BLOCK, 2 * Q_TILE, KEY_BLOCK), F32),
            pltpu.VMEM((2 * Q_TILE, LANES), F32),
            pltpu.VMEM((2 * Q_TILE, LANES), F32),
            pltpu.VMEM((2 * Q_TILE, LANES), F32)]


def _dsa_kernel(qa_ref, qi_ref, wi_ref, ka_ref, va_ref, ki_ref, bias_ref, o_ref,
                key_scr, neg_scr, w_scr, last_scr, s_scr, mx_scr, l_scr, acc_scr,
                *, topk, idx_bits, s_len):
    qb = pl.program_id(1)

    @pl.when(pl.program_id(2) == 0)
    def _():
        _dsa_select(qb, qi_ref, wi_ref, ki_ref, key_scr, neg_scr, w_scr, last_scr,
                    topk=topk, idx_bits=idx_bits, s_len=s_len)

    def add_far(kb, s):
        return s + _tall(neg_scr[kb])

    def add_near(kb, s, t):
        return s + _tall(neg_scr[kb]) + bias_ref[0, t]

    o = _softmax_pv(_split_heads(qa_ref[0]),
                    lambda kb: ka_ref[0, _key_rows(kb), :], lambda kb: va_ref[0, _key_rows(kb), :],
                    qb, add_far, add_near, s_scr, mx_scr, l_scr, acc_scr)
    o_ref[0] = _merge_heads(o, Q_TILE).astype(o_ref.dtype)


def _dsa_select(qb, qi_ref, wi_ref, ki_ref, key_scr, neg_scr, w_scr, last_scr,
                *, topk, idx_bits, s_len):
    nkb = qb + 1
    kf = float(topk)
    valid_diag = _chunk_causal()

    def cols(kb):
        return kb * KEY_BLOCK + lax.broadcasted_iota(I32, (Q_TILE, KEY_BLOCK), 1)

    qi = qi_ref[0]
    q_heads = [_split_heads(qi[:, pr * LANES:(pr + 1) * LANES]) for pr in range(IDX_HEADS // 2)]
    wi = wi_ref[0].astype(F32)
    for h in range(IDX_HEADS):
        w_scr[h] = jnp.broadcast_to(wi[:, h:h + 1], (Q_TILE, LANES))

    def score_block(kb, valid):
        kk = ki_ref[0, _key_rows(kb), :]
        sc = jnp.zeros((Q_TILE, KEY_BLOCK), F32)
        for pr in range(IDX_HEADS // 2):
            d = _nt_dot(q_heads[pr], kk)
            sc = sc + _wide(w_scr[2 * pr]) * jnp.maximum(d[:Q_TILE], 0.0)
            sc = sc + _wide(w_scr[2 * pr + 1]) * jnp.maximum(d[Q_TILE:], 0.0)
        if valid is not None:
            sc = jnp.where(valid, sc, -jnp.inf)
        bits = pltpu.bitcast(sc, I32)
        key_scr[kb] = bits ^ ((bits >> 31) & 0x7FFFFFFF)

    def score_body(kb, carry):
        score_block(kb, None)
        return carry

    lax.fori_loop(0, qb, score_body, 0)
    score_block(qb, valid_diag)

    lane_sum = jnp.ones((LANES, LANES), BF16)

    def count(pred):
        def body(kb, acc):
            hit = jnp.where(pred(key_scr[kb], kb), 1.0, 0.0)
            return acc + (hit[:, :LANES] + hit[:, LANES:])
        acc = lax.fori_loop(0, nkb, body, jnp.zeros((Q_TILE, LANES), F32))
        return _dot(acc.astype(BF16), lane_sum)

    thr = jnp.where(count(lambda key, kb: key >= 0) >= kf,
                    jnp.zeros((Q_TILE, LANES), I32), jnp.full((Q_TILE, LANES), INT_MIN, I32))

    def thr_body(j, thr):
        cand = thr | lax.shift_left(jnp.int32(1), 30 - j)
        cw = _wide(cand)
        return jnp.where(count(lambda key, kb: key >= cw) >= kf, cand, thr)

    thr = lax.fori_loop(0, 31, thr_body, thr)
    tw = _wide(thr)

    need = kf - count(lambda key, kb: key > tw)
    ties = count(lambda key, kb: key == tw)
    last_scr[...] = jnp.full(last_scr.shape, s_len, I32)

    @pl.when(jnp.max(ties - need) > 0.0)
    def _():
        def tie_body(j, last):
            cand = last | lax.shift_left(jnp.int32(1), idx_bits - 1 - j)
            cw = _wide(cand)
            taken = count(lambda key, kb: (key == tw) & (cols(kb) < cw))
            return jnp.where(taken < need, cand, last)

        last_scr[...] = lax.fori_loop(0, idx_bits, tie_body, jnp.zeros((Q_TILE, LANES), I32))

    lastw = _wide(last_scr[...])

    def mask_block(kb, valid):
        key = key_scr[kb]
        tie_neg = jnp.where(key == tw, jnp.where(cols(kb) <= lastw, 0.0, NEG), NEG)
        neg = jnp.where(key > tw, 0.0, tie_neg)
        if valid is not None:
            neg = jnp.where(valid, neg, NEG)
        neg_scr[kb] = neg

    def mask_body(kb, carry):
        mask_block(kb, None)
        return carry

    lax.fori_loop(0, qb, mask_body, 0)
    mask_block(qb, valid_diag)


def _dsa(proj, bias_tiles, bsz, s_len):
    topk = min(TOPK_MAX, s_len // 4)
    idx_bits = max(1, (s_len - 1).bit_length())
    n_blocks = s_len // KEY_BLOCK
    assert 2 * n_blocks <= 256
    kern = functools.partial(_dsa_kernel, topk=topk, idx_bits=idx_bits, s_len=s_len)
    lane_blk = lambda off: off // LANES
    return pl.pallas_call(
        kern,
        grid=(bsz, s_len // Q_TILE, A_HEADS // 2),
        in_specs=[pl.BlockSpec((1, Q_TILE, LANES), lambda b, i, p: (b, i, lane_blk(OFF_QA) + p)),
                  pl.BlockSpec((1, Q_TILE, 256), lambda b, i, p: (b, i, OFF_QI // 256)),
                  pl.BlockSpec((1, Q_TILE, LANES), lambda b, i, p: (b, i, lane_blk(OFF_WI))),
                  pl.BlockSpec((1, s_len, LANES), lambda b, i, p: (b, 0, lane_blk(OFF_KA) + p)),
                  pl.BlockSpec((1, s_len, LANES), lambda b, i, p: (b, 0, lane_blk(OFF_VA) + p)),
                  pl.BlockSpec((1, s_len, LANES), lambda b, i, p: (b, 0, lane_blk(OFF_KI))),
                  pl.BlockSpec((1, 2, 2 * Q_TILE, KEY_BLOCK), lambda b, i, p: (p, 0, 0, 0))],
        out_specs=pl.BlockSpec((1, Q_TILE, LANES), lambda b, i, p: (b, i, p)),
        out_shape=jax.ShapeDtypeStruct((bsz, s_len, W_A), BF16),
        scratch_shapes=[pltpu.VMEM((n_blocks, Q_TILE, KEY_BLOCK), I32),
                        pltpu.VMEM((n_blocks, Q_TILE, KEY_BLOCK), F32),
                        pltpu.VMEM((IDX_HEADS, Q_TILE, LANES), F32),
                        pltpu.VMEM((Q_TILE, LANES), I32)] + _softmax_scratch(s_len),
        compiler_params=_cparams("parallel", "arbitrary", "arbitrary"),
        name="dsa",
    )(proj, proj, proj, proj, proj, proj, bias_tiles)


def _diff_kernel(q_ref, k_ref, v_ref, bias_ref, lp_ref, g_ref, o_ref,
                 s_scr, mx_scr, l_scr, acc_scr, *, lam_init):
    qb = pl.program_id(2)
    qs = _split_heads(q_ref[0])

    def add_near(kb, s, t):
        return s + _tall(bias_ref[0, t])

    o = _softmax_pv(qs, lambda kb: k_ref[0, _key_rows(kb), :], lambda kb: v_ref[0, _key_rows(kb), :],
                    qb, lambda kb, s: s, add_near, s_scr, mx_scr, l_scr, acc_scr)
    lp = lp_ref[0]
    lam = (jnp.exp(jnp.sum(lp[0:1] * lp[1:2], axis=-1, keepdims=True))
           - jnp.exp(jnp.sum(lp[2:3] * lp[3:4], axis=-1, keepdims=True)) + lam_init)
    o = o[:Q_TILE] - lam * o[Q_TILE:]
    y = o * lax.rsqrt(jnp.mean(o * o, axis=-1, keepdims=True) + SUBLN_EPS)
    o_ref[0] = (y * g_ref[0] * (1.0 - lam_init)).astype(o_ref.dtype)


def _diff(proj, bias_tiles, lam_params, subln_g, layer, bsz, s_len, lam_init):
    kern = functools.partial(_diff_kernel, lam_init=lam_init)
    return pl.pallas_call(
        kern,
        grid=(bsz, B_HEADS, s_len // Q_TILE),
        in_specs=[pl.BlockSpec((1, Q_TILE, LANES), lambda b, h, i: (b, i, OFF_QB // LANES + h)),
                  pl.BlockSpec((1, s_len, LANES), lambda b, h, i: (b, 0, OFF_KB // LANES + h)),
                  pl.BlockSpec((1, s_len, LANES), lambda b, h, i: (b, 0, OFF_VB // LANES + h)),
                  pl.BlockSpec((1, 2, Q_TILE, KEY_BLOCK), lambda b, h, i: (h, 0, 0, 0)),
                  pl.BlockSpec((1, 4, HEAD_DIM), lambda b, h, i: (layer, 0, 0)),
                  pl.BlockSpec((1, 1, 2 * HEAD_DIM), lambda b, h, i: (layer, 0, 0))],
        out_specs=pl.BlockSpec((1, Q_TILE, LANES), lambda b, h, i: (b, i, h)),
        out_shape=jax.ShapeDtypeStruct((bsz, s_len, B_HEADS * 2 * HEAD_DIM), BF16),
        scratch_shapes=_softmax_scratch(s_len),
        compiler_params=_cparams("parallel", "parallel", "arbitrary"),
        name="diff_attn",
    )(proj, proj, proj, bias_tiles, lam_params, subln_g.reshape(-1, 1, 2 * HEAD_DIM))


def _stick_kernel(q_ref, k_ref, v_ref, o_ref, later_scr, run_scr, acc_scr):
    qb = pl.program_id(2)
    qs = _split_heads(q_ref[0])
    jj = lax.broadcasted_iota(I32, (KEY_BLOCK, KEY_BLOCK), 0)
    ss = lax.broadcasted_iota(I32, (KEY_BLOCK, KEY_BLOCK), 1)
    later_scr[...] = jnp.where(jj > ss, 1.0, 0.0).astype(BF16)
    run_scr[...] = jnp.zeros_like(run_scr)
    acc_scr[...] = jnp.zeros_like(acc_scr)

    def sweep(kbs, diagonal):
        parts = []
        for kb in kbs:
            z = _nt_dot(qs, k_ref[0, _key_rows(kb), :])
            soft = jnp.log(1.0 + jnp.exp(-jnp.abs(z)))
            log_beta = jnp.minimum(z, 0.0) - soft
            log_1mb = log_beta - z
            strict = None
            if diagonal:
                t = lax.broadcasted_iota(I32, z.shape, 0) & (Q_TILE - 1)
                strict = lax.broadcasted_iota(I32, z.shape, 1) < t
                log_1mb = jnp.where(strict, log_1mb, 0.0)
            hi = log_1mb.astype(BF16)
            lo = (log_1mb - hi.astype(F32)).astype(BF16)
            between = _dot(hi, later_scr[...]) + _dot(lo, later_scr[...])
            total = between[:, 0:1] + log_1mb[:, 0:1]
            parts.append((kb, log_beta + between, total, strict))
        run = run_scr[...]
        acc = None
        for kb, logit, total, strict in parts:
            w = jnp.exp(logit + _wide(run))
            if strict is not None:
                w = jnp.where(strict, w, 0.0)
            pv = _dot(w.astype(BF16), v_ref[0, _key_rows(kb), :])
            acc = pv if acc is None else acc + pv
            run = run + jnp.broadcast_to(total, run.shape)
        acc_scr[...] += acc
        run_scr[...] = run

    sweep([qb], True)
    groups = qb // GROUP

    def group_body(g, carry):
        top = qb - 1 - g * GROUP
        sweep([top - u for u in range(GROUP)], False)
        return carry

    def single_body(j, carry):
        sweep([qb - 1 - groups * GROUP - j], False)
        return carry

    lax.fori_loop(0, groups, group_body, 0)
    lax.fori_loop(0, qb - groups * GROUP, single_body, 0)
    o_ref[0] = _merge_heads(acc_scr[...], Q_TILE).astype(o_ref.dtype)


def _stick(proj, bsz, s_len):
    return pl.pallas_call(
        _stick_kernel,
        grid=(bsz, C_HEADS // 2, s_len // Q_TILE),
        in_specs=[pl.BlockSpec((1, Q_TILE, LANES), lambda b, h, i: (b, i, OFF_QC // LANES + h)),
                  pl.BlockSpec((1, s_len, LANES), lambda b, h, i: (b, 0, OFF_KC // LANES + h)),
                  pl.BlockSpec((1, s_len, LANES), lambda b, h, i: (b, 0, OFF_VC // LANES + h))],
        out_specs=pl.BlockSpec((1, Q_TILE, LANES), lambda b, h, i: (b, i, h)),
        out_shape=jax.ShapeDtypeStruct((bsz, s_len, C_HEADS * HEAD_DIM), BF16),
        scratch_shapes=[pltpu.VMEM((KEY_BLOCK, KEY_BLOCK), BF16),
                        pltpu.VMEM((2 * Q_TILE, LANES), F32),
                        pltpu.VMEM((2 * Q_TILE, LANES), F32)],
        compiler_params=_cparams("parallel", "parallel", "arbitrary"),
        name="stick_attn",
    )(proj, proj, proj)


def _merge_kernel(oa_ref, ob_ref, oc_ref, gate_ref, x_ref, g1_ref, wbr_ref, wout_ref, o_ref):
    d = x_ref.shape[-1]
    merged = jnp.zeros(x_ref.shape, F32)
    for j, o_br in enumerate((oa_ref, ob_ref, oc_ref)):
        w = o_br.shape[-1]
        gate = jax.nn.sigmoid(gate_ref[:, j * d:(j + 1) * d].astype(F32))
        merged = merged + gate * _dot(o_br[...], wbr_ref[j * w:(j + 1) * w, :])
    o_ref[...] = x_ref[...] + g1_ref[0] * _dot(merged.astype(BF16), wout_ref[...])


def _merge(o_a, o_b, o_c, proj, x2, mod, w_br, w_out, s_len):
    t, d = x2.shape
    tm = 512
    per_b = s_len // tm
    w = o_a.shape[-1]
    return pl.pallas_call(
        _merge_kernel,
        grid=(t // tm,),
        in_specs=[pl.BlockSpec((tm, w), lambda i: (i, 0)),
                  pl.BlockSpec((tm, w), lambda i: (i, 0)),
                  pl.BlockSpec((tm, w), lambda i: (i, 0)),
                  pl.BlockSpec((tm, 3 * d), lambda i: (i, OFF_G // (3 * d))),
                  pl.BlockSpec((tm, d), lambda i: (i, 0)),
                  pl.BlockSpec((1, 1, d), lambda i: (i // per_b, 0, 2)),
                  pl.BlockSpec(w_br.shape, lambda i: (0, 0)),
                  pl.BlockSpec(w_out.shape, lambda i: (0, 0))],
        out_specs=pl.BlockSpec((tm, d), lambda i: (i, 0)),
        out_shape=jax.ShapeDtypeStruct((t, d), F32),
        compiler_params=_cparams("parallel"),
        name="merge",
    )(o_a, o_b, o_c, proj, x2, mod, w_br, w_out)


def _finish(x, gate, acc, fg_ref, final):
    y = x + gate * acc
    if final:
        y = y * lax.rsqrt(jnp.mean(y * y, axis=-1, keepdims=True) + EPS) * fg_ref[...]
    return y


def _swiglu_partial(h, w1, w3, w2):
    a = _dot(h, w1)
    act = (a * jax.nn.sigmoid(a)) * _dot(h, w3)
    return _dot(act.astype(BF16), w2)


def _ffn_kernel(x_ref, g_ref, sc_ref, sh_ref, gate_ref, w1_ref, w3_ref, w2_ref, fg_ref, o_ref,
                h_scr, acc_scr, *, final):
    f = pl.program_id(1)

    @pl.when(f == 0)
    def _():
        h_scr[...] = _modulated_norm(x_ref[...], g_ref[...], sc_ref[0], sh_ref[0]).astype(BF16)
        acc_scr[...] = jnp.zeros_like(acc_scr)

    acc_scr[...] += _swiglu_partial(h_scr[...], w1_ref[...], w3_ref[...], w2_ref[...])

    @pl.when(f == pl.num_programs(1) - 1)
    def _():
        o_ref[...] = _finish(x_ref[...], gate_ref[0], acc_scr[...], fg_ref, final)


def _ffn(x2, g, mod, w1, w3, w2, final_g, s_len, final):
    t, d = x2.shape
    d_ff = w1.shape[1]
    tm, tf = 1024, 256
    per_b = s_len // tm
    kern = functools.partial(_ffn_kernel, final=final)
    return pl.pallas_call(
        kern,
        grid=(t // tm, d_ff // tf),
        in_specs=[pl.BlockSpec((tm, d), lambda i, f: (i, 0)),
                  pl.BlockSpec((1, d), lambda i, f: (0, 0)),
                  pl.BlockSpec((1, 1, d), lambda i, f: (i // per_b, 0, 4)),
                  pl.BlockSpec((1, 1, d), lambda i, f: (i // per_b, 0, 3)),
                  pl.BlockSpec((1, 1, d), lambda i, f: (i // per_b, 0, 5)),
                  pl.BlockSpec((d, tf), lambda i, f: (0, f)),
                  pl.BlockSpec((d, tf), lambda i, f: (0, f)),
                  pl.BlockSpec((tf, d), lambda i, f: (f, 0)),
                  pl.BlockSpec((1, d), lambda i, f: (0, 0))],
        out_specs=pl.BlockSpec((tm, d), lambda i, f: (i, 0)),
        out_shape=jax.ShapeDtypeStruct((t, d), F32),
        scratch_shapes=[pltpu.VMEM((tm, d), BF16), pltpu.VMEM((tm, d), F32)],
        compiler_params=_cparams("parallel", "arbitrary"),
        name="ffn",
    )(x2, g, mod, mod, mod, w1, w3, w2, final_g)


MOE_TM = 512
MOE_ROWS = 512
R_E1, R_E2, R_RANK1, R_RANK2, R_W1, R_W2 = 0, 1, 2, 3, 4, 5


def _lane_pick(tile, lane, idx):
    return jnp.sum(jnp.where(lane == idx, tile, 0.0), axis=-1, keepdims=True)


def _route_kernel(x_ref, g_ref, sc_ref, sh_ref, wr_ref, br_ref, h_ref, rec_ref, cnt_ref, cnt_scr):
    @pl.when(pl.program_id(0) == 0)
    def _():
        cnt_scr[...] = jnp.zeros_like(cnt_scr)

    h = _modulated_norm(x_ref[...], g_ref[...], sc_ref[0], sh_ref[0])
    half = h.shape[1] // 2
    bits = pltpu.bitcast(h.astype(BF16).astype(F32), jnp.uint32)
    h_ref[...] = (bits[:, half:] & jnp.uint32(0xFFFF0000)) | (bits[:, :half] >> 16)

    h_hi = h.astype(BF16)
    h_lo = (h - h_hi.astype(F32)).astype(BF16)
    wr = wr_ref[...]
    w_hi = wr.astype(BF16)
    w_lo = (wr - w_hi.astype(F32)).astype(BF16)
    logits = _dot(h_hi, w_hi) + _dot(h_hi, w_lo) + _dot(h_lo, w_hi) + br_ref[...]
    lane = lax.broadcasted_iota(I32, logits.shape, 1).astype(F32)
    lg = jnp.where(lane < N_EXPERTS, logits, -jnp.inf)
    m1 = jnp.max(lg, axis=-1, keepdims=True)
    i1 = jnp.min(jnp.where(lg == m1, lane, float(LANES)), axis=-1, keepdims=True)
    lg2 = jnp.where(lane == i1, -jnp.inf, lg)
    m2 = jnp.max(lg2, axis=-1, keepdims=True)
    i2 = jnp.min(jnp.where(lg2 == m2, lane, float(LANES)), axis=-1, keepdims=True)
    e2 = jnp.exp(m2 - m1)
    w_top = 1.0 / (1.0 + e2)

    chosen = jnp.where(lane == i1, 1.0, 0.0) + jnp.where(lane == i2, 1.0, 0.0)
    tm = chosen.shape[0]
    earlier = jnp.where(lax.broadcasted_iota(I32, (tm, tm), 1) < lax.broadcasted_iota(I32, (tm, tm), 0),
                        1.0, 0.0).astype(BF16)
    rank = _dot(earlier, chosen.astype(BF16)) + cnt_scr[0:1, :]
    rec = jnp.zeros_like(logits)
    for slot, val in ((R_E1, i1), (R_E2, i2), (R_RANK1, _lane_pick(rank, lane, i1)),
                      (R_RANK2, _lane_pick(rank, lane, i2)), (R_W1, w_top), (R_W2, e2 * w_top)):
        rec = jnp.where(lane == float(slot), val, rec)
    rec_ref[...] = rec
    cnt_scr[0:1, :] = cnt_scr[0:1, :] + jnp.sum(chosen, axis=0, keepdims=True)
    cnt_ref[...] = cnt_scr[...]


def _route(x2, g, mod, wr_pad, br_pad, s_len):
    t, d = x2.shape
    tm = MOE_TM
    per_b = s_len // tm
    return pl.pallas_call(
        _route_kernel,
        grid=(t // tm,),
        in_specs=[pl.BlockSpec((tm, d), lambda i: (i, 0)),
                  pl.BlockSpec((1, d), lambda i: (0, 0)),
                  pl.BlockSpec((1, 1, d), lambda i: (i // per_b, 0, 4)),
                  pl.BlockSpec((1, 1, d), lambda i: (i // per_b, 0, 3)),
                  pl.BlockSpec((d, LANES), lambda i: (0, 0)),
                  pl.BlockSpec((1, LANES), lambda i: (0, 0))],
        out_specs=[pl.BlockSpec((tm, d // 2), lambda i: (i, 0)),
                   pl.BlockSpec((tm, LANES), lambda i: (i, 0)),
                   pl.BlockSpec((8, LANES), lambda i: (0, 0))],
        out_shape=[jax.ShapeDtypeStruct((t, d // 2), jnp.uint32),
                   jax.ShapeDtypeStruct((t, LANES), F32),
                   jax.ShapeDtypeStruct((8, LANES), F32)],
        scratch_shapes=[pltpu.VMEM((8, LANES), F32)],
        compiler_params=_cparams("arbitrary"),
        name="moe_route",
    )(x2, g, mod, mod, wr_pad, br_pad)


def _row_copy(src_ref, src_row, dst_ref, dst_row, sem):
    return pltpu.make_async_copy(src_ref.at[pl.ds(src_row, 1)], dst_ref.at[pl.ds(dst_row, 1)], sem)


def _dispatch_kernel(pos_ref, h_ref, xs_in_ref, xs_ref, sem):
    del xs_in_ref
    tm = h_ref.shape[0]

    def start(t, carry):
        _row_copy(h_ref, t, xs_ref, pos_ref[0, 0, t], sem).start()
        _row_copy(h_ref, t, xs_ref, pos_ref[0, 0, tm + t], sem).start()
        return carry

    def wait(t, carry):
        _row_copy(h_ref, 0, xs_ref, 0, sem).wait()
        _row_copy(h_ref, 0, xs_ref, 0, sem).wait()
        return carry

    lax.fori_loop(0, tm, start, 0)
    lax.fori_loop(0, tm, wait, 0)


def _dispatch(pos, h_packed, n_rows):
    t, w = h_packed.shape
    tm = MOE_TM
    xs0 = jnp.zeros((n_rows, w), h_packed.dtype)
    return pl.pallas_call(
        _dispatch_kernel,
        grid=(t // tm,),
        in_specs=[pl.BlockSpec((1, 1, 2 * tm), lambda i: (i, 0, 0), memory_space=pltpu.SMEM),
                  pl.BlockSpec((tm, w), lambda i: (i, 0)),
                  pl.BlockSpec(memory_space=pl.ANY)],
        out_specs=pl.BlockSpec(memory_space=pl.ANY),
        out_shape=jax.ShapeDtypeStruct((n_rows, w), h_packed.dtype),
        scratch_shapes=[pltpu.SemaphoreType.DMA(())],
        input_output_aliases={2: 0},
        compiler_params=_cparams("arbitrary"),
        name="moe_dispatch",
    )(pos, h_packed, xs0)


def _expert_kernel(te_ref, nu_ref, xs_ref, w1_ref, w3_ref, w2_ref, o_ref, h_scr, acc_scr):
    i = pl.program_id(0)
    f = pl.program_id(1)

    @pl.when(i < nu_ref[0])
    def _():
        @pl.when(f == 0)
        def _():
            word = xs_ref[...]
            lo = pltpu.bitcast(word << 16, F32)
            hi = pltpu.bitcast(word & jnp.uint32(0xFFFF0000), F32)
            h_scr[...] = jnp.concatenate([lo, hi], axis=1).astype(BF16)
            acc_scr[...] = jnp.zeros_like(acc_scr)

        acc_scr[...] += _swiglu_partial(h_scr[...], w1_ref[0], w3_ref[0], w2_ref[0])

        @pl.when(f == pl.num_programs(1) - 1)
        def _():
            o_ref[...] = acc_scr[...]

    @pl.when((i >= nu_ref[0]) & (f == pl.num_programs(1) - 1))
    def _():
        o_ref[...] = jnp.zeros_like(o_ref)


def _experts(tile_expert, n_used, xs, w1, w3, w2):
    n_rows, half = xs.shape
    d = 2 * half
    d_ff = w1.shape[2]
    tf = 896
    nf = d_ff // tf
    tile = lambda i, nu: jnp.minimum(i, nu[0] - 1)
    ff = lambda i, f, nu: jnp.where(i < nu[0], f, nf - 1)
    return pl.pallas_call(
        _expert_kernel,
        grid_spec=pltpu.PrefetchScalarGridSpec(
            num_scalar_prefetch=2,
            grid=(n_rows // MOE_ROWS, nf),
            in_specs=[pl.BlockSpec((MOE_ROWS, half), lambda i, f, te, nu: (tile(i, nu), 0)),
                      pl.BlockSpec((1, d, tf), lambda i, f, te, nu: (te[i], 0, ff(i, f, nu))),
                      pl.BlockSpec((1, d, tf), lambda i, f, te, nu: (te[i], 0, ff(i, f, nu))),
                      pl.BlockSpec((1, tf, d), lambda i, f, te, nu: (te[i], ff(i, f, nu), 0))],
            out_specs=pl.BlockSpec((MOE_ROWS, d), lambda i, f, te, nu: (i, 0)),
            scratch_shapes=[pltpu.VMEM((MOE_ROWS, d), BF16), pltpu.VMEM((MOE_ROWS, d), F32)]),
        out_shape=jax.ShapeDtypeStruct((n_rows, d), F32),
        compiler_params=_cparams("arbitrary", "arbitrary"),
        name="moe_experts",
    )(tile_expert, n_used, xs, w1, w3, w2)


def _combine_kernel(pos_ref, x_ref, gate_ref, rec_ref, fg_ref, ys_ref, o_ref, y_scr, sem, *, final):
    tm = x_ref.shape[0]

    def start(t, carry):
        _row_copy(ys_ref, pos_ref[0, 0, t], y_scr.at[0], t, sem).start()
        _row_copy(ys_ref, pos_ref[0, 0, tm + t], y_scr.at[1], t, sem).start()
        return carry

    def wait(t, carry):
        _row_copy(ys_ref, 0, y_scr.at[0], 0, sem).wait()
        _row_copy(ys_ref, 0, y_scr.at[0], 0, sem).wait()
        return carry

    lax.fori_loop(0, tm, start, 0)
    lax.fori_loop(0, tm, wait, 0)
    rec = rec_ref[...]
    lane = lax.broadcasted_iota(I32, rec.shape, 1)
    w_first = jnp.sum(jnp.where(lane == R_W1, rec, 0.0), axis=-1, keepdims=True)
    w_second = jnp.sum(jnp.where(lane == R_W2, rec, 0.0), axis=-1, keepdims=True)
    f = w_first * y_scr[0] + w_second * y_scr[1]
    o_ref[...] = _finish(x_ref[...], gate_ref[0], f, fg_ref, final)


def _combine(pos, x2, mod, rec, final_g, ys, s_len, final):
    t, d = x2.shape
    tm = MOE_TM
    per_b = s_len // tm
    kern = functools.partial(_combine_kernel, final=final)
    return pl.pallas_call(
        kern,
        grid=(t // tm,),
        in_specs=[pl.BlockSpec((1, 1, 2 * tm), lambda i: (i, 0, 0), memory_space=pltpu.SMEM),
                  pl.BlockSpec((tm, d), lambda i: (i, 0)),
                  pl.BlockSpec((1, 1, d), lambda i: (i // per_b, 0, 5)),
                  pl.BlockSpec((tm, LANES), lambda i: (i, 0)),
                  pl.BlockSpec((1, d), lambda i: (0, 0)),
                  pl.BlockSpec(memory_space=pl.ANY)],
        out_specs=pl.BlockSpec((tm, d), lambda i: (i, 0)),
        out_shape=jax.ShapeDtypeStruct((t, d), F32),
        scratch_shapes=[pltpu.VMEM((2, tm, d), F32), pltpu.SemaphoreType.DMA(())],
        compiler_params=_cparams("arbitrary"),
        name="moe_combine",
    )(pos, x2, mod, rec, final_g, ys)


def _moe(x2, g, mod, wr_pad, br_pad, w1, w3, w2, final_g, s_len, final):
    t, d = x2.shape
    n_e = w1.shape[0]
    h_packed, rec, cnt = _route(x2, g, mod, wr_pad, br_pad, s_len)

    n_tiles = 2 * t // MOE_ROWS + n_e
    counts = cnt[0, :n_e].astype(I32)
    padded = (counts + MOE_ROWS - 1) // MOE_ROWS * MOE_ROWS
    ends = jnp.cumsum(padded)
    starts = ends - padded
    e1, e2 = rec[:, R_E1].astype(I32), rec[:, R_E2].astype(I32)
    pos1 = starts[e1] + rec[:, R_RANK1].astype(I32)
    pos2 = starts[e2] + rec[:, R_RANK2].astype(I32)
    pos = jnp.concatenate([pos1.reshape(-1, 1, MOE_TM), pos2.reshape(-1, 1, MOE_TM)], axis=2)
    n_used = (ends[-1] // MOE_ROWS).reshape(1)
    tile_start = jnp.minimum(jnp.arange(n_tiles, dtype=I32), n_used[0] - 1) * MOE_ROWS
    tile_expert = jnp.sum(tile_start[:, None] >= ends[None, :], axis=1).astype(I32)

    xs = _dispatch(pos, h_packed, n_tiles * MOE_ROWS)
    ys = _experts(tile_expert, n_used, xs, w1, w3, w2)
    return _combine(pos, x2, mod, rec, final_g, ys, s_len, final)


def _pack_w_in(w):
    d = w.shape[0]
    sizes = (W_A, W_A, W_A, IDX_HEADS * IDX_DIM, IDX_DIM, IDX_HEADS,
             512, 512, 512, 512, 512, 512, 3 * d)
    offs = [0]
    for s in sizes:
        offs.append(offs[-1] + s)
    (qa, ka, va, qi, ki, wi, qb, kb, vb, qc, kc, vc, gl) = [
        w[:, offs[j]:offs[j + 1]] for j in range(len(sizes))]
    scale = HEAD_DIM ** -0.5
    pad_wi = jnp.zeros((d, LANES - IDX_HEADS), w.dtype)
    packed = jnp.concatenate(
        [gl, qa * scale, ka, va,
         qi * (IDX_DIM ** -0.5), ki, ki, wi * (IDX_HEADS ** -0.5), pad_wi,
         qb * scale, kb, vb, qc * scale, kc, vc], axis=1)
    assert packed.shape[1] == PACKED
    return packed.astype(BF16)


def kernel(x, c, w_ada, b_ada, norm1_g, norm2_g, w_in, w_br, w_out, rel_bias, lam_params,
           subln_g, ffn_w1, ffn_w3, ffn_w2, router_w, router_b, moe_w1, moe_w3, moe_w2, final_g):
    bsz, s_len, d = x.shape
    depth = w_ada.shape[0]
    assert s_len % 1024 == 0 and d == 1024 and OFF_G + 3 * d == OFF_QA

    c_pad = jnp.concatenate([c, jnp.zeros((8 - bsz % 8 if bsz % 8 else 0, d), c.dtype)], axis=0)
    mod_all = _ada(c_pad, w_ada, b_ada)
    dsa_tiles, diff_tiles = _bias_tiles(rel_bias)
    fg = final_g.reshape(1, d)

    x2 = x.reshape(bsz * s_len, d)
    for l in range(depth):
        mod = mod_all[l, :bsz].reshape(bsz, 1, 6 * d)
        proj = _inproj(x2, norm1_g[l].reshape(1, d), mod, _pack_w_in(w_in[l]), s_len)
        proj3 = proj.reshape(bsz, s_len, PACKED)
        lam_init = 0.8 - 0.6 * math.exp(-0.3 * l)
        o_a = _dsa(proj3, dsa_tiles, bsz, s_len)
        o_b = _diff(proj3, diff_tiles, lam_params, subln_g, l, bsz, s_len, lam_init)
        o_c = _stick(proj3, bsz, s_len)
        x2 = _merge(o_a.reshape(-1, o_a.shape[-1]), o_b.reshape(-1, o_b.shape[-1]),
                    o_c.reshape(-1, o_c.shape[-1]), proj, x2, mod,
                    w_br[l].astype(BF16), w_out[l].astype(BF16), s_len)
        g2 = norm2_g[l].reshape(1, d)
        final = l == depth - 1
        j = l // 2
        if l % 2 == 0:
            x2 = _ffn(x2, g2, mod, ffn_w1[j].astype(BF16), ffn_w3[j].astype(BF16),
                      ffn_w2[j].astype(BF16), fg, s_len, final)
        else:
            wr_pad = jnp.pad(router_w[j], ((0, 0), (0, LANES - N_EXPERTS)))
            br_pad = jnp.pad(router_b[j], (0, LANES - N_EXPERTS)).reshape(1, LANES)
            x2 = _moe(x2, g2, mod, wr_pad, br_pad, moe_w1[j].astype(BF16), moe_w3[j].astype(BF16),
                      moe_w2[j].astype(BF16), fg, s_len, final)
    return x2.reshape(bsz, s_len, d)
```

```python
import functools
import math

import jax
import jax.numpy as jnp
from jax import lax
from jax.experimental import pallas as pl
from jax.experimental.pallas import tpu as pltpu

F32 = jnp.float32
BF16 = jnp.bfloat16
I32 = jnp.int32

LANES = 128
VMEM_LIMIT_BYTES = 56 * 1024 * 1024

CHUNK = 64
A_HEADS = 8
IDX_HEADS = 4
IDX_DIM = 64
TOPK_MAX = 256
B_HEADS = 4
C_HEADS = 8
HEAD_DIM = 64
REL_BUCKETS = 32
FAR_BUCKET = REL_BUCKETS // 2 - 1
N_EXPERTS = 8
EPS = 1e-6
SUBLN_EPS = 1e-5
NEG = -1e30
INT_MIN = -(2 ** 31)

KEY_BLOCK = 256
Q_TILE = 256
GROUP = 4

W_A = A_HEADS * HEAD_DIM
OFF_G = 0
OFF_QA, OFF_KA, OFF_VA = 3072, 3584, 4096
OFF_QI, OFF_KI, OFF_WI = 4608, 4864, 4992
OFF_QB, OFF_KB, OFF_VB = 5120, 5632, 6144
OFF_QC, OFF_KC, OFF_VC = 6656, 7168, 7680
PACKED = 8192

LOG_BUCKET_STEPS = (12, 16, 23, 32, 46, 64, 91)


def _nt_dot(a, b):
    return lax.dot_general(a, b, (((1,), (1,)), ((), ())), preferred_element_type=F32)


def _dot(a, b):
    return jnp.dot(a, b, preferred_element_type=F32)


def _cparams(*sem):
    return pltpu.CompilerParams(dimension_semantics=sem, vmem_limit_bytes=VMEM_LIMIT_BYTES)


def _split_heads(x):
    lane = lax.broadcasted_iota(I32, x.shape, 1)
    keep_a = jnp.where(lane < HEAD_DIM, 1.0, 0.0).astype(x.dtype)
    keep_b = jnp.where(lane < HEAD_DIM, 0.0, 1.0).astype(x.dtype)
    return jnp.concatenate([x * keep_a, x * keep_b], axis=0)


def _merge_heads(o, m):
    lane = lax.broadcasted_iota(I32, (m, LANES), 1)
    return jnp.where(lane < HEAD_DIM, o[:m], o[m:])


def _wide(x):
    return jnp.concatenate([x, x], axis=1)


def _tall(x):
    return jnp.concatenate([x, x], axis=0)


def _key_rows(kb):
    return pl.ds(pl.multiple_of(kb * KEY_BLOCK, KEY_BLOCK), KEY_BLOCK)


def _sortable(x):
    bits = pltpu.bitcast(x, I32)
    return bits ^ ((bits >> 31) & 0x7FFFFFFF)


def _chunk_causal():
    r = lax.broadcasted_iota(I32, (Q_TILE, KEY_BLOCK), 0)
    c = lax.broadcasted_iota(I32, (Q_TILE, KEY_BLOCK), 1)
    return (c // CHUNK) <= (r // CHUNK)


def _ada_kernel(c_ref, w_ref, b_ref, o_ref):
    c = c_ref[...]
    a = c * jax.nn.sigmoid(c)
    o_ref[0] = jnp.dot(a, w_ref[0], preferred_element_type=F32,
                       precision=lax.Precision.HIGHEST) + b_ref[0]


def _ada(c_pad, w_ada, b_ada):
    depth, d, n = w_ada.shape
    tn = 1024
    return pl.pallas_call(
        _ada_kernel,
        grid=(depth, n // tn),
        in_specs=[pl.BlockSpec(c_pad.shape, lambda l, j: (0, 0)),
                  pl.BlockSpec((1, d, tn), lambda l, j: (l, 0, j)),
                  pl.BlockSpec((1, 1, tn), lambda l, j: (l, 0, j))],
        out_specs=pl.BlockSpec((1, c_pad.shape[0], tn), lambda l, j: (l, 0, j)),
        out_shape=jax.ShapeDtypeStruct((depth, c_pad.shape[0], n), F32),
        compiler_params=_cparams("parallel", "parallel"),
        name="ada",
    )(c_pad, w_ada, b_ada.reshape(depth, 1, n))


def _rel_bias_tile(tab_ref, head, d0, n_heads_total):
    r = lax.broadcasted_iota(I32, (Q_TILE, KEY_BLOCK), 0)
    c = lax.broadcasted_iota(I32, (Q_TILE, KEY_BLOCK), 1)
    d = c - r + d0
    n = jnp.abs(d)
    large = jnp.full(d.shape, REL_BUCKETS // 4, I32)
    for step in LOG_BUCKET_STEPS:
        large = large + jnp.where(n >= step, 1, 0)
    bucket = jnp.where(d > 0, REL_BUCKETS // 2, 0) + jnp.where(n < REL_BUCKETS // 4, n, large)
    out = jnp.zeros(d.shape, F32)
    for b in range(REL_BUCKETS):
        out = jnp.where(bucket == b, tab_ref[b * n_heads_total + head], out)
    return out - tab_ref[FAR_BUCKET * n_heads_total + head]


def _bias_kernel(tab_ref, dsa_ref, diff_ref):
    p = pl.program_id(0)
    n_heads = A_HEADS + B_HEADS
    for t, d0 in enumerate((-KEY_BLOCK, 0)):
        dsa_ref[0, t, 0:Q_TILE, :] = _rel_bias_tile(tab_ref, 2 * p, d0, n_heads)
        dsa_ref[0, t, Q_TILE:2 * Q_TILE, :] = _rel_bias_tile(tab_ref, 2 * p + 1, d0, n_heads)
        tile = _rel_bias_tile(tab_ref, A_HEADS + p, d0, n_heads)
        if t == 1:
            tile = jnp.where(_chunk_causal(), tile, NEG)
        diff_ref[0, t] = tile


def _bias_tiles(rel_bias):
    tab = rel_bias.reshape(-1)
    return pl.pallas_call(
        _bias_kernel,
        grid=(4,),
        in_specs=[pl.BlockSpec(memory_space=pltpu.SMEM)],
        out_specs=[pl.BlockSpec((1, 2, 2 * Q_TILE, KEY_BLOCK), lambda p: (p, 0, 0, 0)),
                   pl.BlockSpec((1, 2, Q_TILE, KEY_BLOCK), lambda p: (p, 0, 0, 0))],
        out_shape=[jax.ShapeDtypeStruct((A_HEADS // 2, 2, 2 * Q_TILE, KEY_BLOCK), F32),
                   jax.ShapeDtypeStruct((B_HEADS, 2, Q_TILE, KEY_BLOCK), F32)],
        compiler_params=_cparams("parallel"),
        name="rel_bias_tiles",
    )(tab)


def _modulated_norm(x, g, sc, sh):
    y = x * lax.rsqrt(jnp.mean(x * x, axis=-1, keepdims=True) + EPS)
    return y * g * (1.0 + sc) + sh


def _inproj_kernel(x_ref, g_ref, sc_ref, sh_ref, w_ref, o_ref, h_scr):
    @pl.when(pl.program_id(1) == 0)
    def _():
        h_scr[...] = _modulated_norm(x_ref[...], g_ref[...], sc_ref[0], sh_ref[0]).astype(BF16)

    o_ref[...] = _dot(h_scr[...], w_ref[...]).astype(o_ref.dtype)


def _inproj(x2, g, mod, w_packed, s_len):
    t, d = x2.shape
    n = w_packed.shape[1]
    tm, tn = 1024, 1024
    per_b = s_len // tm
    return pl.pallas_call(
        _inproj_kernel,
        grid=(t // tm, n // tn),
        in_specs=[pl.BlockSpec((tm, d), lambda i, j: (i, 0)),
                  pl.BlockSpec((1, d), lambda i, j: (0, 0)),
                  pl.BlockSpec((1, 1, d), lambda i, j: (i // per_b, 0, 1)),
                  pl.BlockSpec((1, 1, d), lambda i, j: (i // per_b, 0, 0)),
                  pl.BlockSpec((d, tn), lambda i, j: (0, j))],
        out_specs=pl.BlockSpec((tm, tn), lambda i, j: (i, j)),
        out_shape=jax.ShapeDtypeStruct((t, n), BF16),
        scratch_shapes=[pltpu.VMEM((tm, d), BF16)],
        compiler_params=_cparams("parallel", "arbitrary"),
        name="inproj",
    )(x2, g, mod, mod, w_packed)


def _softmax_pv(qs, k_at, v_at, qb, add_far, add_near, s_scr, mx_scr, l_scr, acc_scr):
    n = qb + 1
    groups = n // GROUP
    rest = n - groups * GROUP
    grouped = groups > 0
    far_singles = jnp.where(grouped, rest, jnp.maximum(qb - 1, 0))
    near_singles_lo = jnp.where(grouped, n, jnp.maximum(qb - 1, 0))

    def pass1(blocks):
        mx = None
        for kb, add in blocks:
            s = add(kb, _nt_dot(qs, k_at(kb)))
            s_scr[kb] = s
            fold = jnp.maximum(s[:, :LANES], s[:, LANES:])
            mx = fold if mx is None else jnp.maximum(mx, fold)
        mx_scr[...] = jnp.maximum(mx_scr[...], mx)

    def loop(lo, hi, fn):
        def body(i, carry):
            fn(i)
            return carry
        lax.fori_loop(lo, hi, body, 0)

    mx_scr[...] = jnp.full(mx_scr.shape, NEG, F32)
    loop(0, far_singles, lambda kb: pass1([(kb, add_far)]))
    loop(0, groups - 1,
         lambda g: pass1([(rest + g * GROUP + u, add_far) for u in range(GROUP)]))

    @pl.when(grouped)
    def _():
        pass1([(qb - u, add_far) for u in range(GROUP - 1, 1, -1)]
              + [(qb - 1, lambda kb, s: add_near(kb, s, 0)), (qb, lambda kb, s: add_near(kb, s, 1))])

    loop(near_singles_lo, n, lambda kb: pass1([(kb, lambda kb, s: add_near(kb, s, kb - (qb - 1)))]))

    m = jnp.max(mx_scr[...], axis=-1, keepdims=True)
    mx_scr[...] = jnp.broadcast_to(m, mx_scr.shape)
    l_scr[...] = jnp.zeros_like(l_scr)
    acc_scr[...] = jnp.zeros_like(acc_scr)

    def pass2(kbs):
        mw = _wide(mx_scr[...])
        l_add = acc_add = None
        for kb in kbs:
            p = jnp.exp(s_scr[kb] - mw)
            fold = p[:, :LANES] + p[:, LANES:]
            pv = _dot(p.astype(BF16), v_at(kb))
            l_add = fold if l_add is None else l_add + fold
            acc_add = pv if acc_add is None else acc_add + pv
        l_scr[...] += l_add
        acc_scr[...] += acc_add

    loop(0, rest, lambda kb: pass2([kb]))
    loop(0, groups, lambda g: pass2([rest + g * GROUP + u for u in range(GROUP)]))
    return acc_scr[...] / jnp.sum(l_scr[...], axis=-1, keepdims=True)


def _softmax_scratch(s_len):
    return [pltpu.VMEM((s_len // KEY_BLOCK, 2 * Q_TILE, KEY_BLOCK), F32),
            pltpu.VMEM((2 * Q_TILE, LANES), F32),
            pltpu.VMEM((2 * Q_TILE, LANES), F32),
            pltpu.VMEM((2 * Q_TILE, LANES), F32)]


def _dsa_kernel(qa_ref, qi_ref, wi_ref, ka_ref, va_ref, ki_ref, bias_ref, o_ref,
                key_scr, keyt_scr, neg_scr, w_scr, last_scr, s_scr, mx_scr, l_scr, acc_scr,
                *, topk, idx_bits, s_len):
    qb = pl.program_id(1)

    @pl.when(pl.program_id(2) == 0)
    def _():
        _dsa_select(qb, qi_ref, wi_ref, ki_ref, key_scr, keyt_scr, neg_scr, w_scr, last_scr,
                    topk=topk, idx_bits=idx_bits, s_len=s_len)

    def add_far(kb, s):
        return s + _tall(neg_scr[kb])

    def add_near(kb, s, t):
        return s + _tall(neg_scr[kb]) + bias_ref[0, t]

    o = _softmax_pv(_split_heads(qa_ref[0]),
                    lambda kb: ka_ref[0, _key_rows(kb), :], lambda kb: va_ref[0, _key_rows(kb), :],
                    qb, add_far, add_near, s_scr, mx_scr, l_scr, acc_scr)
    o_ref[0] = _merge_heads(o, Q_TILE).astype(o_ref.dtype)


def _dsa_select(qb, qi_ref, wi_ref, ki_ref, key_scr, keyt_scr, neg_scr, w_scr, last_scr,
                *, topk, idx_bits, s_len):
    nkb = qb + 1
    kf = float(topk)
    valid_diag = _chunk_causal()

    def cols(kb):
        return kb * KEY_BLOCK + lax.broadcasted_iota(I32, (Q_TILE, KEY_BLOCK), 1)

    qi = qi_ref[0]
    q_heads = [_split_heads(qi[:, pr * LANES:(pr + 1) * LANES]) for pr in range(IDX_HEADS // 2)]
    wi = wi_ref[0].astype(F32)
    for h in range(IDX_HEADS):
        w_scr[h] = jnp.broadcast_to(wi[:, h:h + 1], (Q_TILE, LANES))

    def score_block(kb, valid):
        kk = ki_ref[0, _key_rows(kb), :]
        sc = jnp.zeros((Q_TILE, KEY_BLOCK), F32)
        for pr in range(IDX_HEADS // 2):
            d = _nt_dot(q_heads[pr], kk)
            sc = sc + _wide(w_scr[2 * pr]) * jnp.maximum(d[:Q_TILE], 0.0)
            sc = sc + _wide(w_scr[2 * pr + 1]) * jnp.maximum(d[Q_TILE:], 0.0)
        sc = jnp.where(sc == 0.0, 0.0, sc)
        if valid is not None:
            sc = jnp.where(valid, sc, -jnp.inf)
        key_scr[kb] = _sortable(sc)
        keyt_scr[kb] = _sortable(sc.T)

    def score_body(kb, carry):
        score_block(kb, None)
        return carry

    lax.fori_loop(0, qb, score_body, 0)
    score_block(qb, valid_diag)

    def per_query(x):
        return jnp.broadcast_to(x[None], (KEY_BLOCK // 8, 8, Q_TILE)).reshape(KEY_BLOCK, Q_TILE)

    def key_index(kb):
        return kb * KEY_BLOCK + lax.broadcasted_iota(I32, (KEY_BLOCK, Q_TILE), 0)

    def count(pred):
        def one(kb, acc):
            hit = jnp.where(pred(keyt_scr[kb], kb), 1.0, 0.0)
            return acc + jnp.sum(hit.reshape(KEY_BLOCK // 8, 8, Q_TILE), axis=0)

        def group(g, acc):
            for u in range(GROUP):
                acc = one(g * GROUP + u, acc)
            return acc

        acc = lax.fori_loop(0, nkb // GROUP, group, jnp.zeros((8, Q_TILE), F32))
        acc = lax.fori_loop(nkb // GROUP * GROUP, nkb, one, acc)
        return jnp.broadcast_to(jnp.sum(acc, axis=0, keepdims=True), (8, Q_TILE))

    thr = jnp.where(count(lambda key, kb: key >= 0) >= kf,
                    jnp.zeros((8, Q_TILE), I32), jnp.full((8, Q_TILE), INT_MIN, I32))

    def thr_body(j, thr):
        cand = thr | lax.shift_left(jnp.int32(1), 30 - j)
        cw = per_query(cand)
        return jnp.where(count(lambda key, kb: key >= cw) >= kf, cand, thr)

    thr = lax.fori_loop(0, 31, thr_body, thr)
    thr_t = per_query(thr)

    need = kf - count(lambda key, kb: key > thr_t)
    ties = count(lambda key, kb: key == thr_t)
    last_scr[...] = jnp.full(last_scr.shape, s_len, I32)

    @pl.when(jnp.max(ties - need) > 0.0)
    def _():
        def tie_body(j, last):
            cand = last | lax.shift_left(jnp.int32(1), idx_bits - 1 - j)
            cw = per_query(cand)
            taken = count(lambda key, kb: (key == thr_t) & (key_index(kb) < cw))
            return jnp.where(taken < need, cand, last)

        last_scr[...] = lax.fori_loop(0, idx_bits, tie_body, jnp.zeros((8, Q_TILE), I32))

    def per_row(x):
        return _wide(jnp.broadcast_to(x[0:1], (LANES, Q_TILE)).T)

    tw = per_row(thr)
    lastw = per_row(last_scr[...])

    def mask_block(kb, valid):
        key = key_scr[kb]
        tie_neg = jnp.where(key == tw, jnp.where(cols(kb) <= lastw, 0.0, NEG), NEG)
        neg = jnp.where(key > tw, 0.0, tie_neg)
        if valid is not None:
            neg = jnp.where(valid, neg, NEG)
        neg_scr[kb] = neg

    def mask_body(kb, carry):
        mask_block(kb, None)
        return carry

    lax.fori_loop(0, qb, mask_body, 0)
    mask_block(qb, valid_diag)


def _dsa(proj, bias_tiles, bsz, s_len):
    topk = min(TOPK_MAX, s_len // 4)
    idx_bits = max(1, (s_len - 1).bit_length())
    n_blocks = s_len // KEY_BLOCK
    assert 2 * n_blocks <= 256
    kern = functools.partial(_dsa_kernel, topk=topk, idx_bits=idx_bits, s_len=s_len)
    lane_blk = lambda off: off // LANES
    return pl.pallas_call(
        kern,
        grid=(bsz, s_len // Q_TILE, A_HEADS // 2),
        in_specs=[pl.BlockSpec((1, Q_TILE, LANES), lambda b, i, p: (b, i, lane_blk(OFF_QA) + p)),
                  pl.BlockSpec((1, Q_TILE, 256), lambda b, i, p: (b, i, OFF_QI // 256)),
                  pl.BlockSpec((1, Q_TILE, LANES), lambda b, i, p: (b, i, lane_blk(OFF_WI))),
                  pl.BlockSpec((1, s_len, LANES), lambda b, i, p: (b, 0, lane_blk(OFF_KA) + p)),
                  pl.BlockSpec((1, s_len, LANES), lambda b, i, p: (b, 0, lane_blk(OFF_VA) + p)),
                  pl.BlockSpec((1, s_len, LANES), lambda b, i, p: (b, 0, lane_blk(OFF_KI))),
                  pl.BlockSpec((1, 2, 2 * Q_TILE, KEY_BLOCK), lambda b, i, p: (p, 0, 0, 0))],
        out_specs=pl.BlockSpec((1, Q_TILE, LANES), lambda b, i, p: (b, i, p)),
        out_shape=jax.ShapeDtypeStruct((bsz, s_len, W_A), BF16),
        scratch_shapes=[pltpu.VMEM((n_blocks, Q_TILE, KEY_BLOCK), I32),
                        pltpu.VMEM((n_blocks, KEY_BLOCK, Q_TILE), I32),
                        pltpu.VMEM((n_blocks, Q_TILE, KEY_BLOCK), F32),
                        pltpu.VMEM((IDX_HEADS, Q_TILE, LANES), F32),
                        pltpu.VMEM((8, Q_TILE), I32)] + _softmax_scratch(s_len),
        compiler_params=_cparams("parallel", "arbitrary", "arbitrary"),
        name="dsa",
    )(proj, proj, proj, proj, proj, proj, bias_tiles)


def _diff_kernel(q_ref, k_ref, v_ref, bias_ref, lp_ref, g_ref, o_ref,
                 s_scr, mx_scr, l_scr, acc_scr, *, lam_init):
    qb = pl.program_id(2)
    qs = _split_heads(q_ref[0])

    def add_near(kb, s, t):
        return s + _tall(bias_ref[0, t])

    o = _softmax_pv(qs, lambda kb: k_ref[0, _key_rows(kb), :], lambda kb: v_ref[0, _key_rows(kb), :],
                    qb, lambda kb, s: s, add_near, s_scr, mx_scr, l_scr, acc_scr)
    lp = lp_ref[0]
    lam = (jnp.exp(jnp.sum(lp[0:1] * lp[1:2], axis=-1, keepdims=True))
           - jnp.exp(jnp.sum(lp[2:3] * lp[3:4], axis=-1, keepdims=True)) + lam_init)
    o = o[:Q_TILE] - lam * o[Q_TILE:]
    y = o * lax.rsqrt(jnp.mean(o * o, axis=-1, keepdims=True) + SUBLN_EPS)
    o_ref[0] = (y * g_ref[0] * (1.0 - lam_init)).astype(o_ref.dtype)


def _diff(proj, bias_tiles, lam_params, subln_g, layer, bsz, s_len, lam_init):
    kern = functools.partial(_diff_kernel, lam_init=lam_init)
    return pl.pallas_call(
        kern,
        grid=(bsz, B_HEADS, s_len // Q_TILE),
        in_specs=[pl.BlockSpec((1, Q_TILE, LANES), lambda b, h, i: (b, i, OFF_QB // LANES + h)),
                  pl.BlockSpec((1, s_len, LANES), lambda b, h, i: (b, 0, OFF_KB // LANES + h)),
                  pl.BlockSpec((1, s_len, LANES), lambda b, h, i: (b, 0, OFF_VB // LANES + h)),
                  pl.BlockSpec((1, 2, Q_TILE, KEY_BLOCK), lambda b, h, i: (h, 0, 0, 0)),
                  pl.BlockSpec((1, 4, HEAD_DIM), lambda b, h, i: (layer, 0, 0)),
                  pl.BlockSpec((1, 1, 2 * HEAD_DIM), lambda b, h, i: (layer, 0, 0))],
        out_specs=pl.BlockSpec((1, Q_TILE, LANES), lambda b, h, i: (b, i, h)),
        out_shape=jax.ShapeDtypeStruct((bsz, s_len, B_HEADS * 2 * HEAD_DIM), BF16),
        scratch_shapes=_softmax_scratch(s_len),
        compiler_params=_cparams("parallel", "parallel", "arbitrary"),
        name="diff_attn",
    )(proj, proj, proj, bias_tiles, lam_params, subln_g.reshape(-1, 1, 2 * HEAD_DIM))


def _stick_kernel(q_ref, k_ref, v_ref, o_ref, later_scr, run_scr, acc_scr):
    qb = pl.program_id(2)
    qs = _split_heads(q_ref[0])
    jj = lax.broadcasted_iota(I32, (KEY_BLOCK, KEY_BLOCK), 0)
    ss = lax.broadcasted_iota(I32, (KEY_BLOCK, KEY_BLOCK), 1)
    later_scr[...] = jnp.where(jj > ss, 1.0, 0.0).astype(BF16)
    run_scr[...] = jnp.zeros_like(run_scr)
    acc_scr[...] = jnp.zeros_like(acc_scr)

    def sweep(kbs, diagonal):
        parts = []
        for kb in kbs:
            z = _nt_dot(qs, k_ref[0, _key_rows(kb), :])
            soft = jnp.log(1.0 + jnp.exp(-jnp.abs(z)))
            log_beta = jnp.minimum(z, 0.0) - soft
            log_1mb = log_beta - z
            strict = None
            if diagonal:
                t = lax.broadcasted_iota(I32, z.shape, 0) & (Q_TILE - 1)
                strict = lax.broadcasted_iota(I32, z.shape, 1) < t
                log_1mb = jnp.where(strict, log_1mb, 0.0)
            hi = log_1mb.astype(BF16)
            lo = (log_1mb - hi.astype(F32)).astype(BF16)
            between = _dot(hi, later_scr[...]) + _dot(lo, later_scr[...])
            total = between[:, 0:1] + log_1mb[:, 0:1]
            parts.append((kb, log_beta + between, total, strict))
        run = run_scr[...]
        acc = None
        for kb, logit, total, strict in parts:
            w = jnp.exp(logit + _wide(run))
            if strict is not None:
                w = jnp.where(strict, w, 0.0)
            pv = _dot(w.astype(BF16), v_ref[0, _key_rows(kb), :])
            acc = pv if acc is None else acc + pv
            run = run + jnp.broadcast_to(total, run.shape)
        acc_scr[...] += acc
        run_scr[...] = run

    sweep([qb], True)
    groups = qb // GROUP

    def group_body(g, carry):
        top = qb - 1 - g * GROUP
        sweep([top - u for u in range(GROUP)], False)
        return carry

    def single_body(j, carry):
        sweep([qb - 1 - groups * GROUP - j], False)
        return carry

    lax.fori_loop(0, groups, group_body, 0)
    lax.fori_loop(0, qb - groups * GROUP, single_body, 0)
    o_ref[0] = _merge_heads(acc_scr[...], Q_TILE).astype(o_ref.dtype)


def _stick(proj, bsz, s_len):
    return pl.pallas_call(
        _stick_kernel,
        grid=(bsz, C_HEADS // 2, s_len // Q_TILE),
        in_specs=[pl.BlockSpec((1, Q_TILE, LANES), lambda b, h, i: (b, i, OFF_QC // LANES + h)),
                  pl.BlockSpec((1, s_len, LANES), lambda b, h, i: (b, 0, OFF_KC // LANES + h)),
                  pl.BlockSpec((1, s_len, LANES), lambda b, h, i: (b, 0, OFF_VC // LANES + h))],
        out_specs=pl.BlockSpec((1, Q_TILE, LANES), lambda b, h, i: (b, i, h)),
        out_shape=jax.ShapeDtypeStruct((bsz, s_len, C_HEADS * HEAD_DIM), BF16),
        scratch_shapes=[pltpu.VMEM((KEY_BLOCK, KEY_BLOCK), BF16),
                        pltpu.VMEM((2 * Q_TILE, LANES), F32),
                        pltpu.VMEM((2 * Q_TILE, LANES), F32)],
        compiler_params=_cparams("parallel", "parallel", "arbitrary"),
        name="stick_attn",
    )(proj, proj, proj)


def _merge_kernel(oa_ref, ob_ref, oc_ref, gate_ref, x_ref, g1_ref, wbr_ref, wout_ref, o_ref):
    d = x_ref.shape[-1]
    merged = jnp.zeros(x_ref.shape, F32)
    for j, o_br in enumerate((oa_ref, ob_ref, oc_ref)):
        w = o_br.shape[-1]
        gate = jax.nn.sigmoid(gate_ref[:, j * d:(j + 1) * d].astype(F32))
        merged = merged + gate * _dot(o_br[...], wbr_ref[j * w:(j + 1) * w, :])
    o_ref[...] = x_ref[...] + g1_ref[0] * _dot(merged.astype(BF16), wout_ref[...])


def _merge(o_a, o_b, o_c, proj, x2, mod, w_br, w_out, s_len):
    t, d = x2.shape
    tm = 512
    per_b = s_len // tm
    w = o_a.shape[-1]
    return pl.pallas_call(
        _merge_kernel,
        grid=(t // tm,),
        in_specs=[pl.BlockSpec((tm, w), lambda i: (i, 0)),
                  pl.BlockSpec((tm, w), lambda i: (i, 0)),
                  pl.BlockSpec((tm, w), lambda i: (i, 0)),
                  pl.BlockSpec((tm, 3 * d), lambda i: (i, OFF_G // (3 * d))),
                  pl.BlockSpec((tm, d), lambda i: (i, 0)),
                  pl.BlockSpec((1, 1, d), lambda i: (i // per_b, 0, 2)),
                  pl.BlockSpec(w_br.shape, lambda i: (0, 0)),
                  pl.BlockSpec(w_out.shape, lambda i: (0, 0))],
        out_specs=pl.BlockSpec((tm, d), lambda i: (i, 0)),
        out_shape=jax.ShapeDtypeStruct((t, d), F32),
        compiler_params=_cparams("parallel"),
        name="merge",
    )(o_a, o_b, o_c, proj, x2, mod, w_br, w_out)


def _finish(x, gate, acc, fg_ref, final):
    y = x + gate * acc
    if final:
        y = y * lax.rsqrt(jnp.mean(y * y, axis=-1, keepdims=True) + EPS) * fg_ref[...]
    return y


def _swiglu_partial(h, w1, w3, w2):
    a = _dot(h, w1)
    act = (a * jax.nn.sigmoid(a)) * _dot(h, w3)
    return _dot(act.astype(BF16), w2)


def _ffn_kernel(x_ref, g_ref, sc_ref, sh_ref, gate_ref, w1_ref, w3_ref, w2_ref, fg_ref, o_ref,
                h_scr, acc_scr, *, final):
    f = pl.program_id(1)

    @pl.when(f == 0)
    def _():
        h_scr[...] = _modulated_norm(x_ref[...], g_ref[...], sc_ref[0], sh_ref[0]).astype(BF16)
        acc_scr[...] = jnp.zeros_like(acc_scr)

    acc_scr[...] += _swiglu_partial(h_scr[...], w1_ref[...], w3_ref[...], w2_ref[...])

    @pl.when(f == pl.num_programs(1) - 1)
    def _():
        o_ref[...] = _finish(x_ref[...], gate_ref[0], acc_scr[...], fg_ref, final)


def _ffn(x2, g, mod, w1, w3, w2, final_g, s_len, final):
    t, d = x2.shape
    d_ff = w1.shape[1]
    tm, tf = 1024, 256
    per_b = s_len // tm
    kern = functools.partial(_ffn_kernel, final=final)
    return pl.pallas_call(
        kern,
        grid=(t // tm, d_ff // tf),
        in_specs=[pl.BlockSpec((tm, d), lambda i, f: (i, 0)),
                  pl.BlockSpec((1, d), lambda i, f: (0, 0)),
                  pl.BlockSpec((1, 1, d), lambda i, f: (i // per_b, 0, 4)),
                  pl.BlockSpec((1, 1, d), lambda i, f: (i // per_b, 0, 3)),
                  pl.BlockSpec((1, 1, d), lambda i, f: (i // per_b, 0, 5)),
                  pl.BlockSpec((d, tf), lambda i, f: (0, f)),
                  pl.BlockSpec((d, tf), lambda i, f: (0, f)),
                  pl.BlockSpec((tf, d), lambda i, f: (f, 0)),
                  pl.BlockSpec((1, d), lambda i, f: (0, 0))],
        out_specs=pl.BlockSpec((tm, d), lambda i, f: (i, 0)),
        out_shape=jax.ShapeDtypeStruct((t, d), F32),
        scratch_shapes=[pltpu.VMEM((tm, d), BF16), pltpu.VMEM((tm, d), F32)],
        compiler_params=_cparams("parallel", "arbitrary"),
        name="ffn",
    )(x2, g, mod, mod, mod, w1, w3, w2, final_g)


MOE_TM = 512
MOE_ROWS = 512
R_E1, R_E2, R_RANK1, R_RANK2, R_W1, R_W2 = 0, 1, 2, 3, 4, 5


def _lane_pick(tile, lane, idx):
    return jnp.sum(jnp.where(lane == idx, tile, 0.0), axis=-1, keepdims=True)


def _route_kernel(x_ref, g_ref, sc_ref, sh_ref, wr_ref, br_ref, h_ref, rec_ref, cnt_ref, cnt_scr):
    @pl.when(pl.program_id(0) == 0)
    def _():
        cnt_scr[...] = jnp.zeros_like(cnt_scr)

    h = _modulated_norm(x_ref[...], g_ref[...], sc_ref[0], sh_ref[0])
    half = h.shape[1] // 2
    bits = pltpu.bitcast(h.astype(BF16).astype(F32), jnp.uint32)
    h_ref[...] = (bits[:, half:] & jnp.uint32(0xFFFF0000)) | (bits[:, :half] >> 16)

    h_hi = h.astype(BF16)
    h_lo = (h - h_hi.astype(F32)).astype(BF16)
    wr = wr_ref[...]
    w_hi = wr.astype(BF16)
    w_lo = (wr - w_hi.astype(F32)).astype(BF16)
    logits = _dot(h_hi, w_hi) + _dot(h_hi, w_lo) + _dot(h_lo, w_hi) + br_ref[...]
    lane = lax.broadcasted_iota(I32, logits.shape, 1).astype(F32)
    lg = jnp.where(lane < N_EXPERTS, logits, -jnp.inf)
    m1 = jnp.max(lg, axis=-1, keepdims=True)
    i1 = jnp.min(jnp.where(lg == m1, lane, float(LANES)), axis=-1, keepdims=True)
    lg2 = jnp.where(lane == i1, -jnp.inf, lg)
    m2 = jnp.max(lg2, axis=-1, keepdims=True)
    i2 = jnp.min(jnp.where(lg2 == m2, lane, float(LANES)), axis=-1, keepdims=True)
    e2 = jnp.exp(m2 - m1)
    w_top = 1.0 / (1.0 + e2)

    chosen = jnp.where(lane == i1, 1.0, 0.0) + jnp.where(lane == i2, 1.0, 0.0)
    tm = chosen.shape[0]
    earlier = jnp.where(lax.broadcasted_iota(I32, (tm, tm), 1) < lax.broadcasted_iota(I32, (tm, tm), 0),
                        1.0, 0.0).astype(BF16)
    rank = _dot(earlier, chosen.astype(BF16)) + cnt_scr[0:1, :]
    rec = jnp.zeros_like(logits)
    for slot, val in ((R_E1, i1), (R_E2, i2), (R_RANK1, _lane_pick(rank, lane, i1)),
                      (R_RANK2, _lane_pick(rank, lane, i2)), (R_W1, w_top), (R_W2, e2 * w_top)):
        rec = jnp.where(lane == float(slot), val, rec)
    rec_ref[...] = rec
    cnt_scr[0:1, :] = cnt_scr[0:1, :] + jnp.sum(chosen, axis=0, keepdims=True)
    cnt_ref[...] = cnt_scr[...]


def _route(x2, g, mod, wr_pad, br_pad, s_len):
    t, d = x2.shape
    tm = MOE_TM
    per_b = s_len // tm
    return pl.pallas_call(
        _route_kernel,
        grid=(t // tm,),
        in_specs=[pl.BlockSpec((tm, d), lambda i: (i, 0)),
                  pl.BlockSpec((1, d), lambda i: (0, 0)),
                  pl.BlockSpec((1, 1, d), lambda i: (i // per_b, 0, 4)),
                  pl.BlockSpec((1, 1, d), lambda i: (i // per_b, 0, 3)),
                  pl.BlockSpec((d, LANES), lambda i: (0, 0)),
                  pl.BlockSpec((1, LANES), lambda i: (0, 0))],
        out_specs=[pl.BlockSpec((tm, d // 2), lambda i: (i, 0)),
                   pl.BlockSpec((tm, LANES), lambda i: (i, 0)),
                   pl.BlockSpec((8, LANES), lambda i: (0, 0))],
        out_shape=[jax.ShapeDtypeStruct((t, d // 2), jnp.uint32),
                   jax.ShapeDtypeStruct((t, LANES), F32),
                   jax.ShapeDtypeStruct((8, LANES), F32)],
        scratch_shapes=[pltpu.VMEM((8, LANES), F32)],
        compiler_params=_cparams("arbitrary"),
        name="moe_route",
    )(x2, g, mod, mod, wr_pad, br_pad)


def _row_copy(src_ref, src_row, dst_ref, dst_row, sem):
    return pltpu.make_async_copy(src_ref.at[pl.ds(src_row, 1)], dst_ref.at[pl.ds(dst_row, 1)], sem)


def _dispatch_kernel(pos_ref, h_ref, xs_in_ref, xs_ref, sem):
    del xs_in_ref
    tm = h_ref.shape[0]

    def start(t, carry):
        _row_copy(h_ref, t, xs_ref, pos_ref[0, 0, t], sem).start()
        _row_copy(h_ref, t, xs_ref, pos_ref[0, 0, tm + t], sem).start()
        return carry

    def wait(t, carry):
        _row_copy(h_ref, 0, xs_ref, 0, sem).wait()
        _row_copy(h_ref, 0, xs_ref, 0, sem).wait()
        return carry

    lax.fori_loop(0, tm, start, 0)
    lax.fori_loop(0, tm, wait, 0)


def _dispatch(pos, h_packed, n_rows):
    t, w = h_packed.shape
    tm = MOE_TM
    xs0 = jnp.zeros((n_rows, w), h_packed.dtype)
    return pl.pallas_call(
        _dispatch_kernel,
        grid=(t // tm,),
        in_specs=[pl.BlockSpec((1, 1, 2 * tm), lambda i: (i, 0, 0), memory_space=pltpu.SMEM),
                  pl.BlockSpec((tm, w), lambda i: (i, 0)),
                  pl.BlockSpec(memory_space=pl.ANY)],
        out_specs=pl.BlockSpec(memory_space=pl.ANY),
        out_shape=jax.ShapeDtypeStruct((n_rows, w), h_packed.dtype),
        scratch_shapes=[pltpu.SemaphoreType.DMA(())],
        input_output_aliases={2: 0},
        compiler_params=_cparams("arbitrary"),
        name="moe_dispatch",
    )(pos, h_packed, xs0)


def _expert_kernel(te_ref, nu_ref, xs_ref, w1_ref, w3_ref, w2_ref, o_ref, h_scr, acc_scr):
    i = pl.program_id(0)
    f = pl.program_id(1)

    @pl.when(i < nu_ref[0])
    def _():
        @pl.when(f == 0)
        def _():
            word = xs_ref[...]
            lo = pltpu.bitcast(word << 16, F32)
            hi = pltpu.bitcast(word & jnp.uint32(0xFFFF0000), F32)
            h_scr[...] = jnp.concatenate([lo, hi], axis=1).astype(BF16)
            acc_scr[...] = jnp.zeros_like(acc_scr)

        acc_scr[...] += _swiglu_partial(h_scr[...], w1_ref[0], w3_ref[0], w2_ref[0])

        @pl.when(f == pl.num_programs(1) - 1)
        def _():
            o_ref[...] = acc_scr[...]

    @pl.when((i >= nu_ref[0]) & (f == pl.num_programs(1) - 1))
    def _():
        o_ref[...] = jnp.zeros_like(o_ref)


def _experts(tile_expert, n_used, xs, w1, w3, w2):
    n_rows, half = xs.shape
    d = 2 * half
    d_ff = w1.shape[2]
    tf = 896
    nf = d_ff // tf
    tile = lambda i, nu: jnp.minimum(i, nu[0] - 1)
    ff = lambda i, f, nu: jnp.where(i < nu[0], f, nf - 1)
    return pl.pallas_call(
        _expert_kernel,
        grid_spec=pltpu.PrefetchScalarGridSpec(
            num_scalar_prefetch=2,
            grid=(n_rows // MOE_ROWS, nf),
            in_specs=[pl.BlockSpec((MOE_ROWS, half), lambda i, f, te, nu: (tile(i, nu), 0)),
                      pl.BlockSpec((1, d, tf), lambda i, f, te, nu: (te[i], 0, ff(i, f, nu))),
                      pl.BlockSpec((1, d, tf), lambda i, f, te, nu: (te[i], 0, ff(i, f, nu))),
                      pl.BlockSpec((1, tf, d), lambda i, f, te, nu: (te[i], ff(i, f, nu), 0))],
            out_specs=pl.BlockSpec((MOE_ROWS, d), lambda i, f, te, nu: (i, 0)),
            scratch_shapes=[pltpu.VMEM((MOE_ROWS, d), BF16), pltpu.VMEM((MOE_ROWS, d), F32)]),
        out_shape=jax.ShapeDtypeStruct((n_rows, d), F32),
        compiler_params=_cparams("arbitrary", "arbitrary"),
        name="moe_experts",
    )(tile_expert, n_used, xs, w1, w3, w2)


def _combine_kernel(pos_ref, x_ref, gate_ref, rec_ref, fg_ref, ys_ref, o_ref, y_scr, sem, *, final):
    tm = x_ref.shape[0]

    def start(t, carry):
        _row_copy(ys_ref, pos_ref[0, 0, t], y_scr.at[0], t, sem).start()
        _row_copy(ys_ref, pos_ref[0, 0, tm + t], y_scr.at[1], t, sem).start()
        return carry

    def wait(t, carry):
        _row_copy(ys_ref, 0, y_scr.at[0], 0, sem).wait()
        _row_copy(ys_ref, 0, y_scr.at[0], 0, sem).wait()
        return carry

    lax.fori_loop(0, tm, start, 0)
    lax.fori_loop(0, tm, wait, 0)
    rec = rec_ref[...]
    lane = lax.broadcasted_iota(I32, rec.shape, 1)
    w_first = jnp.sum(jnp.where(lane == R_W1, rec, 0.0), axis=-1, keepdims=True)
    w_second = jnp.sum(jnp.where(lane == R_W2, rec, 0.0), axis=-1, keepdims=True)
    f = w_first * y_scr[0] + w_second * y_scr[1]
    o_ref[...] = _finish(x_ref[...], gate_ref[0], f, fg_ref, final)


def _combine(pos, x2, mod, rec, final_g, ys, s_len, final):
    t, d = x2.shape
    tm = MOE_TM
    per_b = s_len // tm
    kern = functools.partial(_combine_kernel, final=final)
    return pl.pallas_call(
        kern,
        grid=(t // tm,),
        in_specs=[pl.BlockSpec((1, 1, 2 * tm), lambda i: (i, 0, 0), memory_space=pltpu.SMEM),
                  pl.BlockSpec((tm, d), lambda i: (i, 0)),
                  pl.BlockSpec((1, 1, d), lambda i: (i // per_b, 0, 5)),
                  pl.BlockSpec((tm, LANES), lambda i: (i, 0)),
                  pl.BlockSpec((1, d), lambda i: (0, 0)),
                  pl.BlockSpec(memory_space=pl.ANY)],
        out_specs=pl.BlockSpec((tm, d), lambda i: (i, 0)),
        out_shape=jax.ShapeDtypeStruct((t, d), F32),
        scratch_shapes=[pltpu.VMEM((2, tm, d), F32), pltpu.SemaphoreType.DMA(())],
        compiler_params=_cparams("arbitrary"),
        name="moe_combine",
    )(pos, x2, mod, rec, final_g, ys)


def _moe(x2, g, mod, wr_pad, br_pad, w1, w3, w2, final_g, s_len, final):
    t, d = x2.shape
    n_e = w1.shape[0]
    h_packed, rec, cnt = _route(x2, g, mod, wr_pad, br_pad, s_len)

    n_tiles = 2 * t // MOE_ROWS + n_e
    counts = cnt[0, :n_e].astype(I32)
    padded = (counts + MOE_ROWS - 1) // MOE_ROWS * MOE_ROWS
    ends = jnp.cumsum(padded)
    starts = ends - padded
    e1, e2 = rec[:, R_E1].astype(I32), rec[:, R_E2].astype(I32)
    pos1 = starts[e1] + rec[:, R_RANK1].astype(I32)
    pos2 = starts[e2] + rec[:, R_RANK2].astype(I32)
    pos = jnp.concatenate([pos1.reshape(-1, 1, MOE_TM), pos2.reshape(-1, 1, MOE_TM)], axis=2)
    n_used = (ends[-1] // MOE_ROWS).reshape(1)
    tile_start = jnp.minimum(jnp.arange(n_tiles, dtype=I32), n_used[0] - 1) * MOE_ROWS
    tile_expert = jnp.sum(tile_start[:, None] >= ends[None, :], axis=1).astype(I32)

    xs = _dispatch(pos, h_packed, n_tiles * MOE_ROWS)
    ys = _experts(tile_expert, n_used, xs, w1, w3, w2)
    return _combine(pos, x2, mod, rec, final_g, ys, s_len, final)


def _pack_w_in(w):
    d = w.shape[0]
    sizes = (W_A, W_A, W_A, IDX_HEADS * IDX_DIM, IDX_DIM, IDX_HEADS,
             512, 512, 512, 512, 512, 512, 3 * d)
    offs = [0]
    for s in sizes:
        offs.append(offs[-1] + s)
    (qa, ka, va, qi, ki, wi, qb, kb, vb, qc, kc, vc, gl) = [
        w[:, offs[j]:offs[j + 1]] for j in range(len(sizes))]
    scale = HEAD_DIM ** -0.5
    pad_wi = jnp.zeros((d, LANES - IDX_HEADS), w.dtype)
    packed = jnp.concatenate(
        [gl, qa * scale, ka, va,
         qi * (IDX_DIM ** -0.5), ki, ki, wi * (IDX_HEADS ** -0.5), pad_wi,
         qb * scale, kb, vb, qc * scale, kc, vc], axis=1)
    assert packed.shape[1] == PACKED
    return packed.astype(BF16)


def kernel(x, c, w_ada, b_ada, norm1_g, norm2_g, w_in, w_br, w_out, rel_bias, lam_params,
           subln_g, ffn_w1, ffn_w3, ffn_w2, router_w, router_b, moe_w1, moe_w3, moe_w2, final_g):
    bsz, s_len, d = x.shape
    depth = w_ada.shape[0]
    assert s_len % 1024 == 0 and d == 1024 and OFF_G + 3 * d == OFF_QA

    c_pad = jnp.concatenate([c, jnp.zeros((8 - bsz % 8 if bsz % 8 else 0, d), c.dtype)], axis=0)
    mod_all = _ada(c_pad, w_ada, b_ada)
    dsa_tiles, diff_tiles = _bias_tiles(rel_bias)
    fg = final_g.reshape(1, d)

    x2 = x.reshape(bsz * s_len, d)
    for l in range(depth):
        mod = mod_all[l, :bsz].reshape(bsz, 1, 6 * d)
        proj = _inproj(x2, norm1_g[l].reshape(1, d), mod, _pack_w_in(w_in[l]), s_len)
        proj3 = proj.reshape(bsz, s_len, PACKED)
        lam_init = 0.8 - 0.6 * math.exp(-0.3 * l)
        o_a = _dsa(proj3, dsa_tiles, bsz, s_len)
        o_b = _diff(proj3, diff_tiles, lam_params, subln_g, l, bsz, s_len, lam_init)
        o_c = _stick(proj3, bsz, s_len)
        x2 = _merge(o_a.reshape(-1, o_a.shape[-1]), o_b.reshape(-1, o_b.shape[-1]),
                    o_c.reshape(-1, o_c.shape[-1]), proj, x2, mod,
                    w_br[l].astype(BF16), w_out[l].astype(BF16), s_len)
        g2 = norm2_g[l].reshape(1, d)
        final = l == depth - 1
        j = l // 2
        if l % 2 == 0:
            x2 = _ffn(x2, g2, mod, ffn_w1[j].astype(BF16), ffn_w3[j].astype(BF16),
                      ffn_w2[j].astype(BF16), fg, s_len, final)
        else:
            wr_pad = jnp.pad(router_w[j], ((0, 0), (0, LANES - N_EXPERTS)))
            br_pad = jnp.pad(router_b[j], (0, LANES - N_EXPERTS)).reshape(1, LANES)
            x2 = _moe(x2, g2, mod, wr_pad, br_pad, moe_w1[j].astype(BF16), moe_w3[j].astype(BF16),
                      moe_w2[j].astype(BF16), fg, s_len, final)
    return x2.reshape(bsz, s_len, d)
```

```python
import functools
import math

import jax
import jax.numpy as jnp
from jax import lax
from jax.experimental import pallas as pl
from jax.experimental.pallas import tpu as pltpu

F32 = jnp.float32
BF16 = jnp.bfloat16
I32 = jnp.int32

LANES = 128
VMEM_LIMIT_BYTES = 56 * 1024 * 1024

CHUNK = 64
A_HEADS = 8
IDX_HEADS = 4
IDX_DIM = 64
TOPK_MAX = 256
B_HEADS = 4
C_HEADS = 8
HEAD_DIM = 64
REL_BUCKETS = 32
FAR_BUCKET = REL_BUCKETS // 2 - 1
N_EXPERTS = 8
EPS = 1e-6
SUBLN_EPS = 1e-5
NEG = -1e30
LOG2E = 1.4426950408889634
INT_MIN = -(2 ** 31)

KEY_BLOCK = 256
Q_TILE = 256
GROUP = 4

W_A = A_HEADS * HEAD_DIM
OFF_G = 0
OFF_QA, OFF_KA, OFF_VA = 3072, 3584, 4096
OFF_QI, OFF_KI, OFF_WI = 4608, 4864, 4992
OFF_QB, OFF_KB, OFF_VB = 5120, 5632, 6144
OFF_QC, OFF_KC, OFF_VC = 6656, 7168, 7680
PACKED = 8192

LOG_BUCKET_STEPS = (12, 16, 23, 32, 46, 64, 91)


def _nt_dot(a, b):
    return lax.dot_general(a, b, (((1,), (1,)), ((), ())), preferred_element_type=F32)


def _dot(a, b):
    return jnp.dot(a, b, preferred_element_type=F32)


def _cparams(*sem):
    return pltpu.CompilerParams(dimension_semantics=sem, vmem_limit_bytes=VMEM_LIMIT_BYTES)


def _split_heads(x):
    lane = lax.broadcasted_iota(I32, x.shape, 1)
    keep_a = jnp.where(lane < HEAD_DIM, 1.0, 0.0).astype(x.dtype)
    keep_b = jnp.where(lane < HEAD_DIM, 0.0, 1.0).astype(x.dtype)
    return jnp.concatenate([x * keep_a, x * keep_b], axis=0)


def _merge_heads(o, m):
    lane = lax.broadcasted_iota(I32, (m, LANES), 1)
    return jnp.where(lane < HEAD_DIM, o[:m], o[m:])


def _wide(x):
    return jnp.concatenate([x, x], axis=1)


def _tall(x):
    return jnp.concatenate([x, x], axis=0)


def _key_rows(kb):
    return pl.ds(pl.multiple_of(kb * KEY_BLOCK, KEY_BLOCK), KEY_BLOCK)


def _sortable(x):
    bits = pltpu.bitcast(x, I32)
    return bits ^ ((bits >> 31) & 0x7FFFFFFF)


def _chunk_causal():
    r = lax.broadcasted_iota(I32, (Q_TILE, KEY_BLOCK), 0)
    c = lax.broadcasted_iota(I32, (Q_TILE, KEY_BLOCK), 1)
    return (c // CHUNK) <= (r // CHUNK)


def _ada_kernel(c_ref, w_ref, b_ref, o_ref):
    c = c_ref[...]
    a = c * jax.nn.sigmoid(c)
    o_ref[0] = jnp.dot(a, w_ref[0], preferred_element_type=F32,
                       precision=lax.Precision.HIGHEST) + b_ref[0]


def _ada(c_pad, w_ada, b_ada):
    depth, d, n = w_ada.shape
    tn = 1024
    return pl.pallas_call(
        _ada_kernel,
        grid=(depth, n // tn),
        in_specs=[pl.BlockSpec(c_pad.shape, lambda l, j: (0, 0)),
                  pl.BlockSpec((1, d, tn), lambda l, j: (l, 0, j)),
                  pl.BlockSpec((1, 1, tn), lambda l, j: (l, 0, j))],
        out_specs=pl.BlockSpec((1, c_pad.shape[0], tn), lambda l, j: (l, 0, j)),
        out_shape=jax.ShapeDtypeStruct((depth, c_pad.shape[0], n), F32),
        compiler_params=_cparams("parallel", "parallel"),
        name="ada",
    )(c_pad, w_ada, b_ada.reshape(depth, 1, n))


def _rel_bias_tile(tab_ref, head, d0, n_heads_total):
    r = lax.broadcasted_iota(I32, (Q_TILE, KEY_BLOCK), 0)
    c = lax.broadcasted_iota(I32, (Q_TILE, KEY_BLOCK), 1)
    d = c - r + d0
    n = jnp.abs(d)
    large = jnp.full(d.shape, REL_BUCKETS // 4, I32)
    for step in LOG_BUCKET_STEPS:
        large = large + jnp.where(n >= step, 1, 0)
    bucket = jnp.where(d > 0, REL_BUCKETS // 2, 0) + jnp.where(n < REL_BUCKETS // 4, n, large)
    out = jnp.zeros(d.shape, F32)
    for b in range(REL_BUCKETS):
        out = jnp.where(bucket == b, tab_ref[b * n_heads_total + head], out)
    return out - tab_ref[FAR_BUCKET * n_heads_total + head]


def _bias_kernel(tab_ref, dsa_ref, diff_ref):
    p = pl.program_id(0)
    n_heads = A_HEADS + B_HEADS
    for t, d0 in enumerate((-KEY_BLOCK, 0)):
        dsa_ref[0, t, 0:Q_TILE, :] = _rel_bias_tile(tab_ref, 2 * p, d0, n_heads)
        dsa_ref[0, t, Q_TILE:2 * Q_TILE, :] = _rel_bias_tile(tab_ref, 2 * p + 1, d0, n_heads)
        tile = _rel_bias_tile(tab_ref, A_HEADS + p, d0, n_heads)
        if t == 1:
            tile = jnp.where(_chunk_causal(), tile, NEG)
        diff_ref[0, t] = tile


def _bias_tiles(rel_bias):
    tab = rel_bias.reshape(-1)
    return pl.pallas_call(
        _bias_kernel,
        grid=(4,),
        in_specs=[pl.BlockSpec(memory_space=pltpu.SMEM)],
        out_specs=[pl.BlockSpec((1, 2, 2 * Q_TILE, KEY_BLOCK), lambda p: (p, 0, 0, 0)),
                   pl.BlockSpec((1, 2, Q_TILE, KEY_BLOCK), lambda p: (p, 0, 0, 0))],
        out_shape=[jax.ShapeDtypeStruct((A_HEADS // 2, 2, 2 * Q_TILE, KEY_BLOCK), F32),
                   jax.ShapeDtypeStruct((B_HEADS, 2, Q_TILE, KEY_BLOCK), F32)],
        compiler_params=_cparams("parallel"),
        name="rel_bias_tiles",
    )(tab)


def _modulated_norm(x, g, sc, sh):
    y = x * lax.rsqrt(jnp.mean(x * x, axis=-1, keepdims=True) + EPS)
    return y * g * (1.0 + sc) + sh


def _inproj_kernel(x_ref, g_ref, sc_ref, sh_ref, w_ref, o_ref, h_scr):
    @pl.when(pl.program_id(1) == 0)
    def _():
        h_scr[...] = _modulated_norm(x_ref[...], g_ref[...], sc_ref[0], sh_ref[0]).astype(BF16)

    o_ref[...] = _dot(h_scr[...], w_ref[...]).astype(o_ref.dtype)


def _inproj(x2, g, mod, w_packed, s_len):
    t, d = x2.shape
    n = w_packed.shape[1]
    tm, tn = 1024, 1024
    per_b = s_len // tm
    return pl.pallas_call(
        _inproj_kernel,
        grid=(t // tm, n // tn),
        in_specs=[pl.BlockSpec((tm, d), lambda i, j: (i, 0)),
                  pl.BlockSpec((1, d), lambda i, j: (0, 0)),
                  pl.BlockSpec((1, 1, d), lambda i, j: (i // per_b, 0, 1)),
                  pl.BlockSpec((1, 1, d), lambda i, j: (i // per_b, 0, 0)),
                  pl.BlockSpec((d, tn), lambda i, j: (0, j))],
        out_specs=pl.BlockSpec((tm, tn), lambda i, j: (i, j)),
        out_shape=jax.ShapeDtypeStruct((t, n), BF16),
        scratch_shapes=[pltpu.VMEM((tm, d), BF16)],
        compiler_params=_cparams("parallel", "arbitrary"),
        name="inproj",
    )(x2, g, mod, mod, w_packed)


def _softmax_pv(qs, k_at, v_at, qb, add_far, add_near, s_scr, mx_scr, l_scr, acc_scr):
    n = qb + 1
    groups = n // GROUP
    rest = n - groups * GROUP
    grouped = groups > 0
    far_singles = jnp.where(grouped, rest, jnp.maximum(qb - 1, 0))
    near_singles_lo = jnp.where(grouped, n, jnp.maximum(qb - 1, 0))

    def pass1(blocks):
        mx = None
        for kb, add in blocks:
            s = add(kb, _nt_dot(qs, k_at(kb))) * LOG2E
            s_scr[kb] = s
            fold = jnp.maximum(s[:, :LANES], s[:, LANES:])
            mx = fold if mx is None else jnp.maximum(mx, fold)
        mx_scr[...] = jnp.maximum(mx_scr[...], mx)

    def loop(lo, hi, fn):
        def body(i, carry):
            fn(i)
            return carry
        lax.fori_loop(lo, hi, body, 0)

    mx_scr[...] = jnp.full(mx_scr.shape, NEG, F32)
    loop(0, far_singles, lambda kb: pass1([(kb, add_far)]))
    loop(0, groups - 1,
         lambda g: pass1([(rest + g * GROUP + u, add_far) for u in range(GROUP)]))

    @pl.when(grouped)
    def _():
        pass1([(qb - u, add_far) for u in range(GROUP - 1, 1, -1)]
              + [(qb - 1, lambda kb, s: add_near(kb, s, 0)), (qb, lambda kb, s: add_near(kb, s, 1))])

    loop(near_singles_lo, n, lambda kb: pass1([(kb, lambda kb, s: add_near(kb, s, kb - (qb - 1)))]))

    m = jnp.max(mx_scr[...], axis=-1, keepdims=True)
    mx_scr[...] = jnp.broadcast_to(m, mx_scr.shape)
    l_scr[...] = jnp.zeros_like(l_scr)
    acc_scr[...] = jnp.zeros_like(acc_scr)

    def pass2(kbs):
        mw = _wide(mx_scr[...])
        l_add = acc_add = None
        for kb in kbs:
            p = jnp.exp2(s_scr[kb] - mw)
            fold = p[:, :LANES] + p[:, LANES:]
            pv = _dot(p.astype(BF16), v_at(kb))
            l_add = fold if l_add is None else l_add + fold
            acc_add = pv if acc_add is None else acc_add + pv
        l_scr[...] += l_add
        acc_scr[...] += acc_add

    loop(0, rest, lambda kb: pass2([kb]))
    loop(0, groups, lambda g: pass2([rest + g * GROUP + u for u in range(GROUP)]))
    return acc_scr[...] / jnp.sum(l_scr[...], axis=-1, keepdims=True)


def _softmax_scratch(s_len):
    return [pltpu.VMEM((s_len // KEY_BLOCK, 2 * Q_TILE, KEY_BLOCK), F32),
            pltpu.VMEM((2 * Q_TILE, LANES), F32),
            pltpu.VMEM((2 * Q_TILE, LANES), F32),
            pltpu.VMEM((2 * Q_TILE, LANES), F32)]


def _dsa_kernel(qa_ref, qi_ref, wi_ref, ka_ref, va_ref, ki_ref, bias_ref, o_ref,
                key_scr, keyt_scr, neg_scr, w_scr, last_scr, s_scr, mx_scr, l_scr, acc_scr,
                *, topk, idx_bits, s_len):
    qb = pl.program_id(1)

    @pl.when(pl.program_id(2) == 0)
    def _():
        _dsa_select(qb, qi_ref, wi_ref, ki_ref, key_scr, keyt_scr, neg_scr, w_scr, last_scr,
                    topk=topk, idx_bits=idx_bits, s_len=s_len)

    def add_far(kb, s):
        return s + _tall(neg_scr[kb])

    pr = pl.program_id(2)
    lanes = pl.ds(pl.multiple_of(pr * LANES, LANES), LANES)

    def add_near(kb, s, t):
        return s + _tall(neg_scr[kb]) + bias_ref[pr, t]

    o = _softmax_pv(_split_heads(qa_ref[0]),
                    lambda kb: ka_ref[0, _key_rows(kb), lanes],
                    lambda kb: va_ref[0, _key_rows(kb), lanes],
                    qb, add_far, add_near, s_scr, mx_scr, l_scr, acc_scr)
    o_ref[0] = _merge_heads(o, Q_TILE).astype(o_ref.dtype)


def _dsa_select(qb, qi_ref, wi_ref, ki_ref, key_scr, keyt_scr, neg_scr, w_scr, last_scr,
                *, topk, idx_bits, s_len):
    nkb = qb + 1
    kf = float(topk)
    valid_diag = _chunk_causal()

    def cols(kb):
        return kb * KEY_BLOCK + lax.broadcasted_iota(I32, (Q_TILE, KEY_BLOCK), 1)

    qi = qi_ref[0]
    q_heads = [_split_heads(qi[:, pr * LANES:(pr + 1) * LANES]) for pr in range(IDX_HEADS // 2)]
    wi = wi_ref[0].astype(F32)
    for h in range(IDX_HEADS):
        w_scr[h] = jnp.broadcast_to(wi[:, h:h + 1], (Q_TILE, LANES))

    def score_block(kb, valid):
        kk = ki_ref[0, _key_rows(kb), :]
        sc = jnp.zeros((Q_TILE, KEY_BLOCK), F32)
        for pr in range(IDX_HEADS // 2):
            d = _nt_dot(q_heads[pr], kk)
            sc = sc + _wide(w_scr[2 * pr]) * jnp.maximum(d[:Q_TILE], 0.0)
            sc = sc + _wide(w_scr[2 * pr + 1]) * jnp.maximum(d[Q_TILE:], 0.0)
        sc = jnp.where(sc == 0.0, 0.0, sc)
        if valid is not None:
            sc = jnp.where(valid, sc, -jnp.inf)
        key_scr[kb] = _sortable(sc)
        keyt_scr[kb] = _sortable(sc.T)

    def score_body(kb, carry):
        score_block(kb, None)
        return carry

    lax.fori_loop(0, qb, score_body, 0)
    score_block(qb, valid_diag)

    def per_query(x):
        return jnp.broadcast_to(x[None], (KEY_BLOCK // 8, 8, Q_TILE)).reshape(KEY_BLOCK, Q_TILE)

    def key_index(kb):
        return kb * KEY_BLOCK + lax.broadcasted_iota(I32, (KEY_BLOCK, Q_TILE), 0)

    def count(pred):
        def one(kb, acc):
            hit = jnp.where(pred(keyt_scr[kb], kb), 1.0, 0.0)
            return acc + jnp.sum(hit.reshape(KEY_BLOCK // 8, 8, Q_TILE), axis=0)

        def group(g, acc):
            for u in range(GROUP):
                acc = one(g * GROUP + u, acc)
            return acc

        acc = lax.fori_loop(0, nkb // GROUP, group, jnp.zeros((8, Q_TILE), F32))
        acc = lax.fori_loop(nkb // GROUP * GROUP, nkb, one, acc)
        return jnp.broadcast_to(jnp.sum(acc, axis=0, keepdims=True), (8, Q_TILE))

    thr = jnp.where(count(lambda key, kb: key >= 0) >= kf,
                    jnp.zeros((8, Q_TILE), I32), jnp.full((8, Q_TILE), INT_MIN, I32))

    def thr_body(j, thr):
        cand = thr | lax.shift_left(jnp.int32(1), 30 - j)
        cw = per_query(cand)
        return jnp.where(count(lambda key, kb: key >= cw) >= kf, cand, thr)

    thr = lax.fori_loop(0, 31, thr_body, thr)
    thr_t = per_query(thr)

    need = kf - count(lambda key, kb: key > thr_t)
    ties = count(lambda key, kb: key == thr_t)
    last_scr[...] = jnp.full(last_scr.shape, s_len, I32)

    @pl.when(jnp.max(ties - need) > 0.0)
    def _():
        def tie_body(j, last):
            cand = last | lax.shift_left(jnp.int32(1), idx_bits - 1 - j)
            cw = per_query(cand)
            taken = count(lambda key, kb: (key == thr_t) & (key_index(kb) < cw))
            return jnp.where(taken < need, cand, last)

        last_scr[...] = lax.fori_loop(0, idx_bits, tie_body, jnp.zeros((8, Q_TILE), I32))

    def per_row(x):
        return _wide(jnp.broadcast_to(x[0:1], (LANES, Q_TILE)).T)

    tw = per_row(thr)
    lastw = per_row(last_scr[...])

    def mask_block(kb, valid):
        key = key_scr[kb]
        tie_neg = jnp.where(key == tw, jnp.where(cols(kb) <= lastw, 0.0, NEG), NEG)
        neg = jnp.where(key > tw, 0.0, tie_neg)
        if valid is not None:
            neg = jnp.where(valid, neg, NEG)
        neg_scr[kb] = neg

    def mask_body(kb, carry):
        mask_block(kb, None)
        return carry

    lax.fori_loop(0, qb, mask_body, 0)
    mask_block(qb, valid_diag)


def _dsa(proj, bias_tiles, bsz, s_len):
    topk = min(TOPK_MAX, s_len // 4)
    idx_bits = max(1, (s_len - 1).bit_length())
    n_blocks = s_len // KEY_BLOCK
    assert 2 * n_blocks <= 256
    kern = functools.partial(_dsa_kernel, topk=topk, idx_bits=idx_bits, s_len=s_len)
    lane_blk = lambda off: off // LANES
    return pl.pallas_call(
        kern,
        grid=(bsz, s_len // Q_TILE, A_HEADS // 2),
        in_specs=[pl.BlockSpec((1, Q_TILE, LANES), lambda b, i, p: (b, i, lane_blk(OFF_QA) + p)),
                  pl.BlockSpec((1, Q_TILE, 256), lambda b, i, p: (b, i, OFF_QI // 256)),
                  pl.BlockSpec((1, Q_TILE, LANES), lambda b, i, p: (b, i, lane_blk(OFF_WI))),
                  pl.BlockSpec((1, s_len, W_A), lambda b, i, p: (b, 0, OFF_KA // W_A),
                               pipeline_mode=pl.Buffered(1)),
                  pl.BlockSpec((1, s_len, W_A), lambda b, i, p: (b, 0, OFF_VA // W_A),
                               pipeline_mode=pl.Buffered(1)),
                  pl.BlockSpec((1, s_len, LANES), lambda b, i, p: (b, 0, lane_blk(OFF_KI))),
                  pl.BlockSpec(bias_tiles.shape, lambda b, i, p: (0, 0, 0, 0),
                               pipeline_mode=pl.Buffered(1))],
        out_specs=pl.BlockSpec((1, Q_TILE, LANES), lambda b, i, p: (b, i, p)),
        out_shape=jax.ShapeDtypeStruct((bsz, s_len, W_A), BF16),
        scratch_shapes=[pltpu.VMEM((n_blocks, Q_TILE, KEY_BLOCK), I32),
                        pltpu.VMEM((n_blocks, KEY_BLOCK, Q_TILE), I32),
                        pltpu.VMEM((n_blocks, Q_TILE, KEY_BLOCK), F32),
                        pltpu.VMEM((IDX_HEADS, Q_TILE, LANES), F32),
                        pltpu.VMEM((8, Q_TILE), I32)] + _softmax_scratch(s_len),
        compiler_params=_cparams("parallel", "arbitrary", "arbitrary"),
        name="dsa",
    )(proj, proj, proj, proj, proj, proj, bias_tiles)


def _diff_kernel(q_ref, k_ref, v_ref, bias_ref, lp_ref, g_ref, o_ref,
                 s_scr, mx_scr, l_scr, acc_scr, *, lam_init):
    qb = pl.program_id(2)
    qs = _split_heads(q_ref[0])

    def add_near(kb, s, t):
        return s + _tall(bias_ref[0, t])

    o = _softmax_pv(qs, lambda kb: k_ref[0, _key_rows(kb), :], lambda kb: v_ref[0, _key_rows(kb), :],
                    qb, lambda kb, s: s, add_near, s_scr, mx_scr, l_scr, acc_scr)
    lp = lp_ref[0]
    lam = (jnp.exp(jnp.sum(lp[0:1] * lp[1:2], axis=-1, keepdims=True))
           - jnp.exp(jnp.sum(lp[2:3] * lp[3:4], axis=-1, keepdims=True)) + lam_init)
    o = o[:Q_TILE] - lam * o[Q_TILE:]
    y = o * lax.rsqrt(jnp.mean(o * o, axis=-1, keepdims=True) + SUBLN_EPS)
    o_ref[0] = (y * g_ref[0] * (1.0 - lam_init)).astype(o_ref.dtype)


def _diff(proj, bias_tiles, lam_params, subln_g, layer, bsz, s_len, lam_init):
    kern = functools.partial(_diff_kernel, lam_init=lam_init)
    return pl.pallas_call(
        kern,
        grid=(bsz, B_HEADS, s_len // Q_TILE),
        in_specs=[pl.BlockSpec((1, Q_TILE, LANES), lambda b, h, i: (b, i, OFF_QB // LANES + h)),
                  pl.BlockSpec((1, s_len, LANES), lambda b, h, i: (b, 0, OFF_KB // LANES + h)),
                  pl.BlockSpec((1, s_len, LANES), lambda b, h, i: (b, 0, OFF_VB // LANES + h)),
                  pl.BlockSpec((1, 2, Q_TILE, KEY_BLOCK), lambda b, h, i: (h, 0, 0, 0)),
                  pl.BlockSpec((1, 4, HEAD_DIM), lambda b, h, i: (layer, 0, 0)),
                  pl.BlockSpec((1, 1, 2 * HEAD_DIM), lambda b, h, i: (layer, 0, 0))],
        out_specs=pl.BlockSpec((1, Q_TILE, LANES), lambda b, h, i: (b, i, h)),
        out_shape=jax.ShapeDtypeStruct((bsz, s_len, B_HEADS * 2 * HEAD_DIM), BF16),
        scratch_shapes=_softmax_scratch(s_len),
        compiler_params=_cparams("parallel", "parallel", "arbitrary"),
        name="diff_attn",
    )(proj, proj, proj, bias_tiles, lam_params, subln_g.reshape(-1, 1, 2 * HEAD_DIM))


def _stick_kernel(q_ref, k_ref, v_ref, o_ref, later_scr, run_scr, acc_scr):
    qb = pl.program_id(2)
    qs = _split_heads(q_ref[0])
    jj = lax.broadcasted_iota(I32, (2 * KEY_BLOCK, KEY_BLOCK), 0) & (KEY_BLOCK - 1)
    ss = lax.broadcasted_iota(I32, (2 * KEY_BLOCK, KEY_BLOCK), 1)
    later_scr[...] = jnp.where(jj > ss, 1.0, 0.0).astype(BF16)
    run_scr[...] = jnp.zeros_like(run_scr)
    acc_scr[...] = jnp.zeros_like(acc_scr)

    def sweep(kbs, diagonal):
        parts = []
        for kb in kbs:
            z = _nt_dot(qs, k_ref[0, _key_rows(kb), :])
            soft = jnp.log(1.0 + jnp.exp2(jnp.abs(z) * (-LOG2E)))
            log_beta = jnp.minimum(z, 0.0) - soft
            log_1mb = log_beta - z
            strict = None
            if diagonal:
                t = lax.broadcasted_iota(I32, z.shape, 0) & (Q_TILE - 1)
                strict = lax.broadcasted_iota(I32, z.shape, 1) < t
                log_1mb = jnp.where(strict, log_1mb, 0.0)
            hi = log_1mb.astype(BF16)
            lo = (log_1mb - hi.astype(F32)).astype(BF16)
            between = _dot(jnp.concatenate([hi, lo], axis=1), later_scr[...])
            total = between[:, 0:1] + log_1mb[:, 0:1]
            parts.append((kb, log_beta + between, total, strict))
        run = run_scr[...]
        acc = None
        for kb, logit, total, strict in parts:
            w = jnp.exp(logit + _wide(run))
            if strict is not None:
                w = jnp.where(strict, w, 0.0)
            pv = _dot(w.astype(BF16), v_ref[0, _key_rows(kb), :])
            acc = pv if acc is None else acc + pv
            run = run + jnp.broadcast_to(total, run.shape)
        acc_scr[...] += acc
        run_scr[...] = run

    sweep([qb], True)
    groups = qb // GROUP

    def group_body(g, carry):
        top = qb - 1 - g * GROUP
        sweep([top - u for u in range(GROUP)], False)
        return carry

    def single_body(j, carry):
        sweep([qb - 1 - groups * GROUP - j], False)
        return carry

    lax.fori_loop(0, groups, group_body, 0)
    lax.fori_loop(0, qb - groups * GROUP, single_body, 0)
    o_ref[0] = _merge_heads(acc_scr[...], Q_TILE).astype(o_ref.dtype)


def _stick(proj, bsz, s_len):
    return pl.pallas_call(
        _stick_kernel,
        grid=(bsz, C_HEADS // 2, s_len // Q_TILE),
        in_specs=[pl.BlockSpec((1, Q_TILE, LANES), lambda b, h, i: (b, i, OFF_QC // LANES + h)),
                  pl.BlockSpec((1, s_len, LANES), lambda b, h, i: (b, 0, OFF_KC // LANES + h)),
                  pl.BlockSpec((1, s_len, LANES), lambda b, h, i: (b, 0, OFF_VC // LANES + h))],
        out_specs=pl.BlockSpec((1, Q_TILE, LANES), lambda b, h, i: (b, i, h)),
        out_shape=jax.ShapeDtypeStruct((bsz, s_len, C_HEADS * HEAD_DIM), BF16),
        scratch_shapes=[pltpu.VMEM((2 * KEY_BLOCK, KEY_BLOCK), BF16),
                        pltpu.VMEM((2 * Q_TILE, LANES), F32),
                        pltpu.VMEM((2 * Q_TILE, LANES), F32)],
        compiler_params=_cparams("parallel", "parallel", "arbitrary"),
        name="stick_attn",
    )(proj, proj, proj)


def _merge_kernel(oa_ref, ob_ref, oc_ref, gate_ref, x_ref, g1_ref, wbr_ref, wout_ref, o_ref):
    d = x_ref.shape[-1]
    merged = jnp.zeros(x_ref.shape, F32)
    for j, o_br in enumerate((oa_ref, ob_ref, oc_ref)):
        w = o_br.shape[-1]
        gate = jax.nn.sigmoid(gate_ref[:, j * d:(j + 1) * d].astype(F32))
        merged = merged + gate * _dot(o_br[...], wbr_ref[j * w:(j + 1) * w, :])
    o_ref[...] = x_ref[...] + g1_ref[0] * _dot(merged.astype(BF16), wout_ref[...])


def _merge(o_a, o_b, o_c, proj, x2, mod, w_br, w_out, s_len):
    t, d = x2.shape
    tm = 512
    per_b = s_len // tm
    w = o_a.shape[-1]
    return pl.pallas_call(
        _merge_kernel,
        grid=(t // tm,),
        in_specs=[pl.BlockSpec((tm, w), lambda i: (i, 0)),
                  pl.BlockSpec((tm, w), lambda i: (i, 0)),
                  pl.BlockSpec((tm, w), lambda i: (i, 0)),
                  pl.BlockSpec((tm, 3 * d), lambda i: (i, OFF_G // (3 * d))),
                  pl.BlockSpec((tm, d), lambda i: (i, 0)),
                  pl.BlockSpec((1, 1, d), lambda i: (i // per_b, 0, 2)),
                  pl.BlockSpec(w_br.shape, lambda i: (0, 0)),
                  pl.BlockSpec(w_out.shape, lambda i: (0, 0))],
        out_specs=pl.BlockSpec((tm, d), lambda i: (i, 0)),
        out_shape=jax.ShapeDtypeStruct((t, d), F32),
        compiler_params=_cparams("parallel"),
        name="merge",
    )(o_a, o_b, o_c, proj, x2, mod, w_br, w_out)


def _finish(x, gate, acc, fg_ref, final):
    y = x + gate * acc
    if final:
        y = y * lax.rsqrt(jnp.mean(y * y, axis=-1, keepdims=True) + EPS) * fg_ref[...]
    return y


def _swiglu_partial(h, w1, w3, w2):
    a = _dot(h, w1)
    act = (a * jax.nn.sigmoid(a)) * _dot(h, w3)
    return _dot(act.astype(BF16), w2)


def _ffn_kernel(x_ref, g_ref, sc_ref, sh_ref, gate_ref, w1_ref, w3_ref, w2_ref, fg_ref, o_ref,
                h_scr, acc_scr, *, final):
    f = pl.program_id(1)

    @pl.when(f == 0)
    def _():
        h_scr[...] = _modulated_norm(x_ref[...], g_ref[...], sc_ref[0], sh_ref[0]).astype(BF16)
        acc_scr[...] = jnp.zeros_like(acc_scr)

    acc_scr[...] += _swiglu_partial(h_scr[...], w1_ref[...], w3_ref[...], w2_ref[...])

    @pl.when(f == pl.num_programs(1) - 1)
    def _():
        o_ref[...] = _finish(x_ref[...], gate_ref[0], acc_scr[...], fg_ref, final)


def _ffn(x2, g, mod, w1, w3, w2, final_g, s_len, final):
    t, d = x2.shape
    d_ff = w1.shape[1]
    tm, tf = 1024, 256
    per_b = s_len // tm
    kern = functools.partial(_ffn_kernel, final=final)
    return pl.pallas_call(
        kern,
        grid=(t // tm, d_ff // tf),
        in_specs=[pl.BlockSpec((tm, d), lambda i, f: (i, 0)),
                  pl.BlockSpec((1, d), lambda i, f: (0, 0)),
                  pl.BlockSpec((1, 1, d), lambda i, f: (i // per_b, 0, 4)),
                  pl.BlockSpec((1, 1, d), lambda i, f: (i // per_b, 0, 3)),
                  pl.BlockSpec((1, 1, d), lambda i, f: (i // per_b, 0, 5)),
                  pl.BlockSpec((d, tf), lambda i, f: (0, f)),
                  pl.BlockSpec((d, tf), lambda i, f: (0, f)),
                  pl.BlockSpec((tf, d), lambda i, f: (f, 0)),
                  pl.BlockSpec((1, d), lambda i, f: (0, 0))],
        out_specs=pl.BlockSpec((tm, d), lambda i, f: (i, 0)),
        out_shape=jax.ShapeDtypeStruct((t, d), F32),
        scratch_shapes=[pltpu.VMEM((tm, d), BF16), pltpu.VMEM((tm, d), F32)],
        compiler_params=_cparams("parallel", "arbitrary"),
        name="ffn",
    )(x2, g, mod, mod, mod, w1, w3, w2, final_g)


MOE_TM = 512
MOE_ROWS = 512
R_E1, R_E2, R_RANK1, R_RANK2, R_W1, R_W2 = 0, 1, 2, 3, 4, 5


def _lane_pick(tile, lane, idx):
    return jnp.sum(jnp.where(lane == idx, tile, 0.0), axis=-1, keepdims=True)


def _route_kernel(x_ref, g_ref, sc_ref, sh_ref, wr_ref, br_ref, h_ref, rec_ref, cnt_ref, cnt_scr):
    @pl.when(pl.program_id(0) == 0)
    def _():
        cnt_scr[...] = jnp.zeros_like(cnt_scr)

    h = _modulated_norm(x_ref[...], g_ref[...], sc_ref[0], sh_ref[0])
    half = h.shape[1] // 2
    bits = pltpu.bitcast(h.astype(BF16).astype(F32), jnp.uint32)
    h_ref[...] = (bits[:, half:] & jnp.uint32(0xFFFF0000)) | (bits[:, :half] >> 16)

    h_hi = h.astype(BF16)
    h_lo = (h - h_hi.astype(F32)).astype(BF16)
    wr = wr_ref[...]
    w_hi = wr.astype(BF16)
    w_lo = (wr - w_hi.astype(F32)).astype(BF16)
    logits = _dot(h_hi, w_hi) + _dot(h_hi, w_lo) + _dot(h_lo, w_hi) + br_ref[...]
    lane = lax.broadcasted_iota(I32, logits.shape, 1).astype(F32)
    lg = jnp.where(lane < N_EXPERTS, logits, -jnp.inf)
    m1 = jnp.max(lg, axis=-1, keepdims=True)
    i1 = jnp.min(jnp.where(lg == m1, lane, float(LANES)), axis=-1, keepdims=True)
    lg2 = jnp.where(lane == i1, -jnp.inf, lg)
    m2 = jnp.max(lg2, axis=-1, keepdims=True)
    i2 = jnp.min(jnp.where(lg2 == m2, lane, float(LANES)), axis=-1, keepdims=True)
    e2 = jnp.exp(m2 - m1)
    w_top = 1.0 / (1.0 + e2)

    chosen = jnp.where(lane == i1, 1.0, 0.0) + jnp.where(lane == i2, 1.0, 0.0)
    tm = chosen.shape[0]
    earlier = jnp.where(lax.broadcasted_iota(I32, (tm, tm), 1) < lax.broadcasted_iota(I32, (tm, tm), 0),
                        1.0, 0.0).astype(BF16)
    rank = _dot(earlier, chosen.astype(BF16)) + cnt_scr[0:1, :]
    rec = jnp.zeros_like(logits)
    for slot, val in ((R_E1, i1), (R_E2, i2), (R_RANK1, _lane_pick(rank, lane, i1)),
                      (R_RANK2, _lane_pick(rank, lane, i2)), (R_W1, w_top), (R_W2, e2 * w_top)):
        rec = jnp.where(lane == float(slot), val, rec)
    rec_ref[...] = rec
    cnt_scr[0:1, :] = cnt_scr[0:1, :] + jnp.sum(chosen, axis=0, keepdims=True)
    cnt_ref[...] = cnt_scr[...]


def _route(x2, g, mod, wr_pad, br_pad, s_len):
    t, d = x2.shape
    tm = MOE_TM
    per_b = s_len // tm
    return pl.pallas_call(
        _route_kernel,
        grid=(t // tm,),
        in_specs=[pl.BlockSpec((tm, d), lambda i: (i, 0)),
                  pl.BlockSpec((1, d), lambda i: (0, 0)),
                  pl.BlockSpec((1, 1, d), lambda i: (i // per_b, 0, 4)),
                  pl.BlockSpec((1, 1, d), lambda i: (i // per_b, 0, 3)),
                  pl.BlockSpec((d, LANES), lambda i: (0, 0)),
                  pl.BlockSpec((1, LANES), lambda i: (0, 0))],
        out_specs=[pl.BlockSpec((tm, d // 2), lambda i: (i, 0)),
                   pl.BlockSpec((tm, LANES), lambda i: (i, 0)),
                   pl.BlockSpec((8, LANES), lambda i: (0, 0))],
        out_shape=[jax.ShapeDtypeStruct((t, d // 2), jnp.uint32),
                   jax.ShapeDtypeStruct((t, LANES), F32),
                   jax.ShapeDtypeStruct((8, LANES), F32)],
        scratch_shapes=[pltpu.VMEM((8, LANES), F32)],
        compiler_params=_cparams("arbitrary"),
        name="moe_route",
    )(x2, g, mod, mod, wr_pad, br_pad)


def _row_copy(src_ref, src_row, dst_ref, dst_row, sem):
    return pltpu.make_async_copy(src_ref.at[pl.ds(src_row, 1)], dst_ref.at[pl.ds(dst_row, 1)], sem)


def _dispatch_kernel(pos_ref, h_ref, xs_in_ref, xs_ref, sem):
    del xs_in_ref
    tm = h_ref.shape[0]

    def start(t, carry):
        _row_copy(h_ref, t, xs_ref, pos_ref[0, 0, t], sem).start(priority=0)
        _row_copy(h_ref, t, xs_ref, pos_ref[0, 0, tm + t], sem).start(priority=1)
        return carry

    lax.fori_loop(0, tm, start, 0)
    for _ in range(2):
        pltpu.make_async_copy(h_ref, xs_ref.at[pl.ds(0, tm)], sem).wait()


def _dispatch(pos, h_packed, n_rows):
    t, w = h_packed.shape
    tm = MOE_TM
    xs0 = jnp.zeros((n_rows, w), h_packed.dtype)
    return pl.pallas_call(
        _dispatch_kernel,
        grid=(t // tm,),
        in_specs=[pl.BlockSpec((1, 1, 2 * tm), lambda i: (i, 0, 0), memory_space=pltpu.SMEM),
                  pl.BlockSpec((tm, w), lambda i: (i, 0)),
                  pl.BlockSpec(memory_space=pl.ANY)],
        out_specs=pl.BlockSpec(memory_space=pl.ANY),
        out_shape=jax.ShapeDtypeStruct((n_rows, w), h_packed.dtype),
        scratch_shapes=[pltpu.SemaphoreType.DMA(())],
        input_output_aliases={2: 0},
        compiler_params=_cparams("arbitrary"),
        name="moe_dispatch",
    )(pos, h_packed, xs0)


def _expert_kernel(te_ref, nu_ref, xs_ref, w1_ref, w3_ref, w2_ref, o_ref, h_scr, acc_scr):
    i = pl.program_id(0)
    f = pl.program_id(1)

    @pl.when(i < nu_ref[0])
    def _():
        @pl.when(f == 0)
        def _():
            word = xs_ref[...]
            lo = pltpu.bitcast(word << 16, F32)
            hi = pltpu.bitcast(word & jnp.uint32(0xFFFF0000), F32)
            h_scr[...] = jnp.concatenate([lo, hi], axis=1).astype(BF16)
            acc_scr[...] = jnp.zeros_like(acc_scr)

        acc_scr[...] += _swiglu_partial(h_scr[...], w1_ref[0], w3_ref[0], w2_ref[0])

        @pl.when(f == pl.num_programs(1) - 1)
        def _():
            o_ref[...] = acc_scr[...]

    @pl.when((i >= nu_ref[0]) & (f == pl.num_programs(1) - 1))
    def _():
        o_ref[...] = jnp.zeros_like(o_ref)


def _experts(tile_expert, n_used, xs, w1, w3, w2):
    n_rows, half = xs.shape
    d = 2 * half
    d_ff = w1.shape[2]
    tf = 896
    nf = d_ff // tf
    tile = lambda i, nu: jnp.minimum(i, nu[0] - 1)
    ff = lambda i, f, nu: jnp.where(i < nu[0], f, nf - 1)
    return pl.pallas_call(
        _expert_kernel,
        grid_spec=pltpu.PrefetchScalarGridSpec(
            num_scalar_prefetch=2,
            grid=(n_rows // MOE_ROWS, nf),
            in_specs=[pl.BlockSpec((MOE_ROWS, half), lambda i, f, te, nu: (tile(i, nu), 0)),
                      pl.BlockSpec((1, d, tf), lambda i, f, te, nu: (te[i], 0, ff(i, f, nu))),
                      pl.BlockSpec((1, d, tf), lambda i, f, te, nu: (te[i], 0, ff(i, f, nu))),
                      pl.BlockSpec((1, tf, d), lambda i, f, te, nu: (te[i], ff(i, f, nu), 0))],
            out_specs=pl.BlockSpec((MOE_ROWS, d), lambda i, f, te, nu: (i, 0)),
            scratch_shapes=[pltpu.VMEM((MOE_ROWS, d), BF16), pltpu.VMEM((MOE_ROWS, d), F32)]),
        out_shape=jax.ShapeDtypeStruct((n_rows, d), F32),
        compiler_params=_cparams("arbitrary", "arbitrary"),
        name="moe_experts",
    )(tile_expert, n_used, xs, w1, w3, w2)


def _combine_kernel(pos_ref, x_ref, gate_ref, rec_ref, fg_ref, ys_ref, o_ref, y_scr, sem, *, final):
    tm = x_ref.shape[0]

    def start(t, carry):
        _row_copy(ys_ref, pos_ref[0, 0, t], y_scr.at[0], t, sem).start(priority=0)
        _row_copy(ys_ref, pos_ref[0, 0, tm + t], y_scr.at[1], t, sem).start(priority=1)
        return carry

    lax.fori_loop(0, tm, start, 0)
    for slot in range(2):
        pltpu.make_async_copy(ys_ref.at[pl.ds(0, tm)], y_scr.at[slot], sem).wait()
    rec = rec_ref[...]
    lane = lax.broadcasted_iota(I32, rec.shape, 1)
    w_first = jnp.sum(jnp.where(lane == R_W1, rec, 0.0), axis=-1, keepdims=True)
    w_second = jnp.sum(jnp.where(lane == R_W2, rec, 0.0), axis=-1, keepdims=True)
    f = w_first * y_scr[0] + w_second * y_scr[1]
    o_ref[...] = _finish(x_ref[...], gate_ref[0], f, fg_ref, final)


def _combine(pos, x2, mod, rec, final_g, ys, s_len, final):
    t, d = x2.shape
    tm = MOE_TM
    per_b = s_len // tm
    kern = functools.partial(_combine_kernel, final=final)
    return pl.pallas_call(
        kern,
        grid=(t // tm,),
        in_specs=[pl.BlockSpec((1, 1, 2 * tm), lambda i: (i, 0, 0), memory_space=pltpu.SMEM),
                  pl.BlockSpec((tm, d), lambda i: (i, 0)),
                  pl.BlockSpec((1, 1, d), lambda i: (i // per_b, 0, 5)),
                  pl.BlockSpec((tm, LANES), lambda i: (i, 0)),
                  pl.BlockSpec((1, d), lambda i: (0, 0)),
                  pl.BlockSpec(memory_space=pl.ANY)],
        out_specs=pl.BlockSpec((tm, d), lambda i: (i, 0)),
        out_shape=jax.ShapeDtypeStruct((t, d), F32),
        scratch_shapes=[pltpu.VMEM((2, tm, d), F32), pltpu.SemaphoreType.DMA(())],
        compiler_params=_cparams("arbitrary"),
        name="moe_combine",
    )(pos, x2, mod, rec, final_g, ys)


def _moe(x2, g, mod, wr_pad, br_pad, w1, w3, w2, final_g, s_len, final):
    t, d = x2.shape
    n_e = w1.shape[0]
    h_packed, rec, cnt = _route(x2, g, mod, wr_pad, br_pad, s_len)

    n_tiles = 2 * t // MOE_ROWS + n_e
    counts = cnt[0, :n_e].astype(I32)
    padded = (counts + MOE_ROWS - 1) // MOE_ROWS * MOE_ROWS
    ends = jnp.cumsum(padded)
    starts = ends - padded
    e1, e2 = rec[:, R_E1].astype(I32), rec[:, R_E2].astype(I32)
    pos1 = starts[e1] + rec[:, R_RANK1].astype(I32)
    pos2 = starts[e2] + rec[:, R_RANK2].astype(I32)
    pos = jnp.concatenate([pos1.reshape(-1, 1, MOE_TM), pos2.reshape(-1, 1, MOE_TM)], axis=2)
    n_used = (ends[-1] // MOE_ROWS).reshape(1)
    tile_start = jnp.minimum(jnp.arange(n_tiles, dtype=I32), n_used[0] - 1) * MOE_ROWS
    tile_expert = jnp.sum(tile_start[:, None] >= ends[None, :], axis=1).astype(I32)

    xs = _dispatch(pos, h_packed, n_tiles * MOE_ROWS)
    ys = _experts(tile_expert, n_used, xs, w1, w3, w2)
    return _combine(pos, x2, mod, rec, final_g, ys, s_len, final)


def _pack_w_in(w):
    d = w.shape[0]
    sizes = (W_A, W_A, W_A, IDX_HEADS * IDX_DIM, IDX_DIM, IDX_HEADS,
             512, 512, 512, 512, 512, 512, 3 * d)
    offs = [0]
    for s in sizes:
        offs.append(offs[-1] + s)
    (qa, ka, va, qi, ki, wi, qb, kb, vb, qc, kc, vc, gl) = [
        w[:, offs[j]:offs[j + 1]] for j in range(len(sizes))]
    scale = HEAD_DIM ** -0.5
    pad_wi = jnp.zeros((d, LANES - IDX_HEADS), w.dtype)
    packed = jnp.concatenate(
        [gl, qa * scale, ka, va,
         qi * (IDX_DIM ** -0.5), ki, ki, wi * (IDX_HEADS ** -0.5), pad_wi,
         qb * scale, kb, vb, qc * scale, kc, vc], axis=1)
    assert packed.shape[1] == PACKED
    return packed.astype(BF16)


def kernel(x, c, w_ada, b_ada, norm1_g, norm2_g, w_in, w_br, w_out, rel_bias, lam_params,
           subln_g, ffn_w1, ffn_w3, ffn_w2, router_w, router_b, moe_w1, moe_w3, moe_w2, final_g):
    bsz, s_len, d = x.shape
    depth = w_ada.shape[0]
    assert s_len % 1024 == 0 and d == 1024 and OFF_G + 3 * d == OFF_QA

    c_pad = jnp.concatenate([c, jnp.zeros((8 - bsz % 8 if bsz % 8 else 0, d), c.dtype)], axis=0)
    mod_all = _ada(c_pad, w_ada, b_ada)
    dsa_tiles, diff_tiles = _bias_tiles(rel_bias)
    fg = final_g.reshape(1, d)

    x2 = x.reshape(bsz * s_len, d)
    for l in range(depth):
        mod = mod_all[l, :bsz].reshape(bsz, 1, 6 * d)
        proj = _inproj(x2, norm1_g[l].reshape(1, d), mod, _pack_w_in(w_in[l]), s_len)
        proj3 = proj.reshape(bsz, s_len, PACKED)
        lam_init = 0.8 - 0.6 * math.exp(-0.3 * l)
        o_a = _dsa(proj3, dsa_tiles, bsz, s_len)
        o_b = _diff(proj3, diff_tiles, lam_params, subln_g, l, bsz, s_len, lam_init)
        o_c = _stick(proj3, bsz, s_len)
        x2 = _merge(o_a.reshape(-1, o_a.shape[-1]), o_b.reshape(-1, o_b.shape[-1]),
                    o_c.reshape(-1, o_c.shape[-1]), proj, x2, mod,
                    w_br[l].astype(BF16), w_out[l].astype(BF16), s_len)
        g2 = norm2_g[l].reshape(1, d)
        final = l == depth - 1
        j = l // 2
        if l % 2 == 0:
            x2 = _ffn(x2, g2, mod, ffn_w1[j].astype(BF16), ffn_w3[j].astype(BF16),
                      ffn_w2[j].astype(BF16), fg, s_len, final)
        else:
            wr_pad = jnp.pad(router_w[j], ((0, 0), (0, LANES - N_EXPERTS)))
            br_pad = jnp.pad(router_b[j], (0, LANES - N_EXPERTS)).reshape(1, LANES)
            x2 = _moe(x2, g2, mod, wr_pad, br_pad, moe_w1[j].astype(BF16), moe_w3[j].astype(BF16),
                      moe_w2[j].astype(BF16), fg, s_len, final)
    return x2.reshape(bsz, s_len, d)
```

```python
import functools
import math

import jax
import jax.numpy as jnp
from jax import lax
from jax.experimental import pallas as pl
from jax.experimental.pallas import tpu as pltpu

F32 = jnp.float32
BF16 = jnp.bfloat16
I32 = jnp.int32

LANES = 128
VMEM_LIMIT_BYTES = 56 * 1024 * 1024

CHUNK = 64
A_HEADS = 8
IDX_HEADS = 4
IDX_DIM = 64
TOPK_MAX = 256
B_HEADS = 4
C_HEADS = 8
HEAD_DIM = 64
REL_BUCKETS = 32
FAR_BUCKET = REL_BUCKETS // 2 - 1
N_EXPERTS = 8
EPS = 1e-6
SUBLN_EPS = 1e-5
NEG = -1e30
LOG2E = 1.4426950408889634
INT_MIN = -(2 ** 31)

KEY_BLOCK = 256
Q_TILE = 256
GROUP = 4
BLOCK_GROUPS = (8, 4, 2, 1)

W_A = A_HEADS * HEAD_DIM
OFF_G = 0
OFF_QA, OFF_KA, OFF_VA = 3072, 3584, 4096
OFF_QI, OFF_KI, OFF_WI = 4608, 4864, 4992
OFF_QB, OFF_KB, OFF_VB = 5120, 5632, 6144
OFF_QC, OFF_KC, OFF_VC = 6656, 7168, 7680
PACKED = 8192

LOG_BUCKET_STEPS = (12, 16, 23, 32, 46, 64, 91)


def _nt_dot(a, b):
    return lax.dot_general(a, b, (((1,), (1,)), ((), ())), preferred_element_type=F32)


def _dot(a, b):
    return jnp.dot(a, b, preferred_element_type=F32)


def _cparams(*sem):
    return pltpu.CompilerParams(dimension_semantics=sem, vmem_limit_bytes=VMEM_LIMIT_BYTES)


def _split_heads(x):
    lane = lax.broadcasted_iota(I32, x.shape, 1)
    keep_a = jnp.where(lane < HEAD_DIM, 1.0, 0.0).astype(x.dtype)
    keep_b = jnp.where(lane < HEAD_DIM, 0.0, 1.0).astype(x.dtype)
    return jnp.concatenate([x * keep_a, x * keep_b], axis=0)


def _merge_heads(o, m):
    lane = lax.broadcasted_iota(I32, (m, LANES), 1)
    return jnp.where(lane < HEAD_DIM, o[:m], o[m:])


def _wide(x):
    return jnp.concatenate([x, x], axis=1)


def _tall(x):
    return jnp.concatenate([x, x], axis=0)


def _key_rows(kb):
    return pl.ds(pl.multiple_of(kb * KEY_BLOCK, KEY_BLOCK), KEY_BLOCK)


def _sortable(x):
    bits = pltpu.bitcast(x, I32)
    return bits ^ ((bits >> 31) & 0x7FFFFFFF)


def _chunk_causal():
    r = lax.broadcasted_iota(I32, (Q_TILE, KEY_BLOCK), 0)
    c = lax.broadcasted_iota(I32, (Q_TILE, KEY_BLOCK), 1)
    return (c // CHUNK) <= (r // CHUNK)


def _ada_kernel(c_ref, w_ref, b_ref, o_ref):
    c = c_ref[...]
    a = c * jax.nn.sigmoid(c)
    o_ref[0] = jnp.dot(a, w_ref[0], preferred_element_type=F32,
                       precision=lax.Precision.HIGHEST) + b_ref[0]


def _ada(c_pad, w_ada, b_ada):
    depth, d, n = w_ada.shape
    tn = 1024
    return pl.pallas_call(
        _ada_kernel,
        grid=(depth, n // tn),
        in_specs=[pl.BlockSpec(c_pad.shape, lambda l, j: (0, 0)),
                  pl.BlockSpec((1, d, tn), lambda l, j: (l, 0, j)),
                  pl.BlockSpec((1, 1, tn), lambda l, j: (l, 0, j))],
        out_specs=pl.BlockSpec((1, c_pad.shape[0], tn), lambda l, j: (l, 0, j)),
        out_shape=jax.ShapeDtypeStruct((depth, c_pad.shape[0], n), F32),
        compiler_params=_cparams("parallel", "parallel"),
        name="ada",
    )(c_pad, w_ada, b_ada.reshape(depth, 1, n))


def _rel_bias_tile(tab_ref, head, d0, n_heads_total):
    r = lax.broadcasted_iota(I32, (Q_TILE, KEY_BLOCK), 0)
    c = lax.broadcasted_iota(I32, (Q_TILE, KEY_BLOCK), 1)
    d = c - r + d0
    n = jnp.abs(d)
    large = jnp.full(d.shape, REL_BUCKETS // 4, I32)
    for step in LOG_BUCKET_STEPS:
        large = large + jnp.where(n >= step, 1, 0)
    bucket = jnp.where(d > 0, REL_BUCKETS // 2, 0) + jnp.where(n < REL_BUCKETS // 4, n, large)
    out = jnp.zeros(d.shape, F32)
    for b in range(REL_BUCKETS):
        out = jnp.where(bucket == b, tab_ref[b * n_heads_total + head], out)
    return out - tab_ref[FAR_BUCKET * n_heads_total + head]


def _bias_kernel(tab_ref, dsa_ref, diff_ref):
    p = pl.program_id(0)
    n_heads = A_HEADS + B_HEADS
    for t, d0 in enumerate((-KEY_BLOCK, 0)):
        dsa_ref[0, t, 0:Q_TILE, :] = _rel_bias_tile(tab_ref, 2 * p, d0, n_heads)
        dsa_ref[0, t, Q_TILE:2 * Q_TILE, :] = _rel_bias_tile(tab_ref, 2 * p + 1, d0, n_heads)
        tile = _rel_bias_tile(tab_ref, A_HEADS + p, d0, n_heads)
        if t == 1:
            tile = jnp.where(_chunk_causal(), tile, NEG)
        diff_ref[0, t] = tile


def _bias_tiles(rel_bias):
    tab = rel_bias.reshape(-1)
    return pl.pallas_call(
        _bias_kernel,
        grid=(4,),
        in_specs=[pl.BlockSpec(memory_space=pltpu.SMEM)],
        out_specs=[pl.BlockSpec((1, 2, 2 * Q_TILE, KEY_BLOCK), lambda p: (p, 0, 0, 0)),
                   pl.BlockSpec((1, 2, Q_TILE, KEY_BLOCK), lambda p: (p, 0, 0, 0))],
        out_shape=[jax.ShapeDtypeStruct((A_HEADS // 2, 2, 2 * Q_TILE, KEY_BLOCK), F32),
                   jax.ShapeDtypeStruct((B_HEADS, 2, Q_TILE, KEY_BLOCK), F32)],
        compiler_params=_cparams("parallel"),
        name="rel_bias_tiles",
    )(tab)


def _modulated_norm(x, g, sc, sh):
    y = x * lax.rsqrt(jnp.mean(x * x, axis=-1, keepdims=True) + EPS)
    return y * g * (1.0 + sc) + sh


def _inproj_kernel(x_ref, g_ref, sc_ref, sh_ref, w_ref, o_ref, h_scr):
    @pl.when(pl.program_id(1) == 0)
    def _():
        h_scr[...] = _modulated_norm(x_ref[...], g_ref[...], sc_ref[0], sh_ref[0]).astype(BF16)

    o_ref[...] = _dot(h_scr[...], w_ref[...]).astype(o_ref.dtype)


def _inproj(x2, g, mod, w_packed, s_len):
    t, d = x2.shape
    n = w_packed.shape[1]
    tm, tn = 1024, 1024
    per_b = s_len // tm
    return pl.pallas_call(
        _inproj_kernel,
        grid=(t // tm, n // tn),
        in_specs=[pl.BlockSpec((tm, d), lambda i, j: (i, 0)),
                  pl.BlockSpec((1, d), lambda i, j: (0, 0)),
                  pl.BlockSpec((1, 1, d), lambda i, j: (i // per_b, 0, 1)),
                  pl.BlockSpec((1, 1, d), lambda i, j: (i // per_b, 0, 0)),
                  pl.BlockSpec((d, tn), lambda i, j: (0, j))],
        out_specs=pl.BlockSpec((tm, tn), lambda i, j: (i, j)),
        out_shape=jax.ShapeDtypeStruct((t, n), BF16),
        scratch_shapes=[pltpu.VMEM((tm, d), BF16)],
        compiler_params=_cparams("parallel", "arbitrary"),
        name="inproj",
    )(x2, g, mod, mod, w_packed)


def _for_blocks(lo, hi, fn, sizes=BLOCK_GROUPS):
    pos = lo
    for size in sizes:
        trips = jnp.maximum(hi - pos, 0) // size

        def body(i, carry, pos=pos, size=size):
            fn([pos + i * size + u for u in range(size)])
            return carry

        lax.fori_loop(0, trips, body, 0)
        pos = pos + trips * size


def _softmax_pv(qs, k_at, v_at, qb, add_far, add_near, s_scr, mx_scr, l_scr, acc_scr):
    def pass1(blocks):
        mx = None
        for kb, add in blocks:
            s = add(kb, _nt_dot(qs, k_at(kb))) * LOG2E
            s_scr[kb] = s
            fold = jnp.maximum(s[:, :LANES], s[:, LANES:])
            mx = fold if mx is None else jnp.maximum(mx, fold)
        mx_scr[...] = jnp.maximum(mx_scr[...], mx)

    diagonal = (qb, lambda kb, s: add_near(kb, s, 1))
    mx_scr[...] = jnp.full(mx_scr.shape, NEG, F32)
    _for_blocks(0, qb - 1, lambda kbs: pass1([(kb, add_far) for kb in kbs]))

    @pl.when(qb > 0)
    def _():
        pass1([(qb - 1, lambda kb, s: add_near(kb, s, 0)), diagonal])

    @pl.when(qb == 0)
    def _():
        pass1([diagonal])

    m = jnp.max(mx_scr[...], axis=-1, keepdims=True)
    mx_scr[...] = jnp.broadcast_to(m, mx_scr.shape)
    l_scr[...] = jnp.zeros_like(l_scr)
    acc_scr[...] = jnp.zeros_like(acc_scr)

    def pass2(kbs):
        mw = _wide(mx_scr[...])
        l_add = acc_add = None
        for kb in kbs:
            p = jnp.exp2(s_scr[kb] - mw)
            fold = p[:, :LANES] + p[:, LANES:]
            pv = _dot(p.astype(BF16), v_at(kb))
            l_add = fold if l_add is None else l_add + fold
            acc_add = pv if acc_add is None else acc_add + pv
        l_scr[...] += l_add
        acc_scr[...] += acc_add

    _for_blocks(0, qb + 1, pass2)
    return acc_scr[...] / jnp.sum(l_scr[...], axis=-1, keepdims=True)


def _softmax_scratch(s_len):
    return [pltpu.VMEM((s_len // KEY_BLOCK, 2 * Q_TILE, KEY_BLOCK), F32),
            pltpu.VMEM((2 * Q_TILE, LANES), F32),
            pltpu.VMEM((2 * Q_TILE, LANES), F32),
            pltpu.VMEM((2 * Q_TILE, LANES), F32)]


def _dsa_kernel(qa_ref, qi_ref, wi_ref, ka_ref, va_ref, ki_ref, bias_ref, o_ref,
                key_scr, keyt_scr, neg_scr, w_scr, last_scr, s_scr, mx_scr, l_scr, acc_scr,
                *, topk, idx_bits, s_len):
    qb = pl.program_id(1)

    @pl.when(pl.program_id(2) == 0)
    def _():
        _dsa_select(qb, qi_ref, wi_ref, ki_ref, key_scr, keyt_scr, neg_scr, w_scr, last_scr,
                    topk=topk, idx_bits=idx_bits, s_len=s_len)

    def add_far(kb, s):
        return s + _tall(neg_scr[kb])

    pr = pl.program_id(2)
    lanes = pl.ds(pl.multiple_of(pr * LANES, LANES), LANES)

    def add_near(kb, s, t):
        return s + _tall(neg_scr[kb]) + bias_ref[pr, t]

    o = _softmax_pv(_split_heads(qa_ref[0]),
                    lambda kb: ka_ref[0, _key_rows(kb), lanes],
                    lambda kb: va_ref[0, _key_rows(kb), lanes],
                    qb, add_far, add_near, s_scr, mx_scr, l_scr, acc_scr)
    o_ref[0] = _merge_heads(o, Q_TILE).astype(o_ref.dtype)


def _dsa_select(qb, qi_ref, wi_ref, ki_ref, key_scr, keyt_scr, neg_scr, w_scr, last_scr,
                *, topk, idx_bits, s_len):
    nkb = qb + 1
    kf = float(topk)
    valid_diag = _chunk_causal()

    def cols(kb):
        return kb * KEY_BLOCK + lax.broadcasted_iota(I32, (Q_TILE, KEY_BLOCK), 1)

    qi = qi_ref[0]
    q_heads = [_split_heads(qi[:, pr * LANES:(pr + 1) * LANES]) for pr in range(IDX_HEADS // 2)]
    wi = wi_ref[0].astype(F32)
    for h in range(IDX_HEADS):
        w_scr[h] = jnp.broadcast_to(wi[:, h:h + 1], (Q_TILE, LANES))

    def score_block(kb, valid):
        kk = ki_ref[0, _key_rows(kb), :]
        sc = jnp.zeros((Q_TILE, KEY_BLOCK), F32)
        for pr in range(IDX_HEADS // 2):
            d = _nt_dot(q_heads[pr], kk)
            sc = sc + _wide(w_scr[2 * pr]) * jnp.maximum(d[:Q_TILE], 0.0)
            sc = sc + _wide(w_scr[2 * pr + 1]) * jnp.maximum(d[Q_TILE:], 0.0)
        sc = jnp.where(sc == 0.0, 0.0, sc)
        if valid is not None:
            sc = jnp.where(valid, sc, -jnp.inf)
        key_scr[kb] = _sortable(sc)
        keyt_scr[kb] = _sortable(sc.T)

    _for_blocks(0, qb, lambda kbs: [score_block(kb, None) for kb in kbs], sizes=(2, 1))
    score_block(qb, valid_diag)

    def per_query(x):
        return jnp.broadcast_to(x[None], (KEY_BLOCK // 8, 8, Q_TILE)).reshape(KEY_BLOCK, Q_TILE)

    def key_index(kb):
        return kb * KEY_BLOCK + lax.broadcasted_iota(I32, (KEY_BLOCK, Q_TILE), 0)

    def count(pred):
        def one(kb, acc):
            hit = jnp.where(pred(keyt_scr[kb], kb), 1.0, 0.0)
            return acc + jnp.sum(hit.reshape(KEY_BLOCK // 8, 8, Q_TILE), axis=0)

        def group(g, acc):
            for u in range(GROUP):
                acc = one(g * GROUP + u, acc)
            return acc

        acc = lax.fori_loop(0, nkb // GROUP, group, jnp.zeros((8, Q_TILE), F32))
        acc = lax.fori_loop(nkb // GROUP * GROUP, nkb, one, acc)
        return jnp.broadcast_to(jnp.sum(acc, axis=0, keepdims=True), (8, Q_TILE))

    thr = jnp.where(count(lambda key, kb: key >= 0) >= kf,
                    jnp.zeros((8, Q_TILE), I32), jnp.full((8, Q_TILE), INT_MIN, I32))

    def thr_body(j, thr):
        cand = thr | lax.shift_left(jnp.int32(1), 30 - j)
        cw = per_query(cand)
        return jnp.where(count(lambda key, kb: key >= cw) >= kf, cand, thr)

    thr = lax.fori_loop(0, 31, thr_body, thr)
    thr_t = per_query(thr)

    need = kf - count(lambda key, kb: key > thr_t)
    ties = count(lambda key, kb: key == thr_t)
    last_scr[...] = jnp.full(last_scr.shape, s_len, I32)

    @pl.when(jnp.max(ties - need) > 0.0)
    def _():
        def tie_body(j, last):
            cand = last | lax.shift_left(jnp.int32(1), idx_bits - 1 - j)
            cw = per_query(cand)
            taken = count(lambda key, kb: (key == thr_t) & (key_index(kb) < cw))
            return jnp.where(taken < need, cand, last)

        last_scr[...] = lax.fori_loop(0, idx_bits, tie_body, jnp.zeros((8, Q_TILE), I32))

    def per_row(x):
        return _wide(jnp.broadcast_to(x[0:1], (LANES, Q_TILE)).T)

    tw = per_row(thr)
    lastw = per_row(last_scr[...])

    def mask_block(kb, valid):
        key = key_scr[kb]
        tie_neg = jnp.where(key == tw, jnp.where(cols(kb) <= lastw, 0.0, NEG), NEG)
        neg = jnp.where(key > tw, 0.0, tie_neg)
        if valid is not None:
            neg = jnp.where(valid, neg, NEG)
        neg_scr[kb] = neg

    def mask_body(kb, carry):
        mask_block(kb, None)
        return carry

    lax.fori_loop(0, qb, mask_body, 0)
    mask_block(qb, valid_diag)


def _dsa(proj, bias_tiles, bsz, s_len):
    topk = min(TOPK_MAX, s_len // 4)
    idx_bits = max(1, (s_len - 1).bit_length())
    n_blocks = s_len // KEY_BLOCK
    assert 2 * n_blocks <= 256
    kern = functools.partial(_dsa_kernel, topk=topk, idx_bits=idx_bits, s_len=s_len)
    lane_blk = lambda off: off // LANES
    return pl.pallas_call(
        kern,
        grid=(bsz, s_len // Q_TILE, A_HEADS // 2),
        in_specs=[pl.BlockSpec((1, Q_TILE, LANES), lambda b, i, p: (b, i, lane_blk(OFF_QA) + p)),
                  pl.BlockSpec((1, Q_TILE, 256), lambda b, i, p: (b, i, OFF_QI // 256)),
                  pl.BlockSpec((1, Q_TILE, LANES), lambda b, i, p: (b, i, lane_blk(OFF_WI))),
                  pl.BlockSpec((1, s_len, W_A), lambda b, i, p: (b, 0, OFF_KA // W_A),
                               pipeline_mode=pl.Buffered(1)),
                  pl.BlockSpec((1, s_len, W_A), lambda b, i, p: (b, 0, OFF_VA // W_A),
                               pipeline_mode=pl.Buffered(1)),
                  pl.BlockSpec((1, s_len, LANES), lambda b, i, p: (b, 0, lane_blk(OFF_KI))),
                  pl.BlockSpec(bias_tiles.shape, lambda b, i, p: (0, 0, 0, 0),
                               pipeline_mode=pl.Buffered(1))],
        out_specs=pl.BlockSpec((1, Q_TILE, LANES), lambda b, i, p: (b, i, p)),
        out_shape=jax.ShapeDtypeStruct((bsz, s_len, W_A), BF16),
        scratch_shapes=[pltpu.VMEM((n_blocks, Q_TILE, KEY_BLOCK), I32),
                        pltpu.VMEM((n_blocks, KEY_BLOCK, Q_TILE), I32),
                        pltpu.VMEM((n_blocks, Q_TILE, KEY_BLOCK), F32),
                        pltpu.VMEM((IDX_HEADS, Q_TILE, LANES), F32),
                        pltpu.VMEM((8, Q_TILE), I32)] + _softmax_scratch(s_len),
        compiler_params=_cparams("parallel", "arbitrary", "arbitrary"),
        name="dsa",
    )(proj, proj, proj, proj, proj, proj, bias_tiles)


def _diff_kernel(q_ref, k_ref, v_ref, bias_ref, lp_ref, g_ref, o_ref,
                 s_scr, mx_scr, l_scr, acc_scr, *, lam_init):
    qb = pl.program_id(2)
    qs = _split_heads(q_ref[0])

    def add_near(kb, s, t):
        return s + _tall(bias_ref[0, t])

    o = _softmax_pv(qs, lambda kb: k_ref[0, _key_rows(kb), :], lambda kb: v_ref[0, _key_rows(kb), :],
                    qb, lambda kb, s: s, add_near, s_scr, mx_scr, l_scr, acc_scr)
    lp = lp_ref[0]
    lam = (jnp.exp(jnp.sum(lp[0:1] * lp[1:2], axis=-1, keepdims=True))
           - jnp.exp(jnp.sum(lp[2:3] * lp[3:4], axis=-1, keepdims=True)) + lam_init)
    o = o[:Q_TILE] - lam * o[Q_TILE:]
    y = o * lax.rsqrt(jnp.mean(o * o, axis=-1, keepdims=True) + SUBLN_EPS)
    o_ref[0] = (y * g_ref[0] * (1.0 - lam_init)).astype(o_ref.dtype)


def _diff(proj, bias_tiles, lam_params, subln_g, layer, bsz, s_len, lam_init):
    kern = functools.partial(_diff_kernel, lam_init=lam_init)
    return pl.pallas_call(
        kern,
        grid=(bsz, B_HEADS, s_len // Q_TILE),
        in_specs=[pl.BlockSpec((1, Q_TILE, LANES), lambda b, h, i: (b, i, OFF_QB // LANES + h)),
                  pl.BlockSpec((1, s_len, LANES), lambda b, h, i: (b, 0, OFF_KB // LANES + h)),
                  pl.BlockSpec((1, s_len, LANES), lambda b, h, i: (b, 0, OFF_VB // LANES + h)),
                  pl.BlockSpec((1, 2, Q_TILE, KEY_BLOCK), lambda b, h, i: (h, 0, 0, 0)),
                  pl.BlockSpec((1, 4, HEAD_DIM), lambda b, h, i: (layer, 0, 0)),
                  pl.BlockSpec((1, 1, 2 * HEAD_DIM), lambda b, h, i: (layer, 0, 0))],
        out_specs=pl.BlockSpec((1, Q_TILE, LANES), lambda b, h, i: (b, i, h)),
        out_shape=jax.ShapeDtypeStruct((bsz, s_len, B_HEADS * 2 * HEAD_DIM), BF16),
        scratch_shapes=_softmax_scratch(s_len),
        compiler_params=_cparams("parallel", "parallel", "arbitrary"),
        name="diff_attn",
    )(proj, proj, proj, bias_tiles, lam_params, subln_g.reshape(-1, 1, 2 * HEAD_DIM))


def _stick_kernel(q_ref, k_ref, v_ref, o_ref, later_scr, run_scr, acc_scr):
    qb = pl.program_id(2)
    qs = _split_heads(q_ref[0])
    jj = lax.broadcasted_iota(I32, (2 * KEY_BLOCK, KEY_BLOCK), 0) & (KEY_BLOCK - 1)
    ss = lax.broadcasted_iota(I32, (2 * KEY_BLOCK, KEY_BLOCK), 1)
    later_scr[...] = jnp.where(jj > ss, 1.0, 0.0).astype(BF16)
    run_scr[...] = jnp.zeros_like(run_scr)
    acc_scr[...] = jnp.zeros_like(acc_scr)

    def sweep(kbs, diagonal):
        parts = []
        for kb in kbs:
            z = _nt_dot(qs, k_ref[0, _key_rows(kb), :])
            soft = jnp.log(1.0 + jnp.exp2(jnp.abs(z) * (-LOG2E)))
            log_beta = jnp.minimum(z, 0.0) - soft
            log_1mb = log_beta - z
            strict = None
            if diagonal:
                t = lax.broadcasted_iota(I32, z.shape, 0) & (Q_TILE - 1)
                strict = lax.broadcasted_iota(I32, z.shape, 1) < t
                log_1mb = jnp.where(strict, log_1mb, 0.0)
            hi = log_1mb.astype(BF16)
            lo = (log_1mb - hi.astype(F32)).astype(BF16)
            between = _dot(jnp.concatenate([hi, lo], axis=1), later_scr[...])
            total = between[:, 0:1] + log_1mb[:, 0:1]
            parts.append((kb, log_beta + between, total, strict))
        run = run_scr[...]
        acc = None
        for kb, logit, total, strict in parts:
            w = jnp.exp(logit + _wide(run))
            if strict is not None:
                w = jnp.where(strict, w, 0.0)
            pv = _dot(w.astype(BF16), v_ref[0, _key_rows(kb), :])
            acc = pv if acc is None else acc + pv
            run = run + jnp.broadcast_to(total, run.shape)
        acc_scr[...] += acc
        run_scr[...] = run

    sweep([qb], True)
    top = qb - 1
    for size in (GROUP, 2, 1):
        trips = (top + 1) // size

        def body(i, carry, top=top, size=size):
            sweep([top - i * size - u for u in range(size)], False)
            return carry

        lax.fori_loop(0, trips, body, 0)
        top = top - trips * size
    o_ref[0] = _merge_heads(acc_scr[...], Q_TILE).astype(o_ref.dtype)


def _stick(proj, bsz, s_len):
    return pl.pallas_call(
        _stick_kernel,
        grid=(bsz, C_HEADS // 2, s_len // Q_TILE),
        in_specs=[pl.BlockSpec((1, Q_TILE, LANES), lambda b, h, i: (b, i, OFF_QC // LANES + h)),
                  pl.BlockSpec((1, s_len, LANES), lambda b, h, i: (b, 0, OFF_KC // LANES + h)),
                  pl.BlockSpec((1, s_len, LANES), lambda b, h, i: (b, 0, OFF_VC // LANES + h))],
        out_specs=pl.BlockSpec((1, Q_TILE, LANES), lambda b, h, i: (b, i, h)),
        out_shape=jax.ShapeDtypeStruct((bsz, s_len, C_HEADS * HEAD_DIM), BF16),
        scratch_shapes=[pltpu.VMEM((2 * KEY_BLOCK, KEY_BLOCK), BF16),
                        pltpu.VMEM((2 * Q_TILE, LANES), F32),
                        pltpu.VMEM((2 * Q_TILE, LANES), F32)],
        compiler_params=_cparams("parallel", "parallel", "arbitrary"),
        name="stick_attn",
    )(proj, proj, proj)


def _merge_kernel(oa_ref, ob_ref, oc_ref, gate_ref, x_ref, g1_ref, wbr_ref, wout_ref, o_ref):
    d = x_ref.shape[-1]
    merged = jnp.zeros(x_ref.shape, F32)
    for j, o_br in enumerate((oa_ref, ob_ref, oc_ref)):
        w = o_br.shape[-1]
        gate = jax.nn.sigmoid(gate_ref[:, j * d:(j + 1) * d].astype(F32))
        merged = merged + gate * _dot(o_br[...], wbr_ref[j * w:(j + 1) * w, :])
    o_ref[...] = x_ref[...] + g1_ref[0] * _dot(merged.astype(BF16), wout_ref[...])


def _merge(o_a, o_b, o_c, proj, x2, mod, w_br, w_out, s_len):
    t, d = x2.shape
    tm = 512
    per_b = s_len // tm
    w = o_a.shape[-1]
    return pl.pallas_call(
        _merge_kernel,
        grid=(t // tm,),
        in_specs=[pl.BlockSpec((tm, w), lambda i: (i, 0)),
                  pl.BlockSpec((tm, w), lambda i: (i, 0)),
                  pl.BlockSpec((tm, w), lambda i: (i, 0)),
                  pl.BlockSpec((tm, 3 * d), lambda i: (i, OFF_G // (3 * d))),
                  pl.BlockSpec((tm, d), lambda i: (i, 0)),
                  pl.BlockSpec((1, 1, d), lambda i: (i // per_b, 0, 2)),
                  pl.BlockSpec(w_br.shape, lambda i: (0, 0)),
                  pl.BlockSpec(w_out.shape, lambda i: (0, 0))],
        out_specs=pl.BlockSpec((tm, d), lambda i: (i, 0)),
        out_shape=jax.ShapeDtypeStruct((t, d), F32),
        compiler_params=_cparams("parallel"),
        name="merge",
    )(o_a, o_b, o_c, proj, x2, mod, w_br, w_out)


def _finish(x, gate, acc, fg_ref, final):
    y = x + gate * acc
    if final:
        y = y * lax.rsqrt(jnp.mean(y * y, axis=-1, keepdims=True) + EPS) * fg_ref[...]
    return y


def _swiglu_partial(h, w1, w3, w2):
    a = _dot(h, w1)
    act = (a * jax.nn.sigmoid(a)) * _dot(h, w3)
    return _dot(act.astype(BF16), w2)


def _ffn_kernel(x_ref, g_ref, sc_ref, sh_ref, gate_ref, w1_ref, w3_ref, w2_ref, fg_ref, o_ref,
                h_scr, acc_scr, *, final):
    f = pl.program_id(1)

    @pl.when(f == 0)
    def _():
        h_scr[...] = _modulated_norm(x_ref[...], g_ref[...], sc_ref[0], sh_ref[0]).astype(BF16)
        acc_scr[...] = jnp.zeros_like(acc_scr)

    acc_scr[...] += _swiglu_partial(h_scr[...], w1_ref[...], w3_ref[...], w2_ref[...])

    @pl.when(f == pl.num_programs(1) - 1)
    def _():
        o_ref[...] = _finish(x_ref[...], gate_ref[0], acc_scr[...], fg_ref, final)


def _ffn(x2, g, mod, w1, w3, w2, final_g, s_len, final):
    t, d = x2.shape
    d_ff = w1.shape[1]
    tm, tf = 1024, 256
    per_b = s_len // tm
    kern = functools.partial(_ffn_kernel, final=final)
    return pl.pallas_call(
        kern,
        grid=(t // tm, d_ff // tf),
        in_specs=[pl.BlockSpec((tm, d), lambda i, f: (i, 0)),
                  pl.BlockSpec((1, d), lambda i, f: (0, 0)),
                  pl.BlockSpec((1, 1, d), lambda i, f: (i // per_b, 0, 4)),
                  pl.BlockSpec((1, 1, d), lambda i, f: (i // per_b, 0, 3)),
                  pl.BlockSpec((1, 1, d), lambda i, f: (i // per_b, 0, 5)),
                  pl.BlockSpec((d, tf), lambda i, f: (0, f)),
                  pl.BlockSpec((d, tf), lambda i, f: (0, f)),
                  pl.BlockSpec((tf, d), lambda i, f: (f, 0)),
                  pl.BlockSpec((1, d), lambda i, f: (0, 0))],
        out_specs=pl.BlockSpec((tm, d), lambda i, f: (i, 0)),
        out_shape=jax.ShapeDtypeStruct((t, d), F32),
        scratch_shapes=[pltpu.VMEM((tm, d), BF16), pltpu.VMEM((tm, d), F32)],
        compiler_params=_cparams("parallel", "arbitrary"),
        name="ffn",
    )(x2, g, mod, mod, mod, w1, w3, w2, final_g)


MOE_TM = 512
MOE_ROWS = 512
R_E1, R_E2, R_RANK1, R_RANK2, R_W1, R_W2 = 0, 1, 2, 3, 4, 5


def _lane_pick(tile, lane, idx):
    return jnp.sum(jnp.where(lane == idx, tile, 0.0), axis=-1, keepdims=True)


def _route_kernel(x_ref, g_ref, sc_ref, sh_ref, wr_ref, br_ref, h_ref, rec_ref, cnt_ref, cnt_scr):
    @pl.when(pl.program_id(0) == 0)
    def _():
        cnt_scr[...] = jnp.zeros_like(cnt_scr)

    h = _modulated_norm(x_ref[...], g_ref[...], sc_ref[0], sh_ref[0])
    half = h.shape[1] // 2
    bits = pltpu.bitcast(h.astype(BF16).astype(F32), jnp.uint32)
    h_ref[...] = (bits[:, half:] & jnp.uint32(0xFFFF0000)) | (bits[:, :half] >> 16)

    h_hi = h.astype(BF16)
    h_lo = (h - h_hi.astype(F32)).astype(BF16)
    wr = wr_ref[...]
    w_hi = wr.astype(BF16)
    w_lo = (wr - w_hi.astype(F32)).astype(BF16)
    logits = _dot(h_hi, w_hi) + _dot(h_hi, w_lo) + _dot(h_lo, w_hi) + br_ref[...]
    lane = lax.broadcasted_iota(I32, logits.shape, 1).astype(F32)
    lg = jnp.where(lane < N_EXPERTS, logits, -jnp.inf)
    m1 = jnp.max(lg, axis=-1, keepdims=True)
    i1 = jnp.min(jnp.where(lg == m1, lane, float(LANES)), axis=-1, keepdims=True)
    lg2 = jnp.where(lane == i1, -jnp.inf, lg)
    m2 = jnp.max(lg2, axis=-1, keepdims=True)
    i2 = jnp.min(jnp.where(lg2 == m2, lane, float(LANES)), axis=-1, keepdims=True)
    e2 = jnp.exp(m2 - m1)
    w_top = 1.0 / (1.0 + e2)

    chosen = jnp.where(lane == i1, 1.0, 0.0) + jnp.where(lane == i2, 1.0, 0.0)
    tm = chosen.shape[0]
    earlier = jnp.where(lax.broadcasted_iota(I32, (tm, tm), 1) < lax.broadcasted_iota(I32, (tm, tm), 0),
                        1.0, 0.0).astype(BF16)
    rank = _dot(earlier, chosen.astype(BF16)) + cnt_scr[0:1, :]
    rec = jnp.zeros_like(logits)
    for slot, val in ((R_E1, i1), (R_E2, i2), (R_RANK1, _lane_pick(rank, lane, i1)),
                      (R_RANK2, _lane_pick(rank, lane, i2)), (R_W1, w_top), (R_W2, e2 * w_top)):
        rec = jnp.where(lane == float(slot), val, rec)
    rec_ref[...] = rec
    cnt_scr[0:1, :] = cnt_scr[0:1, :] + jnp.sum(chosen, axis=0, keepdims=True)
    cnt_ref[...] = cnt_scr[...]


def _route(x2, g, mod, wr_pad, br_pad, s_len):
    t, d = x2.shape
    tm = MOE_TM
    per_b = s_len // tm
    return pl.pallas_call(
        _route_kernel,
        grid=(t // tm,),
        in_specs=[pl.BlockSpec((tm, d), lambda i: (i, 0)),
                  pl.BlockSpec((1, d), lambda i: (0, 0)),
                  pl.BlockSpec((1, 1, d), lambda i: (i // per_b, 0, 4)),
                  pl.BlockSpec((1, 1, d), lambda i: (i // per_b, 0, 3)),
                  pl.BlockSpec((d, LANES), lambda i: (0, 0)),
                  pl.BlockSpec((1, LANES), lambda i: (0, 0))],
        out_specs=[pl.BlockSpec((tm, d // 2), lambda i: (i, 0)),
                   pl.BlockSpec((tm, LANES), lambda i: (i, 0)),
                   pl.BlockSpec((8, LANES), lambda i: (0, 0))],
        out_shape=[jax.ShapeDtypeStruct((t, d // 2), jnp.uint32),
                   jax.ShapeDtypeStruct((t, LANES), F32),
                   jax.ShapeDtypeStruct((8, LANES), F32)],
        scratch_shapes=[pltpu.VMEM((8, LANES), F32)],
        compiler_params=_cparams("arbitrary"),
        name="moe_route",
    )(x2, g, mod, mod, wr_pad, br_pad)


def _row_copy(src_ref, src_row, dst_ref, dst_row, sem):
    return pltpu.make_async_copy(src_ref.at[pl.ds(src_row, 1)], dst_ref.at[pl.ds(dst_row, 1)], sem)


def _dispatch_kernel(pos_ref, h_ref, xs_in_ref, xs_ref, sem):
    del xs_in_ref
    tm = h_ref.shape[0]

    def start(t, carry):
        _row_copy(h_ref, t, xs_ref, pos_ref[0, 0, t], sem).start(priority=0)
        _row_copy(h_ref, t, xs_ref, pos_ref[0, 0, tm + t], sem).start(priority=1)
        return carry

    lax.fori_loop(0, tm, start, 0)
    for _ in range(2):
        pltpu.make_async_copy(h_ref, xs_ref.at[pl.ds(0, tm)], sem).wait()


def _dispatch(pos, h_packed, n_rows):
    t, w = h_packed.shape
    tm = MOE_TM
    xs0 = jnp.zeros((n_rows, w), h_packed.dtype)
    return pl.pallas_call(
        _dispatch_kernel,
        grid=(t // tm,),
        in_specs=[pl.BlockSpec((1, 1, 2 * tm), lambda i: (i, 0, 0), memory_space=pltpu.SMEM),
                  pl.BlockSpec((tm, w), lambda i: (i, 0)),
                  pl.BlockSpec(memory_space=pl.ANY)],
        out_specs=pl.BlockSpec(memory_space=pl.ANY),
        out_shape=jax.ShapeDtypeStruct((n_rows, w), h_packed.dtype),
        scratch_shapes=[pltpu.SemaphoreType.DMA(())],
        input_output_aliases={2: 0},
        compiler_params=_cparams("arbitrary"),
        name="moe_dispatch",
    )(pos, h_packed, xs0)


def _expert_kernel(te_ref, nu_ref, xs_ref, w1_ref, w3_ref, w2_ref, o_ref, h_scr, acc_scr):
    i = pl.program_id(0)
    f = pl.program_id(1)

    @pl.when(i < nu_ref[0])
    def _():
        @pl.when(f == 0)
        def _():
            word = xs_ref[...]
            lo = pltpu.bitcast(word << 16, F32)
            hi = pltpu.bitcast(word & jnp.uint32(0xFFFF0000), F32)
            h_scr[...] = jnp.concatenate([lo, hi], axis=1).astype(BF16)
            acc_scr[...] = jnp.zeros_like(acc_scr)

        acc_scr[...] += _swiglu_partial(h_scr[...], w1_ref[0], w3_ref[0], w2_ref[0])

        @pl.when(f == pl.num_programs(1) - 1)
        def _():
            o_ref[...] = acc_scr[...]

    @pl.when((i >= nu_ref[0]) & (f == pl.num_programs(1) - 1))
    def _():
        o_ref[...] = jnp.zeros_like(o_ref)


def _experts(tile_expert, n_used, xs, w1, w3, w2):
    n_rows, half = xs.shape
    d = 2 * half
    d_ff = w1.shape[2]
    tf = 896
    nf = d_ff // tf
    tile = lambda i, nu: jnp.minimum(i, nu[0] - 1)
    ff = lambda i, f, nu: jnp.where(i < nu[0], f, nf - 1)
    return pl.pallas_call(
        _expert_kernel,
        grid_spec=pltpu.PrefetchScalarGridSpec(
            num_scalar_prefetch=2,
            grid=(n_rows // MOE_ROWS, nf),
            in_specs=[pl.BlockSpec((MOE_ROWS, half), lambda i, f, te, nu: (tile(i, nu), 0)),
                      pl.BlockSpec((1, d, tf), lambda i, f, te, nu: (te[i], 0, ff(i, f, nu))),
                      pl.BlockSpec((1, d, tf), lambda i, f, te, nu: (te[i], 0, ff(i, f, nu))),
                      pl.BlockSpec((1, tf, d), lambda i, f, te, nu: (te[i], ff(i, f, nu), 0))],
            out_specs=pl.BlockSpec((MOE_ROWS, d), lambda i, f, te, nu: (i, 0)),
            scratch_shapes=[pltpu.VMEM((MOE_ROWS, d), BF16), pltpu.VMEM((MOE_ROWS, d), F32)]),
        out_shape=jax.ShapeDtypeStruct((n_rows, d), F32),
        compiler_params=_cparams("arbitrary", "arbitrary"),
        name="moe_experts",
    )(tile_expert, n_used, xs, w1, w3, w2)


def _combine_kernel(pos_ref, x_ref, gate_ref, rec_ref, fg_ref, ys_ref, o_ref, y_scr, sem, *, final):
    tm = x_ref.shape[0]

    def start(t, carry):
        _row_copy(ys_ref, pos_ref[0, 0, t], y_scr.at[0], t, sem).start(priority=0)
        _row_copy(ys_ref, pos_ref[0, 0, tm + t], y_scr.at[1], t, sem).start(priority=1)
        return carry

    lax.fori_loop(0, tm, start, 0)
    for slot in range(2):
        pltpu.make_async_copy(ys_ref.at[pl.ds(0, tm)], y_scr.at[slot], sem).wait()
    rec = rec_ref[...]
    lane = lax.broadcasted_iota(I32, rec.shape, 1)
    w_first = jnp.sum(jnp.where(lane == R_W1, rec, 0.0), axis=-1, keepdims=True)
    w_second = jnp.sum(jnp.where(lane == R_W2, rec, 0.0), axis=-1, keepdims=True)
    f = w_first * y_scr[0] + w_second * y_scr[1]
    o_ref[...] = _finish(x_ref[...], gate_ref[0], f, fg_ref, final)


def _combine(pos, x2, mod, rec, final_g, ys, s_len, final):
    t, d = x2.shape
    tm = MOE_TM
    per_b = s_len // tm
    kern = functools.partial(_combine_kernel, final=final)
    return pl.pallas_call(
        kern,
        grid=(t // tm,),
        in_specs=[pl.BlockSpec((1, 1, 2 * tm), lambda i: (i, 0, 0), memory_space=pltpu.SMEM),
                  pl.BlockSpec((tm, d), lambda i: (i, 0)),
                  pl.BlockSpec((1, 1, d), lambda i: (i // per_b, 0, 5)),
                  pl.BlockSpec((tm, LANES), lambda i: (i, 0)),
                  pl.BlockSpec((1, d), lambda i: (0, 0)),
                  pl.BlockSpec(memory_space=pl.ANY)],
        out_specs=pl.BlockSpec((tm, d), lambda i: (i, 0)),
        out_shape=jax.ShapeDtypeStruct((t, d), F32),
        scratch_shapes=[pltpu.VMEM((2, tm, d), F32), pltpu.SemaphoreType.DMA(())],
        compiler_params=_cparams("arbitrary"),
        name="moe_combine",
    )(pos, x2, mod, rec, final_g, ys)


def _moe(x2, g, mod, wr_pad, br_pad, w1, w3, w2, final_g, s_len, final):
    t, d = x2.shape
    n_e = w1.shape[0]
    h_packed, rec, cnt = _route(x2, g, mod, wr_pad, br_pad, s_len)

    n_tiles = 2 * t // MOE_ROWS + n_e
    counts = cnt[0, :n_e].astype(I32)
    padded = (counts + MOE_ROWS - 1) // MOE_ROWS * MOE_ROWS
    ends = jnp.cumsum(padded)
    starts = ends - padded
    e1, e2 = rec[:, R_E1].astype(I32), rec[:, R_E2].astype(I32)
    pos1 = starts[e1] + rec[:, R_RANK1].astype(I32)
    pos2 = starts[e2] + rec[:, R_RANK2].astype(I32)
    pos = jnp.concatenate([pos1.reshape(-1, 1, MOE_TM), pos2.reshape(-1, 1, MOE_TM)], axis=2)
    n_used = (ends[-1] // MOE_ROWS).reshape(1)
    tile_start = jnp.minimum(jnp.arange(n_tiles, dtype=I32), n_used[0] - 1) * MOE_ROWS
    tile_expert = jnp.sum(tile_start[:, None] >= ends[None, :], axis=1).astype(I32)

    xs = _dispatch(pos, h_packed, n_tiles * MOE_ROWS)
    ys = _experts(tile_expert, n_used, xs, w1, w3, w2)
    return _combine(pos, x2, mod, rec, final_g, ys, s_len, final)


def _pack_w_in(w):
    d = w.shape[0]
    sizes = (W_A, W_A, W_A, IDX_HEADS * IDX_DIM, IDX_DIM, IDX_HEADS,
             512, 512, 512, 512, 512, 512, 3 * d)
    offs = [0]
    for s in sizes:
        offs.append(offs[-1] + s)
    (qa, ka, va, qi, ki, wi, qb, kb, vb, qc, kc, vc, gl) = [
        w[:, offs[j]:offs[j + 1]] for j in range(len(sizes))]
    scale = HEAD_DIM ** -0.5
    pad_wi = jnp.zeros((d, LANES - IDX_HEADS), w.dtype)
    packed = jnp.concatenate(
        [gl, qa * scale, ka, va,
         qi * (IDX_DIM ** -0.5), ki, ki, wi * (IDX_HEADS ** -0.5), pad_wi,
         qb * scale, kb, vb, qc * scale, kc, vc], axis=1)
    assert packed.shape[1] == PACKED
    return packed.astype(BF16)


def kernel(x, c, w_ada, b_ada, norm1_g, norm2_g, w_in, w_br, w_out, rel_bias, lam_params,
           subln_g, ffn_w1, ffn_w3, ffn_w2, router_w, router_b, moe_w1, moe_w3, moe_w2, final_g):
    bsz, s_len, d = x.shape
    depth = w_ada.shape[0]
    assert s_len % 1024 == 0 and d == 1024 and OFF_G + 3 * d == OFF_QA

    c_pad = jnp.concatenate([c, jnp.zeros((8 - bsz % 8 if bsz % 8 else 0, d), c.dtype)], axis=0)
    mod_all = _ada(c_pad, w_ada, b_ada)
    dsa_tiles, diff_tiles = _bias_tiles(rel_bias)
    fg = final_g.reshape(1, d)

    x2 = x.reshape(bsz * s_len, d)
    for l in range(depth):
        mod = mod_all[l, :bsz].reshape(bsz, 1, 6 * d)
        proj = _inproj(x2, norm1_g[l].reshape(1, d), mod, _pack_w_in(w_in[l]), s_len)
        proj3 = proj.reshape(bsz, s_len, PACKED)
        lam_init = 0.8 - 0.6 * math.exp(-0.3 * l)
        o_a = _dsa(proj3, dsa_tiles, bsz, s_len)
        o_b = _diff(proj3, diff_tiles, lam_params, subln_g, l, bsz, s_len, lam_init)
        o_c = _stick(proj3, bsz, s_len)
        x2 = _merge(o_a.reshape(-1, o_a.shape[-1]), o_b.reshape(-1, o_b.shape[-1]),
                    o_c.reshape(-1, o_c.shape[-1]), proj, x2, mod,
                    w_br[l].astype(BF16), w_out[l].astype(BF16), s_len)
        g2 = norm2_g[l].reshape(1, d)
        final = l == depth - 1
        j = l // 2
        if l % 2 == 0:
            x2 = _ffn(x2, g2, mod, ffn_w1[j].astype(BF16), ffn_w3[j].astype(BF16),
                      ffn_w2[j].astype(BF16), fg, s_len, final)
        else:
            wr_pad = jnp.pad(router_w[j], ((0, 0), (0, LANES - N_EXPERTS)))
            br_pad = jnp.pad(router_b[j], (0, LANES - N_EXPERTS)).reshape(1, LANES)
            x2 = _moe(x2, g2, mod, wr_pad, br_pad, moe_w1[j].astype(BF16), moe_w3[j].astype(BF16),
                      moe_w2[j].astype(BF16), fg, s_len, final)
    return x2.reshape(bsz, s_len, d)
```

```python
import functools
import math

import jax
import jax.numpy as jnp
from jax import lax
from jax.experimental import pallas as pl
from jax.experimental.pallas import tpu as pltpu

F32 = jnp.float32
BF16 = jnp.bfloat16
I32 = jnp.int32
I16 = jnp.int16

LANES = 128
VMEM_LIMIT_BYTES = 56 * 1024 * 1024

CHUNK = 64
A_HEADS = 8
IDX_HEADS = 4
IDX_DIM = 64
TOPK_MAX = 256
B_HEADS = 4
C_HEADS = 8
HEAD_DIM = 64
REL_BUCKETS = 32
FAR_BUCKET = REL_BUCKETS // 2 - 1
N_EXPERTS = 8
EPS = 1e-6
SUBLN_EPS = 1e-5
NEG = -1e30
LOG2E = 1.4426950408889634
INT_MIN = -(2 ** 31)

KEY_BLOCK = 256
Q_TILE = 256
GROUP = 4
BLOCK_GROUPS = (8, 4, 2, 1)

W_A = A_HEADS * HEAD_DIM
OFF_G = 0
OFF_QA, OFF_KA, OFF_VA = 3072, 3584, 4096
OFF_QI, OFF_KI, OFF_WI = 4608, 4864, 4992
OFF_QB, OFF_KB, OFF_VB = 5120, 5632, 6144
OFF_QC, OFF_KC, OFF_VC = 6656, 7168, 7680
PACKED = 8192

LOG_BUCKET_STEPS = (12, 16, 23, 32, 46, 64, 91)


def _nt_dot(a, b):
    return lax.dot_general(a, b, (((1,), (1,)), ((), ())), preferred_element_type=F32)


def _dot(a, b):
    return jnp.dot(a, b, preferred_element_type=F32)


def _cparams(*sem):
    return pltpu.CompilerParams(dimension_semantics=sem, vmem_limit_bytes=VMEM_LIMIT_BYTES)


def _split_heads(x):
    lane = lax.broadcasted_iota(I32, x.shape, 1)
    keep_a = jnp.where(lane < HEAD_DIM, 1.0, 0.0).astype(x.dtype)
    keep_b = jnp.where(lane < HEAD_DIM, 0.0, 1.0).astype(x.dtype)
    return jnp.concatenate([x * keep_a, x * keep_b], axis=0)


def _merge_heads(o, m):
    lane = lax.broadcasted_iota(I32, (m, LANES), 1)
    return jnp.where(lane < HEAD_DIM, o[:m], o[m:])


def _wide(x):
    return jnp.concatenate([x, x], axis=1)


def _tall(x):
    return jnp.concatenate([x, x], axis=0)


def _key_rows(kb):
    return pl.ds(pl.multiple_of(kb * KEY_BLOCK, KEY_BLOCK), KEY_BLOCK)


def _sortable(x):
    bits = pltpu.bitcast(x, I32)
    return bits ^ ((bits >> 31) & 0x7FFFFFFF)


def _chunk_causal():
    r = lax.broadcasted_iota(I32, (Q_TILE, KEY_BLOCK), 0)
    c = lax.broadcasted_iota(I32, (Q_TILE, KEY_BLOCK), 1)
    return (c // CHUNK) <= (r // CHUNK)


def _ada_kernel(c_ref, w_ref, b_ref, o_ref):
    c = c_ref[...]
    a = c * jax.nn.sigmoid(c)
    o_ref[0] = jnp.dot(a, w_ref[0], preferred_element_type=F32,
                       precision=lax.Precision.HIGHEST) + b_ref[0]


def _ada(c_pad, w_ada, b_ada):
    depth, d, n = w_ada.shape
    tn = 1024
    return pl.pallas_call(
        _ada_kernel,
        grid=(depth, n // tn),
        in_specs=[pl.BlockSpec(c_pad.shape, lambda l, j: (0, 0)),
                  pl.BlockSpec((1, d, tn), lambda l, j: (l, 0, j)),
                  pl.BlockSpec((1, 1, tn), lambda l, j: (l, 0, j))],
        out_specs=pl.BlockSpec((1, c_pad.shape[0], tn), lambda l, j: (l, 0, j)),
        out_shape=jax.ShapeDtypeStruct((depth, c_pad.shape[0], n), F32),
        compiler_params=_cparams("parallel", "parallel"),
        name="ada",
    )(c_pad, w_ada, b_ada.reshape(depth, 1, n))


def _rel_bias_tile(tab_ref, head, d0, n_heads_total):
    r = lax.broadcasted_iota(I32, (Q_TILE, KEY_BLOCK), 0)
    c = lax.broadcasted_iota(I32, (Q_TILE, KEY_BLOCK), 1)
    d = c - r + d0
    n = jnp.abs(d)
    large = jnp.full(d.shape, REL_BUCKETS // 4, I32)
    for step in LOG_BUCKET_STEPS:
        large = large + jnp.where(n >= step, 1, 0)
    bucket = jnp.where(d > 0, REL_BUCKETS // 2, 0) + jnp.where(n < REL_BUCKETS // 4, n, large)
    out = jnp.zeros(d.shape, F32)
    for b in range(REL_BUCKETS):
        out = jnp.where(bucket == b, tab_ref[b * n_heads_total + head], out)
    return out - tab_ref[FAR_BUCKET * n_heads_total + head]


def _bias_kernel(tab_ref, dsa_ref, diff_ref):
    p = pl.program_id(0)
    n_heads = A_HEADS + B_HEADS
    for t, d0 in enumerate((-KEY_BLOCK, 0)):
        dsa_ref[0, t, 0:Q_TILE, :] = _rel_bias_tile(tab_ref, 2 * p, d0, n_heads)
        dsa_ref[0, t, Q_TILE:2 * Q_TILE, :] = _rel_bias_tile(tab_ref, 2 * p + 1, d0, n_heads)
        tile = _rel_bias_tile(tab_ref, A_HEADS + p, d0, n_heads)
        if t == 1:
            tile = jnp.where(_chunk_causal(), tile, NEG)
        diff_ref[0, t] = tile


def _bias_tiles(rel_bias):
    tab = rel_bias.reshape(-1)
    return pl.pallas_call(
        _bias_kernel,
        grid=(4,),
        in_specs=[pl.BlockSpec(memory_space=pltpu.SMEM)],
        out_specs=[pl.BlockSpec((1, 2, 2 * Q_TILE, KEY_BLOCK), lambda p: (p, 0, 0, 0)),
                   pl.BlockSpec((1, 2, Q_TILE, KEY_BLOCK), lambda p: (p, 0, 0, 0))],
        out_shape=[jax.ShapeDtypeStruct((A_HEADS // 2, 2, 2 * Q_TILE, KEY_BLOCK), F32),
                   jax.ShapeDtypeStruct((B_HEADS, 2, Q_TILE, KEY_BLOCK), F32)],
        compiler_params=_cparams("parallel"),
        name="rel_bias_tiles",
    )(tab)


def _modulated_norm(x, g, sc, sh):
    y = x * lax.rsqrt(jnp.mean(x * x, axis=-1, keepdims=True) + EPS)
    return y * g * (1.0 + sc) + sh


def _inproj_kernel(x_ref, g_ref, sc_ref, sh_ref, w_ref, o_ref, h_scr):
    @pl.when(pl.program_id(1) == 0)
    def _():
        h_scr[...] = _modulated_norm(x_ref[...], g_ref[...], sc_ref[0], sh_ref[0]).astype(BF16)

    o_ref[...] = _dot(h_scr[...], w_ref[...]).astype(o_ref.dtype)


def _inproj(x2, g, mod, w_packed, s_len):
    t, d = x2.shape
    n = w_packed.shape[1]
    tm, tn = 1024, 1024
    per_b = s_len // tm
    return pl.pallas_call(
        _inproj_kernel,
        grid=(t // tm, n // tn),
        in_specs=[pl.BlockSpec((tm, d), lambda i, j: (i, 0)),
                  pl.BlockSpec((1, d), lambda i, j: (0, 0)),
                  pl.BlockSpec((1, 1, d), lambda i, j: (i // per_b, 0, 1)),
                  pl.BlockSpec((1, 1, d), lambda i, j: (i // per_b, 0, 0)),
                  pl.BlockSpec((d, tn), lambda i, j: (0, j))],
        out_specs=pl.BlockSpec((tm, tn), lambda i, j: (i, j)),
        out_shape=jax.ShapeDtypeStruct((t, n), BF16),
        scratch_shapes=[pltpu.VMEM((tm, d), BF16)],
        compiler_params=_cparams("parallel", "arbitrary"),
        name="inproj",
    )(x2, g, mod, mod, w_packed)


def _for_blocks(lo, hi, fn, sizes=BLOCK_GROUPS):
    pos = lo
    for size in sizes:
        trips = jnp.maximum(hi - pos, 0) // size

        def body(i, carry, pos=pos, size=size):
            fn([pos + i * size + u for u in range(size)])
            return carry

        lax.fori_loop(0, trips, body, 0)
        pos = pos + trips * size


def _fold_blocks(n, fn, acc):
    pos = 0
    for size in BLOCK_GROUPS:
        trips = (n - pos) // size

        def body(i, acc, pos=pos, size=size):
            for u in range(size):
                acc = fn(pos + i * size + u, acc)
            return acc

        acc = lax.fori_loop(0, trips, body, acc)
        pos = pos + trips * size
    return acc


def _softmax_pv(qs, k_at, v_at, qb, add_far, add_near, s_scr, mx_scr, l_scr, acc_scr):
    def pass1(blocks):
        mx = None
        for kb, add in blocks:
            s = add(kb, _nt_dot(qs, k_at(kb))) * LOG2E
            s_scr[kb] = s
            fold = jnp.maximum(s[:, :LANES], s[:, LANES:])
            mx = fold if mx is None else jnp.maximum(mx, fold)
        mx_scr[...] = jnp.maximum(mx_scr[...], mx)

    diagonal = (qb, lambda kb, s: add_near(kb, s, 1))
    mx_scr[...] = jnp.full(mx_scr.shape, NEG, F32)
    _for_blocks(0, qb - 1, lambda kbs: pass1([(kb, add_far) for kb in kbs]))

    @pl.when(qb > 0)
    def _():
        pass1([(qb - 1, lambda kb, s: add_near(kb, s, 0)), diagonal])

    @pl.when(qb == 0)
    def _():
        pass1([diagonal])

    m = jnp.max(mx_scr[...], axis=-1, keepdims=True)
    mx_scr[...] = jnp.broadcast_to(m, mx_scr.shape)
    l_scr[...] = jnp.zeros_like(l_scr)
    acc_scr[...] = jnp.zeros_like(acc_scr)

    def pass2(kbs):
        mw = _wide(mx_scr[...])
        l_add = acc_add = None
        for kb in kbs:
            p = jnp.exp2(s_scr[kb] - mw)
            fold = p[:, :LANES] + p[:, LANES:]
            pv = _dot(p.astype(BF16), v_at(kb))
            l_add = fold if l_add is None else l_add + fold
            acc_add = pv if acc_add is None else acc_add + pv
        l_scr[...] += l_add
        acc_scr[...] += acc_add

    _for_blocks(0, qb + 1, pass2)
    return acc_scr[...] / jnp.sum(l_scr[...], axis=-1, keepdims=True)


def _softmax_scratch(s_len):
    return [pltpu.VMEM((s_len // KEY_BLOCK, 2 * Q_TILE, KEY_BLOCK), F32),
            pltpu.VMEM((2 * Q_TILE, LANES), F32),
            pltpu.VMEM((2 * Q_TILE, LANES), F32),
            pltpu.VMEM((2 * Q_TILE, LANES), F32)]


def _dsa_kernel(qa_ref, qi_ref, wi_ref, ka_ref, va_ref, ki_ref, bias_ref, o_ref,
                key_scr, keyt_scr, hi_scr, lo_scr, neg_scr, w_scr, last_scr, s_scr, mx_scr, l_scr, acc_scr,
                *, topk, idx_bits, s_len):
    qb = pl.program_id(1)

    @pl.when(pl.program_id(2) == 0)
    def _():
        _dsa_select(qb, qi_ref, wi_ref, ki_ref, key_scr, keyt_scr, hi_scr, lo_scr, neg_scr, w_scr, last_scr,
                    topk=topk, idx_bits=idx_bits, s_len=s_len)

    def add_far(kb, s):
        return s + _tall(neg_scr[kb])

    pr = pl.program_id(2)
    lanes = pl.ds(pl.multiple_of(pr * LANES, LANES), LANES)

    def add_near(kb, s, t):
        return s + _tall(neg_scr[kb]) + bias_ref[pr, t]

    o = _softmax_pv(_split_heads(qa_ref[0]),
                    lambda kb: ka_ref[0, _key_rows(kb), lanes],
                    lambda kb: va_ref[0, _key_rows(kb), lanes],
                    qb, add_far, add_near, s_scr, mx_scr, l_scr, acc_scr)
    o_ref[0] = _merge_heads(o, Q_TILE).astype(o_ref.dtype)


def _dsa_select(qb, qi_ref, wi_ref, ki_ref, key_scr, keyt_scr, hi_scr, lo_scr, neg_scr, w_scr, last_scr,
                *, topk, idx_bits, s_len):
    nkb = qb + 1
    kf = float(topk)
    valid_diag = _chunk_causal()

    def cols(kb):
        return kb * KEY_BLOCK + lax.broadcasted_iota(I32, (Q_TILE, KEY_BLOCK), 1)

    qi = qi_ref[0]
    q_heads = [_split_heads(qi[:, pr * LANES:(pr + 1) * LANES]) for pr in range(IDX_HEADS // 2)]
    wi = wi_ref[0].astype(F32)
    for h in range(IDX_HEADS):
        w_scr[h] = jnp.broadcast_to(wi[:, h:h + 1], (Q_TILE, LANES))

    def score_block(kb, valid):
        kk = ki_ref[0, _key_rows(kb), :]
        sc = jnp.zeros((Q_TILE, KEY_BLOCK), F32)
        for pr in range(IDX_HEADS // 2):
            d = _nt_dot(q_heads[pr], kk)
            sc = sc + _wide(w_scr[2 * pr]) * jnp.maximum(d[:Q_TILE], 0.0)
            sc = sc + _wide(w_scr[2 * pr + 1]) * jnp.maximum(d[Q_TILE:], 0.0)
        sc = jnp.where(sc == 0.0, 0.0, sc)
        if valid is not None:
            sc = jnp.where(valid, sc, -jnp.inf)
        key_scr[kb] = _sortable(sc)
        key_t = _sortable(sc.T)
        keyt_scr[kb] = key_t
        hi_scr[kb] = (key_t >> 16).astype(I16)
        lo_scr[kb] = ((key_t & 0xFFFF) - 2 ** 15).astype(I16)

    _for_blocks(0, qb, lambda kbs: [score_block(kb, None) for kb in kbs], sizes=(2, 1))
    score_block(qb, valid_diag)

    def per_query(x):
        return jnp.broadcast_to(x[None], (KEY_BLOCK // 8, 8, Q_TILE)).reshape(KEY_BLOCK, Q_TILE)

    def key_index(kb):
        return kb * KEY_BLOCK + lax.broadcasted_iota(I32, (KEY_BLOCK, Q_TILE), 0)

    def count(pred):
        def one(kb, acc):
            hit = jnp.where(pred(keyt_scr[kb], kb), 1.0, 0.0)
            return acc + jnp.sum(hit.reshape(KEY_BLOCK // 8, 8, Q_TILE), axis=0)

        acc = _fold_blocks(nkb, one, jnp.zeros((8, Q_TILE), F32))
        return jnp.broadcast_to(jnp.sum(acc, axis=0, keepdims=True), (8, Q_TILE))

    def per_query16(x):
        x16 = x.astype(I16)
        return jnp.broadcast_to(x16[None], (KEY_BLOCK // 16, 16, Q_TILE)).reshape(KEY_BLOCK, Q_TILE)

    def count16(ref, pred):
        def one(kb, acc):
            hit = jnp.where(pred(ref[kb]), jnp.int16(1), jnp.int16(0))
            for g in range(0, KEY_BLOCK, 16):
                acc = acc + hit[g:g + 16]
            return acc

        acc = _fold_blocks(nkb, one, jnp.zeros((16, Q_TILE), I16))
        return jnp.broadcast_to(jnp.sum(acc.astype(F32), axis=0, keepdims=True), (16, Q_TILE))

    hi_thr = jnp.where(count16(hi_scr, lambda h: h >= 0) >= kf,
                       jnp.zeros((16, Q_TILE), I32), jnp.full((16, Q_TILE), -(2 ** 15), I32))

    def hi_body(j, t):
        cand = t | lax.shift_left(jnp.int32(1), 14 - j)
        cw = per_query16(cand)
        return jnp.where(count16(hi_scr, lambda h: h >= cw) >= kf, cand, t)

    hi_thr = lax.fori_loop(0, 15, hi_body, hi_thr)
    hi_w = per_query16(hi_thr)
    above = count16(hi_scr, lambda h: h > hi_w)

    def keep_ties(kb, carry):
        lo_scr[kb] = jnp.where(hi_scr[kb] == hi_w, lo_scr[kb], jnp.int16(-(2 ** 15)))
        return carry

    lax.fori_loop(0, nkb, keep_ties, 0)

    def lo_body(j, t):
        cand = t | lax.shift_left(jnp.int32(1), 15 - j)
        cw = per_query16(cand - 2 ** 15)
        return jnp.where(above + count16(lo_scr, lambda lo: lo >= cw) >= kf, cand, t)

    lo_thr = lax.fori_loop(0, 16, lo_body, jnp.zeros((16, Q_TILE), I32))
    thr = (lax.shift_left(hi_thr, 16) | lo_thr)[0:8]
    thr_t = per_query(thr)

    need = kf - count(lambda key, kb: key > thr_t)
    ties = count(lambda key, kb: key == thr_t)
    last_scr[...] = jnp.full(last_scr.shape, s_len, I32)

    @pl.when(jnp.max(ties - need) > 0.0)
    def _():
        def tie_body(j, last):
            cand = last | lax.shift_left(jnp.int32(1), idx_bits - 1 - j)
            cw = per_query(cand)
            taken = count(lambda key, kb: (key == thr_t) & (key_index(kb) < cw))
            return jnp.where(taken < need, cand, last)

        last_scr[...] = lax.fori_loop(0, idx_bits, tie_body, jnp.zeros((8, Q_TILE), I32))

    def per_row(x):
        return _wide(jnp.broadcast_to(x[0:1], (LANES, Q_TILE)).T)

    tw = per_row(thr)
    lastw = per_row(last_scr[...])

    def mask_block(kb, valid):
        key = key_scr[kb]
        tie_neg = jnp.where(key == tw, jnp.where(cols(kb) <= lastw, 0.0, NEG), NEG)
        neg = jnp.where(key > tw, 0.0, tie_neg)
        if valid is not None:
            neg = jnp.where(valid, neg, NEG)
        neg_scr[kb] = neg

    def mask_body(kb, carry):
        mask_block(kb, None)
        return carry

    lax.fori_loop(0, qb, mask_body, 0)
    mask_block(qb, valid_diag)


def _dsa(proj, bias_tiles, bsz, s_len):
    topk = min(TOPK_MAX, s_len // 4)
    idx_bits = max(1, (s_len - 1).bit_length())
    n_blocks = s_len // KEY_BLOCK
    assert 2 * n_blocks <= 256
    kern = functools.partial(_dsa_kernel, topk=topk, idx_bits=idx_bits, s_len=s_len)
    lane_blk = lambda off: off // LANES
    return pl.pallas_call(
        kern,
        grid=(bsz, s_len // Q_TILE, A_HEADS // 2),
        in_specs=[pl.BlockSpec((1, Q_TILE, LANES), lambda b, i, p: (b, i, lane_blk(OFF_QA) + p)),
                  pl.BlockSpec((1, Q_TILE, 256), lambda b, i, p: (b, i, OFF_QI // 256)),
                  pl.BlockSpec((1, Q_TILE, LANES), lambda b, i, p: (b, i, lane_blk(OFF_WI))),
                  pl.BlockSpec((1, s_len, W_A), lambda b, i, p: (b, 0, OFF_KA // W_A),
                               pipeline_mode=pl.Buffered(1)),
                  pl.BlockSpec((1, s_len, W_A), lambda b, i, p: (b, 0, OFF_VA // W_A),
                               pipeline_mode=pl.Buffered(1)),
                  pl.BlockSpec((1, s_len, LANES), lambda b, i, p: (b, 0, lane_blk(OFF_KI))),
                  pl.BlockSpec(bias_tiles.shape, lambda b, i, p: (0, 0, 0, 0),
                               pipeline_mode=pl.Buffered(1))],
        out_specs=pl.BlockSpec((1, Q_TILE, LANES), lambda b, i, p: (b, i, p)),
        out_shape=jax.ShapeDtypeStruct((bsz, s_len, W_A), BF16),
        scratch_shapes=[pltpu.VMEM((n_blocks, Q_TILE, KEY_BLOCK), I32),
                        pltpu.VMEM((n_blocks, KEY_BLOCK, Q_TILE), I32),
                        pltpu.VMEM((n_blocks, KEY_BLOCK, Q_TILE), I16),
                        pltpu.VMEM((n_blocks, KEY_BLOCK, Q_TILE), I16),
                        pltpu.VMEM((n_blocks, Q_TILE, KEY_BLOCK), F32),
                        pltpu.VMEM((IDX_HEADS, Q_TILE, LANES), F32),
                        pltpu.VMEM((8, Q_TILE), I32)] + _softmax_scratch(s_len),
        compiler_params=_cparams("parallel", "arbitrary", "arbitrary"),
        name="dsa",
    )(proj, proj, proj, proj, proj, proj, bias_tiles)


def _diff_kernel(q_ref, k_ref, v_ref, bias_ref, lp_ref, g_ref, o_ref,
                 s_scr, mx_scr, l_scr, acc_scr, *, lam_init):
    qb = pl.program_id(2)
    qs = _split_heads(q_ref[0])

    def add_near(kb, s, t):
        return s + _tall(bias_ref[0, t])

    o = _softmax_pv(qs, lambda kb: k_ref[0, _key_rows(kb), :], lambda kb: v_ref[0, _key_rows(kb), :],
                    qb, lambda kb, s: s, add_near, s_scr, mx_scr, l_scr, acc_scr)
    lp = lp_ref[0]
    lam = (jnp.exp(jnp.sum(lp[0:1] * lp[1:2], axis=-1, keepdims=True))
           - jnp.exp(jnp.sum(lp[2:3] * lp[3:4], axis=-1, keepdims=True)) + lam_init)
    o = o[:Q_TILE] - lam * o[Q_TILE:]
    y = o * lax.rsqrt(jnp.mean(o * o, axis=-1, keepdims=True) + SUBLN_EPS)
    o_ref[0] = (y * g_ref[0] * (1.0 - lam_init)).astype(o_ref.dtype)


def _diff(proj, bias_tiles, lam_params, subln_g, layer, bsz, s_len, lam_init):
    kern = functools.partial(_diff_kernel, lam_init=lam_init)
    return pl.pallas_call(
        kern,
        grid=(bsz, B_HEADS, s_len // Q_TILE),
        in_specs=[pl.BlockSpec((1, Q_TILE, LANES), lambda b, h, i: (b, i, OFF_QB // LANES + h)),
                  pl.BlockSpec((1, s_len, LANES), lambda b, h, i: (b, 0, OFF_KB // LANES + h)),
                  pl.BlockSpec((1, s_len, LANES), lambda b, h, i: (b, 0, OFF_VB // LANES + h)),
                  pl.BlockSpec((1, 2, Q_TILE, KEY_BLOCK), lambda b, h, i: (h, 0, 0, 0)),
                  pl.BlockSpec((1, 4, HEAD_DIM), lambda b, h, i: (layer, 0, 0)),
                  pl.BlockSpec((1, 1, 2 * HEAD_DIM), lambda b, h, i: (layer, 0, 0))],
        out_specs=pl.BlockSpec((1, Q_TILE, LANES), lambda b, h, i: (b, i, h)),
        out_shape=jax.ShapeDtypeStruct((bsz, s_len, B_HEADS * 2 * HEAD_DIM), BF16),
        scratch_shapes=_softmax_scratch(s_len),
        compiler_params=_cparams("parallel", "parallel", "arbitrary"),
        name="diff_attn",
    )(proj, proj, proj, bias_tiles, lam_params, subln_g.reshape(-1, 1, 2 * HEAD_DIM))


def _stick_kernel(q_ref, k_ref, v_ref, o_ref, later_scr, run_scr, acc_scr):
    qb = pl.program_id(2)
    qs = _split_heads(q_ref[0])
    jj = lax.broadcasted_iota(I32, (2 * KEY_BLOCK, KEY_BLOCK), 0) & (KEY_BLOCK - 1)
    ss = lax.broadcasted_iota(I32, (2 * KEY_BLOCK, KEY_BLOCK), 1)
    later_scr[...] = jnp.where(jj > ss, 1.0, 0.0).astype(BF16)
    run_scr[...] = jnp.zeros_like(run_scr)
    acc_scr[...] = jnp.zeros_like(acc_scr)

    def sweep(kbs, diagonal):
        parts = []
        for kb in kbs:
            z = _nt_dot(qs, k_ref[0, _key_rows(kb), :])
            soft = jnp.log(1.0 + jnp.exp2(jnp.abs(z) * (-LOG2E)))
            log_beta = jnp.minimum(z, 0.0) - soft
            log_1mb = log_beta - z
            strict = None
            if diagonal:
                t = lax.broadcasted_iota(I32, z.shape, 0) & (Q_TILE - 1)
                strict = lax.broadcasted_iota(I32, z.shape, 1) < t
                log_1mb = jnp.where(strict, log_1mb, 0.0)
            hi = log_1mb.astype(BF16)
            lo = (log_1mb - hi.astype(F32)).astype(BF16)
            between = _dot(jnp.concatenate([hi, lo], axis=1), later_scr[...])
            total = between[:, 0:1] + log_1mb[:, 0:1]
            parts.append((kb, log_beta + between, total, strict))
        run = run_scr[...]
        acc = None
        for kb, logit, total, strict in parts:
            w = jnp.exp(logit + _wide(run))
            if strict is not None:
                w = jnp.where(strict, w, 0.0)
            pv = _dot(w.astype(BF16), v_ref[0, _key_rows(kb), :])
            acc = pv if acc is None else acc + pv
            run = run + jnp.broadcast_to(total, run.shape)
        acc_scr[...] += acc
        run_scr[...] = run

    sweep([qb], True)
    top = qb - 1
    for size in (GROUP, 2, 1):
        trips = (top + 1) // size

        def body(i, carry, top=top, size=size):
            sweep([top - i * size - u for u in range(size)], False)
            return carry

        lax.fori_loop(0, trips, body, 0)
        top = top - trips * size
    o_ref[0] = _merge_heads(acc_scr[...], Q_TILE).astype(o_ref.dtype)


def _stick(proj, bsz, s_len):
    return pl.pallas_call(
        _stick_kernel,
        grid=(bsz, C_HEADS // 2, s_len // Q_TILE),
        in_specs=[pl.BlockSpec((1, Q_TILE, LANES), lambda b, h, i: (b, i, OFF_QC // LANES + h)),
                  pl.BlockSpec((1, s_len, LANES), lambda b, h, i: (b, 0, OFF_KC // LANES + h)),
                  pl.BlockSpec((1, s_len, LANES), lambda b, h, i: (b, 0, OFF_VC // LANES + h))],
        out_specs=pl.BlockSpec((1, Q_TILE, LANES), lambda b, h, i: (b, i, h)),
        out_shape=jax.ShapeDtypeStruct((bsz, s_len, C_HEADS * HEAD_DIM), BF16),
        scratch_shapes=[pltpu.VMEM((2 * KEY_BLOCK, KEY_BLOCK), BF16),
                        pltpu.VMEM((2 * Q_TILE, LANES), F32),
                        pltpu.VMEM((2 * Q_TILE, LANES), F32)],
        compiler_params=_cparams("parallel", "parallel", "arbitrary"),
        name="stick_attn",
    )(proj, proj, proj)


def _merge_kernel(oa_ref, ob_ref, oc_ref, gate_ref, x_ref, g1_ref, wbr_ref, wout_ref, o_ref):
    d = x_ref.shape[-1]
    merged = jnp.zeros(x_ref.shape, F32)
    for j, o_br in enumerate((oa_ref, ob_ref, oc_ref)):
        w = o_br.shape[-1]
        gate = jax.nn.sigmoid(gate_ref[:, j * d:(j + 1) * d].astype(F32))
        merged = merged + gate * _dot(o_br[...], wbr_ref[j * w:(j + 1) * w, :])
    o_ref[...] = x_ref[...] + g1_ref[0] * _dot(merged.astype(BF16), wout_ref[...])


def _merge(o_a, o_b, o_c, proj, x2, mod, w_br, w_out, s_len):
    t, d = x2.shape
    tm = 512
    per_b = s_len // tm
    w = o_a.shape[-1]
    return pl.pallas_call(
        _merge_kernel,
        grid=(t // tm,),
        in_specs=[pl.BlockSpec((tm, w), lambda i: (i, 0)),
                  pl.BlockSpec((tm, w), lambda i: (i, 0)),
                  pl.BlockSpec((tm, w), lambda i: (i, 0)),
                  pl.BlockSpec((tm, 3 * d), lambda i: (i, OFF_G // (3 * d))),
                  pl.BlockSpec((tm, d), lambda i: (i, 0)),
                  pl.BlockSpec((1, 1, d), lambda i: (i // per_b, 0, 2)),
                  pl.BlockSpec(w_br.shape, lambda i: (0, 0)),
                  pl.BlockSpec(w_out.shape, lambda i: (0, 0))],
        out_specs=pl.BlockSpec((tm, d), lambda i: (i, 0)),
        out_shape=jax.ShapeDtypeStruct((t, d), F32),
        compiler_params=_cparams("parallel"),
        name="merge",
    )(o_a, o_b, o_c, proj, x2, mod, w_br, w_out)


def _finish(x, gate, acc, fg_ref, final):
    y = x + gate * acc
    if final:
        y = y * lax.rsqrt(jnp.mean(y * y, axis=-1, keepdims=True) + EPS) * fg_ref[...]
    return y


def _swiglu_partial(h, w1, w3, w2):
    a = _dot(h, w1)
    act = (a * jax.nn.sigmoid(a)) * _dot(h, w3)
    return _dot(act.astype(BF16), w2)


def _ffn_kernel(x_ref, g_ref, sc_ref, sh_ref, gate_ref, w1_ref, w3_ref, w2_ref, fg_ref, o_ref,
                h_scr, acc_scr, *, final):
    f = pl.program_id(1)

    @pl.when(f == 0)
    def _():
        h_scr[...] = _modulated_norm(x_ref[...], g_ref[...], sc_ref[0], sh_ref[0]).astype(BF16)
        acc_scr[...] = jnp.zeros_like(acc_scr)

    acc_scr[...] += _swiglu_partial(h_scr[...], w1_ref[...], w3_ref[...], w2_ref[...])

    @pl.when(f == pl.num_programs(1) - 1)
    def _():
        o_ref[...] = _finish(x_ref[...], gate_ref[0], acc_scr[...], fg_ref, final)


def _ffn(x2, g, mod, w1, w3, w2, final_g, s_len, final):
    t, d = x2.shape
    d_ff = w1.shape[1]
    tm, tf = 1024, 256
    per_b = s_len // tm
    kern = functools.partial(_ffn_kernel, final=final)
    return pl.pallas_call(
        kern,
        grid=(t // tm, d_ff // tf),
        in_specs=[pl.BlockSpec((tm, d), lambda i, f: (i, 0)),
                  pl.BlockSpec((1, d), lambda i, f: (0, 0)),
                  pl.BlockSpec((1, 1, d), lambda i, f: (i // per_b, 0, 4)),
                  pl.BlockSpec((1, 1, d), lambda i, f: (i // per_b, 0, 3)),
                  pl.BlockSpec((1, 1, d), lambda i, f: (i // per_b, 0, 5)),
                  pl.BlockSpec((d, tf), lambda i, f: (0, f)),
                  pl.BlockSpec((d, tf), lambda i, f: (0, f)),
                  pl.BlockSpec((tf, d), lambda i, f: (f, 0)),
                  pl.BlockSpec((1, d), lambda i, f: (0, 0))],
        out_specs=pl.BlockSpec((tm, d), lambda i, f: (i, 0)),
        out_shape=jax.ShapeDtypeStruct((t, d), F32),
        scratch_shapes=[pltpu.VMEM((tm, d), BF16), pltpu.VMEM((tm, d), F32)],
        compiler_params=_cparams("parallel", "arbitrary"),
        name="ffn",
    )(x2, g, mod, mod, mod, w1, w3, w2, final_g)


MOE_TM = 512
MOE_ROWS = 512
R_E1, R_E2, R_RANK1, R_RANK2, R_W1, R_W2 = 0, 1, 2, 3, 4, 5


def _lane_pick(tile, lane, idx):
    return jnp.sum(jnp.where(lane == idx, tile, 0.0), axis=-1, keepdims=True)


def _route_kernel(x_ref, g_ref, sc_ref, sh_ref, wr_ref, br_ref, h_ref, rec_ref, cnt_ref, cnt_scr):
    @pl.when(pl.program_id(0) == 0)
    def _():
        cnt_scr[...] = jnp.zeros_like(cnt_scr)

    h = _modulated_norm(x_ref[...], g_ref[...], sc_ref[0], sh_ref[0])
    half = h.shape[1] // 2
    bits = pltpu.bitcast(h.astype(BF16).astype(F32), jnp.uint32)
    h_ref[...] = (bits[:, half:] & jnp.uint32(0xFFFF0000)) | (bits[:, :half] >> 16)

    h_hi = h.astype(BF16)
    h_lo = (h - h_hi.astype(F32)).astype(BF16)
    wr = wr_ref[...]
    w_hi = wr.astype(BF16)
    w_lo = (wr - w_hi.astype(F32)).astype(BF16)
    logits = _dot(h_hi, w_hi) + _dot(h_hi, w_lo) + _dot(h_lo, w_hi) + br_ref[...]
    lane = lax.broadcasted_iota(I32, logits.shape, 1).astype(F32)
    lg = jnp.where(lane < N_EXPERTS, logits, -jnp.inf)
    m1 = jnp.max(lg, axis=-1, keepdims=True)
    i1 = jnp.min(jnp.where(lg == m1, lane, float(LANES)), axis=-1, keepdims=True)
    lg2 = jnp.where(lane == i1, -jnp.inf, lg)
    m2 = jnp.max(lg2, axis=-1, keepdims=True)
    i2 = jnp.min(jnp.where(lg2 == m2, lane, float(LANES)), axis=-1, keepdims=True)
    e2 = jnp.exp(m2 - m1)
    w_top = 1.0 / (1.0 + e2)

    chosen = jnp.where(lane == i1, 1.0, 0.0) + jnp.where(lane == i2, 1.0, 0.0)
    tm = chosen.shape[0]
    earlier = jnp.where(lax.broadcasted_iota(I32, (tm, tm), 1) < lax.broadcasted_iota(I32, (tm, tm), 0),
                        1.0, 0.0).astype(BF16)
    rank = _dot(earlier, chosen.astype(BF16)) + cnt_scr[0:1, :]
    rec = jnp.zeros_like(logits)
    for slot, val in ((R_E1, i1), (R_E2, i2), (R_RANK1, _lane_pick(rank, lane, i1)),
                      (R_RANK2, _lane_pick(rank, lane, i2)), (R_W1, w_top), (R_W2, e2 * w_top)):
        rec = jnp.where(lane == float(slot), val, rec)
    rec_ref[...] = rec
    cnt_scr[0:1, :] = cnt_scr[0:1, :] + jnp.sum(chosen, axis=0, keepdims=True)
    cnt_ref[...] = cnt_scr[...]


def _route(x2, g, mod, wr_pad, br_pad, s_len):
    t, d = x2.shape
    tm = MOE_TM
    per_b = s_len // tm
    return pl.pallas_call(
        _route_kernel,
        grid=(t // tm,),
        in_specs=[pl.BlockSpec((tm, d), lambda i: (i, 0)),
                  pl.BlockSpec((1, d), lambda i: (0, 0)),
                  pl.BlockSpec((1, 1, d), lambda i: (i // per_b, 0, 4)),
                  pl.BlockSpec((1, 1, d), lambda i: (i // per_b, 0, 3)),
                  pl.BlockSpec((d, LANES), lambda i: (0, 0)),
                  pl.BlockSpec((1, LANES), lambda i: (0, 0))],
        out_specs=[pl.BlockSpec((tm, d // 2), lambda i: (i, 0)),
                   pl.BlockSpec((tm, LANES), lambda i: (i, 0)),
                   pl.BlockSpec((8, LANES), lambda i: (0, 0))],
        out_shape=[jax.ShapeDtypeStruct((t, d // 2), jnp.uint32),
                   jax.ShapeDtypeStruct((t, LANES), F32),
                   jax.ShapeDtypeStruct((8, LANES), F32)],
        scratch_shapes=[pltpu.VMEM((8, LANES), F32)],
        compiler_params=_cparams("arbitrary"),
        name="moe_route",
    )(x2, g, mod, mod, wr_pad, br_pad)


def _row_copy(src_ref, src_row, dst_ref, dst_row, sem):
    return pltpu.make_async_copy(src_ref.at[pl.ds(src_row, 1)], dst_ref.at[pl.ds(dst_row, 1)], sem)


def _dispatch_kernel(pos_ref, h_ref, xs_in_ref, xs_ref, sem):
    del xs_in_ref
    tm = h_ref.shape[0]

    def start(t, carry):
        _row_copy(h_ref, t, xs_ref, pos_ref[0, 0, t], sem).start(priority=0)
        _row_copy(h_ref, t, xs_ref, pos_ref[0, 0, tm + t], sem).start(priority=1)
        return carry

    lax.fori_loop(0, tm, start, 0)
    for _ in range(2):
        pltpu.make_async_copy(h_ref, xs_ref.at[pl.ds(0, tm)], sem).wait()


def _dispatch(pos, h_packed, n_rows):
    t, w = h_packed.shape
    tm = MOE_TM
    xs0 = jnp.zeros((n_rows, w), h_packed.dtype)
    return pl.pallas_call(
        _dispatch_kernel,
        grid=(t // tm,),
        in_specs=[pl.BlockSpec((1, 1, 2 * tm), lambda i: (i, 0, 0), memory_space=pltpu.SMEM),
                  pl.BlockSpec((tm, w), lambda i: (i, 0)),
                  pl.BlockSpec(memory_space=pl.ANY)],
        out_specs=pl.BlockSpec(memory_space=pl.ANY),
        out_shape=jax.ShapeDtypeStruct((n_rows, w), h_packed.dtype),
        scratch_shapes=[pltpu.SemaphoreType.DMA(())],
        input_output_aliases={2: 0},
        compiler_params=_cparams("arbitrary"),
        name="moe_dispatch",
    )(pos, h_packed, xs0)


def _expert_kernel(te_ref, nu_ref, xs_ref, w1_ref, w3_ref, w2_ref, o_ref, h_scr, acc_scr):
    i = pl.program_id(0)
    f = pl.program_id(1)

    @pl.when(i < nu_ref[0])
    def _():
        @pl.when(f == 0)
        def _():
            word = xs_ref[...]
            lo = pltpu.bitcast(word << 16, F32)
            hi = pltpu.bitcast(word & jnp.uint32(0xFFFF0000), F32)
            h_scr[...] = jnp.concatenate([lo, hi], axis=1).astype(BF16)
            acc_scr[...] = jnp.zeros_like(acc_scr)

        acc_scr[...] += _swiglu_partial(h_scr[...], w1_ref[0], w3_ref[0], w2_ref[0])

        @pl.when(f == pl.num_programs(1) - 1)
        def _():
            o_ref[...] = acc_scr[...]

    @pl.when((i >= nu_ref[0]) & (f == pl.num_programs(1) - 1))
    def _():
        o_ref[...] = jnp.zeros_like(o_ref)


def _experts(tile_expert, n_used, xs, w1, w3, w2):
    n_rows, half = xs.shape
    d = 2 * half
    d_ff = w1.shape[2]
    tf = 896
    nf = d_ff // tf
    tile = lambda i, nu: jnp.minimum(i, nu[0] - 1)
    ff = lambda i, f, nu: jnp.where(i < nu[0], f, nf - 1)
    return pl.pallas_call(
        _expert_kernel,
        grid_spec=pltpu.PrefetchScalarGridSpec(
            num_scalar_prefetch=2,
            grid=(n_rows // MOE_ROWS, nf),
            in_specs=[pl.BlockSpec((MOE_ROWS, half), lambda i, f, te, nu: (tile(i, nu), 0)),
                      pl.BlockSpec((1, d, tf), lambda i, f, te, nu: (te[i], 0, ff(i, f, nu))),
                      pl.BlockSpec((1, d, tf), lambda i, f, te, nu: (te[i], 0, ff(i, f, nu))),
                      pl.BlockSpec((1, tf, d), lambda i, f, te, nu: (te[i], ff(i, f, nu), 0))],
            out_specs=pl.BlockSpec((MOE_ROWS, d), lambda i, f, te, nu: (i, 0)),
            scratch_shapes=[pltpu.VMEM((MOE_ROWS, d), BF16), pltpu.VMEM((MOE_ROWS, d), F32)]),
        out_shape=jax.ShapeDtypeStruct((n_rows, d), F32),
        compiler_params=_cparams("arbitrary", "arbitrary"),
        name="moe_experts",
    )(tile_expert, n_used, xs, w1, w3, w2)


def _combine_kernel(pos_ref, x_ref, gate_ref, rec_ref, fg_ref, ys_ref, o_ref, y_scr, sem, *, final):
    tm = x_ref.shape[0]

    def start(t, carry):
        _row_copy(ys_ref, pos_ref[0, 0, t], y_scr.at[0], t, sem).start(priority=0)
        _row_copy(ys_ref, pos_ref[0, 0, tm + t], y_scr.at[1], t, sem).start(priority=1)
        return carry

    lax.fori_loop(0, tm, start, 0)
    for slot in range(2):
        pltpu.make_async_copy(ys_ref.at[pl.ds(0, tm)], y_scr.at[slot], sem).wait()
    rec = rec_ref[...]
    lane = lax.broadcasted_iota(I32, rec.shape, 1)
    w_first = jnp.sum(jnp.where(lane == R_W1, rec, 0.0), axis=-1, keepdims=True)
    w_second = jnp.sum(jnp.where(lane == R_W2, rec, 0.0), axis=-1, keepdims=True)
    f = w_first * y_scr[0] + w_second * y_scr[1]
    o_ref[...] = _finish(x_ref[...], gate_ref[0], f, fg_ref, final)


def _combine(pos, x2, mod, rec, final_g, ys, s_len, final):
    t, d = x2.shape
    tm = MOE_TM
    per_b = s_len // tm
    kern = functools.partial(_combine_kernel, final=final)
    return pl.pallas_call(
        kern,
        grid=(t // tm,),
        in_specs=[pl.BlockSpec((1, 1, 2 * tm), lambda i: (i, 0, 0), memory_space=pltpu.SMEM),
                  pl.BlockSpec((tm, d), lambda i: (i, 0)),
                  pl.BlockSpec((1, 1, d), lambda i: (i // per_b, 0, 5)),
                  pl.BlockSpec((tm, LANES), lambda i: (i, 0)),
                  pl.BlockSpec((1, d), lambda i: (0, 0)),
                  pl.BlockSpec(memory_space=pl.ANY)],
        out_specs=pl.BlockSpec((tm, d), lambda i: (i, 0)),
        out_shape=jax.ShapeDtypeStruct((t, d), F32),
        scratch_shapes=[pltpu.VMEM((2, tm, d), F32), pltpu.SemaphoreType.DMA(())],
        compiler_params=_cparams("arbitrary"),
        name="moe_combine",
    )(pos, x2, mod, rec, final_g, ys)


def _moe(x2, g, mod, wr_pad, br_pad, w1, w3, w2, final_g, s_len, final):
    t, d = x2.shape
    n_e = w1.shape[0]
    h_packed, rec, cnt = _route(x2, g, mod, wr_pad, br_pad, s_len)

    n_tiles = 2 * t // MOE_ROWS + n_e
    counts = cnt[0, :n_e].astype(I32)
    padded = (counts + MOE_ROWS - 1) // MOE_ROWS * MOE_ROWS
    ends = jnp.cumsum(padded)
    starts = ends - padded
    e1, e2 = rec[:, R_E1].astype(I32), rec[:, R_E2].astype(I32)
    pos1 = starts[e1] + rec[:, R_RANK1].astype(I32)
    pos2 = starts[e2] + rec[:, R_RANK2].astype(I32)
    pos = jnp.concatenate([pos1.reshape(-1, 1, MOE_TM), pos2.reshape(-1, 1, MOE_TM)], axis=2)
    n_used = (ends[-1] // MOE_ROWS).reshape(1)
    tile_start = jnp.minimum(jnp.arange(n_tiles, dtype=I32), n_used[0] - 1) * MOE_ROWS
    tile_expert = jnp.sum(tile_start[:, None] >= ends[None, :], axis=1).astype(I32)

    xs = _dispatch(pos, h_packed, n_tiles * MOE_ROWS)
    ys = _experts(tile_expert, n_used, xs, w1, w3, w2)
    return _combine(pos, x2, mod, rec, final_g, ys, s_len, final)


def _pack_w_in(w):
    d = w.shape[0]
    sizes = (W_A, W_A, W_A, IDX_HEADS * IDX_DIM, IDX_DIM, IDX_HEADS,
             512, 512, 512, 512, 512, 512, 3 * d)
    offs = [0]
    for s in sizes:
        offs.append(offs[-1] + s)
    (qa, ka, va, qi, ki, wi, qb, kb, vb, qc, kc, vc, gl) = [
        w[:, offs[j]:offs[j + 1]] for j in range(len(sizes))]
    scale = HEAD_DIM ** -0.5
    pad_wi = jnp.zeros((d, LANES - IDX_HEADS), w.dtype)
    packed = jnp.concatenate(
        [gl, qa * scale, ka, va,
         qi * (IDX_DIM ** -0.5), ki, ki, wi * (IDX_HEADS ** -0.5), pad_wi,
         qb * scale, kb, vb, qc * scale, kc, vc], axis=1)
    assert packed.shape[1] == PACKED
    return packed.astype(BF16)


def kernel(x, c, w_ada, b_ada, norm1_g, norm2_g, w_in, w_br, w_out, rel_bias, lam_params,
           subln_g, ffn_w1, ffn_w3, ffn_w2, router_w, router_b, moe_w1, moe_w3, moe_w2, final_g):
    bsz, s_len, d = x.shape
    depth = w_ada.shape[0]
    assert s_len % 1024 == 0 and d == 1024 and OFF_G + 3 * d == OFF_QA

    c_pad = jnp.concatenate([c, jnp.zeros((8 - bsz % 8 if bsz % 8 else 0, d), c.dtype)], axis=0)
    mod_all = _ada(c_pad, w_ada, b_ada)
    dsa_tiles, diff_tiles = _bias_tiles(rel_bias)
    fg = final_g.reshape(1, d)

    x2 = x.reshape(bsz * s_len, d)
    for l in range(depth):
        mod = mod_all[l, :bsz].reshape(bsz, 1, 6 * d)
        proj = _inproj(x2, norm1_g[l].reshape(1, d), mod, _pack_w_in(w_in[l]), s_len)
        proj3 = proj.reshape(bsz, s_len, PACKED)
        lam_init = 0.8 - 0.6 * math.exp(-0.3 * l)
        o_a = _dsa(proj3, dsa_tiles, bsz, s_len)
        o_b = _diff(proj3, diff_tiles, lam_params, subln_g, l, bsz, s_len, lam_init)
        o_c = _stick(proj3, bsz, s_len)
        x2 = _merge(o_a.reshape(-1, o_a.shape[-1]), o_b.reshape(-1, o_b.shape[-1]),
                    o_c.reshape(-1, o_c.shape[-1]), proj, x2, mod,
                    w_br[l].astype(BF16), w_out[l].astype(BF16), s_len)
        g2 = norm2_g[l].reshape(1, d)
        final = l == depth - 1
        j = l // 2
        if l % 2 == 0:
            x2 = _ffn(x2, g2, mod, ffn_w1[j].astype(BF16), ffn_w3[j].astype(BF16),
                      ffn_w2[j].astype(BF16), fg, s_len, final)
        else:
            wr_pad = jnp.pad(router_w[j], ((0, 0), (0, LANES - N_EXPERTS)))
            br_pad = jnp.pad(router_b[j], (0, LANES - N_EXPERTS)).reshape(1, LANES)
            x2 = _moe(x2, g2, mod, wr_pad, br_pad, moe_w1[j].astype(BF16), moe_w3[j].astype(BF16),
                      moe_w2[j].astype(BF16), fg, s_len, final)
    return x2.reshape(bsz, s_len, d)
```

```python
import functools
import math

import jax
import jax.numpy as jnp
from jax import lax
from jax.experimental import pallas as pl
from jax.experimental.pallas import tpu as pltpu

F32 = jnp.float32
BF16 = jnp.bfloat16
I32 = jnp.int32
I16 = jnp.int16

LANES = 128
VMEM_LIMIT_BYTES = 56 * 1024 * 1024

CHUNK = 64
A_HEADS = 8
IDX_HEADS = 4
IDX_DIM = 64
TOPK_MAX = 256
B_HEADS = 4
C_HEADS = 8
HEAD_DIM = 64
REL_BUCKETS = 32
FAR_BUCKET = REL_BUCKETS // 2 - 1
N_EXPERTS = 8
EPS = 1e-6
SUBLN_EPS = 1e-5
NEG = -1e30
LOG2E = 1.4426950408889634
INT_MIN = -(2 ** 31)

KEY_BLOCK = 256
Q_TILE = 256
GROUP = 4
BLOCK_GROUPS = (8, 4, 2, 1)

W_A = A_HEADS * HEAD_DIM
OFF_G = 0
OFF_QA, OFF_KA, OFF_VA = 3072, 3584, 4096
OFF_QI, OFF_KI, OFF_WI = 4608, 4864, 4992
OFF_QB, OFF_KB, OFF_VB = 5120, 5632, 6144
OFF_QC, OFF_KC, OFF_VC = 6656, 7168, 7680
PACKED = 8192

LOG_BUCKET_STEPS = (12, 16, 23, 32, 46, 64, 91)


def _nt_dot(a, b):
    return lax.dot_general(a, b, (((1,), (1,)), ((), ())), preferred_element_type=F32)


def _dot(a, b):
    return jnp.dot(a, b, preferred_element_type=F32)


def _cparams(*sem):
    return pltpu.CompilerParams(dimension_semantics=sem, vmem_limit_bytes=VMEM_LIMIT_BYTES)


def _split_heads(x):
    lane = lax.broadcasted_iota(I32, x.shape, 1)
    keep_a = jnp.where(lane < HEAD_DIM, 1.0, 0.0).astype(x.dtype)
    keep_b = jnp.where(lane < HEAD_DIM, 0.0, 1.0).astype(x.dtype)
    return jnp.concatenate([x * keep_a, x * keep_b], axis=0)


def _merge_heads(o, m):
    lane = lax.broadcasted_iota(I32, (m, LANES), 1)
    return jnp.where(lane < HEAD_DIM, o[:m], o[m:])


def _wide(x):
    return jnp.concatenate([x, x], axis=1)


def _tall(x):
    return jnp.concatenate([x, x], axis=0)


def _key_rows(kb):
    return pl.ds(pl.multiple_of(kb * KEY_BLOCK, KEY_BLOCK), KEY_BLOCK)


def _sortable(x):
    bits = pltpu.bitcast(x, I32)
    return bits ^ ((bits >> 31) & 0x7FFFFFFF)


def _chunk_causal():
    r = lax.broadcasted_iota(I32, (Q_TILE, KEY_BLOCK), 0)
    c = lax.broadcasted_iota(I32, (Q_TILE, KEY_BLOCK), 1)
    return (c // CHUNK) <= (r // CHUNK)


def _ada_kernel(c_ref, w_ref, b_ref, o_ref):
    c = c_ref[...]
    a = c * jax.nn.sigmoid(c)
    o_ref[0] = jnp.dot(a, w_ref[0], preferred_element_type=F32,
                       precision=lax.Precision.HIGHEST) + b_ref[0]


def _ada(c_pad, w_ada, b_ada):
    depth, d, n = w_ada.shape
    tn = 1024
    return pl.pallas_call(
        _ada_kernel,
        grid=(depth, n // tn),
        in_specs=[pl.BlockSpec(c_pad.shape, lambda l, j: (0, 0)),
                  pl.BlockSpec((1, d, tn), lambda l, j: (l, 0, j)),
                  pl.BlockSpec((1, 1, tn), lambda l, j: (l, 0, j))],
        out_specs=pl.BlockSpec((1, c_pad.shape[0], tn), lambda l, j: (l, 0, j)),
        out_shape=jax.ShapeDtypeStruct((depth, c_pad.shape[0], n), F32),
        compiler_params=_cparams("parallel", "parallel"),
        name="ada",
    )(c_pad, w_ada, b_ada.reshape(depth, 1, n))


def _rel_bias_tile(tab_ref, head, d0, n_heads_total):
    r = lax.broadcasted_iota(I32, (Q_TILE, KEY_BLOCK), 0)
    c = lax.broadcasted_iota(I32, (Q_TILE, KEY_BLOCK), 1)
    d = c - r + d0
    n = jnp.abs(d)
    large = jnp.full(d.shape, REL_BUCKETS // 4, I32)
    for step in LOG_BUCKET_STEPS:
        large = large + jnp.where(n >= step, 1, 0)
    bucket = jnp.where(d > 0, REL_BUCKETS // 2, 0) + jnp.where(n < REL_BUCKETS // 4, n, large)
    out = jnp.zeros(d.shape, F32)
    for b in range(REL_BUCKETS):
        out = jnp.where(bucket == b, tab_ref[b * n_heads_total + head], out)
    return out - tab_ref[FAR_BUCKET * n_heads_total + head]


def _bias_kernel(tab_ref, dsa_ref, diff_ref):
    p = pl.program_id(0)
    n_heads = A_HEADS + B_HEADS
    dsa_ref[0, 0] = jnp.zeros(dsa_ref.shape[2:], F32)
    diff_ref[0, 0] = jnp.zeros(diff_ref.shape[2:], F32)
    for t, d0 in ((1, -KEY_BLOCK), (2, 0)):
        dsa_ref[0, t, 0:Q_TILE, :] = _rel_bias_tile(tab_ref, 2 * p, d0, n_heads)
        dsa_ref[0, t, Q_TILE:2 * Q_TILE, :] = _rel_bias_tile(tab_ref, 2 * p + 1, d0, n_heads)
        tile = _rel_bias_tile(tab_ref, A_HEADS + p, d0, n_heads)
        if t == 2:
            tile = jnp.where(_chunk_causal(), tile, NEG)
        diff_ref[0, t] = tile


def _bias_tiles(rel_bias):
    tab = rel_bias.reshape(-1)
    return pl.pallas_call(
        _bias_kernel,
        grid=(4,),
        in_specs=[pl.BlockSpec(memory_space=pltpu.SMEM)],
        out_specs=[pl.BlockSpec((1, 3, 2 * Q_TILE, KEY_BLOCK), lambda p: (p, 0, 0, 0)),
                   pl.BlockSpec((1, 3, Q_TILE, KEY_BLOCK), lambda p: (p, 0, 0, 0))],
        out_shape=[jax.ShapeDtypeStruct((A_HEADS // 2, 3, 2 * Q_TILE, KEY_BLOCK), F32),
                   jax.ShapeDtypeStruct((B_HEADS, 3, Q_TILE, KEY_BLOCK), F32)],
        compiler_params=_cparams("parallel"),
        name="rel_bias_tiles",
    )(tab)


def _modulated_norm(x, g, sc, sh):
    y = x * lax.rsqrt(jnp.mean(x * x, axis=-1, keepdims=True) + EPS)
    return y * g * (1.0 + sc) + sh


def _inproj_kernel(x_ref, g_ref, sc_ref, sh_ref, w_ref, o_ref, h_scr):
    @pl.when(pl.program_id(1) == 0)
    def _():
        h_scr[...] = _modulated_norm(x_ref[...], g_ref[...], sc_ref[0], sh_ref[0]).astype(BF16)

    o_ref[...] = _dot(h_scr[...], w_ref[...]).astype(o_ref.dtype)


def _inproj(x2, g, mod, w_packed, s_len):
    t, d = x2.shape
    n = w_packed.shape[1]
    tm, tn = 1024, 1024
    per_b = s_len // tm
    return pl.pallas_call(
        _inproj_kernel,
        grid=(t // tm, n // tn),
        in_specs=[pl.BlockSpec((tm, d), lambda i, j: (i, 0)),
                  pl.BlockSpec((1, d), lambda i, j: (0, 0)),
                  pl.BlockSpec((1, 1, d), lambda i, j: (i // per_b, 0, 1)),
                  pl.BlockSpec((1, 1, d), lambda i, j: (i // per_b, 0, 0)),
                  pl.BlockSpec((d, tn), lambda i, j: (0, j))],
        out_specs=pl.BlockSpec((tm, tn), lambda i, j: (i, j)),
        out_shape=jax.ShapeDtypeStruct((t, n), BF16),
        scratch_shapes=[pltpu.VMEM((tm, d), BF16)],
        compiler_params=_cparams("parallel", "arbitrary"),
        name="inproj",
    )(x2, g, mod, mod, w_packed)


def _for_blocks(lo, hi, fn, sizes=BLOCK_GROUPS):
    pos = lo
    for size in sizes:
        trips = jnp.maximum(hi - pos, 0) // size

        def body(i, carry, pos=pos, size=size):
            fn([pos + i * size + u for u in range(size)])
            return carry

        lax.fori_loop(0, trips, body, 0)
        pos = pos + trips * size


def _fold_blocks(n, fn, acc):
    pos = 0
    for size in BLOCK_GROUPS:
        trips = (n - pos) // size

        def body(i, acc, pos=pos, size=size):
            for u in range(size):
                acc = fn(pos + i * size + u, acc)
            return acc

        acc = lax.fori_loop(0, trips, body, acc)
        pos = pos + trips * size
    return acc


def _softmax_pv(qs, k_at, v_at, qb, add, s_scr, mx_scr, l_scr, acc_scr):
    def pass1(blocks):
        mx = None
        for kb in blocks:
            s = add(kb, _nt_dot(qs, k_at(kb)), jnp.clip(kb - qb + 2, 0, 2))
            s = s * LOG2E
            s_scr[kb] = s
            fold = jnp.maximum(s[:, :LANES], s[:, LANES:])
            mx = fold if mx is None else jnp.maximum(mx, fold)
        mx_scr[...] = jnp.maximum(mx_scr[...], mx)

    mx_scr[...] = jnp.full(mx_scr.shape, NEG, F32)
    _for_blocks(0, qb + 1, pass1)

    m = jnp.max(mx_scr[...], axis=-1, keepdims=True)
    mx_scr[...] = jnp.broadcast_to(m, mx_scr.shape)
    l_scr[...] = jnp.zeros_like(l_scr)
    acc_scr[...] = jnp.zeros_like(acc_scr)

    def pass2(kbs):
        mw = _wide(mx_scr[...])
        l_add = acc_add = None
        for kb in kbs:
            p = jnp.exp2(s_scr[kb] - mw)
            fold = p[:, :LANES] + p[:, LANES:]
            pv = _dot(p.astype(BF16), v_at(kb))
            l_add = fold if l_add is None else l_add + fold
            acc_add = pv if acc_add is None else acc_add + pv
        l_scr[...] += l_add
        acc_scr[...] += acc_add

    _for_blocks(0, qb + 1, pass2)
    return acc_scr[...] / jnp.sum(l_scr[...], axis=-1, keepdims=True)


def _softmax_scratch(s_len):
    return [pltpu.VMEM((s_len // KEY_BLOCK, 2 * Q_TILE, KEY_BLOCK), F32),
            pltpu.VMEM((2 * Q_TILE, LANES), F32),
            pltpu.VMEM((2 * Q_TILE, LANES), F32),
            pltpu.VMEM((2 * Q_TILE, LANES), F32)]


def _dsa_kernel(qa_ref, qi_ref, wi_ref, ka_ref, va_ref, ki_ref, bias_ref, o_ref,
                key_scr, keyt_scr, hi_scr, lo_scr, neg_scr, w_scr, last_scr, s_scr, mx_scr, l_scr, acc_scr,
                *, topk, idx_bits, s_len):
    qb = pl.program_id(1)

    @pl.when(pl.program_id(2) == 0)
    def _():
        _dsa_select(qb, qi_ref, wi_ref, ki_ref, key_scr, keyt_scr, hi_scr, lo_scr, neg_scr, w_scr, last_scr,
                    topk=topk, idx_bits=idx_bits, s_len=s_len)

    pr = pl.program_id(2)
    lanes = pl.ds(pl.multiple_of(pr * LANES, LANES), LANES)

    def add(kb, s, t):
        return s + _tall(neg_scr[kb]) + bias_ref[pr, t]

    o = _softmax_pv(_split_heads(qa_ref[0]),
                    lambda kb: ka_ref[0, _key_rows(kb), lanes],
                    lambda kb: va_ref[0, _key_rows(kb), lanes],
                    qb, add, s_scr, mx_scr, l_scr, acc_scr)
    o_ref[0] = _merge_heads(o, Q_TILE).astype(o_ref.dtype)


def _dsa_select(qb, qi_ref, wi_ref, ki_ref, key_scr, keyt_scr, hi_scr, lo_scr, neg_scr, w_scr, last_scr,
                *, topk, idx_bits, s_len):
    nkb = qb + 1
    kf = float(topk)
    valid_diag = _chunk_causal()

    def cols(kb):
        return kb * KEY_BLOCK + lax.broadcasted_iota(I32, (Q_TILE, KEY_BLOCK), 1)

    qi = qi_ref[0]
    q_heads = [_split_heads(qi[:, pr * LANES:(pr + 1) * LANES]) for pr in range(IDX_HEADS // 2)]
    wi = wi_ref[0].astype(F32)
    for h in range(IDX_HEADS):
        w_scr[h] = jnp.broadcast_to(wi[:, h:h + 1], (Q_TILE, LANES))

    def score_block(kb, valid):
        kk = ki_ref[0, _key_rows(kb), :]
        sc = jnp.zeros((Q_TILE, KEY_BLOCK), F32)
        for pr in range(IDX_HEADS // 2):
            d = _nt_dot(q_heads[pr], kk)
            sc = sc + _wide(w_scr[2 * pr]) * jnp.maximum(d[:Q_TILE], 0.0)
            sc = sc + _wide(w_scr[2 * pr + 1]) * jnp.maximum(d[Q_TILE:], 0.0)
        sc = jnp.where(sc == 0.0, 0.0, sc)
        if valid is not None:
            sc = jnp.where(valid, sc, -jnp.inf)
        key_scr[kb] = _sortable(sc)
        key_t = _sortable(sc.T)
        keyt_scr[kb] = key_t
        hi_scr[kb] = (key_t >> 16).astype(I16)
        lo_scr[kb] = ((key_t & 0xFFFF) - 2 ** 15).astype(I16)

    _for_blocks(0, qb, lambda kbs: [score_block(kb, None) for kb in kbs], sizes=(2, 1))
    score_block(qb, valid_diag)

    def per_query(x):
        return jnp.broadcast_to(x[None], (KEY_BLOCK // 8, 8, Q_TILE)).reshape(KEY_BLOCK, Q_TILE)

    def key_index(kb):
        return kb * KEY_BLOCK + lax.broadcasted_iota(I32, (KEY_BLOCK, Q_TILE), 0)

    def count(pred):
        def one(kb, acc):
            hit = jnp.where(pred(keyt_scr[kb], kb), 1.0, 0.0)
            return acc + jnp.sum(hit.reshape(KEY_BLOCK // 8, 8, Q_TILE), axis=0)

        acc = _fold_blocks(nkb, one, jnp.zeros((8, Q_TILE), F32))
        return jnp.broadcast_to(jnp.sum(acc, axis=0, keepdims=True), (8, Q_TILE))

    def per_query16(x):
        x16 = x.astype(I16)
        return jnp.broadcast_to(x16[None], (KEY_BLOCK // 16, 16, Q_TILE)).reshape(KEY_BLOCK, Q_TILE)

    def count16(ref, pred):
        def one(kb, acc):
            hit = jnp.where(pred(ref[kb]), jnp.int16(1), jnp.int16(0))
            for g in range(0, KEY_BLOCK, 16):
                acc = acc + hit[g:g + 16]
            return acc

        acc = _fold_blocks(nkb, one, jnp.zeros((16, Q_TILE), I16))
        return jnp.broadcast_to(jnp.sum(acc.astype(F32), axis=0, keepdims=True), (16, Q_TILE))

    hi_thr = jnp.where(count16(hi_scr, lambda h: h >= 0) >= kf,
                       jnp.zeros((16, Q_TILE), I32), jnp.full((16, Q_TILE), -(2 ** 15), I32))

    def hi_body(j, t):
        cand = t | lax.shift_left(jnp.int32(1), 14 - j)
        cw = per_query16(cand)
        return jnp.where(count16(hi_scr, lambda h: h >= cw) >= kf, cand, t)

    hi_thr = lax.fori_loop(0, 15, hi_body, hi_thr)
    hi_w = per_query16(hi_thr)
    above = count16(hi_scr, lambda h: h > hi_w)

    def keep_ties(kb, carry):
        lo_scr[kb] = jnp.where(hi_scr[kb] == hi_w, lo_scr[kb], jnp.int16(-(2 ** 15)))
        return carry

    lax.fori_loop(0, nkb, keep_ties, 0)

    def lo_body(j, t):
        cand = t | lax.shift_left(jnp.int32(1), 15 - j)
        cw = per_query16(cand - 2 ** 15)
        return jnp.where(above + count16(lo_scr, lambda lo: lo >= cw) >= kf, cand, t)

    lo_thr = lax.fori_loop(0, 16, lo_body, jnp.zeros((16, Q_TILE), I32))
    thr = (lax.shift_left(hi_thr, 16) | lo_thr)[0:8]
    thr_t = per_query(thr)

    need = kf - count(lambda key, kb: key > thr_t)
    ties = count(lambda key, kb: key == thr_t)
    last_scr[...] = jnp.full(last_scr.shape, s_len, I32)

    @pl.when(jnp.max(ties - need) > 0.0)
    def _():
        def tie_body(j, last):
            cand = last | lax.shift_left(jnp.int32(1), idx_bits - 1 - j)
            cw = per_query(cand)
            taken = count(lambda key, kb: (key == thr_t) & (key_index(kb) < cw))
            return jnp.where(taken < need, cand, last)

        last_scr[...] = lax.fori_loop(0, idx_bits, tie_body, jnp.zeros((8, Q_TILE), I32))

    def per_row(x):
        return _wide(jnp.broadcast_to(x[0:1], (LANES, Q_TILE)).T)

    tw = per_row(thr)
    lastw = per_row(last_scr[...])

    def mask_block(kb, valid):
        key = key_scr[kb]
        tie_neg = jnp.where(key == tw, jnp.where(cols(kb) <= lastw, 0.0, NEG), NEG)
        neg = jnp.where(key > tw, 0.0, tie_neg)
        if valid is not None:
            neg = jnp.where(valid, neg, NEG)
        neg_scr[kb] = neg

    def mask_body(kb, carry):
        mask_block(kb, None)
        return carry

    lax.fori_loop(0, qb, mask_body, 0)
    mask_block(qb, valid_diag)


def _dsa(proj, bias_tiles, bsz, s_len):
    topk = min(TOPK_MAX, s_len // 4)
    idx_bits = max(1, (s_len - 1).bit_length())
    n_blocks = s_len // KEY_BLOCK
    assert 2 * n_blocks <= 256
    kern = functools.partial(_dsa_kernel, topk=topk, idx_bits=idx_bits, s_len=s_len)
    lane_blk = lambda off: off // LANES
    return pl.pallas_call(
        kern,
        grid=(bsz, s_len // Q_TILE, A_HEADS // 2),
        in_specs=[pl.BlockSpec((1, Q_TILE, LANES), lambda b, i, p: (b, i, lane_blk(OFF_QA) + p)),
                  pl.BlockSpec((1, Q_TILE, 256), lambda b, i, p: (b, i, OFF_QI // 256)),
                  pl.BlockSpec((1, Q_TILE, LANES), lambda b, i, p: (b, i, lane_blk(OFF_WI))),
                  pl.BlockSpec((1, s_len, W_A), lambda b, i, p: (b, 0, OFF_KA // W_A),
                               pipeline_mode=pl.Buffered(1)),
                  pl.BlockSpec((1, s_len, W_A), lambda b, i, p: (b, 0, OFF_VA // W_A),
                               pipeline_mode=pl.Buffered(1)),
                  pl.BlockSpec((1, s_len, LANES), lambda b, i, p: (b, 0, lane_blk(OFF_KI))),
                  pl.BlockSpec(bias_tiles.shape, lambda b, i, p: (0, 0, 0, 0),
                               pipeline_mode=pl.Buffered(1))],
        out_specs=pl.BlockSpec((1, Q_TILE, LANES), lambda b, i, p: (b, i, p)),
        out_shape=jax.ShapeDtypeStruct((bsz, s_len, W_A), BF16),
        scratch_shapes=[pltpu.VMEM((n_blocks, Q_TILE, KEY_BLOCK), I32),
                        pltpu.VMEM((n_blocks, KEY_BLOCK, Q_TILE), I32),
                        pltpu.VMEM((n_blocks, KEY_BLOCK, Q_TILE), I16),
                        pltpu.VMEM((n_blocks, KEY_BLOCK, Q_TILE), I16),
                        pltpu.VMEM((n_blocks, Q_TILE, KEY_BLOCK), F32),
                        pltpu.VMEM((IDX_HEADS, Q_TILE, LANES), F32),
                        pltpu.VMEM((8, Q_TILE), I32)] + _softmax_scratch(s_len),
        compiler_params=_cparams("parallel", "arbitrary", "arbitrary"),
        name="dsa",
    )(proj, proj, proj, proj, proj, proj, bias_tiles)


def _diff_kernel(q_ref, k_ref, v_ref, bias_ref, lp_ref, g_ref, o_ref,
                 s_scr, mx_scr, l_scr, acc_scr, *, lam_init):
    qb = pl.program_id(2)
    qs = _split_heads(q_ref[0])

    def add(kb, s, t):
        return s + _tall(bias_ref[0, t])

    o = _softmax_pv(qs, lambda kb: k_ref[0, _key_rows(kb), :], lambda kb: v_ref[0, _key_rows(kb), :],
                    qb, add, s_scr, mx_scr, l_scr, acc_scr)
    lp = lp_ref[0]
    lam = (jnp.exp(jnp.sum(lp[0:1] * lp[1:2], axis=-1, keepdims=True))
           - jnp.exp(jnp.sum(lp[2:3] * lp[3:4], axis=-1, keepdims=True)) + lam_init)
    o = o[:Q_TILE] - lam * o[Q_TILE:]
    y = o * lax.rsqrt(jnp.mean(o * o, axis=-1, keepdims=True) + SUBLN_EPS)
    o_ref[0] = (y * g_ref[0] * (1.0 - lam_init)).astype(o_ref.dtype)


def _diff(proj, bias_tiles, lam_params, subln_g, layer, bsz, s_len, lam_init):
    kern = functools.partial(_diff_kernel, lam_init=lam_init)
    return pl.pallas_call(
        kern,
        grid=(bsz, B_HEADS, s_len // Q_TILE),
        in_specs=[pl.BlockSpec((1, Q_TILE, LANES), lambda b, h, i: (b, i, OFF_QB // LANES + h)),
                  pl.BlockSpec((1, s_len, LANES), lambda b, h, i: (b, 0, OFF_KB // LANES + h)),
                  pl.BlockSpec((1, s_len, LANES), lambda b, h, i: (b, 0, OFF_VB // LANES + h)),
                  pl.BlockSpec((1, 3, Q_TILE, KEY_BLOCK), lambda b, h, i: (h, 0, 0, 0)),
                  pl.BlockSpec((1, 4, HEAD_DIM), lambda b, h, i: (layer, 0, 0)),
                  pl.BlockSpec((1, 1, 2 * HEAD_DIM), lambda b, h, i: (layer, 0, 0))],
        out_specs=pl.BlockSpec((1, Q_TILE, LANES), lambda b, h, i: (b, i, h)),
        out_shape=jax.ShapeDtypeStruct((bsz, s_len, B_HEADS * 2 * HEAD_DIM), BF16),
        scratch_shapes=_softmax_scratch(s_len),
        compiler_params=_cparams("parallel", "parallel", "arbitrary"),
        name="diff_attn",
    )(proj, proj, proj, bias_tiles, lam_params, subln_g.reshape(-1, 1, 2 * HEAD_DIM))


def _stick_kernel(q_ref, k_ref, v_ref, o_ref, later_scr, run_scr, acc_scr):
    qb = pl.program_id(2)
    qs = _split_heads(q_ref[0])
    jj = lax.broadcasted_iota(I32, (2 * KEY_BLOCK, KEY_BLOCK), 0) & (KEY_BLOCK - 1)
    ss = lax.broadcasted_iota(I32, (2 * KEY_BLOCK, KEY_BLOCK), 1)
    later_scr[...] = jnp.where(jj > ss, 1.0, 0.0).astype(BF16)
    run_scr[...] = jnp.zeros_like(run_scr)
    acc_scr[...] = jnp.zeros_like(acc_scr)

    def sweep(kbs, diagonal):
        parts = []
        for kb in kbs:
            z = _nt_dot(qs, k_ref[0, _key_rows(kb), :])
            soft = jnp.log(1.0 + jnp.exp2(jnp.abs(z) * (-LOG2E)))
            log_beta = jnp.minimum(z, 0.0) - soft
            log_1mb = log_beta - z
            strict = None
            if diagonal:
                t = lax.broadcasted_iota(I32, z.shape, 0) & (Q_TILE - 1)
                strict = lax.broadcasted_iota(I32, z.shape, 1) < t
                log_1mb = jnp.where(strict, log_1mb, 0.0)
            hi = log_1mb.astype(BF16)
            lo = (log_1mb - hi.astype(F32)).astype(BF16)
            between = _dot(jnp.concatenate([hi, lo], axis=1), later_scr[...])
            total = between[:, 0:1] + log_1mb[:, 0:1]
            parts.append((kb, log_beta + between, total, strict))
        run = run_scr[...]
        acc = None
        for kb, logit, total, strict in parts:
            w = jnp.exp(logit + _wide(run))
            if strict is not None:
                w = jnp.where(strict, w, 0.0)
            pv = _dot(w.astype(BF16), v_ref[0, _key_rows(kb), :])
            acc = pv if acc is None else acc + pv
            run = run + jnp.broadcast_to(total, run.shape)
        acc_scr[...] += acc
        run_scr[...] = run

    sweep([qb], True)
    top = qb - 1
    for size in (GROUP, 2, 1):
        trips = (top + 1) // size

        def body(i, carry, top=top, size=size):
            sweep([top - i * size - u for u in range(size)], False)
            return carry

        lax.fori_loop(0, trips, body, 0)
        top = top - trips * size
    o_ref[0] = _merge_heads(acc_scr[...], Q_TILE).astype(o_ref.dtype)


def _stick(proj, bsz, s_len):
    return pl.pallas_call(
        _stick_kernel,
        grid=(bsz, C_HEADS // 2, s_len // Q_TILE),
        in_specs=[pl.BlockSpec((1, Q_TILE, LANES), lambda b, h, i: (b, i, OFF_QC // LANES + h)),
                  pl.BlockSpec((1, s_len, LANES), lambda b, h, i: (b, 0, OFF_KC // LANES + h)),
                  pl.BlockSpec((1, s_len, LANES), lambda b, h, i: (b, 0, OFF_VC // LANES + h))],
        out_specs=pl.BlockSpec((1, Q_TILE, LANES), lambda b, h, i: (b, i, h)),
        out_shape=jax.ShapeDtypeStruct((bsz, s_len, C_HEADS * HEAD_DIM), BF16),
        scratch_shapes=[pltpu.VMEM((2 * KEY_BLOCK, KEY_BLOCK), BF16),
                        pltpu.VMEM((2 * Q_TILE, LANES), F32),
                        pltpu.VMEM((2 * Q_TILE, LANES), F32)],
        compiler_params=_cparams("parallel", "parallel", "arbitrary"),
        name="stick_attn",
    )(proj, proj, proj)


def _merge_kernel(oa_ref, ob_ref, oc_ref, gate_ref, x_ref, g1_ref, wbr_ref, wout_ref, o_ref):
    d = x_ref.shape[-1]
    merged = jnp.zeros(x_ref.shape, F32)
    for j, o_br in enumerate((oa_ref, ob_ref, oc_ref)):
        w = o_br.shape[-1]
        gate = jax.nn.sigmoid(gate_ref[:, j * d:(j + 1) * d].astype(F32))
        merged = merged + gate * _dot(o_br[...], wbr_ref[j * w:(j + 1) * w, :])
    o_ref[...] = x_ref[...] + g1_ref[0] * _dot(merged.astype(BF16), wout_ref[...])


def _merge(o_a, o_b, o_c, proj, x2, mod, w_br, w_out, s_len):
    t, d = x2.shape
    tm = 512
    per_b = s_len // tm
    w = o_a.shape[-1]
    return pl.pallas_call(
        _merge_kernel,
        grid=(t // tm,),
        in_specs=[pl.BlockSpec((tm, w), lambda i: (i, 0)),
                  pl.BlockSpec((tm, w), lambda i: (i, 0)),
                  pl.BlockSpec((tm, w), lambda i: (i, 0)),
                  pl.BlockSpec((tm, 3 * d), lambda i: (i, OFF_G // (3 * d))),
                  pl.BlockSpec((tm, d), lambda i: (i, 0)),
                  pl.BlockSpec((1, 1, d), lambda i: (i // per_b, 0, 2)),
                  pl.BlockSpec(w_br.shape, lambda i: (0, 0)),
                  pl.BlockSpec(w_out.shape, lambda i: (0, 0))],
        out_specs=pl.BlockSpec((tm, d), lambda i: (i, 0)),
        out_shape=jax.ShapeDtypeStruct((t, d), F32),
        compiler_params=_cparams("parallel"),
        name="merge",
    )(o_a, o_b, o_c, proj, x2, mod, w_br, w_out)


def _finish(x, gate, acc, fg_ref, final):
    y = x + gate * acc
    if final:
        y = y * lax.rsqrt(jnp.mean(y * y, axis=-1, keepdims=True) + EPS) * fg_ref[...]
    return y


def _swiglu_partial(h, w1, w3, w2):
    a = _dot(h, w1)
    act = (a * jax.nn.sigmoid(a)) * _dot(h, w3)
    return _dot(act.astype(BF16), w2)


def _ffn_kernel(x_ref, g_ref, sc_ref, sh_ref, gate_ref, w1_ref, w3_ref, w2_ref, fg_ref, o_ref,
                h_scr, acc_scr, *, final):
    f = pl.program_id(1)

    @pl.when(f == 0)
    def _():
        h_scr[...] = _modulated_norm(x_ref[...], g_ref[...], sc_ref[0], sh_ref[0]).astype(BF16)
        acc_scr[...] = jnp.zeros_like(acc_scr)

    acc_scr[...] += _swiglu_partial(h_scr[...], w1_ref[...], w3_ref[...], w2_ref[...])

    @pl.when(f == pl.num_programs(1) - 1)
    def _():
        o_ref[...] = _finish(x_ref[...], gate_ref[0], acc_scr[...], fg_ref, final)


def _ffn(x2, g, mod, w1, w3, w2, final_g, s_len, final):
    t, d = x2.shape
    d_ff = w1.shape[1]
    tm, tf = 1024, 256
    per_b = s_len // tm
    kern = functools.partial(_ffn_kernel, final=final)
    return pl.pallas_call(
        kern,
        grid=(t // tm, d_ff // tf),
        in_specs=[pl.BlockSpec((tm, d), lambda i, f: (i, 0)),
                  pl.BlockSpec((1, d), lambda i, f: (0, 0)),
                  pl.BlockSpec((1, 1, d), lambda i, f: (i // per_b, 0, 4)),
                  pl.BlockSpec((1, 1, d), lambda i, f: (i // per_b, 0, 3)),
                  pl.BlockSpec((1, 1, d), lambda i, f: (i // per_b, 0, 5)),
                  pl.BlockSpec((d, tf), lambda i, f: (0, f)),
                  pl.BlockSpec((d, tf), lambda i, f: (0, f)),
                  pl.BlockSpec((tf, d), lambda i, f: (f, 0)),
                  pl.BlockSpec((1, d), lambda i, f: (0, 0))],
        out_specs=pl.BlockSpec((tm, d), lambda i, f: (i, 0)),
        out_shape=jax.ShapeDtypeStruct((t, d), F32),
        scratch_shapes=[pltpu.VMEM((tm, d), BF16), pltpu.VMEM((tm, d), F32)],
        compiler_params=_cparams("parallel", "arbitrary"),
        name="ffn",
    )(x2, g, mod, mod, mod, w1, w3, w2, final_g)


MOE_TM = 512
MOE_ROWS = 512
R_E1, R_E2, R_RANK1, R_RANK2, R_W1, R_W2 = 0, 1, 2, 3, 4, 5


def _lane_pick(tile, lane, idx):
    return jnp.sum(jnp.where(lane == idx, tile, 0.0), axis=-1, keepdims=True)


def _route_kernel(x_ref, g_ref, sc_ref, sh_ref, wr_ref, br_ref, h_ref, rec_ref, cnt_ref, cnt_scr):
    @pl.when(pl.program_id(0) == 0)
    def _():
        cnt_scr[...] = jnp.zeros_like(cnt_scr)

    h = _modulated_norm(x_ref[...], g_ref[...], sc_ref[0], sh_ref[0])
    half = h.shape[1] // 2
    bits = pltpu.bitcast(h.astype(BF16).astype(F32), jnp.uint32)
    h_ref[...] = (bits[:, half:] & jnp.uint32(0xFFFF0000)) | (bits[:, :half] >> 16)

    h_hi = h.astype(BF16)
    h_lo = (h - h_hi.astype(F32)).astype(BF16)
    wr = wr_ref[...]
    w_hi = wr.astype(BF16)
    w_lo = (wr - w_hi.astype(F32)).astype(BF16)
    logits = _dot(h_hi, w_hi) + _dot(h_hi, w_lo) + _dot(h_lo, w_hi) + br_ref[...]
    lane = lax.broadcasted_iota(I32, logits.shape, 1).astype(F32)
    lg = jnp.where(lane < N_EXPERTS, logits, -jnp.inf)
    m1 = jnp.max(lg, axis=-1, keepdims=True)
    i1 = jnp.min(jnp.where(lg == m1, lane, float(LANES)), axis=-1, keepdims=True)
    lg2 = jnp.where(lane == i1, -jnp.inf, lg)
    m2 = jnp.max(lg2, axis=-1, keepdims=True)
    i2 = jnp.min(jnp.where(lg2 == m2, lane, float(LANES)), axis=-1, keepdims=True)
    e2 = jnp.exp(m2 - m1)
    w_top = 1.0 / (1.0 + e2)

    chosen = jnp.where(lane == i1, 1.0, 0.0) + jnp.where(lane == i2, 1.0, 0.0)
    tm = chosen.shape[0]
    earlier = jnp.where(lax.broadcasted_iota(I32, (tm, tm), 1) < lax.broadcasted_iota(I32, (tm, tm), 0),
                        1.0, 0.0).astype(BF16)
    rank = _dot(earlier, chosen.astype(BF16)) + cnt_scr[0:1, :]
    rec = jnp.zeros_like(logits)
    for slot, val in ((R_E1, i1), (R_E2, i2), (R_RANK1, _lane_pick(rank, lane, i1)),
                      (R_RANK2, _lane_pick(rank, lane, i2)), (R_W1, w_top), (R_W2, e2 * w_top)):
        rec = jnp.where(lane == float(slot), val, rec)
    rec_ref[...] = rec
    cnt_scr[0:1, :] = cnt_scr[0:1, :] + jnp.sum(chosen, axis=0, keepdims=True)
    cnt_ref[...] = cnt_scr[...]


def _route(x2, g, mod, wr_pad, br_pad, s_len):
    t, d = x2.shape
    tm = MOE_TM
    per_b = s_len // tm
    return pl.pallas_call(
        _route_kernel,
        grid=(t // tm,),
        in_specs=[pl.BlockSpec((tm, d), lambda i: (i, 0)),
                  pl.BlockSpec((1, d), lambda i: (0, 0)),
                  pl.BlockSpec((1, 1, d), lambda i: (i // per_b, 0, 4)),
                  pl.BlockSpec((1, 1, d), lambda i: (i // per_b, 0, 3)),
                  pl.BlockSpec((d, LANES), lambda i: (0, 0)),
                  pl.BlockSpec((1, LANES), lambda i: (0, 0))],
        out_specs=[pl.BlockSpec((tm, d // 2), lambda i: (i, 0)),
                   pl.BlockSpec((tm, LANES), lambda i: (i, 0)),
                   pl.BlockSpec((8, LANES), lambda i: (0, 0))],
        out_shape=[jax.ShapeDtypeStruct((t, d // 2), jnp.uint32),
                   jax.ShapeDtypeStruct((t, LANES), F32),
                   jax.ShapeDtypeStruct((8, LANES), F32)],
        scratch_shapes=[pltpu.VMEM((8, LANES), F32)],
        compiler_params=_cparams("arbitrary"),
        name="moe_route",
    )(x2, g, mod, mod, wr_pad, br_pad)


def _row_copy(src_ref, src_row, dst_ref, dst_row, sem):
    return pltpu.make_async_copy(src_ref.at[pl.ds(src_row, 1)], dst_ref.at[pl.ds(dst_row, 1)], sem)


def _dispatch_kernel(pos_ref, h_ref, xs_in_ref, xs_ref, sem):
    del xs_in_ref
    tm = h_ref.shape[0]

    def start(t, carry):
        _row_copy(h_ref, t, xs_ref, pos_ref[0, 0, t], sem).start(priority=0)
        _row_copy(h_ref, t, xs_ref, pos_ref[0, 0, tm + t], sem).start(priority=1)
        return carry

    lax.fori_loop(0, tm, start, 0)
    for _ in range(2):
        pltpu.make_async_copy(h_ref, xs_ref.at[pl.ds(0, tm)], sem).wait()


def _dispatch(pos, h_packed, n_rows):
    t, w = h_packed.shape
    tm = MOE_TM
    xs0 = jnp.zeros((n_rows, w), h_packed.dtype)
    return pl.pallas_call(
        _dispatch_kernel,
        grid=(t // tm,),
        in_specs=[pl.BlockSpec((1, 1, 2 * tm), lambda i: (i, 0, 0), memory_space=pltpu.SMEM),
                  pl.BlockSpec((tm, w), lambda i: (i, 0)),
                  pl.BlockSpec(memory_space=pl.ANY)],
        out_specs=pl.BlockSpec(memory_space=pl.ANY),
        out_shape=jax.ShapeDtypeStruct((n_rows, w), h_packed.dtype),
        scratch_shapes=[pltpu.SemaphoreType.DMA(())],
        input_output_aliases={2: 0},
        compiler_params=_cparams("arbitrary"),
        name="moe_dispatch",
    )(pos, h_packed, xs0)


def _expert_kernel(te_ref, nu_ref, xs_ref, w1_ref, w3_ref, w2_ref, o_ref, h_scr, acc_scr):
    i = pl.program_id(0)
    f = pl.program_id(1)

    @pl.when(i < nu_ref[0])
    def _():
        @pl.when(f == 0)
        def _():
            word = xs_ref[...]
            lo = pltpu.bitcast(word << 16, F32)
            hi = pltpu.bitcast(word & jnp.uint32(0xFFFF0000), F32)
            h_scr[...] = jnp.concatenate([lo, hi], axis=1).astype(BF16)
            acc_scr[...] = jnp.zeros_like(acc_scr)

        acc_scr[...] += _swiglu_partial(h_scr[...], w1_ref[0], w3_ref[0], w2_ref[0])

        @pl.when(f == pl.num_programs(1) - 1)
        def _():
            o_ref[...] = acc_scr[...]

    @pl.when((i >= nu_ref[0]) & (f == pl.num_programs(1) - 1))
    def _():
        o_ref[...] = jnp.zeros_like(o_ref)


def _experts(tile_expert, n_used, xs, w1, w3, w2):
    n_rows, half = xs.shape
    d = 2 * half
    d_ff = w1.shape[2]
    tf = 896
    nf = d_ff // tf
    tile = lambda i, nu: jnp.minimum(i, nu[0] - 1)
    ff = lambda i, f, nu: jnp.where(i < nu[0], f, nf - 1)
    return pl.pallas_call(
        _expert_kernel,
        grid_spec=pltpu.PrefetchScalarGridSpec(
            num_scalar_prefetch=2,
            grid=(n_rows // MOE_ROWS, nf),
            in_specs=[pl.BlockSpec((MOE_ROWS, half), lambda i, f, te, nu: (tile(i, nu), 0)),
                      pl.BlockSpec((1, d, tf), lambda i, f, te, nu: (te[i], 0, ff(i, f, nu))),
                      pl.BlockSpec((1, d, tf), lambda i, f, te, nu: (te[i], 0, ff(i, f, nu))),
                      pl.BlockSpec((1, tf, d), lambda i, f, te, nu: (te[i], ff(i, f, nu), 0))],
            out_specs=pl.BlockSpec((MOE_ROWS, d), lambda i, f, te, nu: (i, 0)),
            scratch_shapes=[pltpu.VMEM((MOE_ROWS, d), BF16), pltpu.VMEM((MOE_ROWS, d), F32)]),
        out_shape=jax.ShapeDtypeStruct((n_rows, d), F32),
        compiler_params=_cparams("arbitrary", "arbitrary"),
        name="moe_experts",
    )(tile_expert, n_used, xs, w1, w3, w2)


def _combine_kernel(pos_ref, x_ref, gate_ref, rec_ref, fg_ref, ys_ref, o_ref, y_scr, sem, *, final):
    tm = x_ref.shape[0]

    def start(t, carry):
        _row_copy(ys_ref, pos_ref[0, 0, t], y_scr.at[0], t, sem).start(priority=0)
        _row_copy(ys_ref, pos_ref[0, 0, tm + t], y_scr.at[1], t, sem).start(priority=1)
        return carry

    lax.fori_loop(0, tm, start, 0)
    for slot in range(2):
        pltpu.make_async_copy(ys_ref.at[pl.ds(0, tm)], y_scr.at[slot], sem).wait()
    rec = rec_ref[...]
    lane = lax.broadcasted_iota(I32, rec.shape, 1)
    w_first = jnp.sum(jnp.where(lane == R_W1, rec, 0.0), axis=-1, keepdims=True)
    w_second = jnp.sum(jnp.where(lane == R_W2, rec, 0.0), axis=-1, keepdims=True)
    f = w_first * y_scr[0] + w_second * y_scr[1]
    o_ref[...] = _finish(x_ref[...], gate_ref[0], f, fg_ref, final)


def _combine(pos, x2, mod, rec, final_g, ys, s_len, final):
    t, d = x2.shape
    tm = MOE_TM
    per_b = s_len // tm
    kern = functools.partial(_combine_kernel, final=final)
    return pl.pallas_call(
        kern,
        grid=(t // tm,),
        in_specs=[pl.BlockSpec((1, 1, 2 * tm), lambda i: (i, 0, 0), memory_space=pltpu.SMEM),
                  pl.BlockSpec((tm, d), lambda i: (i, 0)),
                  pl.BlockSpec((1, 1, d), lambda i: (i // per_b, 0, 5)),
                  pl.BlockSpec((tm, LANES), lambda i: (i, 0)),
                  pl.BlockSpec((1, d), lambda i: (0, 0)),
                  pl.BlockSpec(memory_space=pl.ANY)],
        out_specs=pl.BlockSpec((tm, d), lambda i: (i, 0)),
        out_shape=jax.ShapeDtypeStruct((t, d), F32),
        scratch_shapes=[pltpu.VMEM((2, tm, d), F32), pltpu.SemaphoreType.DMA(())],
        compiler_params=_cparams("arbitrary"),
        name="moe_combine",
    )(pos, x2, mod, rec, final_g, ys)


def _moe(x2, g, mod, wr_pad, br_pad, w1, w3, w2, final_g, s_len, final):
    t, d = x2.shape
    n_e = w1.shape[0]
    h_packed, rec, cnt = _route(x2, g, mod, wr_pad, br_pad, s_len)

    n_tiles = 2 * t // MOE_ROWS + n_e
    counts = cnt[0, :n_e].astype(I32)
    padded = (counts + MOE_ROWS - 1) // MOE_ROWS * MOE_ROWS
    ends = jnp.cumsum(padded)
    starts = ends - padded
    e1, e2 = rec[:, R_E1].astype(I32), rec[:, R_E2].astype(I32)
    pos1 = starts[e1] + rec[:, R_RANK1].astype(I32)
    pos2 = starts[e2] + rec[:, R_RANK2].astype(I32)
    pos = jnp.concatenate([pos1.reshape(-1, 1, MOE_TM), pos2.reshape(-1, 1, MOE_TM)], axis=2)
    n_used = (ends[-1] // MOE_ROWS).reshape(1)
    tile_start = jnp.minimum(jnp.arange(n_tiles, dtype=I32), n_used[0] - 1) * MOE_ROWS
    tile_expert = jnp.sum(tile_start[:, None] >= ends[None, :], axis=1).astype(I32)

    xs = _dispatch(pos, h_packed, n_tiles * MOE_ROWS)
    ys = _experts(tile_expert, n_used, xs, w1, w3, w2)
    return _combine(pos, x2, mod, rec, final_g, ys, s_len, final)


def _pack_w_in(w):
    d = w.shape[0]
    sizes = (W_A, W_A, W_A, IDX_HEADS * IDX_DIM, IDX_DIM, IDX_HEADS,
             512, 512, 512, 512, 512, 512, 3 * d)
    offs = [0]
    for s in sizes:
        offs.append(offs[-1] + s)
    (qa, ka, va, qi, ki, wi, qb, kb, vb, qc, kc, vc, gl) = [
        w[:, offs[j]:offs[j + 1]] for j in range(len(sizes))]
    scale = HEAD_DIM ** -0.5
    pad_wi = jnp.zeros((d, LANES - IDX_HEADS), w.dtype)
    packed = jnp.concatenate(
        [gl, qa * scale, ka, va,
         qi * (IDX_DIM ** -0.5), ki, ki, wi * (IDX_HEADS ** -0.5), pad_wi,
         qb * scale, kb, vb, qc * scale, kc, vc], axis=1)
    assert packed.shape[1] == PACKED
    return packed.astype(BF16)


def kernel(x, c, w_ada, b_ada, norm1_g, norm2_g, w_in, w_br, w_out, rel_bias, lam_params,
           subln_g, ffn_w1, ffn_w3, ffn_w2, router_w, router_b, moe_w1, moe_w3, moe_w2, final_g):
    bsz, s_len, d = x.shape
    depth = w_ada.shape[0]
    assert s_len % 1024 == 0 and d == 1024 and OFF_G + 3 * d == OFF_QA

    c_pad = jnp.concatenate([c, jnp.zeros((8 - bsz % 8 if bsz % 8 else 0, d), c.dtype)], axis=0)
    mod_all = _ada(c_pad, w_ada, b_ada)
    dsa_tiles, diff_tiles = _bias_tiles(rel_bias)
    fg = final_g.reshape(1, d)

    x2 = x.reshape(bsz * s_len, d)
    for l in range(depth):
        mod = mod_all[l, :bsz].reshape(bsz, 1, 6 * d)
        proj = _inproj(x2, norm1_g[l].reshape(1, d), mod, _pack_w_in(w_in[l]), s_len)
        proj3 = proj.reshape(bsz, s_len, PACKED)
        lam_init = 0.8 - 0.6 * math.exp(-0.3 * l)
        o_a = _dsa(proj3, dsa_tiles, bsz, s_len)
        o_b = _diff(proj3, diff_tiles, lam_params, subln_g, l, bsz, s_len, lam_init)
        o_c = _stick(proj3, bsz, s_len)
        x2 = _merge(o_a.reshape(-1, o_a.shape[-1]), o_b.reshape(-1, o_b.shape[-1]),
                    o_c.reshape(-1, o_c.shape[-1]), proj, x2, mod,
                    w_br[l].astype(BF16), w_out[l].astype(BF16), s_len)
        g2 = norm2_g[l].reshape(1, d)
        final = l == depth - 1
        j = l // 2
        if l % 2 == 0:
            x2 = _ffn(x2, g2, mod, ffn_w1[j].astype(BF16), ffn_w3[j].astype(BF16),
                      ffn_w2[j].astype(BF16), fg, s_len, final)
        else:
            wr_pad = jnp.pad(router_w[j], ((0, 0), (0, LANES - N_EXPERTS)))
            br_pad = jnp.pad(router_b[j], (0, LANES - N_EXPERTS)).reshape(1, LANES)
            x2 = _moe(x2, g2, mod, wr_pad, br_pad, moe_w1[j].astype(BF16), moe_w3[j].astype(BF16),
                      moe_w2[j].astype(BF16), fg, s_len, final)
    return x2.reshape(bsz, s_len, d)
```

```python
import functools
import math

import jax
import jax.numpy as jnp
from jax import lax
from jax.experimental import pallas as pl
from jax.experimental.pallas import tpu as pltpu

F32 = jnp.float32
BF16 = jnp.bfloat16
I32 = jnp.int32
I16 = jnp.int16

LANES = 128
VMEM_LIMIT_BYTES = 56 * 1024 * 1024

CHUNK = 64
A_HEADS = 8
IDX_HEADS = 4
IDX_DIM = 64
TOPK_MAX = 256
B_HEADS = 4
C_HEADS = 8
HEAD_DIM = 64
REL_BUCKETS = 32
FAR_BUCKET = REL_BUCKETS // 2 - 1
N_EXPERTS = 8
EPS = 1e-6
SUBLN_EPS = 1e-5
NEG = -1e30
LOG2E = 1.4426950408889634
INT_MIN = -(2 ** 31)

KEY_BLOCK = 256
Q_TILE = 256
GROUP = 4
BLOCK_GROUPS = (8, 4, 2, 1)

W_A = A_HEADS * HEAD_DIM
OFF_G = 0
OFF_QA, OFF_KA, OFF_VA = 3072, 3584, 4096
OFF_QI, OFF_KI, OFF_WI = 4608, 4864, 4992
OFF_QB, OFF_KB, OFF_VB = 5120, 5632, 6144
OFF_QC, OFF_KC, OFF_VC = 6656, 7168, 7680
PACKED = 8192

LOG_BUCKET_STEPS = (12, 16, 23, 32, 46, 64, 91)


def _nt_dot(a, b):
    return lax.dot_general(a, b, (((1,), (1,)), ((), ())), preferred_element_type=F32)


def _dot(a, b):
    return jnp.dot(a, b, preferred_element_type=F32)


def _cparams(*sem):
    return pltpu.CompilerParams(dimension_semantics=sem, vmem_limit_bytes=VMEM_LIMIT_BYTES)


def _split_heads(x):
    lane = lax.broadcasted_iota(I32, x.shape, 1)
    keep_a = jnp.where(lane < HEAD_DIM, 1.0, 0.0).astype(x.dtype)
    keep_b = jnp.where(lane < HEAD_DIM, 0.0, 1.0).astype(x.dtype)
    return jnp.concatenate([x * keep_a, x * keep_b], axis=0)


def _merge_heads(o, m):
    lane = lax.broadcasted_iota(I32, (m, LANES), 1)
    return jnp.where(lane < HEAD_DIM, o[:m], o[m:])


def _wide(x):
    return jnp.concatenate([x, x], axis=1)


def _tall(x):
    return jnp.concatenate([x, x], axis=0)


def _key_rows(kb):
    return pl.ds(pl.multiple_of(kb * KEY_BLOCK, KEY_BLOCK), KEY_BLOCK)


def _sortable(x):
    bits = pltpu.bitcast(x, I32)
    return bits ^ ((bits >> 31) & 0x7FFFFFFF)


def _chunk_causal():
    r = lax.broadcasted_iota(I32, (Q_TILE, KEY_BLOCK), 0)
    c = lax.broadcasted_iota(I32, (Q_TILE, KEY_BLOCK), 1)
    return (c // CHUNK) <= (r // CHUNK)


def _ada_kernel(c_ref, w_ref, b_ref, o_ref):
    c = c_ref[...]
    a = c * jax.nn.sigmoid(c)
    o_ref[0] = jnp.dot(a, w_ref[0], preferred_element_type=F32,
                       precision=lax.Precision.HIGHEST) + b_ref[0]


def _ada(c_pad, w_ada, b_ada):
    depth, d, n = w_ada.shape
    tn = 1024
    return pl.pallas_call(
        _ada_kernel,
        grid=(depth, n // tn),
        in_specs=[pl.BlockSpec(c_pad.shape, lambda l, j: (0, 0)),
                  pl.BlockSpec((1, d, tn), lambda l, j: (l, 0, j)),
                  pl.BlockSpec((1, 1, tn), lambda l, j: (l, 0, j))],
        out_specs=pl.BlockSpec((1, c_pad.shape[0], tn), lambda l, j: (l, 0, j)),
        out_shape=jax.ShapeDtypeStruct((depth, c_pad.shape[0], n), F32),
        compiler_params=_cparams("parallel", "parallel"),
        name="ada",
    )(c_pad, w_ada, b_ada.reshape(depth, 1, n))


def _rel_bias_tile(tab_ref, head, d0, n_heads_total):
    r = lax.broadcasted_iota(I32, (Q_TILE, KEY_BLOCK), 0)
    c = lax.broadcasted_iota(I32, (Q_TILE, KEY_BLOCK), 1)
    d = c - r + d0
    n = jnp.abs(d)
    large = jnp.full(d.shape, REL_BUCKETS // 4, I32)
    for step in LOG_BUCKET_STEPS:
        large = large + jnp.where(n >= step, 1, 0)
    bucket = jnp.where(d > 0, REL_BUCKETS // 2, 0) + jnp.where(n < REL_BUCKETS // 4, n, large)
    out = jnp.zeros(d.shape, F32)
    for b in range(REL_BUCKETS):
        out = jnp.where(bucket == b, tab_ref[b * n_heads_total + head], out)
    return out - tab_ref[FAR_BUCKET * n_heads_total + head]


def _bias_kernel(tab_ref, dsa_ref, diff_ref):
    p = pl.program_id(0)
    n_heads = A_HEADS + B_HEADS
    dsa_ref[0, 0] = jnp.zeros(dsa_ref.shape[2:], F32)
    diff_ref[0, 0] = jnp.zeros(diff_ref.shape[2:], F32)
    for t, d0 in ((1, -KEY_BLOCK), (2, 0)):
        dsa_ref[0, t, 0:Q_TILE, :] = _rel_bias_tile(tab_ref, 2 * p, d0, n_heads)
        dsa_ref[0, t, Q_TILE:2 * Q_TILE, :] = _rel_bias_tile(tab_ref, 2 * p + 1, d0, n_heads)
        tile = _rel_bias_tile(tab_ref, A_HEADS + p, d0, n_heads)
        if t == 2:
            tile = jnp.where(_chunk_causal(), tile, NEG)
        diff_ref[0, t] = tile


def _bias_tiles(rel_bias):
    tab = rel_bias.reshape(-1)
    return pl.pallas_call(
        _bias_kernel,
        grid=(4,),
        in_specs=[pl.BlockSpec(memory_space=pltpu.SMEM)],
        out_specs=[pl.BlockSpec((1, 3, 2 * Q_TILE, KEY_BLOCK), lambda p: (p, 0, 0, 0)),
                   pl.BlockSpec((1, 3, Q_TILE, KEY_BLOCK), lambda p: (p, 0, 0, 0))],
        out_shape=[jax.ShapeDtypeStruct((A_HEADS // 2, 3, 2 * Q_TILE, KEY_BLOCK), F32),
                   jax.ShapeDtypeStruct((B_HEADS, 3, Q_TILE, KEY_BLOCK), F32)],
        compiler_params=_cparams("parallel"),
        name="rel_bias_tiles",
    )(tab)


def _modulated_norm(x, g, sc, sh):
    y = x * lax.rsqrt(jnp.mean(x * x, axis=-1, keepdims=True) + EPS)
    return y * g * (1.0 + sc) + sh


def _inproj_kernel(x_ref, g_ref, sc_ref, sh_ref, w_ref, o_ref, h_scr):
    @pl.when(pl.program_id(1) == 0)
    def _():
        h_scr[...] = _modulated_norm(x_ref[...], g_ref[...], sc_ref[0], sh_ref[0]).astype(BF16)

    o_ref[...] = _dot(h_scr[...], w_ref[...]).astype(o_ref.dtype)


def _inproj(x2, g, mod, w_packed, s_len):
    t, d = x2.shape
    n = w_packed.shape[1]
    tm, tn = 1024, 1024
    per_b = s_len // tm
    return pl.pallas_call(
        _inproj_kernel,
        grid=(t // tm, n // tn),
        in_specs=[pl.BlockSpec((tm, d), lambda i, j: (i, 0)),
                  pl.BlockSpec((1, d), lambda i, j: (0, 0)),
                  pl.BlockSpec((1, 1, d), lambda i, j: (i // per_b, 0, 1)),
                  pl.BlockSpec((1, 1, d), lambda i, j: (i // per_b, 0, 0)),
                  pl.BlockSpec((d, tn), lambda i, j: (0, j))],
        out_specs=pl.BlockSpec((tm, tn), lambda i, j: (i, j)),
        out_shape=jax.ShapeDtypeStruct((t, n), BF16),
        scratch_shapes=[pltpu.VMEM((tm, d), BF16)],
        compiler_params=_cparams("parallel", "arbitrary"),
        name="inproj",
    )(x2, g, mod, mod, w_packed)


def _for_blocks(lo, hi, fn, sizes=BLOCK_GROUPS):
    pos = lo
    for size in sizes:
        trips = jnp.maximum(hi - pos, 0) // size

        def body(i, carry, pos=pos, size=size):
            fn([pos + i * size + u for u in range(size)])
            return carry

        lax.fori_loop(0, trips, body, 0)
        pos = pos + trips * size


def _fold_blocks(n, fn, acc):
    pos = 0
    for size in BLOCK_GROUPS:
        trips = (n - pos) // size

        def body(i, acc, pos=pos, size=size):
            for u in range(size):
                acc = fn(pos + i * size + u, acc)
            return acc

        acc = lax.fori_loop(0, trips, body, acc)
        pos = pos + trips * size
    return acc


def _softmax_pv(qs, k_at, v_at, qb, add, s_scr, mx_scr, l_scr, acc_scr):
    def pass1(blocks):
        mx = None
        for kb in blocks:
            s = add(kb, _nt_dot(qs, k_at(kb)), jnp.clip(kb - qb + 2, 0, 2))
            s = s * LOG2E
            s_scr[kb] = s
            fold = jnp.maximum(s[:, :LANES], s[:, LANES:])
            mx = fold if mx is None else jnp.maximum(mx, fold)
        mx_scr[...] = jnp.maximum(mx_scr[...], mx)

    mx_scr[...] = jnp.full(mx_scr.shape, NEG, F32)
    _for_blocks(0, qb + 1, pass1)

    m = jnp.max(mx_scr[...], axis=-1, keepdims=True)
    mx_scr[...] = jnp.broadcast_to(m, mx_scr.shape)
    l_scr[...] = jnp.zeros_like(l_scr)
    acc_scr[...] = jnp.zeros_like(acc_scr)

    def pass2(kbs):
        mw = _wide(mx_scr[...])
        l_add = acc_add = None
        for kb in kbs:
            p = jnp.exp2(s_scr[kb] - mw)
            fold = p[:, :LANES] + p[:, LANES:]
            pv = _dot(p.astype(BF16), v_at(kb))
            l_add = fold if l_add is None else l_add + fold
            acc_add = pv if acc_add is None else acc_add + pv
        l_scr[...] += l_add
        acc_scr[...] += acc_add

    _for_blocks(0, qb + 1, pass2)
    return acc_scr[...] / jnp.sum(l_scr[...], axis=-1, keepdims=True)


def _softmax_scratch(s_len):
    return [pltpu.VMEM((s_len // KEY_BLOCK, 2 * Q_TILE, KEY_BLOCK), F32),
            pltpu.VMEM((2 * Q_TILE, LANES), F32),
            pltpu.VMEM((2 * Q_TILE, LANES), F32),
            pltpu.VMEM((2 * Q_TILE, LANES), F32)]


def _dsa_kernel(qa_ref, qi_ref, wi_ref, ka_ref, va_ref, ki_ref, bias_ref, o_ref,
                key_scr, keyt_scr, hi_scr, lo_scr, neg_scr, w_scr, last_scr, s_scr, mx_scr, l_scr, acc_scr,
                *, topk, idx_bits, s_len):
    qb = pl.program_id(1)

    @pl.when(pl.program_id(2) == 0)
    def _():
        _dsa_select(qb, qi_ref, wi_ref, ki_ref, key_scr, keyt_scr, hi_scr, lo_scr, neg_scr, w_scr, last_scr,
                    topk=topk, idx_bits=idx_bits, s_len=s_len)

    pr = pl.program_id(2)
    lanes = pl.ds(pl.multiple_of(pr * LANES, LANES), LANES)

    def add(kb, s, t):
        return s + _tall(neg_scr[kb]) + bias_ref[pr, t]

    o = _softmax_pv(_split_heads(qa_ref[0]),
                    lambda kb: ka_ref[0, _key_rows(kb), lanes],
                    lambda kb: va_ref[0, _key_rows(kb), lanes],
                    qb, add, s_scr, mx_scr, l_scr, acc_scr)
    o_ref[0] = _merge_heads(o, Q_TILE).astype(o_ref.dtype)


def _dsa_select(qb, qi_ref, wi_ref, ki_ref, key_scr, keyt_scr, hi_scr, lo_scr, neg_scr, w_scr, last_scr,
                *, topk, idx_bits, s_len):
    nkb = qb + 1
    kf = float(topk)
    valid_diag = _chunk_causal()

    def cols(kb):
        return kb * KEY_BLOCK + lax.broadcasted_iota(I32, (Q_TILE, KEY_BLOCK), 1)

    qi = qi_ref[0]
    q_heads = [_split_heads(qi[:, pr * LANES:(pr + 1) * LANES]) for pr in range(IDX_HEADS // 2)]
    wi = wi_ref[0].astype(F32)
    for h in range(IDX_HEADS):
        w_scr[h] = jnp.broadcast_to(wi[:, h:h + 1], (Q_TILE, LANES))

    def score_block(kb, valid):
        kk = ki_ref[0, _key_rows(kb), :]
        sc = jnp.zeros((Q_TILE, KEY_BLOCK), F32)
        for pr in range(IDX_HEADS // 2):
            d = _nt_dot(q_heads[pr], kk)
            sc = sc + _wide(w_scr[2 * pr]) * jnp.maximum(d[:Q_TILE], 0.0)
            sc = sc + _wide(w_scr[2 * pr + 1]) * jnp.maximum(d[Q_TILE:], 0.0)
        sc = jnp.where(sc == 0.0, 0.0, sc)
        if valid is not None:
            sc = jnp.where(valid, sc, -jnp.inf)
        key_scr[kb] = _sortable(sc)
        key_t = _sortable(sc.T)
        keyt_scr[kb] = key_t
        hi_scr[kb] = (key_t >> 16).astype(I16)
        lo_scr[kb] = ((key_t & 0xFFFF) - 2 ** 15).astype(I16)

    _for_blocks(0, qb, lambda kbs: [score_block(kb, None) for kb in kbs], sizes=(2, 1))
    score_block(qb, valid_diag)

    def per_query(x):
        return jnp.broadcast_to(x[None], (KEY_BLOCK // 8, 8, Q_TILE)).reshape(KEY_BLOCK, Q_TILE)

    def key_index(kb):
        return kb * KEY_BLOCK + lax.broadcasted_iota(I32, (KEY_BLOCK, Q_TILE), 0)

    def count(pred):
        def one(kb, acc):
            hit = jnp.where(pred(keyt_scr[kb], kb), 1.0, 0.0)
            return acc + jnp.sum(hit.reshape(KEY_BLOCK // 8, 8, Q_TILE), axis=0)

        acc = _fold_blocks(nkb, one, jnp.zeros((8, Q_TILE), F32))
        return jnp.broadcast_to(jnp.sum(acc, axis=0, keepdims=True), (8, Q_TILE))

    def per_query16(x):
        x16 = x.astype(I16)
        return jnp.broadcast_to(x16[None], (KEY_BLOCK // 16, 16, Q_TILE)).reshape(KEY_BLOCK, Q_TILE)

    def count16(ref, pred):
        def one(kb, acc):
            hit = jnp.where(pred(ref[kb]), jnp.int16(1), jnp.int16(0))
            for g in range(0, KEY_BLOCK, 16):
                acc = acc + hit[g:g + 16]
            return acc

        acc = _fold_blocks(nkb, one, jnp.zeros((16, Q_TILE), I16))
        return jnp.broadcast_to(jnp.sum(acc.astype(F32), axis=0, keepdims=True), (16, Q_TILE))

    hi_thr = jnp.where(count16(hi_scr, lambda h: h >= 0) >= kf,
                       jnp.zeros((16, Q_TILE), I32), jnp.full((16, Q_TILE), -(2 ** 15), I32))

    def hi_body(j, t):
        cand = t | lax.shift_left(jnp.int32(1), 14 - j)
        cw = per_query16(cand)
        return jnp.where(count16(hi_scr, lambda h: h >= cw) >= kf, cand, t)

    hi_thr = lax.fori_loop(0, 15, hi_body, hi_thr)
    hi_w = per_query16(hi_thr)
    above = count16(hi_scr, lambda h: h > hi_w)

    def keep_ties(kb, carry):
        lo_scr[kb] = jnp.where(hi_scr[kb] == hi_w, lo_scr[kb], jnp.int16(-(2 ** 15)))
        return carry

    lax.fori_loop(0, nkb, keep_ties, 0)

    def lo_body(j, t):
        cand = t | lax.shift_left(jnp.int32(1), 15 - j)
        cw = per_query16(cand - 2 ** 15)
        return jnp.where(above + count16(lo_scr, lambda lo: lo >= cw) >= kf, cand, t)

    lo_thr = lax.fori_loop(0, 16, lo_body, jnp.zeros((16, Q_TILE), I32))
    thr = (lax.shift_left(hi_thr, 16) | lo_thr)[0:8]
    thr_t = per_query(thr)

    need = kf - count(lambda key, kb: key > thr_t)
    ties = count(lambda key, kb: key == thr_t)
    last_scr[...] = jnp.full(last_scr.shape, s_len, I32)

    @pl.when(jnp.max(ties - need) > 0.0)
    def _():
        def tie_body(j, last):
            cand = last | lax.shift_left(jnp.int32(1), idx_bits - 1 - j)
            cw = per_query(cand)
            taken = count(lambda key, kb: (key == thr_t) & (key_index(kb) < cw))
            return jnp.where(taken < need, cand, last)

        last_scr[...] = lax.fori_loop(0, idx_bits, tie_body, jnp.zeros((8, Q_TILE), I32))

    def per_row(x):
        return _wide(jnp.broadcast_to(x[0:1], (LANES, Q_TILE)).T)

    tw = per_row(thr)
    lastw = per_row(last_scr[...])

    def mask_block(kb, valid):
        key = key_scr[kb]
        tie_neg = jnp.where(key == tw, jnp.where(cols(kb) <= lastw, 0.0, NEG), NEG)
        neg = jnp.where(key > tw, 0.0, tie_neg)
        if valid is not None:
            neg = jnp.where(valid, neg, NEG)
        neg_scr[kb] = neg

    def mask_body(kb, carry):
        mask_block(kb, None)
        return carry

    lax.fori_loop(0, qb, mask_body, 0)
    mask_block(qb, valid_diag)


def _dsa(proj, bias_tiles, bsz, s_len):
    topk = min(TOPK_MAX, s_len // 4)
    idx_bits = max(1, (s_len - 1).bit_length())
    n_blocks = s_len // KEY_BLOCK
    assert 2 * n_blocks <= 256
    kern = functools.partial(_dsa_kernel, topk=topk, idx_bits=idx_bits, s_len=s_len)
    lane_blk = lambda off: off // LANES
    return pl.pallas_call(
        kern,
        grid=(bsz, s_len // Q_TILE, A_HEADS // 2),
        in_specs=[pl.BlockSpec((1, Q_TILE, LANES), lambda b, i, p: (b, i, lane_blk(OFF_QA) + p)),
                  pl.BlockSpec((1, Q_TILE, 256), lambda b, i, p: (b, i, OFF_QI // 256)),
                  pl.BlockSpec((1, Q_TILE, LANES), lambda b, i, p: (b, i, lane_blk(OFF_WI))),
                  pl.BlockSpec((1, s_len, W_A), lambda b, i, p: (b, 0, OFF_KA // W_A),
                               pipeline_mode=pl.Buffered(1)),
                  pl.BlockSpec((1, s_len, W_A), lambda b, i, p: (b, 0, OFF_VA // W_A),
                               pipeline_mode=pl.Buffered(1)),
                  pl.BlockSpec((1, s_len, LANES), lambda b, i, p: (b, 0, lane_blk(OFF_KI))),
                  pl.BlockSpec(bias_tiles.shape, lambda b, i, p: (0, 0, 0, 0),
                               pipeline_mode=pl.Buffered(1))],
        out_specs=pl.BlockSpec((1, Q_TILE, LANES), lambda b, i, p: (b, i, p)),
        out_shape=jax.ShapeDtypeStruct((bsz, s_len, W_A), BF16),
        scratch_shapes=[pltpu.VMEM((n_blocks, Q_TILE, KEY_BLOCK), I32),
                        pltpu.VMEM((n_blocks, KEY_BLOCK, Q_TILE), I32),
                        pltpu.VMEM((n_blocks, KEY_BLOCK, Q_TILE), I16),
                        pltpu.VMEM((n_blocks, KEY_BLOCK, Q_TILE), I16),
                        pltpu.VMEM((n_blocks, Q_TILE, KEY_BLOCK), F32),
                        pltpu.VMEM((IDX_HEADS, Q_TILE, LANES), F32),
                        pltpu.VMEM((8, Q_TILE), I32)] + _softmax_scratch(s_len),
        compiler_params=_cparams("parallel", "arbitrary", "arbitrary"),
        name="dsa",
    )(proj, proj, proj, proj, proj, proj, bias_tiles)


def _diff_kernel(q_ref, k_ref, v_ref, bias_ref, lp_ref, g_ref, o_ref,
                 s_scr, mx_scr, l_scr, acc_scr, *, lam_init):
    qb = pl.program_id(2)
    qs = _split_heads(q_ref[0])

    def add(kb, s, t):
        return s + _tall(bias_ref[0, t])

    o = _softmax_pv(qs, lambda kb: k_ref[0, _key_rows(kb), :], lambda kb: v_ref[0, _key_rows(kb), :],
                    qb, add, s_scr, mx_scr, l_scr, acc_scr)
    lp = lp_ref[0]
    lam = (jnp.exp(jnp.sum(lp[0:1] * lp[1:2], axis=-1, keepdims=True))
           - jnp.exp(jnp.sum(lp[2:3] * lp[3:4], axis=-1, keepdims=True)) + lam_init)
    o = o[:Q_TILE] - lam * o[Q_TILE:]
    y = o * lax.rsqrt(jnp.mean(o * o, axis=-1, keepdims=True) + SUBLN_EPS)
    o_ref[0] = (y * g_ref[0] * (1.0 - lam_init)).astype(o_ref.dtype)


def _diff(proj, bias_tiles, lam_params, subln_g, layer, bsz, s_len, lam_init):
    kern = functools.partial(_diff_kernel, lam_init=lam_init)
    return pl.pallas_call(
        kern,
        grid=(bsz, B_HEADS, s_len // Q_TILE),
        in_specs=[pl.BlockSpec((1, Q_TILE, LANES), lambda b, h, i: (b, i, OFF_QB // LANES + h)),
                  pl.BlockSpec((1, s_len, LANES), lambda b, h, i: (b, 0, OFF_KB // LANES + h)),
                  pl.BlockSpec((1, s_len, LANES), lambda b, h, i: (b, 0, OFF_VB // LANES + h)),
                  pl.BlockSpec((1, 3, Q_TILE, KEY_BLOCK), lambda b, h, i: (h, 0, 0, 0)),
                  pl.BlockSpec((1, 4, HEAD_DIM), lambda b, h, i: (layer, 0, 0)),
                  pl.BlockSpec((1, 1, 2 * HEAD_DIM), lambda b, h, i: (layer, 0, 0))],
        out_specs=pl.BlockSpec((1, Q_TILE, LANES), lambda b, h, i: (b, i, h)),
        out_shape=jax.ShapeDtypeStruct((bsz, s_len, B_HEADS * 2 * HEAD_DIM), BF16),
        scratch_shapes=_softmax_scratch(s_len),
        compiler_params=_cparams("parallel", "parallel", "arbitrary"),
        name="diff_attn",
    )(proj, proj, proj, bias_tiles, lam_params, subln_g.reshape(-1, 1, 2 * HEAD_DIM))


def _stick_kernel(q_ref, k_ref, v_ref, o_ref, later_scr, run_scr, acc_scr):
    qb = pl.program_id(2)
    qs = _split_heads(q_ref[0])
    jj = lax.broadcasted_iota(I32, (2 * KEY_BLOCK, KEY_BLOCK), 0) & (KEY_BLOCK - 1)
    ss = lax.broadcasted_iota(I32, (2 * KEY_BLOCK, KEY_BLOCK), 1)
    later_scr[...] = jnp.where(jj > ss, 1.0, 0.0).astype(BF16)
    run_scr[...] = jnp.zeros_like(run_scr)
    acc_scr[...] = jnp.zeros_like(acc_scr)

    def sweep(kbs, diagonal):
        parts = []
        for j, kb in enumerate(kbs):
            z = _nt_dot(qs, k_ref[0, _key_rows(kb), :])
            soft = jnp.log(1.0 + jnp.exp2(jnp.abs(z) * (-LOG2E)))
            log_beta = jnp.minimum(z, 0.0) - soft
            log_1mb = log_beta - z
            strict = None
            if diagonal and j == 0:
                t = lax.broadcasted_iota(I32, z.shape, 0) & (Q_TILE - 1)
                strict = lax.broadcasted_iota(I32, z.shape, 1) < t
                log_1mb = jnp.where(strict, log_1mb, 0.0)
            hi = log_1mb.astype(BF16)
            lo = (log_1mb - hi.astype(F32)).astype(BF16)
            between = _dot(jnp.concatenate([hi, lo], axis=1), later_scr[...])
            total = between[:, 0:1] + log_1mb[:, 0:1]
            parts.append((kb, log_beta + between, total, strict))
        run = run_scr[...]
        acc = None
        for kb, logit, total, strict in parts:
            w = jnp.exp(logit + _wide(run))
            if strict is not None:
                w = jnp.where(strict, w, 0.0)
            pv = _dot(w.astype(BF16), v_ref[0, _key_rows(kb), :])
            acc = pv if acc is None else acc + pv
            run = run + jnp.broadcast_to(total, run.shape)
        acc_scr[...] += acc
        run_scr[...] = run

    first = jnp.where(qb + 1 >= GROUP, GROUP, jnp.where(qb + 1 >= 2, 2, 1))
    for size in (GROUP, 2, 1):
        @pl.when(first == size)
        def _(size=size):
            sweep([qb - u for u in range(size)], True)
    top = qb - first
    for size in (GROUP, 2, 1):
        trips = (top + 1) // size

        def body(i, carry, top=top, size=size):
            sweep([top - i * size - u for u in range(size)], False)
            return carry

        lax.fori_loop(0, trips, body, 0)
        top = top - trips * size
    o_ref[0] = _merge_heads(acc_scr[...], Q_TILE).astype(o_ref.dtype)


def _stick(proj, bsz, s_len):
    return pl.pallas_call(
        _stick_kernel,
        grid=(bsz, C_HEADS // 2, s_len // Q_TILE),
        in_specs=[pl.BlockSpec((1, Q_TILE, LANES), lambda b, h, i: (b, i, OFF_QC // LANES + h)),
                  pl.BlockSpec((1, s_len, LANES), lambda b, h, i: (b, 0, OFF_KC // LANES + h)),
                  pl.BlockSpec((1, s_len, LANES), lambda b, h, i: (b, 0, OFF_VC // LANES + h))],
        out_specs=pl.BlockSpec((1, Q_TILE, LANES), lambda b, h, i: (b, i, h)),
        out_shape=jax.ShapeDtypeStruct((bsz, s_len, C_HEADS * HEAD_DIM), BF16),
        scratch_shapes=[pltpu.VMEM((2 * KEY_BLOCK, KEY_BLOCK), BF16),
                        pltpu.VMEM((2 * Q_TILE, LANES), F32),
                        pltpu.VMEM((2 * Q_TILE, LANES), F32)],
        compiler_params=_cparams("parallel", "parallel", "arbitrary"),
        name="stick_attn",
    )(proj, proj, proj)


def _merge_kernel(oa_ref, ob_ref, oc_ref, gate_ref, x_ref, g1_ref, wbr_ref, wout_ref, o_ref):
    d = x_ref.shape[-1]
    merged = jnp.zeros(x_ref.shape, F32)
    for j, o_br in enumerate((oa_ref, ob_ref, oc_ref)):
        w = o_br.shape[-1]
        gate = jax.nn.sigmoid(gate_ref[:, j * d:(j + 1) * d].astype(F32))
        merged = merged + gate * _dot(o_br[...], wbr_ref[j * w:(j + 1) * w, :])
    o_ref[...] = x_ref[...] + g1_ref[0] * _dot(merged.astype(BF16), wout_ref[...])


def _merge(o_a, o_b, o_c, proj, x2, mod, w_br, w_out, s_len):
    t, d = x2.shape
    tm = 512
    per_b = s_len // tm
    w = o_a.shape[-1]
    return pl.pallas_call(
        _merge_kernel,
        grid=(t // tm,),
        in_specs=[pl.BlockSpec((tm, w), lambda i: (i, 0)),
                  pl.BlockSpec((tm, w), lambda i: (i, 0)),
                  pl.BlockSpec((tm, w), lambda i: (i, 0)),
                  pl.BlockSpec((tm, 3 * d), lambda i: (i, OFF_G // (3 * d))),
                  pl.BlockSpec((tm, d), lambda i: (i, 0)),
                  pl.BlockSpec((1, 1, d), lambda i: (i // per_b, 0, 2)),
                  pl.BlockSpec(w_br.shape, lambda i: (0, 0)),
                  pl.BlockSpec(w_out.shape, lambda i: (0, 0))],
        out_specs=pl.BlockSpec((tm, d), lambda i: (i, 0)),
        out_shape=jax.ShapeDtypeStruct((t, d), F32),
        compiler_params=_cparams("parallel"),
        name="merge",
    )(o_a, o_b, o_c, proj, x2, mod, w_br, w_out)


def _finish(x, gate, acc, fg_ref, final):
    y = x + gate * acc
    if final:
        y = y * lax.rsqrt(jnp.mean(y * y, axis=-1, keepdims=True) + EPS) * fg_ref[...]
    return y


def _swiglu_partial(h, w1, w3, w2):
    a = _dot(h, w1)
    act = (a * jax.nn.sigmoid(a)) * _dot(h, w3)
    return _dot(act.astype(BF16), w2)


def _ffn_kernel(x_ref, g_ref, sc_ref, sh_ref, gate_ref, w1_ref, w3_ref, w2_ref, fg_ref, o_ref,
                h_scr, acc_scr, *, final):
    f = pl.program_id(1)

    @pl.when(f == 0)
    def _():
        h_scr[...] = _modulated_norm(x_ref[...], g_ref[...], sc_ref[0], sh_ref[0]).astype(BF16)
        acc_scr[...] = jnp.zeros_like(acc_scr)

    acc_scr[...] += _swiglu_partial(h_scr[...], w1_ref[...], w3_ref[...], w2_ref[...])

    @pl.when(f == pl.num_programs(1) - 1)
    def _():
        o_ref[...] = _finish(x_ref[...], gate_ref[0], acc_scr[...], fg_ref, final)


def _ffn(x2, g, mod, w1, w3, w2, final_g, s_len, final):
    t, d = x2.shape
    d_ff = w1.shape[1]
    tm, tf = 1024, 256
    per_b = s_len // tm
    kern = functools.partial(_ffn_kernel, final=final)
    return pl.pallas_call(
        kern,
        grid=(t // tm, d_ff // tf),
        in_specs=[pl.BlockSpec((tm, d), lambda i, f: (i, 0)),
                  pl.BlockSpec((1, d), lambda i, f: (0, 0)),
                  pl.BlockSpec((1, 1, d), lambda i, f: (i // per_b, 0, 4)),
                  pl.BlockSpec((1, 1, d), lambda i, f: (i // per_b, 0, 3)),
                  pl.BlockSpec((1, 1, d), lambda i, f: (i // per_b, 0, 5)),
                  pl.BlockSpec((d, tf), lambda i, f: (0, f)),
                  pl.BlockSpec((d, tf), lambda i, f: (0, f)),
                  pl.BlockSpec((tf, d), lambda i, f: (f, 0)),
                  pl.BlockSpec((1, d), lambda i, f: (0, 0))],
        out_specs=pl.BlockSpec((tm, d), lambda i, f: (i, 0)),
        out_shape=jax.ShapeDtypeStruct((t, d), F32),
        scratch_shapes=[pltpu.VMEM((tm, d), BF16), pltpu.VMEM((tm, d), F32)],
        compiler_params=_cparams("parallel", "arbitrary"),
        name="ffn",
    )(x2, g, mod, mod, mod, w1, w3, w2, final_g)


MOE_TM = 512
MOE_ROWS = 512
R_E1, R_E2, R_RANK1, R_RANK2, R_W1, R_W2 = 0, 1, 2, 3, 4, 5


def _lane_pick(tile, lane, idx):
    return jnp.sum(jnp.where(lane == idx, tile, 0.0), axis=-1, keepdims=True)


def _route_kernel(x_ref, g_ref, sc_ref, sh_ref, wr_ref, br_ref, h_ref, rec_ref, cnt_ref, cnt_scr):
    @pl.when(pl.program_id(0) == 0)
    def _():
        cnt_scr[...] = jnp.zeros_like(cnt_scr)

    h = _modulated_norm(x_ref[...], g_ref[...], sc_ref[0], sh_ref[0])
    half = h.shape[1] // 2
    bits = pltpu.bitcast(h.astype(BF16).astype(F32), jnp.uint32)
    h_ref[...] = (bits[:, half:] & jnp.uint32(0xFFFF0000)) | (bits[:, :half] >> 16)

    h_hi = h.astype(BF16)
    h_lo = (h - h_hi.astype(F32)).astype(BF16)
    wr = wr_ref[...]
    w_hi = wr.astype(BF16)
    w_lo = (wr - w_hi.astype(F32)).astype(BF16)
    logits = _dot(h_hi, w_hi) + _dot(h_hi, w_lo) + _dot(h_lo, w_hi) + br_ref[...]
    lane = lax.broadcasted_iota(I32, logits.shape, 1).astype(F32)
    lg = jnp.where(lane < N_EXPERTS, logits, -jnp.inf)
    m1 = jnp.max(lg, axis=-1, keepdims=True)
    i1 = jnp.min(jnp.where(lg == m1, lane, float(LANES)), axis=-1, keepdims=True)
    lg2 = jnp.where(lane == i1, -jnp.inf, lg)
    m2 = jnp.max(lg2, axis=-1, keepdims=True)
    i2 = jnp.min(jnp.where(lg2 == m2, lane, float(LANES)), axis=-1, keepdims=True)
    e2 = jnp.exp(m2 - m1)
    w_top = 1.0 / (1.0 + e2)

    chosen = jnp.where(lane == i1, 1.0, 0.0) + jnp.where(lane == i2, 1.0, 0.0)
    tm = chosen.shape[0]
    earlier = jnp.where(lax.broadcasted_iota(I32, (tm, tm), 1) < lax.broadcasted_iota(I32, (tm, tm), 0),
                        1.0, 0.0).astype(BF16)
    rank = _dot(earlier, chosen.astype(BF16)) + cnt_scr[0:1, :]
    rec = jnp.zeros_like(logits)
    for slot, val in ((R_E1, i1), (R_E2, i2), (R_RANK1, _lane_pick(rank, lane, i1)),
                      (R_RANK2, _lane_pick(rank, lane, i2)), (R_W1, w_top), (R_W2, e2 * w_top)):
        rec = jnp.where(lane == float(slot), val, rec)
    rec_ref[...] = rec
    cnt_scr[0:1, :] = cnt_scr[0:1, :] + jnp.sum(chosen, axis=0, keepdims=True)
    cnt_ref[...] = cnt_scr[...]


def _route(x2, g, mod, wr_pad, br_pad, s_len):
    t, d = x2.shape
    tm = MOE_TM
    per_b = s_len // tm
    return pl.pallas_call(
        _route_kernel,
        grid=(t // tm,),
        in_specs=[pl.BlockSpec((tm, d), lambda i: (i, 0)),
                  pl.BlockSpec((1, d), lambda i: (0, 0)),
                  pl.BlockSpec((1, 1, d), lambda i: (i // per_b, 0, 4)),
                  pl.BlockSpec((1, 1, d), lambda i: (i // per_b, 0, 3)),
                  pl.BlockSpec((d, LANES), lambda i: (0, 0)),
                  pl.BlockSpec((1, LANES), lambda i: (0, 0))],
        out_specs=[pl.BlockSpec((tm, d // 2), lambda i: (i, 0)),
                   pl.BlockSpec((tm, LANES), lambda i: (i, 0)),
                   pl.BlockSpec((8, LANES), lambda i: (0, 0))],
        out_shape=[jax.ShapeDtypeStruct((t, d // 2), jnp.uint32),
                   jax.ShapeDtypeStruct((t, LANES), F32),
                   jax.ShapeDtypeStruct((8, LANES), F32)],
        scratch_shapes=[pltpu.VMEM((8, LANES), F32)],
        compiler_params=_cparams("arbitrary"),
        name="moe_route",
    )(x2, g, mod, mod, wr_pad, br_pad)


def _row_copy(src_ref, src_row, dst_ref, dst_row, sem):
    return pltpu.make_async_copy(src_ref.at[pl.ds(src_row, 1)], dst_ref.at[pl.ds(dst_row, 1)], sem)


def _dispatch_kernel(pos_ref, h_ref, xs_in_ref, xs_ref, sem):
    del xs_in_ref
    tm = h_ref.shape[0]

    def start(t, carry):
        _row_copy(h_ref, t, xs_ref, pos_ref[0, 0, t], sem).start(priority=0)
        _row_copy(h_ref, t, xs_ref, pos_ref[0, 0, tm + t], sem).start(priority=1)
        return carry

    lax.fori_loop(0, tm, start, 0)
    for _ in range(2):
        pltpu.make_async_copy(h_ref, xs_ref.at[pl.ds(0, tm)], sem).wait()


def _dispatch(pos, h_packed, n_rows):
    t, w = h_packed.shape
    tm = MOE_TM
    xs0 = jnp.zeros((n_rows, w), h_packed.dtype)
    return pl.pallas_call(
        _dispatch_kernel,
        grid=(t // tm,),
        in_specs=[pl.BlockSpec((1, 1, 2 * tm), lambda i: (i, 0, 0), memory_space=pltpu.SMEM),
                  pl.BlockSpec((tm, w), lambda i: (i, 0)),
                  pl.BlockSpec(memory_space=pl.ANY)],
        out_specs=pl.BlockSpec(memory_space=pl.ANY),
        out_shape=jax.ShapeDtypeStruct((n_rows, w), h_packed.dtype),
        scratch_shapes=[pltpu.SemaphoreType.DMA(())],
        input_output_aliases={2: 0},
        compiler_params=_cparams("arbitrary"),
        name="moe_dispatch",
    )(pos, h_packed, xs0)


def _expert_kernel(te_ref, nu_ref, xs_ref, w1_ref, w3_ref, w2_ref, o_ref, h_scr, acc_scr):
    i = pl.program_id(0)
    f = pl.program_id(1)

    @pl.when(i < nu_ref[0])
    def _():
        @pl.when(f == 0)
        def _():
            word = xs_ref[...]
            lo = pltpu.bitcast(word << 16, F32)
            hi = pltpu.bitcast(word & jnp.uint32(0xFFFF0000), F32)
            h_scr[...] = jnp.concatenate([lo, hi], axis=1).astype(BF16)
            acc_scr[...] = jnp.zeros_like(acc_scr)

        acc_scr[...] += _swiglu_partial(h_scr[...], w1_ref[0], w3_ref[0], w2_ref[0])

        @pl.when(f == pl.num_programs(1) - 1)
        def _():
            o_ref[...] = acc_scr[...]

    @pl.when((i >= nu_ref[0]) & (f == pl.num_programs(1) - 1))
    def _():
        o_ref[...] = jnp.zeros_like(o_ref)


def _experts(tile_expert, n_used, xs, w1, w3, w2):
    n_rows, half = xs.shape
    d = 2 * half
    d_ff = w1.shape[2]
    tf = 896
    nf = d_ff // tf
    tile = lambda i, nu: jnp.minimum(i, nu[0] - 1)
    ff = lambda i, f, nu: jnp.where(i < nu[0], f, nf - 1)
    return pl.pallas_call(
        _expert_kernel,
        grid_spec=pltpu.PrefetchScalarGridSpec(
            num_scalar_prefetch=2,
            grid=(n_rows // MOE_ROWS, nf),
            in_specs=[pl.BlockSpec((MOE_ROWS, half), lambda i, f, te, nu: (tile(i, nu), 0)),
                      pl.BlockSpec((1, d, tf), lambda i, f, te, nu: (te[i], 0, ff(i, f, nu))),
                      pl.BlockSpec((1, d, tf), lambda i, f, te, nu: (te[i], 0, ff(i, f, nu))),
                      pl.BlockSpec((1, tf, d), lambda i, f, te, nu: (te[i], ff(i, f, nu), 0))],
            out_specs=pl.BlockSpec((MOE_ROWS, d), lambda i, f, te, nu: (i, 0)),
            scratch_shapes=[pltpu.VMEM((MOE_ROWS, d), BF16), pltpu.VMEM((MOE_ROWS, d), F32)]),
        out_shape=jax.ShapeDtypeStruct((n_rows, d), F32),
        compiler_params=_cparams("arbitrary", "arbitrary"),
        name="moe_experts",
    )(tile_expert, n_used, xs, w1, w3, w2)


def _combine_kernel(pos_ref, x_ref, gate_ref, rec_ref, fg_ref, ys_ref, o_ref, y_scr, sem, *, final):
    tm = x_ref.shape[0]

    def start(t, carry):
        _row_copy(ys_ref, pos_ref[0, 0, t], y_scr.at[0], t, sem).start(priority=0)
        _row_copy(ys_ref, pos_ref[0, 0, tm + t], y_scr.at[1], t, sem).start(priority=1)
        return carry

    lax.fori_loop(0, tm, start, 0)
    for slot in range(2):
        pltpu.make_async_copy(ys_ref.at[pl.ds(0, tm)], y_scr.at[slot], sem).wait()
    rec = rec_ref[...]
    lane = lax.broadcasted_iota(I32, rec.shape, 1)
    w_first = jnp.sum(jnp.where(lane == R_W1, rec, 0.0), axis=-1, keepdims=True)
    w_second = jnp.sum(jnp.where(lane == R_W2, rec, 0.0), axis=-1, keepdims=True)
    f = w_first * y_scr[0] + w_second * y_scr[1]
    o_ref[...] = _finish(x_ref[...], gate_ref[0], f, fg_ref, final)


def _combine(pos, x2, mod, rec, final_g, ys, s_len, final):
    t, d = x2.shape
    tm = MOE_TM
    per_b = s_len // tm
    kern = functools.partial(_combine_kernel, final=final)
    return pl.pallas_call(
        kern,
        grid=(t // tm,),
        in_specs=[pl.BlockSpec((1, 1, 2 * tm), lambda i: (i, 0, 0), memory_space=pltpu.SMEM),
                  pl.BlockSpec((tm, d), lambda i: (i, 0)),
                  pl.BlockSpec((1, 1, d), lambda i: (i // per_b, 0, 5)),
                  pl.BlockSpec((tm, LANES), lambda i: (i, 0)),
                  pl.BlockSpec((1, d), lambda i: (0, 0)),
                  pl.BlockSpec(memory_space=pl.ANY)],
        out_specs=pl.BlockSpec((tm, d), lambda i: (i, 0)),
        out_shape=jax.ShapeDtypeStruct((t, d), F32),
        scratch_shapes=[pltpu.VMEM((2, tm, d), F32), pltpu.SemaphoreType.DMA(())],
        compiler_params=_cparams("arbitrary"),
        name="moe_combine",
    )(pos, x2, mod, rec, final_g, ys)


def _moe(x2, g, mod, wr_pad, br_pad, w1, w3, w2, final_g, s_len, final):
    t, d = x2.shape
    n_e = w1.shape[0]
    h_packed, rec, cnt = _route(x2, g, mod, wr_pad, br_pad, s_len)

    n_tiles = 2 * t // MOE_ROWS + n_e
    counts = cnt[0, :n_e].astype(I32)
    padded = (counts + MOE_ROWS - 1) // MOE_ROWS * MOE_ROWS
    ends = jnp.cumsum(padded)
    starts = ends - padded
    e1, e2 = rec[:, R_E1].astype(I32), rec[:, R_E2].astype(I32)
    pos1 = starts[e1] + rec[:, R_RANK1].astype(I32)
    pos2 = starts[e2] + rec[:, R_RANK2].astype(I32)
    pos = jnp.concatenate([pos1.reshape(-1, 1, MOE_TM), pos2.reshape(-1, 1, MOE_TM)], axis=2)
    n_used = (ends[-1] // MOE_ROWS).reshape(1)
    tile_start = jnp.minimum(jnp.arange(n_tiles, dtype=I32), n_used[0] - 1) * MOE_ROWS
    tile_expert = jnp.sum(tile_start[:, None] >= ends[None, :], axis=1).astype(I32)

    xs = _dispatch(pos, h_packed, n_tiles * MOE_ROWS)
    ys = _experts(tile_expert, n_used, xs, w1, w3, w2)
    return _combine(pos, x2, mod, rec, final_g, ys, s_len, final)


def _pack_w_in(w):
    d = w.shape[0]
    sizes = (W_A, W_A, W_A, IDX_HEADS * IDX_DIM, IDX_DIM, IDX_HEADS,
             512, 512, 512, 512, 512, 512, 3 * d)
    offs = [0]
    for s in sizes:
        offs.append(offs[-1] + s)
    (qa, ka, va, qi, ki, wi, qb, kb, vb, qc, kc, vc, gl) = [
        w[:, offs[j]:offs[j + 1]] for j in range(len(sizes))]
    scale = HEAD_DIM ** -0.5
    pad_wi = jnp.zeros((d, LANES - IDX_HEADS), w.dtype)
    packed = jnp.concatenate(
        [gl, qa * scale, ka, va,
         qi * (IDX_DIM ** -0.5), ki, ki, wi * (IDX_HEADS ** -0.5), pad_wi,
         qb * scale, kb, vb, qc * scale, kc, vc], axis=1)
    assert packed.shape[1] == PACKED
    return packed.astype(BF16)


def kernel(x, c, w_ada, b_ada, norm1_g, norm2_g, w_in, w_br, w_out, rel_bias, lam_params,
           subln_g, ffn_w1, ffn_w3, ffn_w2, router_w, router_b, moe_w1, moe_w3, moe_w2, final_g):
    bsz, s_len, d = x.shape
    depth = w_ada.shape[0]
    assert s_len % 1024 == 0 and d == 1024 and OFF_G + 3 * d == OFF_QA

    c_pad = jnp.concatenate([c, jnp.zeros((8 - bsz % 8 if bsz % 8 else 0, d), c.dtype)], axis=0)
    mod_all = _ada(c_pad, w_ada, b_ada)
    dsa_tiles, diff_tiles = _bias_tiles(rel_bias)
    fg = final_g.reshape(1, d)

    x2 = x.reshape(bsz * s_len, d)
    for l in range(depth):
        mod = mod_all[l, :bsz].reshape(bsz, 1, 6 * d)
        proj = _inproj(x2, norm1_g[l].reshape(1, d), mod, _pack_w_in(w_in[l]), s_len)
        proj3 = proj.reshape(bsz, s_len, PACKED)
        lam_init = 0.8 - 0.6 * math.exp(-0.3 * l)
        o_a = _dsa(proj3, dsa_tiles, bsz, s_len)
        o_b = _diff(proj3, diff_tiles, lam_params, subln_g, l, bsz, s_len, lam_init)
        o_c = _stick(proj3, bsz, s_len)
        x2 = _merge(o_a.reshape(-1, o_a.shape[-1]), o_b.reshape(-1, o_b.shape[-1]),
                    o_c.reshape(-1, o_c.shape[-1]), proj, x2, mod,
                    w_br[l].astype(BF16), w_out[l].astype(BF16), s_len)
        g2 = norm2_g[l].reshape(1, d)
        final = l == depth - 1
        j = l // 2
        if l % 2 == 0:
            x2 = _ffn(x2, g2, mod, ffn_w1[j].astype(BF16), ffn_w3[j].astype(BF16),
                      ffn_w2[j].astype(BF16), fg, s_len, final)
        else:
            wr_pad = jnp.pad(router_w[j], ((0, 0), (0, LANES - N_EXPERTS)))
            br_pad = jnp.pad(router_b[j], (0, LANES - N_EXPERTS)).reshape(1, LANES)
            x2 = _moe(x2, g2, mod, wr_pad, br_pad, moe_w1[j].astype(BF16), moe_w3[j].astype(BF16),
                      moe_w2[j].astype(BF16), fg, s_len, final)
    return x2.reshape(bsz, s_len, d)
```

```python
import functools
import math

import jax
import jax.numpy as jnp
from jax import lax
from jax.experimental import pallas as pl
from jax.experimental.pallas import tpu as pltpu

F32 = jnp.float32
BF16 = jnp.bfloat16
I32 = jnp.int32
I16 = jnp.int16

LANES = 128
VMEM_LIMIT_BYTES = 56 * 1024 * 1024

CHUNK = 64
A_HEADS = 8
IDX_HEADS = 4
IDX_DIM = 64
TOPK_MAX = 256
B_HEADS = 4
C_HEADS = 8
HEAD_DIM = 64
REL_BUCKETS = 32
FAR_BUCKET = REL_BUCKETS // 2 - 1
N_EXPERTS = 8
EPS = 1e-6
SUBLN_EPS = 1e-5
NEG = -1e30
LOG2E = 1.4426950408889634
INT_MIN = -(2 ** 31)

KEY_BLOCK = 256
Q_TILE = 256
GROUP = 4
BLOCK_GROUPS = (8, 4, 2, 1)

W_A = A_HEADS * HEAD_DIM
OFF_G = 0
OFF_QA, OFF_KA, OFF_VA = 3072, 3584, 4096
OFF_QI, OFF_KI, OFF_WI = 4608, 4864, 4992
OFF_QB, OFF_KB, OFF_VB = 5120, 5632, 6144
OFF_QC, OFF_KC, OFF_VC = 6656, 7168, 7680
PACKED = 8192

LOG_BUCKET_STEPS = (12, 16, 23, 32, 46, 64, 91)


def _nt_dot(a, b):
    return lax.dot_general(a, b, (((1,), (1,)), ((), ())), preferred_element_type=F32)


def _dot(a, b):
    return jnp.dot(a, b, preferred_element_type=F32)


def _cparams(*sem):
    return pltpu.CompilerParams(dimension_semantics=sem, vmem_limit_bytes=VMEM_LIMIT_BYTES)


def _split_heads(x):
    lane = lax.broadcasted_iota(I32, x.shape, 1)
    keep_a = jnp.where(lane < HEAD_DIM, 1.0, 0.0).astype(x.dtype)
    keep_b = jnp.where(lane < HEAD_DIM, 0.0, 1.0).astype(x.dtype)
    return jnp.concatenate([x * keep_a, x * keep_b], axis=0)


def _merge_heads(o, m):
    lane = lax.broadcasted_iota(I32, (m, LANES), 1)
    return jnp.where(lane < HEAD_DIM, o[:m], o[m:])


def _wide(x):
    return jnp.concatenate([x, x], axis=1)


def _tall(x):
    return jnp.concatenate([x, x], axis=0)


def _key_rows(kb):
    return pl.ds(pl.multiple_of(kb * KEY_BLOCK, KEY_BLOCK), KEY_BLOCK)


def _sortable(x):
    bits = pltpu.bitcast(x, I32)
    return bits ^ ((bits >> 31) & 0x7FFFFFFF)


def _chunk_causal():
    r = lax.broadcasted_iota(I32, (Q_TILE, KEY_BLOCK), 0)
    c = lax.broadcasted_iota(I32, (Q_TILE, KEY_BLOCK), 1)
    return (c // CHUNK) <= (r // CHUNK)


def _ada_kernel(c_ref, w_ref, b_ref, o_ref):
    c = c_ref[...]
    a = c * jax.nn.sigmoid(c)
    o_ref[0] = jnp.dot(a, w_ref[0], preferred_element_type=F32,
                       precision=lax.Precision.HIGHEST) + b_ref[0]


def _ada(c_pad, w_ada, b_ada):
    depth, d, n = w_ada.shape
    tn = 1024
    return pl.pallas_call(
        _ada_kernel,
        grid=(depth, n // tn),
        in_specs=[pl.BlockSpec(c_pad.shape, lambda l, j: (0, 0)),
                  pl.BlockSpec((1, d, tn), lambda l, j: (l, 0, j)),
                  pl.BlockSpec((1, 1, tn), lambda l, j: (l, 0, j))],
        out_specs=pl.BlockSpec((1, c_pad.shape[0], tn), lambda l, j: (l, 0, j)),
        out_shape=jax.ShapeDtypeStruct((depth, c_pad.shape[0], n), F32),
        compiler_params=_cparams("parallel", "parallel"),
        name="ada",
    )(c_pad, w_ada, b_ada.reshape(depth, 1, n))


def _rel_bias_tile(tab_ref, head, d0, n_heads_total):
    r = lax.broadcasted_iota(I32, (Q_TILE, KEY_BLOCK), 0)
    c = lax.broadcasted_iota(I32, (Q_TILE, KEY_BLOCK), 1)
    d = c - r + d0
    n = jnp.abs(d)
    large = jnp.full(d.shape, REL_BUCKETS // 4, I32)
    for step in LOG_BUCKET_STEPS:
        large = large + jnp.where(n >= step, 1, 0)
    bucket = jnp.where(d > 0, REL_BUCKETS // 2, 0) + jnp.where(n < REL_BUCKETS // 4, n, large)
    out = jnp.zeros(d.shape, F32)
    for b in range(REL_BUCKETS):
        out = jnp.where(bucket == b, tab_ref[b * n_heads_total + head], out)
    return out - tab_ref[FAR_BUCKET * n_heads_total + head]


def _bias_kernel(tab_ref, dsa_ref, diff_ref):
    p = pl.program_id(0)
    n_heads = A_HEADS + B_HEADS
    dsa_ref[0, 0] = jnp.zeros(dsa_ref.shape[2:], F32)
    diff_ref[0, 0] = jnp.zeros(diff_ref.shape[2:], F32)
    for t, d0 in ((1, -KEY_BLOCK), (2, 0)):
        dsa_ref[0, t, 0:Q_TILE, :] = _rel_bias_tile(tab_ref, 2 * p, d0, n_heads)
        dsa_ref[0, t, Q_TILE:2 * Q_TILE, :] = _rel_bias_tile(tab_ref, 2 * p + 1, d0, n_heads)
        tile = _rel_bias_tile(tab_ref, A_HEADS + p, d0, n_heads)
        if t == 2:
            tile = jnp.where(_chunk_causal(), tile, NEG)
        diff_ref[0, t] = tile


def _bias_tiles(rel_bias):
    tab = rel_bias.reshape(-1)
    return pl.pallas_call(
        _bias_kernel,
        grid=(4,),
        in_specs=[pl.BlockSpec(memory_space=pltpu.SMEM)],
        out_specs=[pl.BlockSpec((1, 3, 2 * Q_TILE, KEY_BLOCK), lambda p: (p, 0, 0, 0)),
                   pl.BlockSpec((1, 3, Q_TILE, KEY_BLOCK), lambda p: (p, 0, 0, 0))],
        out_shape=[jax.ShapeDtypeStruct((A_HEADS // 2, 3, 2 * Q_TILE, KEY_BLOCK), F32),
                   jax.ShapeDtypeStruct((B_HEADS, 3, Q_TILE, KEY_BLOCK), F32)],
        compiler_params=_cparams("parallel"),
        name="rel_bias_tiles",
    )(tab)


def _modulated_norm(x, g, sc, sh):
    y = x * lax.rsqrt(jnp.mean(x * x, axis=-1, keepdims=True) + EPS)
    return y * g * (1.0 + sc) + sh


def _inproj_kernel(x_ref, g_ref, sc_ref, sh_ref, w_ref, o_ref, h_scr):
    @pl.when(pl.program_id(1) == 0)
    def _():
        h_scr[...] = _modulated_norm(x_ref[...], g_ref[...], sc_ref[0], sh_ref[0]).astype(BF16)

    o_ref[...] = _dot(h_scr[...], w_ref[...]).astype(o_ref.dtype)


def _inproj(x2, g, mod, w_packed, s_len):
    t, d = x2.shape
    n = w_packed.shape[1]
    tm, tn = 1024, 1024
    per_b = s_len // tm
    return pl.pallas_call(
        _inproj_kernel,
        grid=(t // tm, n // tn),
        in_specs=[pl.BlockSpec((tm, d), lambda i, j: (i, 0)),
                  pl.BlockSpec((1, d), lambda i, j: (0, 0)),
                  pl.BlockSpec((1, 1, d), lambda i, j: (i // per_b, 0, 1)),
                  pl.BlockSpec((1, 1, d), lambda i, j: (i // per_b, 0, 0)),
                  pl.BlockSpec((d, tn), lambda i, j: (0, j))],
        out_specs=pl.BlockSpec((tm, tn), lambda i, j: (i, j)),
        out_shape=jax.ShapeDtypeStruct((t, n), BF16),
        scratch_shapes=[pltpu.VMEM((tm, d), BF16)],
        compiler_params=_cparams("parallel", "arbitrary"),
        name="inproj",
    )(x2, g, mod, mod, w_packed)


def _for_blocks(lo, hi, fn, sizes=BLOCK_GROUPS):
    pos = lo
    for size in sizes:
        trips = jnp.maximum(hi - pos, 0) // size

        def body(i, carry, pos=pos, size=size):
            fn([pos + i * size + u for u in range(size)])
            return carry

        lax.fori_loop(0, trips, body, 0)
        pos = pos + trips * size


def _fold_blocks(n, fn, acc):
    pos = 0
    for size in BLOCK_GROUPS:
        trips = (n - pos) // size

        def body(i, acc, pos=pos, size=size):
            for u in range(size):
                acc = fn(pos + i * size + u, acc)
            return acc

        acc = lax.fori_loop(0, trips, body, acc)
        pos = pos + trips * size
    return acc


def _softmax_pv(qs, k_at, v_at, qb, add, s_scr, mx_scr, l_scr, acc_scr):
    def pass1(blocks):
        mx = None
        for kb in blocks:
            s = add(kb, _nt_dot(qs, k_at(kb)), jnp.clip(kb - qb + 2, 0, 2))
            s = s * LOG2E
            s_scr[kb] = s
            fold = jnp.maximum(s[:, :LANES], s[:, LANES:])
            mx = fold if mx is None else jnp.maximum(mx, fold)
        mx_scr[...] = jnp.maximum(mx_scr[...], mx)

    mx_scr[...] = jnp.full(mx_scr.shape, NEG, F32)
    _for_blocks(0, qb + 1, pass1)

    m = jnp.max(mx_scr[...], axis=-1, keepdims=True)
    mx_scr[...] = jnp.broadcast_to(m, mx_scr.shape)
    l_scr[...] = jnp.zeros_like(l_scr)
    acc_scr[...] = jnp.zeros_like(acc_scr)

    def pass2(kbs):
        mw = _wide(mx_scr[...])
        l_add = acc_add = None
        for kb in kbs:
            p = jnp.exp2(s_scr[kb] - mw)
            fold = p[:, :LANES] + p[:, LANES:]
            pv = _dot(p.astype(BF16), v_at(kb))
            l_add = fold if l_add is None else l_add + fold
            acc_add = pv if acc_add is None else acc_add + pv
        l_scr[...] += l_add
        acc_scr[...] += acc_add

    _for_blocks(0, qb + 1, pass2)
    return acc_scr[...] / jnp.sum(l_scr[...], axis=-1, keepdims=True)


def _softmax_scratch(s_len):
    return [pltpu.VMEM((s_len // KEY_BLOCK, 2 * Q_TILE, KEY_BLOCK), F32),
            pltpu.VMEM((2 * Q_TILE, LANES), F32),
            pltpu.VMEM((2 * Q_TILE, LANES), F32),
            pltpu.VMEM((2 * Q_TILE, LANES), F32)]


def _dsa_kernel(qa_ref, qi_ref, wi_ref, ka_ref, va_ref, ki_ref, bias_ref, o_ref,
                key_scr, keyt_scr, hi_scr, lo_scr, neg_scr, w_scr, last_scr, s_scr, mx_scr, l_scr, acc_scr,
                *, topk, idx_bits, s_len):
    qb = pl.program_id(1)

    @pl.when(pl.program_id(2) == 0)
    def _():
        _dsa_select(qb, qi_ref, wi_ref, ki_ref, key_scr, keyt_scr, hi_scr, lo_scr, neg_scr, w_scr, last_scr,
                    topk=topk, idx_bits=idx_bits, s_len=s_len)

    pr = pl.program_id(2)
    lanes = pl.ds(pl.multiple_of(pr * LANES, LANES), LANES)

    def add(kb, s, t):
        return s + _tall(neg_scr[kb]) + bias_ref[pr, t]

    o = _softmax_pv(_split_heads(qa_ref[0]),
                    lambda kb: ka_ref[0, _key_rows(kb), lanes],
                    lambda kb: va_ref[0, _key_rows(kb), lanes],
                    qb, add, s_scr, mx_scr, l_scr, acc_scr)
    o_ref[0] = _merge_heads(o, Q_TILE).astype(o_ref.dtype)


def _dsa_select(qb, qi_ref, wi_ref, ki_ref, key_scr, keyt_scr, hi_scr, lo_scr, neg_scr, w_scr, last_scr,
                *, topk, idx_bits, s_len):
    nkb = qb + 1
    kf = float(topk)
    valid_diag = _chunk_causal()

    def cols(kb):
        return kb * KEY_BLOCK + lax.broadcasted_iota(I32, (Q_TILE, KEY_BLOCK), 1)

    qi = qi_ref[0]
    q_heads = [_split_heads(qi[:, pr * LANES:(pr + 1) * LANES]) for pr in range(IDX_HEADS // 2)]
    wi = wi_ref[0].astype(F32)
    for h in range(IDX_HEADS):
        w_scr[h] = jnp.broadcast_to(wi[:, h:h + 1], (Q_TILE, LANES))

    def score_block(kb, valid):
        kk = ki_ref[0, _key_rows(kb), :]
        sc = jnp.zeros((Q_TILE, KEY_BLOCK), F32)
        for pr in range(IDX_HEADS // 2):
            d = _nt_dot(q_heads[pr], kk)
            sc = sc + _wide(w_scr[2 * pr]) * jnp.maximum(d[:Q_TILE], 0.0)
            sc = sc + _wide(w_scr[2 * pr + 1]) * jnp.maximum(d[Q_TILE:], 0.0)
        sc = jnp.where(sc == 0.0, 0.0, sc)
        if valid is not None:
            sc = jnp.where(valid, sc, -jnp.inf)
        key_scr[kb] = _sortable(sc)
        key_t = _sortable(sc.T)
        keyt_scr[kb] = key_t
        hi_scr[kb] = (key_t >> 16).astype(I16)
        lo_scr[kb] = ((key_t & 0xFFFF) - 2 ** 15).astype(I16)

    _for_blocks(0, qb, lambda kbs: [score_block(kb, None) for kb in kbs], sizes=(2, 1))
    score_block(qb, valid_diag)

    def per_query(x):
        return jnp.broadcast_to(x[None], (KEY_BLOCK // 8, 8, Q_TILE)).reshape(KEY_BLOCK, Q_TILE)

    def key_index(kb):
        return kb * KEY_BLOCK + lax.broadcasted_iota(I32, (KEY_BLOCK, Q_TILE), 0)

    def count(pred):
        def one(kb, acc):
            hit = jnp.where(pred(keyt_scr[kb], kb), 1.0, 0.0)
            return acc + jnp.sum(hit.reshape(KEY_BLOCK // 8, 8, Q_TILE), axis=0)

        acc = _fold_blocks(nkb, one, jnp.zeros((8, Q_TILE), F32))
        return jnp.broadcast_to(jnp.sum(acc, axis=0, keepdims=True), (8, Q_TILE))

    def per_query16(x):
        x16 = x.astype(I16)
        return jnp.broadcast_to(x16[None], (KEY_BLOCK // 16, 16, Q_TILE)).reshape(KEY_BLOCK, Q_TILE)

    def count16(ref, pred):
        def one(kb, acc):
            hit = jnp.where(pred(ref[kb]), jnp.int16(1), jnp.int16(0))
            for g in range(0, KEY_BLOCK, 16):
                acc = acc + hit[g:g + 16]
            return acc

        acc = _fold_blocks(nkb, one, jnp.zeros((16, Q_TILE), I16))
        return jnp.broadcast_to(jnp.sum(acc.astype(F32), axis=0, keepdims=True), (16, Q_TILE))

    hi_thr = jnp.where(count16(hi_scr, lambda h: h >= 0) >= kf,
                       jnp.zeros((16, Q_TILE), I32), jnp.full((16, Q_TILE), -(2 ** 15), I32))

    def hi_body(j, t):
        cand = t | lax.shift_left(jnp.int32(1), 14 - j)
        cw = per_query16(cand)
        return jnp.where(count16(hi_scr, lambda h: h >= cw) >= kf, cand, t)

    hi_thr = lax.fori_loop(0, 15, hi_body, hi_thr)
    hi_w = per_query16(hi_thr)
    above = count16(hi_scr, lambda h: h > hi_w)

    def keep_ties(kb, carry):
        lo_scr[kb] = jnp.where(hi_scr[kb] == hi_w, lo_scr[kb], jnp.int16(-(2 ** 15)))
        return carry

    lax.fori_loop(0, nkb, keep_ties, 0)

    def lo_body(j, t):
        cand = t | lax.shift_left(jnp.int32(1), 15 - j)
        cw = per_query16(cand - 2 ** 15)
        return jnp.where(above + count16(lo_scr, lambda lo: lo >= cw) >= kf, cand, t)

    lo_thr = lax.fori_loop(0, 16, lo_body, jnp.zeros((16, Q_TILE), I32))
    thr = (lax.shift_left(hi_thr, 16) | lo_thr)[0:8]
    thr_t = per_query(thr)

    need = kf - count(lambda key, kb: key > thr_t)
    ties = count(lambda key, kb: key == thr_t)
    last_scr[...] = jnp.full(last_scr.shape, s_len, I32)

    @pl.when(jnp.max(ties - need) > 0.0)
    def _():
        def tie_body(j, last):
            cand = last | lax.shift_left(jnp.int32(1), idx_bits - 1 - j)
            cw = per_query(cand)
            taken = count(lambda key, kb: (key == thr_t) & (key_index(kb) < cw))
            return jnp.where(taken < need, cand, last)

        last_scr[...] = lax.fori_loop(0, idx_bits, tie_body, jnp.zeros((8, Q_TILE), I32))

    def per_row(x):
        return _wide(jnp.broadcast_to(x[0:1], (LANES, Q_TILE)).T)

    tw = per_row(thr)
    lastw = per_row(last_scr[...])

    def mask_block(kb, valid):
        key = key_scr[kb]
        tie_neg = jnp.where(key == tw, jnp.where(cols(kb) <= lastw, 0.0, NEG), NEG)
        neg = jnp.where(key > tw, 0.0, tie_neg)
        if valid is not None:
            neg = jnp.where(valid, neg, NEG)
        neg_scr[kb] = neg

    def mask_body(kb, carry):
        mask_block(kb, None)
        return carry

    lax.fori_loop(0, qb, mask_body, 0)
    mask_block(qb, valid_diag)


def _dsa(proj, bias_tiles, bsz, s_len):
    topk = min(TOPK_MAX, s_len // 4)
    idx_bits = max(1, (s_len - 1).bit_length())
    n_blocks = s_len // KEY_BLOCK
    assert 2 * n_blocks <= 256
    kern = functools.partial(_dsa_kernel, topk=topk, idx_bits=idx_bits, s_len=s_len)
    lane_blk = lambda off: off // LANES
    return pl.pallas_call(
        kern,
        grid=(bsz, s_len // Q_TILE, A_HEADS // 2),
        in_specs=[pl.BlockSpec((1, Q_TILE, LANES), lambda b, i, p: (b, i, lane_blk(OFF_QA) + p)),
                  pl.BlockSpec((1, Q_TILE, 256), lambda b, i, p: (b, i, OFF_QI // 256)),
                  pl.BlockSpec((1, Q_TILE, LANES), lambda b, i, p: (b, i, lane_blk(OFF_WI))),
                  pl.BlockSpec((1, s_len, W_A), lambda b, i, p: (b, 0, OFF_KA // W_A),
                               pipeline_mode=pl.Buffered(1)),
                  pl.BlockSpec((1, s_len, W_A), lambda b, i, p: (b, 0, OFF_VA // W_A),
                               pipeline_mode=pl.Buffered(1)),
                  pl.BlockSpec((1, s_len, LANES), lambda b, i, p: (b, 0, lane_blk(OFF_KI))),
                  pl.BlockSpec(bias_tiles.shape, lambda b, i, p: (0, 0, 0, 0),
                               pipeline_mode=pl.Buffered(1))],
        out_specs=pl.BlockSpec((1, Q_TILE, LANES), lambda b, i, p: (b, i, p)),
        out_shape=jax.ShapeDtypeStruct((bsz, s_len, W_A), BF16),
        scratch_shapes=[pltpu.VMEM((n_blocks, Q_TILE, KEY_BLOCK), I32),
                        pltpu.VMEM((n_blocks, KEY_BLOCK, Q_TILE), I32),
                        pltpu.VMEM((n_blocks, KEY_BLOCK, Q_TILE), I16),
                        pltpu.VMEM((n_blocks, KEY_BLOCK, Q_TILE), I16),
                        pltpu.VMEM((n_blocks, Q_TILE, KEY_BLOCK), F32),
                        pltpu.VMEM((IDX_HEADS, Q_TILE, LANES), F32),
                        pltpu.VMEM((8, Q_TILE), I32)] + _softmax_scratch(s_len),
        compiler_params=_cparams("parallel", "arbitrary", "arbitrary"),
        name="dsa",
    )(proj, proj, proj, proj, proj, proj, bias_tiles)


def _diff_kernel(q_ref, k_ref, v_ref, bias_ref, lp_ref, g_ref, o_ref,
                 s_scr, mx_scr, l_scr, acc_scr, *, lam_init):
    qb = pl.program_id(2)
    qs = _split_heads(q_ref[0])

    def add(kb, s, t):
        return s + _tall(bias_ref[0, t])

    o = _softmax_pv(qs, lambda kb: k_ref[0, _key_rows(kb), :], lambda kb: v_ref[0, _key_rows(kb), :],
                    qb, add, s_scr, mx_scr, l_scr, acc_scr)
    lp = lp_ref[0]
    lam = (jnp.exp(jnp.sum(lp[0:1] * lp[1:2], axis=-1, keepdims=True))
           - jnp.exp(jnp.sum(lp[2:3] * lp[3:4], axis=-1, keepdims=True)) + lam_init)
    o = o[:Q_TILE] - lam * o[Q_TILE:]
    y = o * lax.rsqrt(jnp.mean(o * o, axis=-1, keepdims=True) + SUBLN_EPS)
    o_ref[0] = (y * g_ref[0] * (1.0 - lam_init)).astype(o_ref.dtype)


def _diff(proj, bias_tiles, lam_params, subln_g, layer, bsz, s_len, lam_init):
    kern = functools.partial(_diff_kernel, lam_init=lam_init)
    return pl.pallas_call(
        kern,
        grid=(bsz, B_HEADS, s_len // Q_TILE),
        in_specs=[pl.BlockSpec((1, Q_TILE, LANES), lambda b, h, i: (b, i, OFF_QB // LANES + h)),
                  pl.BlockSpec((1, s_len, LANES), lambda b, h, i: (b, 0, OFF_KB // LANES + h)),
                  pl.BlockSpec((1, s_len, LANES), lambda b, h, i: (b, 0, OFF_VB // LANES + h)),
                  pl.BlockSpec((1, 3, Q_TILE, KEY_BLOCK), lambda b, h, i: (h, 0, 0, 0)),
                  pl.BlockSpec((1, 4, HEAD_DIM), lambda b, h, i: (layer, 0, 0)),
                  pl.BlockSpec((1, 1, 2 * HEAD_DIM), lambda b, h, i: (layer, 0, 0))],
        out_specs=pl.BlockSpec((1, Q_TILE, LANES), lambda b, h, i: (b, i, h)),
        out_shape=jax.ShapeDtypeStruct((bsz, s_len, B_HEADS * 2 * HEAD_DIM), BF16),
        scratch_shapes=_softmax_scratch(s_len),
        compiler_params=_cparams("parallel", "parallel", "arbitrary"),
        name="diff_attn",
    )(proj, proj, proj, bias_tiles, lam_params, subln_g.reshape(-1, 1, 2 * HEAD_DIM))


def _stick_kernel(q_ref, k_ref, v_ref, o_ref, later_scr, run_scr, acc_scr):
    qb = pl.program_id(2)
    qs = _split_heads(q_ref[0])
    jj = lax.broadcasted_iota(I32, (2 * KEY_BLOCK, KEY_BLOCK), 0) & (KEY_BLOCK - 1)
    ss = lax.broadcasted_iota(I32, (2 * KEY_BLOCK, KEY_BLOCK), 1)
    later_scr[...] = jnp.where(jj > ss, 1.0, 0.0).astype(BF16)
    run_scr[...] = jnp.zeros_like(run_scr)
    acc_scr[...] = jnp.zeros_like(acc_scr)

    def sweep(kbs, diagonal):
        parts = []
        for j, kb in enumerate(kbs):
            z = _nt_dot(qs, k_ref[0, _key_rows(kb), :])
            soft = jnp.log(1.0 + jnp.exp2(jnp.abs(z) * (-LOG2E)))
            log_beta = jnp.minimum(z, 0.0) - soft
            log_1mb = log_beta - z
            strict = None
            if diagonal and j == 0:
                t = lax.broadcasted_iota(I32, z.shape, 0) & (Q_TILE - 1)
                strict = lax.broadcasted_iota(I32, z.shape, 1) < t
                log_1mb = jnp.where(strict, log_1mb, 0.0)
            hi = log_1mb.astype(BF16)
            lo = (log_1mb - hi.astype(F32)).astype(BF16)
            between = _dot(jnp.concatenate([hi, lo], axis=1), later_scr[...])
            total = between[:, 0:1] + log_1mb[:, 0:1]
            parts.append((kb, log_beta + between, total, strict))
        run = run_scr[...]
        acc = None
        for kb, logit, total, strict in parts:
            w = jnp.exp(logit + _wide(run))
            if strict is not None:
                w = jnp.where(strict, w, 0.0)
            pv = _dot(w.astype(BF16), v_ref[0, _key_rows(kb), :])
            acc = pv if acc is None else acc + pv
            run = run + jnp.broadcast_to(total, run.shape)
        acc_scr[...] += acc
        run_scr[...] = run

    first = jnp.where(qb + 1 >= GROUP, GROUP, jnp.where(qb + 1 >= 2, 2, 1))
    for size in (GROUP, 2, 1):
        @pl.when(first == size)
        def _(size=size):
            sweep([qb - u for u in range(size)], True)
    top = qb - first
    for size in BLOCK_GROUPS:
        trips = (top + 1) // size

        def body(i, carry, top=top, size=size):
            sweep([top - i * size - u for u in range(size)], False)
            return carry

        lax.fori_loop(0, trips, body, 0)
        top = top - trips * size
    o_ref[0] = _merge_heads(acc_scr[...], Q_TILE).astype(o_ref.dtype)


def _stick(proj, bsz, s_len):
    return pl.pallas_call(
        _stick_kernel,
        grid=(bsz, C_HEADS // 2, s_len // Q_TILE),
        in_specs=[pl.BlockSpec((1, Q_TILE, LANES), lambda b, h, i: (b, i, OFF_QC // LANES + h)),
                  pl.BlockSpec((1, s_len, LANES), lambda b, h, i: (b, 0, OFF_KC // LANES + h)),
                  pl.BlockSpec((1, s_len, LANES), lambda b, h, i: (b, 0, OFF_VC // LANES + h))],
        out_specs=pl.BlockSpec((1, Q_TILE, LANES), lambda b, h, i: (b, i, h)),
        out_shape=jax.ShapeDtypeStruct((bsz, s_len, C_HEADS * HEAD_DIM), BF16),
        scratch_shapes=[pltpu.VMEM((2 * KEY_BLOCK, KEY_BLOCK), BF16),
                        pltpu.VMEM((2 * Q_TILE, LANES), F32),
                        pltpu.VMEM((2 * Q_TILE, LANES), F32)],
        compiler_params=_cparams("parallel", "parallel", "arbitrary"),
        name="stick_attn",
    )(proj, proj, proj)


def _merge_kernel(oa_ref, ob_ref, oc_ref, gate_ref, x_ref, g1_ref, wbr_ref, wout_ref, o_ref):
    d = x_ref.shape[-1]
    merged = jnp.zeros(x_ref.shape, F32)
    for j, o_br in enumerate((oa_ref, ob_ref, oc_ref)):
        w = o_br.shape[-1]
        gate = jax.nn.sigmoid(gate_ref[:, j * d:(j + 1) * d].astype(F32))
        merged = merged + gate * _dot(o_br[...], wbr_ref[j * w:(j + 1) * w, :])
    o_ref[...] = x_ref[...] + g1_ref[0] * _dot(merged.astype(BF16), wout_ref[...])


def _merge(o_a, o_b, o_c, proj, x2, mod, w_br, w_out, s_len):
    t, d = x2.shape
    tm = 512
    per_b = s_len // tm
    w = o_a.shape[-1]
    return pl.pallas_call(
        _merge_kernel,
        grid=(t // tm,),
        in_specs=[pl.BlockSpec((tm, w), lambda i: (i, 0)),
                  pl.BlockSpec((tm, w), lambda i: (i, 0)),
                  pl.BlockSpec((tm, w), lambda i: (i, 0)),
                  pl.BlockSpec((tm, 3 * d), lambda i: (i, OFF_G // (3 * d))),
                  pl.BlockSpec((tm, d), lambda i: (i, 0)),
                  pl.BlockSpec((1, 1, d), lambda i: (i // per_b, 0, 2)),
                  pl.BlockSpec(w_br.shape, lambda i: (0, 0)),
                  pl.BlockSpec(w_out.shape, lambda i: (0, 0))],
        out_specs=pl.BlockSpec((tm, d), lambda i: (i, 0)),
        out_shape=jax.ShapeDtypeStruct((t, d), F32),
        compiler_params=_cparams("parallel"),
        name="merge",
    )(o_a, o_b, o_c, proj, x2, mod, w_br, w_out)


def _finish(x, gate, acc, fg_ref, final):
    y = x + gate * acc
    if final:
        y = y * lax.rsqrt(jnp.mean(y * y, axis=-1, keepdims=True) + EPS) * fg_ref[...]
    return y


def _swiglu_partial(h, w1, w3, w2):
    a = _dot(h, w1)
    act = (a * jax.nn.sigmoid(a)) * _dot(h, w3)
    return _dot(act.astype(BF16), w2)


def _ffn_kernel(x_ref, g_ref, sc_ref, sh_ref, gate_ref, w1_ref, w3_ref, w2_ref, fg_ref, o_ref,
                h_scr, acc_scr, *, final):
    f = pl.program_id(1)

    @pl.when(f == 0)
    def _():
        h_scr[...] = _modulated_norm(x_ref[...], g_ref[...], sc_ref[0], sh_ref[0]).astype(BF16)
        acc_scr[...] = jnp.zeros_like(acc_scr)

    acc_scr[...] += _swiglu_partial(h_scr[...], w1_ref[...], w3_ref[...], w2_ref[...])

    @pl.when(f == pl.num_programs(1) - 1)
    def _():
        o_ref[...] = _finish(x_ref[...], gate_ref[0], acc_scr[...], fg_ref, final)


def _ffn(x2, g, mod, w1, w3, w2, final_g, s_len, final):
    t, d = x2.shape
    d_ff = w1.shape[1]
    tm, tf = 1024, 256
    per_b = s_len // tm
    kern = functools.partial(_ffn_kernel, final=final)
    return pl.pallas_call(
        kern,
        grid=(t // tm, d_ff // tf),
        in_specs=[pl.BlockSpec((tm, d), lambda i, f: (i, 0)),
                  pl.BlockSpec((1, d), lambda i, f: (0, 0)),
                  pl.BlockSpec((1, 1, d), lambda i, f: (i // per_b, 0, 4)),
                  pl.BlockSpec((1, 1, d), lambda i, f: (i // per_b, 0, 3)),
                  pl.BlockSpec((1, 1, d), lambda i, f: (i // per_b, 0, 5)),
                  pl.BlockSpec((d, tf), lambda i, f: (0, f)),
                  pl.BlockSpec((d, tf), lambda i, f: (0, f)),
                  pl.BlockSpec((tf, d), lambda i, f: (f, 0)),
                  pl.BlockSpec((1, d), lambda i, f: (0, 0))],
        out_specs=pl.BlockSpec((tm, d), lambda i, f: (i, 0)),
        out_shape=jax.ShapeDtypeStruct((t, d), F32),
        scratch_shapes=[pltpu.VMEM((tm, d), BF16), pltpu.VMEM((tm, d), F32)],
        compiler_params=_cparams("parallel", "arbitrary"),
        name="ffn",
    )(x2, g, mod, mod, mod, w1, w3, w2, final_g)


MOE_TM = 512
MOE_ROWS = 512
R_E1, R_E2, R_RANK1, R_RANK2, R_W1, R_W2 = 0, 1, 2, 3, 4, 5


def _lane_pick(tile, lane, idx):
    return jnp.sum(jnp.where(lane == idx, tile, 0.0), axis=-1, keepdims=True)


def _route_kernel(x_ref, g_ref, sc_ref, sh_ref, wr_ref, br_ref, h_ref, rec_ref, cnt_ref, cnt_scr):
    @pl.when(pl.program_id(0) == 0)
    def _():
        cnt_scr[...] = jnp.zeros_like(cnt_scr)

    h = _modulated_norm(x_ref[...], g_ref[...], sc_ref[0], sh_ref[0])
    half = h.shape[1] // 2
    bits = pltpu.bitcast(h.astype(BF16).astype(F32), jnp.uint32)
    h_ref[...] = (bits[:, half:] & jnp.uint32(0xFFFF0000)) | (bits[:, :half] >> 16)

    h_hi = h.astype(BF16)
    h_lo = (h - h_hi.astype(F32)).astype(BF16)
    wr = wr_ref[...]
    w_hi = wr.astype(BF16)
    w_lo = (wr - w_hi.astype(F32)).astype(BF16)
    logits = _dot(h_hi, w_hi) + _dot(h_hi, w_lo) + _dot(h_lo, w_hi) + br_ref[...]
    lane = lax.broadcasted_iota(I32, logits.shape, 1).astype(F32)
    lg = jnp.where(lane < N_EXPERTS, logits, -jnp.inf)
    m1 = jnp.max(lg, axis=-1, keepdims=True)
    i1 = jnp.min(jnp.where(lg == m1, lane, float(LANES)), axis=-1, keepdims=True)
    lg2 = jnp.where(lane == i1, -jnp.inf, lg)
    m2 = jnp.max(lg2, axis=-1, keepdims=True)
    i2 = jnp.min(jnp.where(lg2 == m2, lane, float(LANES)), axis=-1, keepdims=True)
    e2 = jnp.exp(m2 - m1)
    w_top = 1.0 / (1.0 + e2)

    chosen = jnp.where(lane == i1, 1.0, 0.0) + jnp.where(lane == i2, 1.0, 0.0)
    tm = chosen.shape[0]
    earlier = jnp.where(lax.broadcasted_iota(I32, (tm, tm), 1) < lax.broadcasted_iota(I32, (tm, tm), 0),
                        1.0, 0.0).astype(BF16)
    rank = _dot(earlier, chosen.astype(BF16)) + cnt_scr[0:1, :]
    rec = jnp.zeros_like(logits)
    for slot, val in ((R_E1, i1), (R_E2, i2), (R_RANK1, _lane_pick(rank, lane, i1)),
                      (R_RANK2, _lane_pick(rank, lane, i2)), (R_W1, w_top), (R_W2, e2 * w_top)):
        rec = jnp.where(lane == float(slot), val, rec)
    rec_ref[...] = rec
    cnt_scr[0:1, :] = cnt_scr[0:1, :] + jnp.sum(chosen, axis=0, keepdims=True)
    cnt_ref[...] = cnt_scr[...]


def _route(x2, g, mod, wr_pad, br_pad, s_len):
    t, d = x2.shape
    tm = MOE_TM
    per_b = s_len // tm
    return pl.pallas_call(
        _route_kernel,
        grid=(t // tm,),
        in_specs=[pl.BlockSpec((tm, d), lambda i: (i, 0)),
                  pl.BlockSpec((1, d), lambda i: (0, 0)),
                  pl.BlockSpec((1, 1, d), lambda i: (i // per_b, 0, 4)),
                  pl.BlockSpec((1, 1, d), lambda i: (i // per_b, 0, 3)),
                  pl.BlockSpec((d, LANES), lambda i: (0, 0)),
                  pl.BlockSpec((1, LANES), lambda i: (0, 0))],
        out_specs=[pl.BlockSpec((tm, d // 2), lambda i: (i, 0)),
                   pl.BlockSpec((tm, LANES), lambda i: (i, 0)),
                   pl.BlockSpec((8, LANES), lambda i: (0, 0))],
        out_shape=[jax.ShapeDtypeStruct((t, d // 2), jnp.uint32),
                   jax.ShapeDtypeStruct((t, LANES), F32),
                   jax.ShapeDtypeStruct((8, LANES), F32)],
        scratch_shapes=[pltpu.VMEM((8, LANES), F32)],
        compiler_params=_cparams("arbitrary"),
        name="moe_route",
    )(x2, g, mod, mod, wr_pad, br_pad)


def _row_copy(src_ref, src_row, dst_ref, dst_row, sem):
    return pltpu.make_async_copy(src_ref.at[pl.ds(src_row, 1)], dst_ref.at[pl.ds(dst_row, 1)], sem)


def _dispatch_kernel(pos_ref, h_ref, xs_in_ref, xs_ref, sem):
    del xs_in_ref
    tm = h_ref.shape[0]

    def start(t, carry):
        _row_copy(h_ref, t, xs_ref, pos_ref[0, 0, t], sem).start(priority=0)
        _row_copy(h_ref, t, xs_ref, pos_ref[0, 0, tm + t], sem).start(priority=1)
        return carry

    lax.fori_loop(0, tm, start, 0, unroll=4)
    for _ in range(2):
        pltpu.make_async_copy(h_ref, xs_ref.at[pl.ds(0, tm)], sem).wait()


def _dispatch(pos, h_packed, n_rows):
    t, w = h_packed.shape
    tm = MOE_TM
    xs0 = jnp.zeros((n_rows, w), h_packed.dtype)
    return pl.pallas_call(
        _dispatch_kernel,
        grid=(t // tm,),
        in_specs=[pl.BlockSpec((1, 1, 2 * tm), lambda i: (i, 0, 0), memory_space=pltpu.SMEM),
                  pl.BlockSpec((tm, w), lambda i: (i, 0)),
                  pl.BlockSpec(memory_space=pl.ANY)],
        out_specs=pl.BlockSpec(memory_space=pl.ANY),
        out_shape=jax.ShapeDtypeStruct((n_rows, w), h_packed.dtype),
        scratch_shapes=[pltpu.SemaphoreType.DMA(())],
        input_output_aliases={2: 0},
        compiler_params=_cparams("arbitrary"),
        name="moe_dispatch",
    )(pos, h_packed, xs0)


def _expert_kernel(te_ref, nu_ref, xs_ref, w1_ref, w3_ref, w2_ref, o_ref, h_scr, acc_scr):
    i = pl.program_id(0)
    f = pl.program_id(1)

    @pl.when(i < nu_ref[0])
    def _():
        @pl.when(f == 0)
        def _():
            word = xs_ref[...]
            lo = pltpu.bitcast(word << 16, F32)
            hi = pltpu.bitcast(word & jnp.uint32(0xFFFF0000), F32)
            h_scr[...] = jnp.concatenate([lo, hi], axis=1).astype(BF16)
            acc_scr[...] = jnp.zeros_like(acc_scr)

        acc_scr[...] += _swiglu_partial(h_scr[...], w1_ref[0], w3_ref[0], w2_ref[0])

        @pl.when(f == pl.num_programs(1) - 1)
        def _():
            o_ref[...] = acc_scr[...]

    @pl.when((i >= nu_ref[0]) & (f == pl.num_programs(1) - 1))
    def _():
        o_ref[...] = jnp.zeros_like(o_ref)


def _experts(tile_expert, n_used, xs, w1, w3, w2):
    n_rows, half = xs.shape
    d = 2 * half
    d_ff = w1.shape[2]
    tf = 896
    nf = d_ff // tf
    tile = lambda i, nu: jnp.minimum(i, nu[0] - 1)
    ff = lambda i, f, nu: jnp.where(i < nu[0], f, nf - 1)
    return pl.pallas_call(
        _expert_kernel,
        grid_spec=pltpu.PrefetchScalarGridSpec(
            num_scalar_prefetch=2,
            grid=(n_rows // MOE_ROWS, nf),
            in_specs=[pl.BlockSpec((MOE_ROWS, half), lambda i, f, te, nu: (tile(i, nu), 0)),
                      pl.BlockSpec((1, d, tf), lambda i, f, te, nu: (te[i], 0, ff(i, f, nu))),
                      pl.BlockSpec((1, d, tf), lambda i, f, te, nu: (te[i], 0, ff(i, f, nu))),
                      pl.BlockSpec((1, tf, d), lambda i, f, te, nu: (te[i], ff(i, f, nu), 0))],
            out_specs=pl.BlockSpec((MOE_ROWS, d), lambda i, f, te, nu: (i, 0)),
            scratch_shapes=[pltpu.VMEM((MOE_ROWS, d), BF16), pltpu.VMEM((MOE_ROWS, d), F32)]),
        out_shape=jax.ShapeDtypeStruct((n_rows, d), F32),
        compiler_params=_cparams("arbitrary", "arbitrary"),
        name="moe_experts",
    )(tile_expert, n_used, xs, w1, w3, w2)


def _combine_kernel(pos_ref, x_ref, gate_ref, rec_ref, fg_ref, ys_ref, o_ref, y_scr, sem, *, final):
    tm = x_ref.shape[0]

    def start(t, carry):
        _row_copy(ys_ref, pos_ref[0, 0, t], y_scr.at[0], t, sem).start(priority=0)
        _row_copy(ys_ref, pos_ref[0, 0, tm + t], y_scr.at[1], t, sem).start(priority=1)
        return carry

    lax.fori_loop(0, tm, start, 0, unroll=4)
    for slot in range(2):
        pltpu.make_async_copy(ys_ref.at[pl.ds(0, tm)], y_scr.at[slot], sem).wait()
    rec = rec_ref[...]
    lane = lax.broadcasted_iota(I32, rec.shape, 1)
    w_first = jnp.sum(jnp.where(lane == R_W1, rec, 0.0), axis=-1, keepdims=True)
    w_second = jnp.sum(jnp.where(lane == R_W2, rec, 0.0), axis=-1, keepdims=True)
    f = w_first * y_scr[0] + w_second * y_scr[1]
    o_ref[...] = _finish(x_ref[...], gate_ref[0], f, fg_ref, final)


def _combine(pos, x2, mod, rec, final_g, ys, s_len, final):
    t, d = x2.shape
    tm = MOE_TM
    per_b = s_len // tm
    kern = functools.partial(_combine_kernel, final=final)
    return pl.pallas_call(
        kern,
        grid=(t // tm,),
        in_specs=[pl.BlockSpec((1, 1, 2 * tm), lambda i: (i, 0, 0), memory_space=pltpu.SMEM),
                  pl.BlockSpec((tm, d), lambda i: (i, 0)),
                  pl.BlockSpec((1, 1, d), lambda i: (i // per_b, 0, 5)),
                  pl.BlockSpec((tm, LANES), lambda i: (i, 0)),
                  pl.BlockSpec((1, d), lambda i: (0, 0)),
                  pl.BlockSpec(memory_space=pl.ANY)],
        out_specs=pl.BlockSpec((tm, d), lambda i: (i, 0)),
        out_shape=jax.ShapeDtypeStruct((t, d), F32),
        scratch_shapes=[pltpu.VMEM((2, tm, d), F32), pltpu.SemaphoreType.DMA(())],
        compiler_params=_cparams("arbitrary"),
        name="moe_combine",
    )(pos, x2, mod, rec, final_g, ys)


def _moe(x2, g, mod, wr_pad, br_pad, w1, w3, w2, final_g, s_len, final):
    t, d = x2.shape
    n_e = w1.shape[0]
    h_packed, rec, cnt = _route(x2, g, mod, wr_pad, br_pad, s_len)

    n_tiles = 2 * t // MOE_ROWS + n_e
    counts = cnt[0, :n_e].astype(I32)
    padded = (counts + MOE_ROWS - 1) // MOE_ROWS * MOE_ROWS
    ends = jnp.cumsum(padded)
    starts = ends - padded
    e1, e2 = rec[:, R_E1].astype(I32), rec[:, R_E2].astype(I32)
    pos1 = starts[e1] + rec[:, R_RANK1].astype(I32)
    pos2 = starts[e2] + rec[:, R_RANK2].astype(I32)
    pos = jnp.concatenate([pos1.reshape(-1, 1, MOE_TM), pos2.reshape(-1, 1, MOE_TM)], axis=2)
    n_used = (ends[-1] // MOE_ROWS).reshape(1)
    tile_start = jnp.minimum(jnp.arange(n_tiles, dtype=I32), n_used[0] - 1) * MOE_ROWS
    tile_expert = jnp.sum(tile_start[:, None] >= ends[None, :], axis=1).astype(I32)

    xs = _dispatch(pos, h_packed, n_tiles * MOE_ROWS)
    ys = _experts(tile_expert, n_used, xs, w1, w3, w2)
    return _combine(pos, x2, mod, rec, final_g, ys, s_len, final)


def _pack_w_in(w):
    d = w.shape[0]
    sizes = (W_A, W_A, W_A, IDX_HEADS * IDX_DIM, IDX_DIM, IDX_HEADS,
             512, 512, 512, 512, 512, 512, 3 * d)
    offs = [0]
    for s in sizes:
        offs.append(offs[-1] + s)
    (qa, ka, va, qi, ki, wi, qb, kb, vb, qc, kc, vc, gl) = [
        w[:, offs[j]:offs[j + 1]] for j in range(len(sizes))]
    scale = HEAD_DIM ** -0.5
    pad_wi = jnp.zeros((d, LANES - IDX_HEADS), w.dtype)
    packed = jnp.concatenate(
        [gl, qa * scale, ka, va,
         qi * (IDX_DIM ** -0.5), ki, ki, wi * (IDX_HEADS ** -0.5), pad_wi,
         qb * scale, kb, vb, qc * scale, kc, vc], axis=1)
    assert packed.shape[1] == PACKED
    return packed.astype(BF16)


def kernel(x, c, w_ada, b_ada, norm1_g, norm2_g, w_in, w_br, w_out, rel_bias, lam_params,
           subln_g, ffn_w1, ffn_w3, ffn_w2, router_w, router_b, moe_w1, moe_w3, moe_w2, final_g):
    bsz, s_len, d = x.shape
    depth = w_ada.shape[0]
    assert s_len % 1024 == 0 and d == 1024 and OFF_G + 3 * d == OFF_QA

    c_pad = jnp.concatenate([c, jnp.zeros((8 - bsz % 8 if bsz % 8 else 0, d), c.dtype)], axis=0)
    mod_all = _ada(c_pad, w_ada, b_ada)
    dsa_tiles, diff_tiles = _bias_tiles(rel_bias)
    fg = final_g.reshape(1, d)

    x2 = x.reshape(bsz * s_len, d)
    for l in range(depth):
        mod = mod_all[l, :bsz].reshape(bsz, 1, 6 * d)
        proj = _inproj(x2, norm1_g[l].reshape(1, d), mod, _pack_w_in(w_in[l]), s_len)
        proj3 = proj.reshape(bsz, s_len, PACKED)
        lam_init = 0.8 - 0.6 * math.exp(-0.3 * l)
        o_a = _dsa(proj3, dsa_tiles, bsz, s_len)
        o_b = _diff(proj3, diff_tiles, lam_params, subln_g, l, bsz, s_len, lam_init)
        o_c = _stick(proj3, bsz, s_len)
        x2 = _merge(o_a.reshape(-1, o_a.shape[-1]), o_b.reshape(-1, o_b.shape[-1]),
                    o_c.reshape(-1, o_c.shape[-1]), proj, x2, mod,
                    w_br[l].astype(BF16), w_out[l].astype(BF16), s_len)
        g2 = norm2_g[l].reshape(1, d)
        final = l == depth - 1
        j = l // 2
        if l % 2 == 0:
            x2 = _ffn(x2, g2, mod, ffn_w1[j].astype(BF16), ffn_w3[j].astype(BF16),
                      ffn_w2[j].astype(BF16), fg, s_len, final)
        else:
            wr_pad = jnp.pad(router_w[j], ((0, 0), (0, LANES - N_EXPERTS)))
            br_pad = jnp.pad(router_b[j], (0, LANES - N_EXPERTS)).reshape(1, LANES)
            x2 = _moe(x2, g2, mod, wr_pad, br_pad, moe_w1[j].astype(BF16), moe_w3[j].astype(BF16),
                      moe_w2[j].astype(BF16), fg, s_len, final)
    return x2.reshape(bsz, s_len, d)
```

```python
import functools
import math

import jax
import jax.numpy as jnp
from jax import lax
from jax.experimental import pallas as pl
from jax.experimental.pallas import tpu as pltpu

F32 = jnp.float32
BF16 = jnp.bfloat16
I32 = jnp.int32
I16 = jnp.int16

LANES = 128
VMEM_LIMIT_BYTES = 56 * 1024 * 1024

CHUNK = 64
A_HEADS = 8
IDX_HEADS = 4
IDX_DIM = 64
TOPK_MAX = 256
B_HEADS = 4
C_HEADS = 8
HEAD_DIM = 64
REL_BUCKETS = 32
FAR_BUCKET = REL_BUCKETS // 2 - 1
N_EXPERTS = 8
EPS = 1e-6
SUBLN_EPS = 1e-5
NEG = -1e30
LOG2E = 1.4426950408889634
INT_MIN = -(2 ** 31)

KEY_BLOCK = 256
Q_TILE = 256
GROUP = 4
BLOCK_GROUPS = (8, 4, 2, 1)

W_A = A_HEADS * HEAD_DIM
OFF_G = 0
OFF_QA, OFF_KA, OFF_VA = 3072, 3584, 4096
OFF_QI, OFF_KI, OFF_WI = 4608, 4864, 4992
OFF_QB, OFF_KB, OFF_VB = 5120, 5632, 6144
OFF_QC, OFF_KC, OFF_VC = 6656, 7168, 7680
PACKED = 8192

LOG_BUCKET_STEPS = (12, 16, 23, 32, 46, 64, 91)


def _nt_dot(a, b):
    return lax.dot_general(a, b, (((1,), (1,)), ((), ())), preferred_element_type=F32)


def _dot(a, b):
    return jnp.dot(a, b, preferred_element_type=F32)


def _cparams(*sem):
    return pltpu.CompilerParams(dimension_semantics=sem, vmem_limit_bytes=VMEM_LIMIT_BYTES)


def _split_heads(x):
    lane = lax.broadcasted_iota(I32, x.shape, 1)
    keep_a = jnp.where(lane < HEAD_DIM, 1.0, 0.0).astype(x.dtype)
    keep_b = jnp.where(lane < HEAD_DIM, 0.0, 1.0).astype(x.dtype)
    return jnp.concatenate([x * keep_a, x * keep_b], axis=0)


def _merge_heads(o, m):
    lane = lax.broadcasted_iota(I32, (m, LANES), 1)
    return jnp.where(lane < HEAD_DIM, o[:m], o[m:])


def _wide(x):
    return jnp.concatenate([x, x], axis=1)


def _tall(x):
    return jnp.concatenate([x, x], axis=0)


def _key_rows(kb):
    return pl.ds(pl.multiple_of(kb * KEY_BLOCK, KEY_BLOCK), KEY_BLOCK)


def _sortable(x):
    bits = pltpu.bitcast(x, I32)
    return bits ^ ((bits >> 31) & 0x7FFFFFFF)


def _chunk_causal():
    r = lax.broadcasted_iota(I32, (Q_TILE, KEY_BLOCK), 0)
    c = lax.broadcasted_iota(I32, (Q_TILE, KEY_BLOCK), 1)
    return (c // CHUNK) <= (r // CHUNK)


def _ada_kernel(c_ref, w_ref, b_ref, o_ref):
    c = c_ref[...]
    a = c * jax.nn.sigmoid(c)
    o_ref[0] = jnp.dot(a, w_ref[0], preferred_element_type=F32,
                       precision=lax.Precision.HIGHEST) + b_ref[0]


def _ada(c_pad, w_ada, b_ada):
    depth, d, n = w_ada.shape
    tn = 1024
    return pl.pallas_call(
        _ada_kernel,
        grid=(depth, n // tn),
        in_specs=[pl.BlockSpec(c_pad.shape, lambda l, j: (0, 0)),
                  pl.BlockSpec((1, d, tn), lambda l, j: (l, 0, j)),
                  pl.BlockSpec((1, 1, tn), lambda l, j: (l, 0, j))],
        out_specs=pl.BlockSpec((1, c_pad.shape[0], tn), lambda l, j: (l, 0, j)),
        out_shape=jax.ShapeDtypeStruct((depth, c_pad.shape[0], n), F32),
        compiler_params=_cparams("parallel", "parallel"),
        name="ada",
    )(c_pad, w_ada, b_ada.reshape(depth, 1, n))


def _rel_bias_tile(tab_ref, head, d0, n_heads_total):
    r = lax.broadcasted_iota(I32, (Q_TILE, KEY_BLOCK), 0)
    c = lax.broadcasted_iota(I32, (Q_TILE, KEY_BLOCK), 1)
    d = c - r + d0
    n = jnp.abs(d)
    large = jnp.full(d.shape, REL_BUCKETS // 4, I32)
    for step in LOG_BUCKET_STEPS:
        large = large + jnp.where(n >= step, 1, 0)
    bucket = jnp.where(d > 0, REL_BUCKETS // 2, 0) + jnp.where(n < REL_BUCKETS // 4, n, large)
    out = jnp.zeros(d.shape, F32)
    for b in range(REL_BUCKETS):
        out = jnp.where(bucket == b, tab_ref[b * n_heads_total + head], out)
    return out - tab_ref[FAR_BUCKET * n_heads_total + head]


def _bias_kernel(tab_ref, dsa_ref, diff_ref):
    p = pl.program_id(0)
    n_heads = A_HEADS + B_HEADS
    dsa_ref[0, 0] = jnp.zeros(dsa_ref.shape[2:], F32)
    diff_ref[0, 0] = jnp.zeros(diff_ref.shape[2:], F32)
    for t, d0 in ((1, -KEY_BLOCK), (2, 0)):
        dsa_ref[0, t, 0:Q_TILE, :] = _rel_bias_tile(tab_ref, 2 * p, d0, n_heads)
        dsa_ref[0, t, Q_TILE:2 * Q_TILE, :] = _rel_bias_tile(tab_ref, 2 * p + 1, d0, n_heads)
        tile = _rel_bias_tile(tab_ref, A_HEADS + p, d0, n_heads)
        if t == 2:
            tile = jnp.where(_chunk_causal(), tile, NEG)
        diff_ref[0, t] = tile


def _bias_tiles(rel_bias):
    tab = rel_bias.reshape(-1)
    return pl.pallas_call(
        _bias_kernel,
        grid=(4,),
        in_specs=[pl.BlockSpec(memory_space=pltpu.SMEM)],
        out_specs=[pl.BlockSpec((1, 3, 2 * Q_TILE, KEY_BLOCK), lambda p: (p, 0, 0, 0)),
                   pl.BlockSpec((1, 3, Q_TILE, KEY_BLOCK), lambda p: (p, 0, 0, 0))],
        out_shape=[jax.ShapeDtypeStruct((A_HEADS // 2, 3, 2 * Q_TILE, KEY_BLOCK), F32),
                   jax.ShapeDtypeStruct((B_HEADS, 3, Q_TILE, KEY_BLOCK), F32)],
        compiler_params=_cparams("parallel"),
        name="rel_bias_tiles",
    )(tab)


def _modulated_norm(x, g, sc, sh):
    y = x * lax.rsqrt(jnp.mean(x * x, axis=-1, keepdims=True) + EPS)
    return y * g * (1.0 + sc) + sh


def _inproj_kernel(x_ref, g_ref, sc_ref, sh_ref, w_ref, o_ref, h_scr):
    @pl.when(pl.program_id(1) == 0)
    def _():
        h_scr[...] = _modulated_norm(x_ref[...], g_ref[...], sc_ref[0], sh_ref[0]).astype(BF16)

    o_ref[...] = _dot(h_scr[...], w_ref[...]).astype(o_ref.dtype)


def _inproj(x2, g, mod, w_packed, s_len):
    t, d = x2.shape
    n = w_packed.shape[1]
    tm, tn = 1024, 1024
    per_b = s_len // tm
    return pl.pallas_call(
        _inproj_kernel,
        grid=(t // tm, n // tn),
        in_specs=[pl.BlockSpec((tm, d), lambda i, j: (i, 0)),
                  pl.BlockSpec((1, d), lambda i, j: (0, 0)),
                  pl.BlockSpec((1, 1, d), lambda i, j: (i // per_b, 0, 1)),
                  pl.BlockSpec((1, 1, d), lambda i, j: (i // per_b, 0, 0)),
                  pl.BlockSpec((d, tn), lambda i, j: (0, j))],
        out_specs=pl.BlockSpec((tm, tn), lambda i, j: (i, j)),
        out_shape=jax.ShapeDtypeStruct((t, n), BF16),
        scratch_shapes=[pltpu.VMEM((tm, d), BF16)],
        compiler_params=_cparams("parallel", "arbitrary"),
        name="inproj",
    )(x2, g, mod, mod, w_packed)


def _for_blocks(lo, hi, fn, sizes=BLOCK_GROUPS):
    pos = lo
    for size in sizes:
        trips = jnp.maximum(hi - pos, 0) // size

        def body(i, carry, pos=pos, size=size):
            fn([pos + i * size + u for u in range(size)])
            return carry

        lax.fori_loop(0, trips, body, 0)
        pos = pos + trips * size


def _fold_blocks(n, fn, acc):
    pos = 0
    for size in BLOCK_GROUPS:
        trips = (n - pos) // size

        def body(i, acc, pos=pos, size=size):
            for u in range(size):
                acc = fn(pos + i * size + u, acc)
            return acc

        acc = lax.fori_loop(0, trips, body, acc)
        pos = pos + trips * size
    return acc


def _softmax_pv(qs, k_at, v_at, qb, add, s_scr, mx_scr, l_scr, acc_scr):
    def pass1(blocks):
        mx = None
        for kb in blocks:
            s = add(kb, _nt_dot(qs, k_at(kb)), jnp.clip(kb - qb + 2, 0, 2))
            s = s * LOG2E
            s_scr[kb] = s
            fold = jnp.maximum(s[:, :LANES], s[:, LANES:])
            mx = fold if mx is None else jnp.maximum(mx, fold)
        mx_scr[...] = jnp.maximum(mx_scr[...], mx)

    mx_scr[...] = jnp.full(mx_scr.shape, NEG, F32)
    _for_blocks(0, qb + 1, pass1)

    m = jnp.max(mx_scr[...], axis=-1, keepdims=True)
    mx_scr[...] = jnp.broadcast_to(m, mx_scr.shape)
    l_scr[...] = jnp.zeros_like(l_scr)
    acc_scr[...] = jnp.zeros_like(acc_scr)

    def pass2(kbs):
        mw = _wide(mx_scr[...])
        l_add = acc_add = None
        for kb in kbs:
            p = jnp.exp2(s_scr[kb] - mw)
            fold = p[:, :LANES] + p[:, LANES:]
            pv = _dot(p.astype(BF16), v_at(kb))
            l_add = fold if l_add is None else l_add + fold
            acc_add = pv if acc_add is None else acc_add + pv
        l_scr[...] += l_add
        acc_scr[...] += acc_add

    _for_blocks(0, qb + 1, pass2)
    return acc_scr[...] / jnp.sum(l_scr[...], axis=-1, keepdims=True)


def _softmax_scratch(s_len):
    return [pltpu.VMEM((s_len // KEY_BLOCK, 2 * Q_TILE, KEY_BLOCK), F32),
            pltpu.VMEM((2 * Q_TILE, LANES), F32),
            pltpu.VMEM((2 * Q_TILE, LANES), F32),
            pltpu.VMEM((2 * Q_TILE, LANES), F32)]


def _dsa_kernel(qa_ref, qi_ref, wi_ref, ka_ref, va_ref, ki_ref, bias_ref, o_ref,
                key_scr, keyt_scr, hi_scr, lo_scr, neg_scr, w_scr, last_scr, s_scr, mx_scr, l_scr, acc_scr,
                *, topk, idx_bits, s_len):
    qb = pl.program_id(1)

    @pl.when(pl.program_id(2) == 0)
    def _():
        _dsa_select(qb, qi_ref, wi_ref, ki_ref, key_scr, keyt_scr, hi_scr, lo_scr, neg_scr, w_scr, last_scr,
                    topk=topk, idx_bits=idx_bits, s_len=s_len)

    pr = pl.program_id(2)
    lanes = pl.ds(pl.multiple_of(pr * LANES, LANES), LANES)

    def add(kb, s, t):
        return s + _tall(neg_scr[kb]) + bias_ref[pr, t]

    o = _softmax_pv(_split_heads(qa_ref[0]),
                    lambda kb: ka_ref[0, _key_rows(kb), lanes],
                    lambda kb: va_ref[0, _key_rows(kb), lanes],
                    qb, add, s_scr, mx_scr, l_scr, acc_scr)
    o_ref[0] = _merge_heads(o, Q_TILE).astype(o_ref.dtype)


def _dsa_select(qb, qi_ref, wi_ref, ki_ref, key_scr, keyt_scr, hi_scr, lo_scr, neg_scr, w_scr, last_scr,
                *, topk, idx_bits, s_len):
    nkb = qb + 1
    kf = float(topk)
    valid_diag = _chunk_causal()

    def cols(kb):
        return kb * KEY_BLOCK + lax.broadcasted_iota(I32, (Q_TILE, KEY_BLOCK), 1)

    qi = qi_ref[0]
    q_heads = [_split_heads(qi[:, pr * LANES:(pr + 1) * LANES]) for pr in range(IDX_HEADS // 2)]
    wi = wi_ref[0].astype(F32)
    for h in range(IDX_HEADS):
        w_scr[h] = jnp.broadcast_to(wi[:, h:h + 1], (Q_TILE, LANES))

    def score_block(kb, valid):
        kk = ki_ref[0, _key_rows(kb), :]
        sc = jnp.zeros((Q_TILE, KEY_BLOCK), F32)
        for pr in range(IDX_HEADS // 2):
            d = _nt_dot(q_heads[pr], kk)
            sc = sc + _wide(w_scr[2 * pr]) * jnp.maximum(d[:Q_TILE], 0.0)
            sc = sc + _wide(w_scr[2 * pr + 1]) * jnp.maximum(d[Q_TILE:], 0.0)
        sc = jnp.where(sc == 0.0, 0.0, sc)
        if valid is not None:
            sc = jnp.where(valid, sc, -jnp.inf)
        key_scr[kb] = _sortable(sc)
        key_t = _sortable(sc.T)
        keyt_scr[kb] = key_t
        hi_scr[kb] = (key_t >> 16).astype(I16)
        lo_scr[kb] = ((key_t & 0xFFFF) - 2 ** 15).astype(I16)

    _for_blocks(0, qb, lambda kbs: [score_block(kb, None) for kb in kbs], sizes=(4, 2, 1))
    score_block(qb, valid_diag)

    def per_query(x):
        return jnp.broadcast_to(x[None], (KEY_BLOCK // 8, 8, Q_TILE)).reshape(KEY_BLOCK, Q_TILE)

    def key_index(kb):
        return kb * KEY_BLOCK + lax.broadcasted_iota(I32, (KEY_BLOCK, Q_TILE), 0)

    def count(pred):
        def one(kb, acc):
            hit = jnp.where(pred(keyt_scr[kb], kb), 1.0, 0.0)
            return acc + jnp.sum(hit.reshape(KEY_BLOCK // 8, 8, Q_TILE), axis=0)

        acc = _fold_blocks(nkb, one, jnp.zeros((8, Q_TILE), F32))
        return jnp.broadcast_to(jnp.sum(acc, axis=0, keepdims=True), (8, Q_TILE))

    def per_query16(x):
        x16 = x.astype(I16)
        return jnp.broadcast_to(x16[None], (KEY_BLOCK // 16, 16, Q_TILE)).reshape(KEY_BLOCK, Q_TILE)

    def count16(ref, pred):
        def one(kb, acc):
            hit = jnp.where(pred(ref[kb]), jnp.int16(1), jnp.int16(0))
            parts = [hit[g:g + 16] for g in range(0, KEY_BLOCK, 16)]
            while len(parts) > 1:
                parts = [parts[i] + parts[i + 1] for i in range(0, len(parts), 2)]
            return acc + parts[0]

        acc = _fold_blocks(nkb, one, jnp.zeros((16, Q_TILE), I16))
        return jnp.broadcast_to(jnp.sum(acc.astype(F32), axis=0, keepdims=True), (16, Q_TILE))

    hi_thr = jnp.where(count16(hi_scr, lambda h: h >= 0) >= kf,
                       jnp.zeros((16, Q_TILE), I32), jnp.full((16, Q_TILE), -(2 ** 15), I32))

    def hi_body(j, t):
        cand = t | lax.shift_left(jnp.int32(1), 14 - j)
        cw = per_query16(cand)
        return jnp.where(count16(hi_scr, lambda h: h >= cw) >= kf, cand, t)

    hi_thr = lax.fori_loop(0, 15, hi_body, hi_thr)
    hi_w = per_query16(hi_thr)
    above = count16(hi_scr, lambda h: h > hi_w)

    def keep_ties(kb, carry):
        lo_scr[kb] = jnp.where(hi_scr[kb] == hi_w, lo_scr[kb], jnp.int16(-(2 ** 15)))
        return carry

    lax.fori_loop(0, nkb, keep_ties, 0)

    def lo_body(j, t):
        cand = t | lax.shift_left(jnp.int32(1), 15 - j)
        cw = per_query16(cand - 2 ** 15)
        return jnp.where(above + count16(lo_scr, lambda lo: lo >= cw) >= kf, cand, t)

    lo_thr = lax.fori_loop(0, 16, lo_body, jnp.zeros((16, Q_TILE), I32))
    thr = (lax.shift_left(hi_thr, 16) | lo_thr)[0:8]
    thr_t = per_query(thr)

    need = kf - count(lambda key, kb: key > thr_t)
    ties = count(lambda key, kb: key == thr_t)
    last_scr[...] = jnp.full(last_scr.shape, s_len, I32)

    @pl.when(jnp.max(ties - need) > 0.0)
    def _():
        def tie_body(j, last):
            cand = last | lax.shift_left(jnp.int32(1), idx_bits - 1 - j)
            cw = per_query(cand)
            taken = count(lambda key, kb: (key == thr_t) & (key_index(kb) < cw))
            return jnp.where(taken < need, cand, last)

        last_scr[...] = lax.fori_loop(0, idx_bits, tie_body, jnp.zeros((8, Q_TILE), I32))

    def per_row(x):
        return _wide(jnp.broadcast_to(x[0:1], (LANES, Q_TILE)).T)

    tw = per_row(thr)
    lastw = per_row(last_scr[...])

    def mask_block(kb, valid):
        key = key_scr[kb]
        tie_neg = jnp.where(key == tw, jnp.where(cols(kb) <= lastw, 0.0, NEG), NEG)
        neg = jnp.where(key > tw, 0.0, tie_neg)
        if valid is not None:
            neg = jnp.where(valid, neg, NEG)
        neg_scr[kb] = neg

    def mask_body(kb, carry):
        mask_block(kb, None)
        return carry

    lax.fori_loop(0, qb, mask_body, 0)
    mask_block(qb, valid_diag)


def _dsa(proj, bias_tiles, bsz, s_len):
    topk = min(TOPK_MAX, s_len // 4)
    idx_bits = max(1, (s_len - 1).bit_length())
    n_blocks = s_len // KEY_BLOCK
    assert 2 * n_blocks <= 256
    kern = functools.partial(_dsa_kernel, topk=topk, idx_bits=idx_bits, s_len=s_len)
    lane_blk = lambda off: off // LANES
    return pl.pallas_call(
        kern,
        grid=(bsz, s_len // Q_TILE, A_HEADS // 2),
        in_specs=[pl.BlockSpec((1, Q_TILE, LANES), lambda b, i, p: (b, i, lane_blk(OFF_QA) + p)),
                  pl.BlockSpec((1, Q_TILE, 256), lambda b, i, p: (b, i, OFF_QI // 256)),
                  pl.BlockSpec((1, Q_TILE, LANES), lambda b, i, p: (b, i, lane_blk(OFF_WI))),
                  pl.BlockSpec((1, s_len, W_A), lambda b, i, p: (b, 0, OFF_KA // W_A),
                               pipeline_mode=pl.Buffered(1)),
                  pl.BlockSpec((1, s_len, W_A), lambda b, i, p: (b, 0, OFF_VA // W_A),
                               pipeline_mode=pl.Buffered(1)),
                  pl.BlockSpec((1, s_len, LANES), lambda b, i, p: (b, 0, lane_blk(OFF_KI))),
                  pl.BlockSpec(bias_tiles.shape, lambda b, i, p: (0, 0, 0, 0),
                               pipeline_mode=pl.Buffered(1))],
        out_specs=pl.BlockSpec((1, Q_TILE, LANES), lambda b, i, p: (b, i, p)),
        out_shape=jax.ShapeDtypeStruct((bsz, s_len, W_A), BF16),
        scratch_shapes=[pltpu.VMEM((n_blocks, Q_TILE, KEY_BLOCK), I32),
                        pltpu.VMEM((n_blocks, KEY_BLOCK, Q_TILE), I32),
                        pltpu.VMEM((n_blocks, KEY_BLOCK, Q_TILE), I16),
                        pltpu.VMEM((n_blocks, KEY_BLOCK, Q_TILE), I16),
                        pltpu.VMEM((n_blocks, Q_TILE, KEY_BLOCK), F32),
                        pltpu.VMEM((IDX_HEADS, Q_TILE, LANES), F32),
                        pltpu.VMEM((8, Q_TILE), I32)] + _softmax_scratch(s_len),
        compiler_params=_cparams("parallel", "arbitrary", "arbitrary"),
        name="dsa",
    )(proj, proj, proj, proj, proj, proj, bias_tiles)


def _diff_kernel(q_ref, k_ref, v_ref, bias_ref, lp_ref, g_ref, o_ref,
                 s_scr, mx_scr, l_scr, acc_scr, *, lam_init):
    qb = pl.program_id(2)
    qs = _split_heads(q_ref[0])

    def add(kb, s, t):
        return s + _tall(bias_ref[0, t])

    o = _softmax_pv(qs, lambda kb: k_ref[0, _key_rows(kb), :], lambda kb: v_ref[0, _key_rows(kb), :],
                    qb, add, s_scr, mx_scr, l_scr, acc_scr)
    lp = lp_ref[0]
    lam = (jnp.exp(jnp.sum(lp[0:1] * lp[1:2], axis=-1, keepdims=True))
           - jnp.exp(jnp.sum(lp[2:3] * lp[3:4], axis=-1, keepdims=True)) + lam_init)
    o = o[:Q_TILE] - lam * o[Q_TILE:]
    y = o * lax.rsqrt(jnp.mean(o * o, axis=-1, keepdims=True) + SUBLN_EPS)
    o_ref[0] = (y * g_ref[0] * (1.0 - lam_init)).astype(o_ref.dtype)


def _diff(proj, bias_tiles, lam_params, subln_g, layer, bsz, s_len, lam_init):
    kern = functools.partial(_diff_kernel, lam_init=lam_init)
    return pl.pallas_call(
        kern,
        grid=(bsz, B_HEADS, s_len // Q_TILE),
        in_specs=[pl.BlockSpec((1, Q_TILE, LANES), lambda b, h, i: (b, i, OFF_QB // LANES + h)),
                  pl.BlockSpec((1, s_len, LANES), lambda b, h, i: (b, 0, OFF_KB // LANES + h)),
                  pl.BlockSpec((1, s_len, LANES), lambda b, h, i: (b, 0, OFF_VB // LANES + h)),
                  pl.BlockSpec((1, 3, Q_TILE, KEY_BLOCK), lambda b, h, i: (h, 0, 0, 0)),
                  pl.BlockSpec((1, 4, HEAD_DIM), lambda b, h, i: (layer, 0, 0)),
                  pl.BlockSpec((1, 1, 2 * HEAD_DIM), lambda b, h, i: (layer, 0, 0))],
        out_specs=pl.BlockSpec((1, Q_TILE, LANES), lambda b, h, i: (b, i, h)),
        out_shape=jax.ShapeDtypeStruct((bsz, s_len, B_HEADS * 2 * HEAD_DIM), BF16),
        scratch_shapes=_softmax_scratch(s_len),
        compiler_params=_cparams("parallel", "parallel", "arbitrary"),
        name="diff_attn",
    )(proj, proj, proj, bias_tiles, lam_params, subln_g.reshape(-1, 1, 2 * HEAD_DIM))


def _stick_kernel(q_ref, k_ref, v_ref, o_ref, later_scr, run_scr, acc_scr):
    qb = pl.program_id(2)
    qs = _split_heads(q_ref[0])
    jj = lax.broadcasted_iota(I32, (2 * KEY_BLOCK, KEY_BLOCK), 0) & (KEY_BLOCK - 1)
    ss = lax.broadcasted_iota(I32, (2 * KEY_BLOCK, KEY_BLOCK), 1)
    later_scr[...] = jnp.where(jj > ss, 1.0, 0.0).astype(BF16)
    run_scr[...] = jnp.zeros_like(run_scr)
    acc_scr[...] = jnp.zeros_like(acc_scr)

    def sweep(kbs, diagonal):
        parts = []
        for j, kb in enumerate(kbs):
            z = _nt_dot(qs, k_ref[0, _key_rows(kb), :])
            soft = jnp.log(1.0 + jnp.exp2(jnp.abs(z) * (-LOG2E)))
            log_beta = jnp.minimum(z, 0.0) - soft
            log_1mb = log_beta - z
            strict = None
            if diagonal and j == 0:
                t = lax.broadcasted_iota(I32, z.shape, 0) & (Q_TILE - 1)
                strict = lax.broadcasted_iota(I32, z.shape, 1) < t
                log_1mb = jnp.where(strict, log_1mb, 0.0)
            hi = log_1mb.astype(BF16)
            lo = (log_1mb - hi.astype(F32)).astype(BF16)
            between = _dot(jnp.concatenate([hi, lo], axis=1), later_scr[...])
            total = between[:, 0:1] + log_1mb[:, 0:1]
            parts.append((kb, log_beta + between, total, strict))
        run = run_scr[...]
        acc = None
        for kb, logit, total, strict in parts:
            w = jnp.exp(logit + _wide(run))
            if strict is not None:
                w = jnp.where(strict, w, 0.0)
            pv = _dot(w.astype(BF16), v_ref[0, _key_rows(kb), :])
            acc = pv if acc is None else acc + pv
            run = run + jnp.broadcast_to(total, run.shape)
        acc_scr[...] += acc
        run_scr[...] = run

    first = jnp.where(qb + 1 >= GROUP, GROUP, jnp.where(qb + 1 >= 2, 2, 1))
    for size in (GROUP, 2, 1):
        @pl.when(first == size)
        def _(size=size):
            sweep([qb - u for u in range(size)], True)
    top = qb - first
    for size in BLOCK_GROUPS:
        trips = (top + 1) // size

        def body(i, carry, top=top, size=size):
            sweep([top - i * size - u for u in range(size)], False)
            return carry

        lax.fori_loop(0, trips, body, 0)
        top = top - trips * size
    o_ref[0] = _merge_heads(acc_scr[...], Q_TILE).astype(o_ref.dtype)


def _stick(proj, bsz, s_len):
    return pl.pallas_call(
        _stick_kernel,
        grid=(bsz, C_HEADS // 2, s_len // Q_TILE),
        in_specs=[pl.BlockSpec((1, Q_TILE, LANES), lambda b, h, i: (b, i, OFF_QC // LANES + h)),
                  pl.BlockSpec((1, s_len, LANES), lambda b, h, i: (b, 0, OFF_KC // LANES + h)),
                  pl.BlockSpec((1, s_len, LANES), lambda b, h, i: (b, 0, OFF_VC // LANES + h))],
        out_specs=pl.BlockSpec((1, Q_TILE, LANES), lambda b, h, i: (b, i, h)),
        out_shape=jax.ShapeDtypeStruct((bsz, s_len, C_HEADS * HEAD_DIM), BF16),
        scratch_shapes=[pltpu.VMEM((2 * KEY_BLOCK, KEY_BLOCK), BF16),
                        pltpu.VMEM((2 * Q_TILE, LANES), F32),
                        pltpu.VMEM((2 * Q_TILE, LANES), F32)],
        compiler_params=_cparams("parallel", "parallel", "arbitrary"),
        name="stick_attn",
    )(proj, proj, proj)


def _merge_kernel(oa_ref, ob_ref, oc_ref, gate_ref, x_ref, g1_ref, wbr_ref, wout_ref, o_ref):
    d = x_ref.shape[-1]
    merged = jnp.zeros(x_ref.shape, F32)
    for j, o_br in enumerate((oa_ref, ob_ref, oc_ref)):
        w = o_br.shape[-1]
        gate = jax.nn.sigmoid(gate_ref[:, j * d:(j + 1) * d].astype(F32))
        merged = merged + gate * _dot(o_br[...], wbr_ref[j * w:(j + 1) * w, :])
    o_ref[...] = x_ref[...] + g1_ref[0] * _dot(merged.astype(BF16), wout_ref[...])


def _merge(o_a, o_b, o_c, proj, x2, mod, w_br, w_out, s_len):
    t, d = x2.shape
    tm = 512
    per_b = s_len // tm
    w = o_a.shape[-1]
    return pl.pallas_call(
        _merge_kernel,
        grid=(t // tm,),
        in_specs=[pl.BlockSpec((tm, w), lambda i: (i, 0)),
                  pl.BlockSpec((tm, w), lambda i: (i, 0)),
                  pl.BlockSpec((tm, w), lambda i: (i, 0)),
                  pl.BlockSpec((tm, 3 * d), lambda i: (i, OFF_G // (3 * d))),
                  pl.BlockSpec((tm, d), lambda i: (i, 0)),
                  pl.BlockSpec((1, 1, d), lambda i: (i // per_b, 0, 2)),
                  pl.BlockSpec(w_br.shape, lambda i: (0, 0)),
                  pl.BlockSpec(w_out.shape, lambda i: (0, 0))],
        out_specs=pl.BlockSpec((tm, d), lambda i: (i, 0)),
        out_shape=jax.ShapeDtypeStruct((t, d), F32),
        compiler_params=_cparams("parallel"),
        name="merge",
    )(o_a, o_b, o_c, proj, x2, mod, w_br, w_out)


def _finish(x, gate, acc, fg_ref, final):
    y = x + gate * acc
    if final:
        y = y * lax.rsqrt(jnp.mean(y * y, axis=-1, keepdims=True) + EPS) * fg_ref[...]
    return y


def _swiglu_partial(h, w1, w3, w2):
    a = _dot(h, w1)
    act = (a * jax.nn.sigmoid(a)) * _dot(h, w3)
    return _dot(act.astype(BF16), w2)


def _ffn_kernel(x_ref, g_ref, sc_ref, sh_ref, gate_ref, w1_ref, w3_ref, w2_ref, fg_ref, o_ref,
                h_scr, acc_scr, *, final):
    f = pl.program_id(1)

    @pl.when(f == 0)
    def _():
        h_scr[...] = _modulated_norm(x_ref[...], g_ref[...], sc_ref[0], sh_ref[0]).astype(BF16)
        acc_scr[...] = jnp.zeros_like(acc_scr)

    acc_scr[...] += _swiglu_partial(h_scr[...], w1_ref[...], w3_ref[...], w2_ref[...])

    @pl.when(f == pl.num_programs(1) - 1)
    def _():
        o_ref[...] = _finish(x_ref[...], gate_ref[0], acc_scr[...], fg_ref, final)


def _ffn(x2, g, mod, w1, w3, w2, final_g, s_len, final):
    t, d = x2.shape
    d_ff = w1.shape[1]
    tm, tf = 1024, 256
    per_b = s_len // tm
    kern = functools.partial(_ffn_kernel, final=final)
    return pl.pallas_call(
        kern,
        grid=(t // tm, d_ff // tf),
        in_specs=[pl.BlockSpec((tm, d), lambda i, f: (i, 0)),
                  pl.BlockSpec((1, d), lambda i, f: (0, 0)),
                  pl.BlockSpec((1, 1, d), lambda i, f: (i // per_b, 0, 4)),
                  pl.BlockSpec((1, 1, d), lambda i, f: (i // per_b, 0, 3)),
                  pl.BlockSpec((1, 1, d), lambda i, f: (i // per_b, 0, 5)),
                  pl.BlockSpec((d, tf), lambda i, f: (0, f)),
                  pl.BlockSpec((d, tf), lambda i, f: (0, f)),
                  pl.BlockSpec((tf, d), lambda i, f: (f, 0)),
                  pl.BlockSpec((1, d), lambda i, f: (0, 0))],
        out_specs=pl.BlockSpec((tm, d), lambda i, f: (i, 0)),
        out_shape=jax.ShapeDtypeStruct((t, d), F32),
        scratch_shapes=[pltpu.VMEM((tm, d), BF16), pltpu.VMEM((tm, d), F32)],
        compiler_params=_cparams("parallel", "arbitrary"),
        name="ffn",
    )(x2, g, mod, mod, mod, w1, w3, w2, final_g)


MOE_TM = 512
MOE_ROWS = 512
R_E1, R_E2, R_RANK1, R_RANK2, R_W1, R_W2 = 0, 1, 2, 3, 4, 5


def _lane_pick(tile, lane, idx):
    return jnp.sum(jnp.where(lane == idx, tile, 0.0), axis=-1, keepdims=True)


def _route_kernel(x_ref, g_ref, sc_ref, sh_ref, wr_ref, br_ref, h_ref, rec_ref, cnt_ref, cnt_scr):
    @pl.when(pl.program_id(0) == 0)
    def _():
        cnt_scr[...] = jnp.zeros_like(cnt_scr)

    h = _modulated_norm(x_ref[...], g_ref[...], sc_ref[0], sh_ref[0])
    half = h.shape[1] // 2
    bits = pltpu.bitcast(h.astype(BF16).astype(F32), jnp.uint32)
    h_ref[...] = (bits[:, half:] & jnp.uint32(0xFFFF0000)) | (bits[:, :half] >> 16)

    h_hi = h.astype(BF16)
    h_lo = (h - h_hi.astype(F32)).astype(BF16)
    wr = wr_ref[...]
    w_hi = wr.astype(BF16)
    w_lo = (wr - w_hi.astype(F32)).astype(BF16)
    logits = _dot(h_hi, w_hi) + _dot(h_hi, w_lo) + _dot(h_lo, w_hi) + br_ref[...]
    lane = lax.broadcasted_iota(I32, logits.shape, 1).astype(F32)
    lg = jnp.where(lane < N_EXPERTS, logits, -jnp.inf)
    m1 = jnp.max(lg, axis=-1, keepdims=True)
    i1 = jnp.min(jnp.where(lg == m1, lane, float(LANES)), axis=-1, keepdims=True)
    lg2 = jnp.where(lane == i1, -jnp.inf, lg)
    m2 = jnp.max(lg2, axis=-1, keepdims=True)
    i2 = jnp.min(jnp.where(lg2 == m2, lane, float(LANES)), axis=-1, keepdims=True)
    e2 = jnp.exp(m2 - m1)
    w_top = 1.0 / (1.0 + e2)

    chosen = jnp.where(lane == i1, 1.0, 0.0) + jnp.where(lane == i2, 1.0, 0.0)
    tm = chosen.shape[0]
    earlier = jnp.where(lax.broadcasted_iota(I32, (tm, tm), 1) < lax.broadcasted_iota(I32, (tm, tm), 0),
                        1.0, 0.0).astype(BF16)
    rank = _dot(earlier, chosen.astype(BF16)) + cnt_scr[0:1, :]
    rec = jnp.zeros_like(logits)
    for slot, val in ((R_E1, i1), (R_E2, i2), (R_RANK1, _lane_pick(rank, lane, i1)),
                      (R_RANK2, _lane_pick(rank, lane, i2)), (R_W1, w_top), (R_W2, e2 * w_top)):
        rec = jnp.where(lane == float(slot), val, rec)
    rec_ref[...] = rec
    cnt_scr[0:1, :] = cnt_scr[0:1, :] + jnp.sum(chosen, axis=0, keepdims=True)
    cnt_ref[...] = cnt_scr[...]


def _route(x2, g, mod, wr_pad, br_pad, s_len):
    t, d = x2.shape
    tm = MOE_TM
    per_b = s_len // tm
    return pl.pallas_call(
        _route_kernel,
        grid=(t // tm,),
        in_specs=[pl.BlockSpec((tm, d), lambda i: (i, 0)),
                  pl.BlockSpec((1, d), lambda i: (0, 0)),
                  pl.BlockSpec((1, 1, d), lambda i: (i // per_b, 0, 4)),
                  pl.BlockSpec((1, 1, d), lambda i: (i // per_b, 0, 3)),
                  pl.BlockSpec((d, LANES), lambda i: (0, 0)),
                  pl.BlockSpec((1, LANES), lambda i: (0, 0))],
        out_specs=[pl.BlockSpec((tm, d // 2), lambda i: (i, 0)),
                   pl.BlockSpec((tm, LANES), lambda i: (i, 0)),
                   pl.BlockSpec((8, LANES), lambda i: (0, 0))],
        out_shape=[jax.ShapeDtypeStruct((t, d // 2), jnp.uint32),
                   jax.ShapeDtypeStruct((t, LANES), F32),
                   jax.ShapeDtypeStruct((8, LANES), F32)],
        scratch_shapes=[pltpu.VMEM((8, LANES), F32)],
        compiler_params=_cparams("arbitrary"),
        name="moe_route",
    )(x2, g, mod, mod, wr_pad, br_pad)


def _row_copy(src_ref, src_row, dst_ref, dst_row, sem):
    return pltpu.make_async_copy(src_ref.at[pl.ds(src_row, 1)], dst_ref.at[pl.ds(dst_row, 1)], sem)


def _dispatch_kernel(pos_ref, h_ref, xs_in_ref, xs_ref, sem):
    del xs_in_ref
    tm = h_ref.shape[0]

    def start(t, carry):
        _row_copy(h_ref, t, xs_ref, pos_ref[0, 0, t], sem).start(priority=0)
        _row_copy(h_ref, t, xs_ref, pos_ref[0, 0, tm + t], sem).start(priority=1)
        return carry

    lax.fori_loop(0, tm, start, 0, unroll=4)
    for _ in range(2):
        pltpu.make_async_copy(h_ref, xs_ref.at[pl.ds(0, tm)], sem).wait()


def _dispatch(pos, h_packed, n_rows):
    t, w = h_packed.shape
    tm = MOE_TM
    xs0 = jnp.zeros((n_rows, w), h_packed.dtype)
    return pl.pallas_call(
        _dispatch_kernel,
        grid=(t // tm,),
        in_specs=[pl.BlockSpec((1, 1, 2 * tm), lambda i: (i, 0, 0), memory_space=pltpu.SMEM),
                  pl.BlockSpec((tm, w), lambda i: (i, 0)),
                  pl.BlockSpec(memory_space=pl.ANY)],
        out_specs=pl.BlockSpec(memory_space=pl.ANY),
        out_shape=jax.ShapeDtypeStruct((n_rows, w), h_packed.dtype),
        scratch_shapes=[pltpu.SemaphoreType.DMA(())],
        input_output_aliases={2: 0},
        compiler_params=_cparams("arbitrary"),
        name="moe_dispatch",
    )(pos, h_packed, xs0)


def _expert_kernel(te_ref, nu_ref, xs_ref, w1_ref, w3_ref, w2_ref, o_ref, h_scr, acc_scr):
    i = pl.program_id(0)
    f = pl.program_id(1)

    @pl.when(i < nu_ref[0])
    def _():
        @pl.when(f == 0)
        def _():
            word = xs_ref[...]
            lo = pltpu.bitcast(word << 16, F32)
            hi = pltpu.bitcast(word & jnp.uint32(0xFFFF0000), F32)
            h_scr[...] = jnp.concatenate([lo, hi], axis=1).astype(BF16)
            acc_scr[...] = jnp.zeros_like(acc_scr)

        acc_scr[...] += _swiglu_partial(h_scr[...], w1_ref[0], w3_ref[0], w2_ref[0])

        @pl.when(f == pl.num_programs(1) - 1)
        def _():
            o_ref[...] = acc_scr[...]

    @pl.when((i >= nu_ref[0]) & (f == pl.num_programs(1) - 1))
    def _():
        o_ref[...] = jnp.zeros_like(o_ref)


def _experts(tile_expert, n_used, xs, w1, w3, w2):
    n_rows, half = xs.shape
    d = 2 * half
    d_ff = w1.shape[2]
    tf = 896
    nf = d_ff // tf
    tile = lambda i, nu: jnp.minimum(i, nu[0] - 1)
    ff = lambda i, f, nu: jnp.where(i < nu[0], f, nf - 1)
    return pl.pallas_call(
        _expert_kernel,
        grid_spec=pltpu.PrefetchScalarGridSpec(
            num_scalar_prefetch=2,
            grid=(n_rows // MOE_ROWS, nf),
            in_specs=[pl.BlockSpec((MOE_ROWS, half), lambda i, f, te, nu: (tile(i, nu), 0)),
                      pl.BlockSpec((1, d, tf), lambda i, f, te, nu: (te[i], 0, ff(i, f, nu))),
                      pl.BlockSpec((1, d, tf), lambda i, f, te, nu: (te[i], 0, ff(i, f, nu))),
                      pl.BlockSpec((1, tf, d), lambda i, f, te, nu: (te[i], ff(i, f, nu), 0))],
            out_specs=pl.BlockSpec((MOE_ROWS, d), lambda i, f, te, nu: (i, 0)),
            scratch_shapes=[pltpu.VMEM((MOE_ROWS, d), BF16), pltpu.VMEM((MOE_ROWS, d), F32)]),
        out_shape=jax.ShapeDtypeStruct((n_rows, d), F32),
        compiler_params=_cparams("arbitrary", "arbitrary"),
        name="moe_experts",
    )(tile_expert, n_used, xs, w1, w3, w2)


def _combine_kernel(pos_ref, x_ref, gate_ref, rec_ref, fg_ref, ys_ref, o_ref, y_scr, sem, *, final):
    tm = x_ref.shape[0]

    def start(t, carry):
        _row_copy(ys_ref, pos_ref[0, 0, t], y_scr.at[0], t, sem).start(priority=0)
        _row_copy(ys_ref, pos_ref[0, 0, tm + t], y_scr.at[1], t, sem).start(priority=1)
        return carry

    lax.fori_loop(0, tm, start, 0, unroll=4)
    for slot in range(2):
        pltpu.make_async_copy(ys_ref.at[pl.ds(0, tm)], y_scr.at[slot], sem).wait()
    rec = rec_ref[...]
    lane = lax.broadcasted_iota(I32, rec.shape, 1)
    w_first = jnp.sum(jnp.where(lane == R_W1, rec, 0.0), axis=-1, keepdims=True)
    w_second = jnp.sum(jnp.where(lane == R_W2, rec, 0.0), axis=-1, keepdims=True)
    f = w_first * y_scr[0] + w_second * y_scr[1]
    o_ref[...] = _finish(x_ref[...], gate_ref[0], f, fg_ref, final)


def _combine(pos, x2, mod, rec, final_g, ys, s_len, final):
    t, d = x2.shape
    tm = MOE_TM
    per_b = s_len // tm
    kern = functools.partial(_combine_kernel, final=final)
    return pl.pallas_call(
        kern,
        grid=(t // tm,),
        in_specs=[pl.BlockSpec((1, 1, 2 * tm), lambda i: (i, 0, 0), memory_space=pltpu.SMEM),
                  pl.BlockSpec((tm, d), lambda i: (i, 0)),
                  pl.BlockSpec((1, 1, d), lambda i: (i // per_b, 0, 5)),
                  pl.BlockSpec((tm, LANES), lambda i: (i, 0)),
                  pl.BlockSpec((1, d), lambda i: (0, 0)),
                  pl.BlockSpec(memory_space=pl.ANY)],
        out_specs=pl.BlockSpec((tm, d), lambda i: (i, 0)),
        out_shape=jax.ShapeDtypeStruct((t, d), F32),
        scratch_shapes=[pltpu.VMEM((2, tm, d), F32), pltpu.SemaphoreType.DMA(())],
        compiler_params=_cparams("arbitrary"),
        name="moe_combine",
    )(pos, x2, mod, rec, final_g, ys)


def _moe(x2, g, mod, wr_pad, br_pad, w1, w3, w2, final_g, s_len, final):
    t, d = x2.shape
    n_e = w1.shape[0]
    h_packed, rec, cnt = _route(x2, g, mod, wr_pad, br_pad, s_len)

    n_tiles = 2 * t // MOE_ROWS + n_e
    counts = cnt[0, :n_e].astype(I32)
    padded = (counts + MOE_ROWS - 1) // MOE_ROWS * MOE_ROWS
    ends = jnp.cumsum(padded)
    starts = ends - padded
    e1, e2 = rec[:, R_E1].astype(I32), rec[:, R_E2].astype(I32)
    pos1 = starts[e1] + rec[:, R_RANK1].astype(I32)
    pos2 = starts[e2] + rec[:, R_RANK2].astype(I32)
    pos = jnp.concatenate([pos1.reshape(-1, 1, MOE_TM), pos2.reshape(-1, 1, MOE_TM)], axis=2)
    n_used = (ends[-1] // MOE_ROWS).reshape(1)
    tile_start = jnp.minimum(jnp.arange(n_tiles, dtype=I32), n_used[0] - 1) * MOE_ROWS
    tile_expert = jnp.sum(tile_start[:, None] >= ends[None, :], axis=1).astype(I32)

    xs = _dispatch(pos, h_packed, n_tiles * MOE_ROWS)
    ys = _experts(tile_expert, n_used, xs, w1, w3, w2)
    return _combine(pos, x2, mod, rec, final_g, ys, s_len, final)


def _pack_w_in(w):
    d = w.shape[0]
    sizes = (W_A, W_A, W_A, IDX_HEADS * IDX_DIM, IDX_DIM, IDX_HEADS,
             512, 512, 512, 512, 512, 512, 3 * d)
    offs = [0]
    for s in sizes:
        offs.append(offs[-1] + s)
    (qa, ka, va, qi, ki, wi, qb, kb, vb, qc, kc, vc, gl) = [
        w[:, offs[j]:offs[j + 1]] for j in range(len(sizes))]
    scale = HEAD_DIM ** -0.5
    pad_wi = jnp.zeros((d, LANES - IDX_HEADS), w.dtype)
    packed = jnp.concatenate(
        [gl, qa * scale, ka, va,
         qi * (IDX_DIM ** -0.5), ki, ki, wi * (IDX_HEADS ** -0.5), pad_wi,
         qb * scale, kb, vb, qc * scale, kc, vc], axis=1)
    assert packed.shape[1] == PACKED
    return packed.astype(BF16)


def kernel(x, c, w_ada, b_ada, norm1_g, norm2_g, w_in, w_br, w_out, rel_bias, lam_params,
           subln_g, ffn_w1, ffn_w3, ffn_w2, router_w, router_b, moe_w1, moe_w3, moe_w2, final_g):
    bsz, s_len, d = x.shape
    depth = w_ada.shape[0]
    assert s_len % 1024 == 0 and d == 1024 and OFF_G + 3 * d == OFF_QA

    c_pad = jnp.concatenate([c, jnp.zeros((8 - bsz % 8 if bsz % 8 else 0, d), c.dtype)], axis=0)
    mod_all = _ada(c_pad, w_ada, b_ada)
    dsa_tiles, diff_tiles = _bias_tiles(rel_bias)
    fg = final_g.reshape(1, d)

    x2 = x.reshape(bsz * s_len, d)
    for l in range(depth):
        mod = mod_all[l, :bsz].reshape(bsz, 1, 6 * d)
        proj = _inproj(x2, norm1_g[l].reshape(1, d), mod, _pack_w_in(w_in[l]), s_len)
        proj3 = proj.reshape(bsz, s_len, PACKED)
        lam_init = 0.8 - 0.6 * math.exp(-0.3 * l)
        o_a = _dsa(proj3, dsa_tiles, bsz, s_len)
        o_b = _diff(proj3, diff_tiles, lam_params, subln_g, l, bsz, s_len, lam_init)
        o_c = _stick(proj3, bsz, s_len)
        x2 = _merge(o_a.reshape(-1, o_a.shape[-1]), o_b.reshape(-1, o_b.shape[-1]),
                    o_c.reshape(-1, o_c.shape[-1]), proj, x2, mod,
                    w_br[l].astype(BF16), w_out[l].astype(BF16), s_len)
        g2 = norm2_g[l].reshape(1, d)
        final = l == depth - 1
        j = l // 2
        if l % 2 == 0:
            x2 = _ffn(x2, g2, mod, ffn_w1[j].astype(BF16), ffn_w3[j].astype(BF16),
                      ffn_w2[j].astype(BF16), fg, s_len, final)
        else:
            wr_pad = jnp.pad(router_w[j], ((0, 0), (0, LANES - N_EXPERTS)))
            br_pad = jnp.pad(router_b[j], (0, LANES - N_EXPERTS)).reshape(1, LANES)
            x2 = _moe(x2, g2, mod, wr_pad, br_pad, moe_w1[j].astype(BF16), moe_w3[j].astype(BF16),
                      moe_w2[j].astype(BF16), fg, s_len, final)
    return x2.reshape(bsz, s_len, d)
```

```python
import functools
import math

import jax
import jax.numpy as jnp
from jax import lax
from jax.experimental import pallas as pl
from jax.experimental.pallas import tpu as pltpu

F32 = jnp.float32
BF16 = jnp.bfloat16
I32 = jnp.int32
I16 = jnp.int16

LANES = 128
VMEM_LIMIT_BYTES = 56 * 1024 * 1024

CHUNK = 64
A_HEADS = 8
IDX_HEADS = 4
IDX_DIM = 64
TOPK_MAX = 256
B_HEADS = 4
C_HEADS = 8
HEAD_DIM = 64
REL_BUCKETS = 32
FAR_BUCKET = REL_BUCKETS // 2 - 1
N_EXPERTS = 8
EPS = 1e-6
SUBLN_EPS = 1e-5
NEG = -1e30
LOG2E = 1.4426950408889634
INT_MIN = -(2 ** 31)

KEY_BLOCK = 256
Q_TILE = 256
GROUP = 4
BLOCK_GROUPS = (8, 4, 2, 1)

W_A = A_HEADS * HEAD_DIM
OFF_G = 0
OFF_QA, OFF_KA, OFF_VA = 3072, 3584, 4096
OFF_QI, OFF_KI, OFF_WI = 4608, 4864, 4992
OFF_QB, OFF_KB, OFF_VB = 5120, 5632, 6144
OFF_QC, OFF_KC, OFF_VC = 6656, 7168, 7680
PACKED = 8192

LOG_BUCKET_STEPS = (12, 16, 23, 32, 46, 64, 91)


def _nt_dot(a, b):
    return lax.dot_general(a, b, (((1,), (1,)), ((), ())), preferred_element_type=F32)


def _dot(a, b):
    return jnp.dot(a, b, preferred_element_type=F32)


def _cparams(*sem):
    return pltpu.CompilerParams(dimension_semantics=sem, vmem_limit_bytes=VMEM_LIMIT_BYTES)


def _split_heads(x):
    lane = lax.broadcasted_iota(I32, x.shape, 1)
    keep_a = jnp.where(lane < HEAD_DIM, 1.0, 0.0).astype(x.dtype)
    keep_b = jnp.where(lane < HEAD_DIM, 0.0, 1.0).astype(x.dtype)
    return jnp.concatenate([x * keep_a, x * keep_b], axis=0)


def _merge_heads(o, m):
    lane = lax.broadcasted_iota(I32, (m, LANES), 1)
    return jnp.where(lane < HEAD_DIM, o[:m], o[m:])


def _wide(x):
    return jnp.concatenate([x, x], axis=1)


def _tall(x):
    return jnp.concatenate([x, x], axis=0)


def _key_rows(kb):
    return pl.ds(pl.multiple_of(kb * KEY_BLOCK, KEY_BLOCK), KEY_BLOCK)


def _sortable(x):
    bits = pltpu.bitcast(x, I32)
    return bits ^ ((bits >> 31) & 0x7FFFFFFF)


def _chunk_causal():
    r = lax.broadcasted_iota(I32, (Q_TILE, KEY_BLOCK), 0)
    c = lax.broadcasted_iota(I32, (Q_TILE, KEY_BLOCK), 1)
    return (c // CHUNK) <= (r // CHUNK)


def _ada_kernel(c_ref, w_ref, b_ref, o_ref):
    c = c_ref[...]
    a = c * jax.nn.sigmoid(c)
    o_ref[0] = jnp.dot(a, w_ref[0], preferred_element_type=F32,
                       precision=lax.Precision.HIGHEST) + b_ref[0]


def _ada(c_pad, w_ada, b_ada):
    depth, d, n = w_ada.shape
    tn = 1024
    return pl.pallas_call(
        _ada_kernel,
        grid=(depth, n // tn),
        in_specs=[pl.BlockSpec(c_pad.shape, lambda l, j: (0, 0)),
                  pl.BlockSpec((1, d, tn), lambda l, j: (l, 0, j)),
                  pl.BlockSpec((1, 1, tn), lambda l, j: (l, 0, j))],
        out_specs=pl.BlockSpec((1, c_pad.shape[0], tn), lambda l, j: (l, 0, j)),
        out_shape=jax.ShapeDtypeStruct((depth, c_pad.shape[0], n), F32),
        compiler_params=_cparams("parallel", "parallel"),
        name="ada",
    )(c_pad, w_ada, b_ada.reshape(depth, 1, n))


def _rel_bias_tile(tab_ref, head, d0, n_heads_total):
    r = lax.broadcasted_iota(I32, (Q_TILE, KEY_BLOCK), 0)
    c = lax.broadcasted_iota(I32, (Q_TILE, KEY_BLOCK), 1)
    d = c - r + d0
    n = jnp.abs(d)
    large = jnp.full(d.shape, REL_BUCKETS // 4, I32)
    for step in LOG_BUCKET_STEPS:
        large = large + jnp.where(n >= step, 1, 0)
    bucket = jnp.where(d > 0, REL_BUCKETS // 2, 0) + jnp.where(n < REL_BUCKETS // 4, n, large)
    out = jnp.zeros(d.shape, F32)
    for b in range(REL_BUCKETS):
        out = jnp.where(bucket == b, tab_ref[b * n_heads_total + head], out)
    return out - tab_ref[FAR_BUCKET * n_heads_total + head]


def _bias_kernel(tab_ref, dsa_ref, diff_ref):
    p = pl.program_id(0)
    n_heads = A_HEADS + B_HEADS
    dsa_ref[0, 0] = jnp.zeros(dsa_ref.shape[2:], F32)
    diff_ref[0, 0] = jnp.zeros(diff_ref.shape[2:], F32)
    for t, d0 in ((1, -KEY_BLOCK), (2, 0)):
        dsa_ref[0, t, 0:Q_TILE, :] = _rel_bias_tile(tab_ref, 2 * p, d0, n_heads)
        dsa_ref[0, t, Q_TILE:2 * Q_TILE, :] = _rel_bias_tile(tab_ref, 2 * p + 1, d0, n_heads)
        tile = _rel_bias_tile(tab_ref, A_HEADS + p, d0, n_heads)
        if t == 2:
            tile = jnp.where(_chunk_causal(), tile, NEG)
        diff_ref[0, t] = tile


def _bias_tiles(rel_bias):
    tab = rel_bias.reshape(-1)
    return pl.pallas_call(
        _bias_kernel,
        grid=(4,),
        in_specs=[pl.BlockSpec(memory_space=pltpu.SMEM)],
        out_specs=[pl.BlockSpec((1, 3, 2 * Q_TILE, KEY_BLOCK), lambda p: (p, 0, 0, 0)),
                   pl.BlockSpec((1, 3, Q_TILE, KEY_BLOCK), lambda p: (p, 0, 0, 0))],
        out_shape=[jax.ShapeDtypeStruct((A_HEADS // 2, 3, 2 * Q_TILE, KEY_BLOCK), F32),
                   jax.ShapeDtypeStruct((B_HEADS, 3, Q_TILE, KEY_BLOCK), F32)],
        compiler_params=_cparams("parallel"),
        name="rel_bias_tiles",
    )(tab)


def _modulated_norm(x, g, sc, sh):
    y = x * lax.rsqrt(jnp.mean(x * x, axis=-1, keepdims=True) + EPS)
    return y * g * (1.0 + sc) + sh


def _inproj_kernel(x_ref, g_ref, sc_ref, sh_ref, w_ref, o_ref, h_scr):
    @pl.when(pl.program_id(1) == 0)
    def _():
        h_scr[...] = _modulated_norm(x_ref[...], g_ref[...], sc_ref[0], sh_ref[0]).astype(BF16)

    o_ref[...] = _dot(h_scr[...], w_ref[...]).astype(o_ref.dtype)


def _inproj(x2, g, mod, w_packed, s_len):
    t, d = x2.shape
    n = w_packed.shape[1]
    tm, tn = 1024, 1024
    per_b = s_len // tm
    return pl.pallas_call(
        _inproj_kernel,
        grid=(t // tm, n // tn),
        in_specs=[pl.BlockSpec((tm, d), lambda i, j: (i, 0)),
                  pl.BlockSpec((1, d), lambda i, j: (0, 0)),
                  pl.BlockSpec((1, 1, d), lambda i, j: (i // per_b, 0, 1)),
                  pl.BlockSpec((1, 1, d), lambda i, j: (i // per_b, 0, 0)),
                  pl.BlockSpec((d, tn), lambda i, j: (0, j))],
        out_specs=pl.BlockSpec((tm, tn), lambda i, j: (i, j)),
        out_shape=jax.ShapeDtypeStruct((t, n), BF16),
        scratch_shapes=[pltpu.VMEM((tm, d), BF16)],
        compiler_params=_cparams("parallel", "arbitrary"),
        name="inproj",
    )(x2, g, mod, mod, w_packed)


def _for_blocks(lo, hi, fn, sizes=BLOCK_GROUPS):
    pos = lo
    for size in sizes:
        trips = jnp.maximum(hi - pos, 0) // size

        def body(i, carry, pos=pos, size=size):
            fn([pos + i * size + u for u in range(size)])
            return carry

        lax.fori_loop(0, trips, body, 0)
        pos = pos + trips * size


def _fold_blocks(n, fn, acc):
    pos = 0
    for size in BLOCK_GROUPS:
        trips = (n - pos) // size

        def body(i, acc, pos=pos, size=size):
            for u in range(size):
                acc = fn(pos + i * size + u, acc)
            return acc

        acc = lax.fori_loop(0, trips, body, acc)
        pos = pos + trips * size
    return acc


def _softmax_pv(qs, k_at, v_at, qb, add, s_scr, mx_scr, l_scr, acc_scr):
    def pass1(blocks):
        mx = None
        for kb in blocks:
            s = add(kb, _nt_dot(qs, k_at(kb)), jnp.clip(kb - qb + 2, 0, 2))
            s = s * LOG2E
            s_scr[kb] = s
            fold = jnp.maximum(s[:, :LANES], s[:, LANES:])
            mx = fold if mx is None else jnp.maximum(mx, fold)
        mx_scr[...] = jnp.maximum(mx_scr[...], mx)

    mx_scr[...] = jnp.full(mx_scr.shape, NEG, F32)
    _for_blocks(0, qb + 1, pass1)

    m = jnp.max(mx_scr[...], axis=-1, keepdims=True)
    mx_scr[...] = jnp.broadcast_to(m, mx_scr.shape)
    l_scr[...] = jnp.zeros_like(l_scr)
    acc_scr[...] = jnp.zeros_like(acc_scr)

    def pass2(kbs):
        mw = _wide(mx_scr[...])
        l_add = acc_add = None
        for kb in kbs:
            p = jnp.exp2(s_scr[kb] - mw)
            fold = p[:, :LANES] + p[:, LANES:]
            pv = _dot(p.astype(BF16), v_at(kb))
            l_add = fold if l_add is None else l_add + fold
            acc_add = pv if acc_add is None else acc_add + pv
        l_scr[...] += l_add
        acc_scr[...] += acc_add

    _for_blocks(0, qb + 1, pass2)
    return acc_scr[...] / jnp.sum(l_scr[...], axis=-1, keepdims=True)


def _softmax_scratch(s_len):
    return [pltpu.VMEM((s_len // KEY_BLOCK, 2 * Q_TILE, KEY_BLOCK), F32),
            pltpu.VMEM((2 * Q_TILE, LANES), F32),
            pltpu.VMEM((2 * Q_TILE, LANES), F32),
            pltpu.VMEM((2 * Q_TILE, LANES), F32)]


def _dsa_kernel(qa_ref, qi_ref, wi_ref, ka_ref, va_ref, ki_ref, bias_ref, o_ref,
                key_scr, keyt_scr, hi_scr, lo_scr, neg_scr, w_scr, last_scr, s_scr, mx_scr, l_scr, acc_scr,
                *, topk, idx_bits, s_len):
    qb = pl.program_id(1)

    @pl.when(pl.program_id(2) == 0)
    def _():
        _dsa_select(qb, qi_ref, wi_ref, ki_ref, key_scr, keyt_scr, hi_scr, lo_scr, neg_scr, w_scr, last_scr,
                    topk=topk, idx_bits=idx_bits, s_len=s_len)

    pr = pl.program_id(2)
    lanes = pl.ds(pl.multiple_of(pr * LANES, LANES), LANES)

    def add(kb, s, t):
        return s + _tall(neg_scr[kb]) + bias_ref[pr, t]

    o = _softmax_pv(_split_heads(qa_ref[0]),
                    lambda kb: ka_ref[0, _key_rows(kb), lanes],
                    lambda kb: va_ref[0, _key_rows(kb), lanes],
                    qb, add, s_scr, mx_scr, l_scr, acc_scr)
    o_ref[0] = _merge_heads(o, Q_TILE).astype(o_ref.dtype)


def _dsa_select(qb, qi_ref, wi_ref, ki_ref, key_scr, keyt_scr, hi_scr, lo_scr, neg_scr, w_scr, last_scr,
                *, topk, idx_bits, s_len):
    nkb = qb + 1
    kf = float(topk)
    valid_diag = _chunk_causal()

    def cols(kb):
        return kb * KEY_BLOCK + lax.broadcasted_iota(I32, (Q_TILE, KEY_BLOCK), 1)

    qi = qi_ref[0]
    q_heads = [_split_heads(qi[:, pr * LANES:(pr + 1) * LANES]) for pr in range(IDX_HEADS // 2)]
    wi = wi_ref[0].astype(F32)
    for h in range(IDX_HEADS):
        w_scr[h] = jnp.broadcast_to(wi[:, h:h + 1], (Q_TILE, LANES))

    def score_block(kb, valid):
        kk = ki_ref[0, _key_rows(kb), :]
        sc = jnp.zeros((Q_TILE, KEY_BLOCK), F32)
        for pr in range(IDX_HEADS // 2):
            d = _nt_dot(q_heads[pr], kk)
            sc = sc + _wide(w_scr[2 * pr]) * jnp.maximum(d[:Q_TILE], 0.0)
            sc = sc + _wide(w_scr[2 * pr + 1]) * jnp.maximum(d[Q_TILE:], 0.0)
        sc = jnp.where(sc == 0.0, 0.0, sc)
        if valid is not None:
            sc = jnp.where(valid, sc, -jnp.inf)
        key_scr[kb] = _sortable(sc)
        key_t = _sortable(sc.T)
        keyt_scr[kb] = key_t
        hi_scr[kb] = (key_t >> 16).astype(I16)
        lo_scr[kb] = ((key_t & 0xFFFF) - 2 ** 15).astype(I16)

    _for_blocks(0, qb, lambda kbs: [score_block(kb, None) for kb in kbs], sizes=(4, 2, 1))
    score_block(qb, valid_diag)

    def per_query(x):
        return jnp.broadcast_to(x[None], (KEY_BLOCK // 8, 8, Q_TILE)).reshape(KEY_BLOCK, Q_TILE)

    def key_index(kb):
        return kb * KEY_BLOCK + lax.broadcasted_iota(I32, (KEY_BLOCK, Q_TILE), 0)

    def count(pred):
        def one(kb, acc):
            hit = jnp.where(pred(keyt_scr[kb], kb), 1.0, 0.0)
            return acc + jnp.sum(hit.reshape(KEY_BLOCK // 8, 8, Q_TILE), axis=0)

        acc = _fold_blocks(nkb, one, jnp.zeros((8, Q_TILE), F32))
        return jnp.broadcast_to(jnp.sum(acc, axis=0, keepdims=True), (8, Q_TILE))

    def per_query16(x):
        x16 = x.astype(I16)
        return jnp.broadcast_to(x16[None], (KEY_BLOCK // 16, 16, Q_TILE)).reshape(KEY_BLOCK, Q_TILE)

    def count16(ref, pred):
        def one(kb, acc):
            hit = jnp.where(pred(ref[kb]), jnp.int16(1), jnp.int16(0))
            parts = [hit[g:g + 16] for g in range(0, KEY_BLOCK, 16)]
            while len(parts) > 1:
                parts = [parts[i] + parts[i + 1] for i in range(0, len(parts), 2)]
            return acc + parts[0]

        acc = _fold_blocks(nkb, one, jnp.zeros((16, Q_TILE), I16))
        return jnp.broadcast_to(jnp.sum(acc.astype(F32), axis=0, keepdims=True), (16, Q_TILE))

    hi_thr = jnp.where(count16(hi_scr, lambda h: h >= 0) >= kf,
                       jnp.zeros((16, Q_TILE), I32), jnp.full((16, Q_TILE), -(2 ** 15), I32))

    def hi_body(j, t):
        cand = t | lax.shift_left(jnp.int32(1), 14 - j)
        cw = per_query16(cand)
        return jnp.where(count16(hi_scr, lambda h: h >= cw) >= kf, cand, t)

    hi_thr = lax.fori_loop(0, 15, hi_body, hi_thr)
    hi_w = per_query16(hi_thr)
    above = count16(hi_scr, lambda h: h > hi_w)

    def keep_ties(kb, carry):
        lo_scr[kb] = jnp.where(hi_scr[kb] == hi_w, lo_scr[kb], jnp.int16(-(2 ** 15)))
        return carry

    lax.fori_loop(0, nkb, keep_ties, 0)

    def lo_body(j, t):
        cand = t | lax.shift_left(jnp.int32(1), 15 - j)
        cw = per_query16(cand - 2 ** 15)
        return jnp.where(above + count16(lo_scr, lambda lo: lo >= cw) >= kf, cand, t)

    lo_thr = lax.fori_loop(0, 16, lo_body, jnp.zeros((16, Q_TILE), I32))
    thr = (lax.shift_left(hi_thr, 16) | lo_thr)[0:8]
    thr_t = per_query(thr)

    need = kf - count(lambda key, kb: key > thr_t)
    ties = count(lambda key, kb: key == thr_t)
    last_scr[...] = jnp.full(last_scr.shape, s_len, I32)

    @pl.when(jnp.max(ties - need) > 0.0)
    def _():
        def tie_body(j, last):
            cand = last | lax.shift_left(jnp.int32(1), idx_bits - 1 - j)
            cw = per_query(cand)
            taken = count(lambda key, kb: (key == thr_t) & (key_index(kb) < cw))
            return jnp.where(taken < need, cand, last)

        last_scr[...] = lax.fori_loop(0, idx_bits, tie_body, jnp.zeros((8, Q_TILE), I32))

    def per_row(x):
        return _wide(jnp.broadcast_to(x[0:1], (LANES, Q_TILE)).T)

    tw = per_row(thr)
    lastw = per_row(last_scr[...])

    def mask_block(kb, valid):
        key = key_scr[kb]
        tie_neg = jnp.where(key == tw, jnp.where(cols(kb) <= lastw, 0.0, NEG), NEG)
        neg = jnp.where(key > tw, 0.0, tie_neg)
        if valid is not None:
            neg = jnp.where(valid, neg, NEG)
        neg_scr[kb] = neg

    def mask_body(kb, carry):
        mask_block(kb, None)
        return carry

    lax.fori_loop(0, qb, mask_body, 0)
    mask_block(qb, valid_diag)


def _dsa(proj, bias_tiles, bsz, s_len):
    topk = min(TOPK_MAX, s_len // 4)
    idx_bits = max(1, (s_len - 1).bit_length())
    n_blocks = s_len // KEY_BLOCK
    assert 2 * n_blocks <= 256
    kern = functools.partial(_dsa_kernel, topk=topk, idx_bits=idx_bits, s_len=s_len)
    lane_blk = lambda off: off // LANES
    return pl.pallas_call(
        kern,
        grid=(bsz, s_len // Q_TILE, A_HEADS // 2),
        in_specs=[pl.BlockSpec((1, Q_TILE, LANES), lambda b, i, p: (b, i, lane_blk(OFF_QA) + p)),
                  pl.BlockSpec((1, Q_TILE, 256), lambda b, i, p: (b, i, OFF_QI // 256)),
                  pl.BlockSpec((1, Q_TILE, LANES), lambda b, i, p: (b, i, lane_blk(OFF_WI))),
                  pl.BlockSpec((1, s_len, W_A), lambda b, i, p: (b, 0, OFF_KA // W_A),
                               pipeline_mode=pl.Buffered(1)),
                  pl.BlockSpec((1, s_len, W_A), lambda b, i, p: (b, 0, OFF_VA // W_A),
                               pipeline_mode=pl.Buffered(1)),
                  pl.BlockSpec((1, s_len, LANES), lambda b, i, p: (b, 0, lane_blk(OFF_KI))),
                  pl.BlockSpec(bias_tiles.shape, lambda b, i, p: (0, 0, 0, 0),
                               pipeline_mode=pl.Buffered(1))],
        out_specs=pl.BlockSpec((1, Q_TILE, LANES), lambda b, i, p: (b, i, p)),
        out_shape=jax.ShapeDtypeStruct((bsz, s_len, W_A), BF16),
        scratch_shapes=[pltpu.VMEM((n_blocks, Q_TILE, KEY_BLOCK), I32),
                        pltpu.VMEM((n_blocks, KEY_BLOCK, Q_TILE), I32),
                        pltpu.VMEM((n_blocks, KEY_BLOCK, Q_TILE), I16),
                        pltpu.VMEM((n_blocks, KEY_BLOCK, Q_TILE), I16),
                        pltpu.VMEM((n_blocks, Q_TILE, KEY_BLOCK), F32),
                        pltpu.VMEM((IDX_HEADS, Q_TILE, LANES), F32),
                        pltpu.VMEM((8, Q_TILE), I32)] + _softmax_scratch(s_len),
        compiler_params=_cparams("parallel", "arbitrary", "arbitrary"),
        name="dsa",
    )(proj, proj, proj, proj, proj, proj, bias_tiles)


def _diff_kernel(q_ref, k_ref, v_ref, bias_ref, lp_ref, g_ref, o_ref,
                 s_scr, mx_scr, l_scr, acc_scr, *, lam_init):
    qb = pl.program_id(2)
    qs = _split_heads(q_ref[0])

    def add(kb, s, t):
        return s + _tall(bias_ref[0, t])

    o = _softmax_pv(qs, lambda kb: k_ref[0, _key_rows(kb), :], lambda kb: v_ref[0, _key_rows(kb), :],
                    qb, add, s_scr, mx_scr, l_scr, acc_scr)
    lp = lp_ref[0]
    lam = (jnp.exp(jnp.sum(lp[0:1] * lp[1:2], axis=-1, keepdims=True))
           - jnp.exp(jnp.sum(lp[2:3] * lp[3:4], axis=-1, keepdims=True)) + lam_init)
    o = o[:Q_TILE] - lam * o[Q_TILE:]
    y = o * lax.rsqrt(jnp.mean(o * o, axis=-1, keepdims=True) + SUBLN_EPS)
    o_ref[0] = (y * g_ref[0] * (1.0 - lam_init)).astype(o_ref.dtype)


def _diff(proj, bias_tiles, lam_params, subln_g, layer, bsz, s_len, lam_init):
    kern = functools.partial(_diff_kernel, lam_init=lam_init)
    return pl.pallas_call(
        kern,
        grid=(bsz, B_HEADS, s_len // Q_TILE),
        in_specs=[pl.BlockSpec((1, Q_TILE, LANES), lambda b, h, i: (b, i, OFF_QB // LANES + h)),
                  pl.BlockSpec((1, s_len, LANES), lambda b, h, i: (b, 0, OFF_KB // LANES + h)),
                  pl.BlockSpec((1, s_len, LANES), lambda b, h, i: (b, 0, OFF_VB // LANES + h)),
                  pl.BlockSpec((1, 3, Q_TILE, KEY_BLOCK), lambda b, h, i: (h, 0, 0, 0)),
                  pl.BlockSpec((1, 4, HEAD_DIM), lambda b, h, i: (layer, 0, 0)),
                  pl.BlockSpec((1, 1, 2 * HEAD_DIM), lambda b, h, i: (layer, 0, 0))],
        out_specs=pl.BlockSpec((1, Q_TILE, LANES), lambda b, h, i: (b, i, h)),
        out_shape=jax.ShapeDtypeStruct((bsz, s_len, B_HEADS * 2 * HEAD_DIM), BF16),
        scratch_shapes=_softmax_scratch(s_len),
        compiler_params=_cparams("parallel", "parallel", "arbitrary"),
        name="diff_attn",
    )(proj, proj, proj, bias_tiles, lam_params, subln_g.reshape(-1, 1, 2 * HEAD_DIM))


def _stick_kernel(q_ref, k_ref, v_ref, o_ref, later_scr, run_scr, acc_scr):
    qb = pl.program_id(2)
    qs = _split_heads(q_ref[0])
    jj = lax.broadcasted_iota(I32, (2 * KEY_BLOCK, KEY_BLOCK), 0) & (KEY_BLOCK - 1)
    ss = lax.broadcasted_iota(I32, (2 * KEY_BLOCK, KEY_BLOCK), 1)
    later_scr[...] = jnp.where(jj > ss, 1.0, 0.0).astype(BF16)
    run_scr[...] = jnp.zeros_like(run_scr)
    acc_scr[...] = jnp.zeros_like(acc_scr)

    def sweep(kbs, diagonal):
        parts = []
        for j, kb in enumerate(kbs):
            z = _nt_dot(qs, k_ref[0, _key_rows(kb), :])
            soft = jnp.log(1.0 + jnp.exp2(jnp.abs(z) * (-LOG2E)))
            log_beta = jnp.minimum(z, 0.0) - soft
            log_1mb = log_beta - z
            strict = None
            if diagonal and j == 0:
                t = lax.broadcasted_iota(I32, z.shape, 0) & (Q_TILE - 1)
                strict = lax.broadcasted_iota(I32, z.shape, 1) < t
                log_1mb = jnp.where(strict, log_1mb, 0.0)
            hi = log_1mb.astype(BF16)
            lo = (log_1mb - hi.astype(F32)).astype(BF16)
            between = _dot(jnp.concatenate([hi, lo], axis=1), later_scr[...])
            total = between[:, 0:1] + log_1mb[:, 0:1]
            parts.append((kb, log_beta + between, total, strict))
        run = run_scr[...]
        acc = None
        for kb, logit, total, strict in parts:
            w = jnp.exp(logit + _wide(run))
            if strict is not None:
                w = jnp.where(strict, w, 0.0)
            pv = _dot(w.astype(BF16), v_ref[0, _key_rows(kb), :])
            acc = pv if acc is None else acc + pv
            run = run + jnp.broadcast_to(total, run.shape)
        acc_scr[...] += acc
        run_scr[...] = run

    first = jnp.where(qb + 1 >= GROUP, GROUP, jnp.where(qb + 1 >= 2, 2, 1))
    for size in (GROUP, 2, 1):
        @pl.when(first == size)
        def _(size=size):
            sweep([qb - u for u in range(size)], True)
    top = qb - first
    for size in BLOCK_GROUPS:
        trips = (top + 1) // size

        def body(i, carry, top=top, size=size):
            sweep([top - i * size - u for u in range(size)], False)
            return carry

        lax.fori_loop(0, trips, body, 0)
        top = top - trips * size
    o_ref[0] = _merge_heads(acc_scr[...], Q_TILE).astype(o_ref.dtype)


def _stick(proj, bsz, s_len):
    return pl.pallas_call(
        _stick_kernel,
        grid=(bsz, C_HEADS // 2, s_len // Q_TILE),
        in_specs=[pl.BlockSpec((1, Q_TILE, LANES), lambda b, h, i: (b, i, OFF_QC // LANES + h)),
                  pl.BlockSpec((1, s_len, LANES), lambda b, h, i: (b, 0, OFF_KC // LANES + h)),
                  pl.BlockSpec((1, s_len, LANES), lambda b, h, i: (b, 0, OFF_VC // LANES + h))],
        out_specs=pl.BlockSpec((1, Q_TILE, LANES), lambda b, h, i: (b, i, h)),
        out_shape=jax.ShapeDtypeStruct((bsz, s_len, C_HEADS * HEAD_DIM), BF16),
        scratch_shapes=[pltpu.VMEM((2 * KEY_BLOCK, KEY_BLOCK), BF16),
                        pltpu.VMEM((2 * Q_TILE, LANES), F32),
                        pltpu.VMEM((2 * Q_TILE, LANES), F32)],
        compiler_params=_cparams("parallel", "parallel", "arbitrary"),
        name="stick_attn",
    )(proj, proj, proj)


def _merge_kernel(oa_ref, ob_ref, oc_ref, gate_ref, x_ref, g1_ref, wbr_ref, wout_ref, o_ref):
    d = x_ref.shape[-1]
    merged = jnp.zeros(x_ref.shape, F32)
    for j, o_br in enumerate((oa_ref, ob_ref, oc_ref)):
        w = o_br.shape[-1]
        gate = jax.nn.sigmoid(gate_ref[:, j * d:(j + 1) * d].astype(F32))
        merged = merged + gate * _dot(o_br[...], wbr_ref[j * w:(j + 1) * w, :])
    o_ref[...] = x_ref[...] + g1_ref[0] * _dot(merged.astype(BF16), wout_ref[...])


def _merge(o_a, o_b, o_c, proj, x2, mod, w_br, w_out, s_len):
    t, d = x2.shape
    tm = 512
    per_b = s_len // tm
    w = o_a.shape[-1]
    return pl.pallas_call(
        _merge_kernel,
        grid=(t // tm,),
        in_specs=[pl.BlockSpec((tm, w), lambda i: (i, 0)),
                  pl.BlockSpec((tm, w), lambda i: (i, 0)),
                  pl.BlockSpec((tm, w), lambda i: (i, 0)),
                  pl.BlockSpec((tm, 3 * d), lambda i: (i, OFF_G // (3 * d))),
                  pl.BlockSpec((tm, d), lambda i: (i, 0)),
                  pl.BlockSpec((1, 1, d), lambda i: (i // per_b, 0, 2)),
                  pl.BlockSpec(w_br.shape, lambda i: (0, 0)),
                  pl.BlockSpec(w_out.shape, lambda i: (0, 0))],
        out_specs=pl.BlockSpec((tm, d), lambda i: (i, 0)),
        out_shape=jax.ShapeDtypeStruct((t, d), F32),
        compiler_params=_cparams("parallel"),
        name="merge",
    )(o_a, o_b, o_c, proj, x2, mod, w_br, w_out)


def _finish(x, gate, acc, fg_ref, final):
    y = x + gate * acc
    if final:
        y = y * lax.rsqrt(jnp.mean(y * y, axis=-1, keepdims=True) + EPS) * fg_ref[...]
    return y


def _swiglu_partial(h, w1, w3, w2):
    a = _dot(h, w1)
    act = (a * jax.nn.sigmoid(a)) * _dot(h, w3)
    return _dot(act.astype(BF16), w2)


def _ffn_kernel(x_ref, g_ref, sc_ref, sh_ref, gate_ref, w1_ref, w3_ref, w2_ref, fg_ref, o_ref,
                h_scr, acc_scr, *, final):
    f = pl.program_id(1)

    @pl.when(f == 0)
    def _():
        h_scr[...] = _modulated_norm(x_ref[...], g_ref[...], sc_ref[0], sh_ref[0]).astype(BF16)
        acc_scr[...] = jnp.zeros_like(acc_scr)

    acc_scr[...] += _swiglu_partial(h_scr[...], w1_ref[...], w3_ref[...], w2_ref[...])

    @pl.when(f == pl.num_programs(1) - 1)
    def _():
        o_ref[...] = _finish(x_ref[...], gate_ref[0], acc_scr[...], fg_ref, final)


def _ffn(x2, g, mod, w1, w3, w2, final_g, s_len, final):
    t, d = x2.shape
    d_ff = w1.shape[1]
    tm, tf = 1024, 256
    per_b = s_len // tm
    kern = functools.partial(_ffn_kernel, final=final)
    return pl.pallas_call(
        kern,
        grid=(t // tm, d_ff // tf),
        in_specs=[pl.BlockSpec((tm, d), lambda i, f: (i, 0)),
                  pl.BlockSpec((1, d), lambda i, f: (0, 0)),
                  pl.BlockSpec((1, 1, d), lambda i, f: (i // per_b, 0, 4)),
                  pl.BlockSpec((1, 1, d), lambda i, f: (i // per_b, 0, 3)),
                  pl.BlockSpec((1, 1, d), lambda i, f: (i // per_b, 0, 5)),
                  pl.BlockSpec((d, tf), lambda i, f: (0, f)),
                  pl.BlockSpec((d, tf), lambda i, f: (0, f)),
                  pl.BlockSpec((tf, d), lambda i, f: (f, 0)),
                  pl.BlockSpec((1, d), lambda i, f: (0, 0))],
        out_specs=pl.BlockSpec((tm, d), lambda i, f: (i, 0)),
        out_shape=jax.ShapeDtypeStruct((t, d), F32),
        scratch_shapes=[pltpu.VMEM((tm, d), BF16), pltpu.VMEM((tm, d), F32)],
        compiler_params=_cparams("parallel", "arbitrary"),
        name="ffn",
    )(x2, g, mod, mod, mod, w1, w3, w2, final_g)


MOE_TM = 512
MOE_ROWS = 512
R_E1, R_E2, R_RANK1, R_RANK2, R_W1, R_W2 = 0, 1, 2, 3, 4, 5


def _lane_pick(tile, lane, idx):
    return jnp.sum(jnp.where(lane == idx, tile, 0.0), axis=-1, keepdims=True)


def _route_kernel(x_ref, g_ref, sc_ref, sh_ref, wr_ref, br_ref, h_ref, rec_ref, cnt_ref, cnt_scr):
    @pl.when(pl.program_id(0) == 0)
    def _():
        cnt_scr[...] = jnp.zeros_like(cnt_scr)

    h = _modulated_norm(x_ref[...], g_ref[...], sc_ref[0], sh_ref[0])
    half = h.shape[1] // 2
    bits = pltpu.bitcast(h.astype(BF16).astype(F32), jnp.uint32)
    h_ref[...] = (bits[:, half:] & jnp.uint32(0xFFFF0000)) | (bits[:, :half] >> 16)

    h_hi = h.astype(BF16)
    h_lo = (h - h_hi.astype(F32)).astype(BF16)
    wr = wr_ref[...]
    w_hi = wr.astype(BF16)
    w_lo = (wr - w_hi.astype(F32)).astype(BF16)
    logits = _dot(h_hi, w_hi) + _dot(h_hi, w_lo) + _dot(h_lo, w_hi) + br_ref[...]
    lane = lax.broadcasted_iota(I32, logits.shape, 1).astype(F32)
    lg = jnp.where(lane < N_EXPERTS, logits, -jnp.inf)
    m1 = jnp.max(lg, axis=-1, keepdims=True)
    i1 = jnp.min(jnp.where(lg == m1, lane, float(LANES)), axis=-1, keepdims=True)
    lg2 = jnp.where(lane == i1, -jnp.inf, lg)
    m2 = jnp.max(lg2, axis=-1, keepdims=True)
    i2 = jnp.min(jnp.where(lg2 == m2, lane, float(LANES)), axis=-1, keepdims=True)
    e2 = jnp.exp(m2 - m1)
    w_top = 1.0 / (1.0 + e2)

    chosen = jnp.where(lane == i1, 1.0, 0.0) + jnp.where(lane == i2, 1.0, 0.0)
    tm = chosen.shape[0]
    earlier = jnp.where(lax.broadcasted_iota(I32, (tm, tm), 1) < lax.broadcasted_iota(I32, (tm, tm), 0),
                        1.0, 0.0).astype(BF16)
    rank = _dot(earlier, chosen.astype(BF16)) + cnt_scr[0:1, :]
    rec = jnp.zeros_like(logits)
    for slot, val in ((R_E1, i1), (R_E2, i2), (R_RANK1, _lane_pick(rank, lane, i1)),
                      (R_RANK2, _lane_pick(rank, lane, i2)), (R_W1, w_top), (R_W2, e2 * w_top)):
        rec = jnp.where(lane == float(slot), val, rec)
    rec_ref[...] = rec
    cnt_scr[0:1, :] = cnt_scr[0:1, :] + jnp.sum(chosen, axis=0, keepdims=True)
    cnt_ref[...] = cnt_scr[...]


def _route(x2, g, mod, wr_pad, br_pad, s_len):
    t, d = x2.shape
    tm = MOE_TM
    per_b = s_len // tm
    return pl.pallas_call(
        _route_kernel,
        grid=(t // tm,),
        in_specs=[pl.BlockSpec((tm, d), lambda i: (i, 0)),
                  pl.BlockSpec((1, d), lambda i: (0, 0)),
                  pl.BlockSpec((1, 1, d), lambda i: (i // per_b, 0, 4)),
                  pl.BlockSpec((1, 1, d), lambda i: (i // per_b, 0, 3)),
                  pl.BlockSpec((d, LANES), lambda i: (0, 0)),
                  pl.BlockSpec((1, LANES), lambda i: (0, 0))],
        out_specs=[pl.BlockSpec((tm, d // 2), lambda i: (i, 0)),
                   pl.BlockSpec((tm, LANES), lambda i: (i, 0)),
                   pl.BlockSpec((8, LANES), lambda i: (0, 0))],
        out_shape=[jax.ShapeDtypeStruct((t, d // 2), jnp.uint32),
                   jax.ShapeDtypeStruct((t, LANES), F32),
                   jax.ShapeDtypeStruct((8, LANES), F32)],
        scratch_shapes=[pltpu.VMEM((8, LANES), F32)],
        compiler_params=_cparams("arbitrary"),
        name="moe_route",
    )(x2, g, mod, mod, wr_pad, br_pad)


def _row_copy(src_ref, src_row, dst_ref, dst_row, sem):
    return pltpu.make_async_copy(src_ref.at[pl.ds(src_row, 1)], dst_ref.at[pl.ds(dst_row, 1)], sem)


def _dispatch_kernel(pos_ref, h_ref, xs_in_ref, xs_ref, sem):
    del xs_in_ref
    tm = h_ref.shape[0]

    def start(t, carry):
        _row_copy(h_ref, t, xs_ref, pos_ref[0, 0, t], sem).start(priority=0)
        _row_copy(h_ref, t, xs_ref, pos_ref[0, 0, tm + t], sem).start(priority=1)
        return carry

    lax.fori_loop(0, tm, start, 0, unroll=4)
    for _ in range(2):
        pltpu.make_async_copy(h_ref, xs_ref.at[pl.ds(0, tm)], sem).wait()


def _dispatch(pos, h_packed, n_rows):
    t, w = h_packed.shape
    tm = MOE_TM
    xs0 = jnp.zeros((n_rows, w), h_packed.dtype)
    return pl.pallas_call(
        _dispatch_kernel,
        grid=(t // tm,),
        in_specs=[pl.BlockSpec((1, 1, 2 * tm), lambda i: (i, 0, 0), memory_space=pltpu.SMEM),
                  pl.BlockSpec((tm, w), lambda i: (i, 0)),
                  pl.BlockSpec(memory_space=pl.ANY)],
        out_specs=pl.BlockSpec(memory_space=pl.ANY),
        out_shape=jax.ShapeDtypeStruct((n_rows, w), h_packed.dtype),
        scratch_shapes=[pltpu.SemaphoreType.DMA(())],
        input_output_aliases={2: 0},
        compiler_params=_cparams("arbitrary"),
        name="moe_dispatch",
    )(pos, h_packed, xs0)


def _expert_kernel(te_ref, nu_ref, xs_ref, w1_ref, w3_ref, w2_ref, o_ref, h_scr, acc_scr):
    i = pl.program_id(0)
    f = pl.program_id(1)

    @pl.when(i < nu_ref[0])
    def _():
        @pl.when(f == 0)
        def _():
            word = xs_ref[...]
            lo = pltpu.bitcast(word << 16, F32)
            hi = pltpu.bitcast(word & jnp.uint32(0xFFFF0000), F32)
            h_scr[...] = jnp.concatenate([lo, hi], axis=1).astype(BF16)
            acc_scr[...] = jnp.zeros_like(acc_scr)

        acc_scr[...] += _swiglu_partial(h_scr[...], w1_ref[0], w3_ref[0], w2_ref[0])

        @pl.when(f == pl.num_programs(1) - 1)
        def _():
            o_ref[...] = acc_scr[...]

    @pl.when((i >= nu_ref[0]) & (f == pl.num_programs(1) - 1))
    def _():
        o_ref[...] = jnp.zeros_like(o_ref)


def _experts(tile_expert, n_used, xs, w1, w3, w2):
    n_rows, half = xs.shape
    d = 2 * half
    d_ff = w1.shape[2]
    tf = 1792
    nf = d_ff // tf
    tile = lambda i, nu: jnp.minimum(i, nu[0] - 1)
    ff = lambda i, f, nu: jnp.where(i < nu[0], f, nf - 1)
    return pl.pallas_call(
        _expert_kernel,
        grid_spec=pltpu.PrefetchScalarGridSpec(
            num_scalar_prefetch=2,
            grid=(n_rows // MOE_ROWS, nf),
            in_specs=[pl.BlockSpec((MOE_ROWS, half), lambda i, f, te, nu: (tile(i, nu), 0)),
                      pl.BlockSpec((1, d, tf), lambda i, f, te, nu: (te[i], 0, ff(i, f, nu))),
                      pl.BlockSpec((1, d, tf), lambda i, f, te, nu: (te[i], 0, ff(i, f, nu))),
                      pl.BlockSpec((1, tf, d), lambda i, f, te, nu: (te[i], ff(i, f, nu), 0))],
            out_specs=pl.BlockSpec((MOE_ROWS, d), lambda i, f, te, nu: (i, 0)),
            scratch_shapes=[pltpu.VMEM((MOE_ROWS, d), BF16), pltpu.VMEM((MOE_ROWS, d), F32)]),
        out_shape=jax.ShapeDtypeStruct((n_rows, d), F32),
        compiler_params=_cparams("arbitrary", "arbitrary"),
        name="moe_experts",
    )(tile_expert, n_used, xs, w1, w3, w2)


def _combine_kernel(pos_ref, x_ref, gate_ref, rec_ref, fg_ref, ys_ref, o_ref, y_scr, sem, *, final):
    tm = x_ref.shape[0]

    def start(t, carry):
        _row_copy(ys_ref, pos_ref[0, 0, t], y_scr.at[0], t, sem).start(priority=0)
        _row_copy(ys_ref, pos_ref[0, 0, tm + t], y_scr.at[1], t, sem).start(priority=1)
        return carry

    lax.fori_loop(0, tm, start, 0, unroll=4)
    for slot in range(2):
        pltpu.make_async_copy(ys_ref.at[pl.ds(0, tm)], y_scr.at[slot], sem).wait()
    rec = rec_ref[...]
    lane = lax.broadcasted_iota(I32, rec.shape, 1)
    w_first = jnp.sum(jnp.where(lane == R_W1, rec, 0.0), axis=-1, keepdims=True)
    w_second = jnp.sum(jnp.where(lane == R_W2, rec, 0.0), axis=-1, keepdims=True)
    f = w_first * y_scr[0] + w_second * y_scr[1]
    o_ref[...] = _finish(x_ref[...], gate_ref[0], f, fg_ref, final)


def _combine(pos, x2, mod, rec, final_g, ys, s_len, final):
    t, d = x2.shape
    tm = MOE_TM
    per_b = s_len // tm
    kern = functools.partial(_combine_kernel, final=final)
    return pl.pallas_call(
        kern,
        grid=(t // tm,),
        in_specs=[pl.BlockSpec((1, 1, 2 * tm), lambda i: (i, 0, 0), memory_space=pltpu.SMEM),
                  pl.BlockSpec((tm, d), lambda i: (i, 0)),
                  pl.BlockSpec((1, 1, d), lambda i: (i // per_b, 0, 5)),
                  pl.BlockSpec((tm, LANES), lambda i: (i, 0)),
                  pl.BlockSpec((1, d), lambda i: (0, 0)),
                  pl.BlockSpec(memory_space=pl.ANY)],
        out_specs=pl.BlockSpec((tm, d), lambda i: (i, 0)),
        out_shape=jax.ShapeDtypeStruct((t, d), F32),
        scratch_shapes=[pltpu.VMEM((2, tm, d), F32), pltpu.SemaphoreType.DMA(())],
        compiler_params=_cparams("arbitrary"),
        name="moe_combine",
    )(pos, x2, mod, rec, final_g, ys)


def _moe(x2, g, mod, wr_pad, br_pad, w1, w3, w2, final_g, s_len, final):
    t, d = x2.shape
    n_e = w1.shape[0]
    h_packed, rec, cnt = _route(x2, g, mod, wr_pad, br_pad, s_len)

    n_tiles = 2 * t // MOE_ROWS + n_e
    counts = cnt[0, :n_e].astype(I32)
    padded = (counts + MOE_ROWS - 1) // MOE_ROWS * MOE_ROWS
    ends = jnp.cumsum(padded)
    starts = ends - padded
    e1, e2 = rec[:, R_E1].astype(I32), rec[:, R_E2].astype(I32)
    pos1 = starts[e1] + rec[:, R_RANK1].astype(I32)
    pos2 = starts[e2] + rec[:, R_RANK2].astype(I32)
    pos = jnp.concatenate([pos1.reshape(-1, 1, MOE_TM), pos2.reshape(-1, 1, MOE_TM)], axis=2)
    n_used = (ends[-1] // MOE_ROWS).reshape(1)
    tile_start = jnp.minimum(jnp.arange(n_tiles, dtype=I32), n_used[0] - 1) * MOE_ROWS
    tile_expert = jnp.sum(tile_start[:, None] >= ends[None, :], axis=1).astype(I32)

    xs = _dispatch(pos, h_packed, n_tiles * MOE_ROWS)
    ys = _experts(tile_expert, n_used, xs, w1, w3, w2)
    return _combine(pos, x2, mod, rec, final_g, ys, s_len, final)


def _pack_w_in(w):
    d = w.shape[0]
    sizes = (W_A, W_A, W_A, IDX_HEADS * IDX_DIM, IDX_DIM, IDX_HEADS,
             512, 512, 512, 512, 512, 512, 3 * d)
    offs = [0]
    for s in sizes:
        offs.append(offs[-1] + s)
    (qa, ka, va, qi, ki, wi, qb, kb, vb, qc, kc, vc, gl) = [
        w[:, offs[j]:offs[j + 1]] for j in range(len(sizes))]
    scale = HEAD_DIM ** -0.5
    pad_wi = jnp.zeros((d, LANES - IDX_HEADS), w.dtype)
    packed = jnp.concatenate(
        [gl, qa * scale, ka, va,
         qi * (IDX_DIM ** -0.5), ki, ki, wi * (IDX_HEADS ** -0.5), pad_wi,
         qb * scale, kb, vb, qc * scale, kc, vc], axis=1)
    assert packed.shape[1] == PACKED
    return packed.astype(BF16)


def kernel(x, c, w_ada, b_ada, norm1_g, norm2_g, w_in, w_br, w_out, rel_bias, lam_params,
           subln_g, ffn_w1, ffn_w3, ffn_w2, router_w, router_b, moe_w1, moe_w3, moe_w2, final_g):
    bsz, s_len, d = x.shape
    depth = w_ada.shape[0]
    assert s_len % 1024 == 0 and d == 1024 and OFF_G + 3 * d == OFF_QA

    c_pad = jnp.concatenate([c, jnp.zeros((8 - bsz % 8 if bsz % 8 else 0, d), c.dtype)], axis=0)
    mod_all = _ada(c_pad, w_ada, b_ada)
    dsa_tiles, diff_tiles = _bias_tiles(rel_bias)
    fg = final_g.reshape(1, d)

    x2 = x.reshape(bsz * s_len, d)
    for l in range(depth):
        mod = mod_all[l, :bsz].reshape(bsz, 1, 6 * d)
        proj = _inproj(x2, norm1_g[l].reshape(1, d), mod, _pack_w_in(w_in[l]), s_len)
        proj3 = proj.reshape(bsz, s_len, PACKED)
        lam_init = 0.8 - 0.6 * math.exp(-0.3 * l)
        o_a = _dsa(proj3, dsa_tiles, bsz, s_len)
        o_b = _diff(proj3, diff_tiles, lam_params, subln_g, l, bsz, s_len, lam_init)
        o_c = _stick(proj3, bsz, s_len)
        x2 = _merge(o_a.reshape(-1, o_a.shape[-1]), o_b.reshape(-1, o_b.shape[-1]),
                    o_c.reshape(-1, o_c.shape[-1]), proj, x2, mod,
                    w_br[l].astype(BF16), w_out[l].astype(BF16), s_len)
        g2 = norm2_g[l].reshape(1, d)
        final = l == depth - 1
        j = l // 2
        if l % 2 == 0:
            x2 = _ffn(x2, g2, mod, ffn_w1[j].astype(BF16), ffn_w3[j].astype(BF16),
                      ffn_w2[j].astype(BF16), fg, s_len, final)
        else:
            wr_pad = jnp.pad(router_w[j], ((0, 0), (0, LANES - N_EXPERTS)))
            br_pad = jnp.pad(router_b[j], (0, LANES - N_EXPERTS)).reshape(1, LANES)
            x2 = _moe(x2, g2, mod, wr_pad, br_pad, moe_w1[j].astype(BF16), moe_w3[j].astype(BF16),
                      moe_w2[j].astype(BF16), fg, s_len, final)
    return x2.reshape(bsz, s_len, d)
```

```python
import functools
import math

import jax
import jax.numpy as jnp
from jax import lax
from jax.experimental import pallas as pl
from jax.experimental.pallas import tpu as pltpu

F32 = jnp.float32
BF16 = jnp.bfloat16
I32 = jnp.int32
I16 = jnp.int16

LANES = 128
VMEM_LIMIT_BYTES = 56 * 1024 * 1024

CHUNK = 64
A_HEADS = 8
IDX_HEADS = 4
IDX_DIM = 64
TOPK_MAX = 256
B_HEADS = 4
C_HEADS = 8
HEAD_DIM = 64
REL_BUCKETS = 32
FAR_BUCKET = REL_BUCKETS // 2 - 1
N_EXPERTS = 8
EPS = 1e-6
SUBLN_EPS = 1e-5
NEG = -1e30
LOG2E = 1.4426950408889634
INT_MIN = -(2 ** 31)

KEY_BLOCK = 256
Q_TILE = 256
GROUP = 4
BLOCK_GROUPS = (8, 4, 2, 1)

W_A = A_HEADS * HEAD_DIM
OFF_G = 0
OFF_QA, OFF_KA, OFF_VA = 3072, 3584, 4096
OFF_QI, OFF_KI, OFF_WI = 4608, 4864, 4992
OFF_QB, OFF_KB, OFF_VB = 5120, 5632, 6144
OFF_QC, OFF_KC, OFF_VC = 6656, 7168, 7680
PACKED = 8192

LOG_BUCKET_STEPS = (12, 16, 23, 32, 46, 64, 91)


def _nt_dot(a, b):
    return lax.dot_general(a, b, (((1,), (1,)), ((), ())), preferred_element_type=F32)


def _dot(a, b):
    return jnp.dot(a, b, preferred_element_type=F32)


def _cparams(*sem):
    return pltpu.CompilerParams(dimension_semantics=sem, vmem_limit_bytes=VMEM_LIMIT_BYTES)


def _split_heads(x):
    lane = lax.broadcasted_iota(I32, x.shape, 1)
    keep_a = jnp.where(lane < HEAD_DIM, 1.0, 0.0).astype(x.dtype)
    keep_b = jnp.where(lane < HEAD_DIM, 0.0, 1.0).astype(x.dtype)
    return jnp.concatenate([x * keep_a, x * keep_b], axis=0)


def _merge_heads(o, m):
    lane = lax.broadcasted_iota(I32, (m, LANES), 1)
    return jnp.where(lane < HEAD_DIM, o[:m], o[m:])


def _wide(x):
    return jnp.concatenate([x, x], axis=1)


def _tall(x):
    return jnp.concatenate([x, x], axis=0)


def _key_rows(kb):
    return pl.ds(pl.multiple_of(kb * KEY_BLOCK, KEY_BLOCK), KEY_BLOCK)


def _sortable(x):
    bits = pltpu.bitcast(x, I32)
    return bits ^ ((bits >> 31) & 0x7FFFFFFF)


def _chunk_causal():
    r = lax.broadcasted_iota(I32, (Q_TILE, KEY_BLOCK), 0)
    c = lax.broadcasted_iota(I32, (Q_TILE, KEY_BLOCK), 1)
    return (c // CHUNK) <= (r // CHUNK)


def _ada_kernel(c_ref, w_ref, b_ref, o_ref):
    c = c_ref[...]
    a = c * jax.nn.sigmoid(c)
    o_ref[0] = jnp.dot(a, w_ref[0], preferred_element_type=F32,
                       precision=lax.Precision.HIGHEST) + b_ref[0]


def _ada(c_pad, w_ada, b_ada):
    depth, d, n = w_ada.shape
    tn = 1024
    return pl.pallas_call(
        _ada_kernel,
        grid=(depth, n // tn),
        in_specs=[pl.BlockSpec(c_pad.shape, lambda l, j: (0, 0)),
                  pl.BlockSpec((1, d, tn), lambda l, j: (l, 0, j)),
                  pl.BlockSpec((1, 1, tn), lambda l, j: (l, 0, j))],
        out_specs=pl.BlockSpec((1, c_pad.shape[0], tn), lambda l, j: (l, 0, j)),
        out_shape=jax.ShapeDtypeStruct((depth, c_pad.shape[0], n), F32),
        compiler_params=_cparams("parallel", "parallel"),
        name="ada",
    )(c_pad, w_ada, b_ada.reshape(depth, 1, n))


def _rel_bias_tile(tab_ref, head, d0, n_heads_total):
    r = lax.broadcasted_iota(I32, (Q_TILE, KEY_BLOCK), 0)
    c = lax.broadcasted_iota(I32, (Q_TILE, KEY_BLOCK), 1)
    d = c - r + d0
    n = jnp.abs(d)
    large = jnp.full(d.shape, REL_BUCKETS // 4, I32)
    for step in LOG_BUCKET_STEPS:
        large = large + jnp.where(n >= step, 1, 0)
    bucket = jnp.where(d > 0, REL_BUCKETS // 2, 0) + jnp.where(n < REL_BUCKETS // 4, n, large)
    out = jnp.zeros(d.shape, F32)
    for b in range(REL_BUCKETS):
        out = jnp.where(bucket == b, tab_ref[b * n_heads_total + head], out)
    return out - tab_ref[FAR_BUCKET * n_heads_total + head]


def _bias_kernel(tab_ref, dsa_ref, diff_ref):
    p = pl.program_id(0)
    n_heads = A_HEADS + B_HEADS
    dsa_ref[0, 0] = jnp.zeros(dsa_ref.shape[2:], F32)
    diff_ref[0, 0] = jnp.zeros(diff_ref.shape[2:], F32)
    for t, d0 in ((1, -KEY_BLOCK), (2, 0)):
        dsa_ref[0, t, 0:Q_TILE, :] = _rel_bias_tile(tab_ref, 2 * p, d0, n_heads)
        dsa_ref[0, t, Q_TILE:2 * Q_TILE, :] = _rel_bias_tile(tab_ref, 2 * p + 1, d0, n_heads)
        tile = _rel_bias_tile(tab_ref, A_HEADS + p, d0, n_heads)
        if t == 2:
            tile = jnp.where(_chunk_causal(), tile, NEG)
        diff_ref[0, t] = tile


def _bias_tiles(rel_bias):
    tab = rel_bias.reshape(-1)
    return pl.pallas_call(
        _bias_kernel,
        grid=(4,),
        in_specs=[pl.BlockSpec(memory_space=pltpu.SMEM)],
        out_specs=[pl.BlockSpec((1, 3, 2 * Q_TILE, KEY_BLOCK), lambda p: (p, 0, 0, 0)),
                   pl.BlockSpec((1, 3, Q_TILE, KEY_BLOCK), lambda p: (p, 0, 0, 0))],
        out_shape=[jax.ShapeDtypeStruct((A_HEADS // 2, 3, 2 * Q_TILE, KEY_BLOCK), F32),
                   jax.ShapeDtypeStruct((B_HEADS, 3, Q_TILE, KEY_BLOCK), F32)],
        compiler_params=_cparams("parallel"),
        name="rel_bias_tiles",
    )(tab)


def _modulated_norm(x, g, sc, sh):
    y = x * lax.rsqrt(jnp.mean(x * x, axis=-1, keepdims=True) + EPS)
    return y * g * (1.0 + sc) + sh


def _inproj_kernel(x_ref, g_ref, sc_ref, sh_ref, w_ref, o_ref, h_scr):
    @pl.when(pl.program_id(1) == 0)
    def _():
        h_scr[...] = _modulated_norm(x_ref[...], g_ref[...], sc_ref[0], sh_ref[0]).astype(BF16)

    o_ref[...] = _dot(h_scr[...], w_ref[...]).astype(o_ref.dtype)


def _inproj(x2, g, mod, w_packed, s_len):
    t, d = x2.shape
    n = w_packed.shape[1]
    tm, tn = 1024, 2048
    per_b = s_len // tm
    return pl.pallas_call(
        _inproj_kernel,
        grid=(t // tm, n // tn),
        in_specs=[pl.BlockSpec((tm, d), lambda i, j: (i, 0)),
                  pl.BlockSpec((1, d), lambda i, j: (0, 0)),
                  pl.BlockSpec((1, 1, d), lambda i, j: (i // per_b, 0, 1)),
                  pl.BlockSpec((1, 1, d), lambda i, j: (i // per_b, 0, 0)),
                  pl.BlockSpec((d, tn), lambda i, j: (0, j))],
        out_specs=pl.BlockSpec((tm, tn), lambda i, j: (i, j)),
        out_shape=jax.ShapeDtypeStruct((t, n), BF16),
        scratch_shapes=[pltpu.VMEM((tm, d), BF16)],
        compiler_params=_cparams("parallel", "arbitrary"),
        name="inproj",
    )(x2, g, mod, mod, w_packed)


def _for_blocks(lo, hi, fn, sizes=BLOCK_GROUPS):
    pos = lo
    for size in sizes:
        trips = jnp.maximum(hi - pos, 0) // size

        def body(i, carry, pos=pos, size=size):
            fn([pos + i * size + u for u in range(size)])
            return carry

        lax.fori_loop(0, trips, body, 0)
        pos = pos + trips * size


def _fold_blocks(n, fn, acc):
    pos = 0
    for size in BLOCK_GROUPS:
        trips = (n - pos) // size

        def body(i, acc, pos=pos, size=size):
            for u in range(size):
                acc = fn(pos + i * size + u, acc)
            return acc

        acc = lax.fori_loop(0, trips, body, acc)
        pos = pos + trips * size
    return acc


def _softmax_pv(qs, k_at, v_at, qb, add, s_scr, mx_scr, l_scr, acc_scr):
    def pass1(blocks):
        mx = None
        for kb in blocks:
            s = add(kb, _nt_dot(qs, k_at(kb)), jnp.clip(kb - qb + 2, 0, 2))
            s = s * LOG2E
            s_scr[kb] = s
            fold = jnp.maximum(s[:, :LANES], s[:, LANES:])
            mx = fold if mx is None else jnp.maximum(mx, fold)
        mx_scr[...] = jnp.maximum(mx_scr[...], mx)

    mx_scr[...] = jnp.full(mx_scr.shape, NEG, F32)
    _for_blocks(0, qb + 1, pass1)

    m = jnp.max(mx_scr[...], axis=-1, keepdims=True)
    mx_scr[...] = jnp.broadcast_to(m, mx_scr.shape)
    l_scr[...] = jnp.zeros_like(l_scr)
    acc_scr[...] = jnp.zeros_like(acc_scr)

    def pass2(kbs):
        mw = _wide(mx_scr[...])
        l_add = acc_add = None
        for kb in kbs:
            p = jnp.exp2(s_scr[kb] - mw)
            fold = p[:, :LANES] + p[:, LANES:]
            pv = _dot(p.astype(BF16), v_at(kb))
            l_add = fold if l_add is None else l_add + fold
            acc_add = pv if acc_add is None else acc_add + pv
        l_scr[...] += l_add
        acc_scr[...] += acc_add

    _for_blocks(0, qb + 1, pass2)
    return acc_scr[...] / jnp.sum(l_scr[...], axis=-1, keepdims=True)


def _softmax_scratch(s_len):
    return [pltpu.VMEM((s_len // KEY_BLOCK, 2 * Q_TILE, KEY_BLOCK), F32),
            pltpu.VMEM((2 * Q_TILE, LANES), F32),
            pltpu.VMEM((2 * Q_TILE, LANES), F32),
            pltpu.VMEM((2 * Q_TILE, LANES), F32)]


def _dsa_kernel(qa_ref, qi_ref, wi_ref, ka_ref, va_ref, ki_ref, bias_ref, o_ref,
                key_scr, keyt_scr, hi_scr, lo_scr, neg_scr, w_scr, last_scr, s_scr, mx_scr, l_scr, acc_scr,
                *, topk, idx_bits, s_len):
    qb = pl.program_id(1)

    @pl.when(pl.program_id(2) == 0)
    def _():
        _dsa_select(qb, qi_ref, wi_ref, ki_ref, key_scr, keyt_scr, hi_scr, lo_scr, neg_scr, w_scr, last_scr,
                    topk=topk, idx_bits=idx_bits, s_len=s_len)

    pr = pl.program_id(2)
    lanes = pl.ds(pl.multiple_of(pr * LANES, LANES), LANES)

    def add(kb, s, t):
        return s + _tall(neg_scr[kb]) + bias_ref[pr, t]

    o = _softmax_pv(_split_heads(qa_ref[0]),
                    lambda kb: ka_ref[0, _key_rows(kb), lanes],
                    lambda kb: va_ref[0, _key_rows(kb), lanes],
                    qb, add, s_scr, mx_scr, l_scr, acc_scr)
    o_ref[0] = _merge_heads(o, Q_TILE).astype(o_ref.dtype)


def _dsa_select(qb, qi_ref, wi_ref, ki_ref, key_scr, keyt_scr, hi_scr, lo_scr, neg_scr, w_scr, last_scr,
                *, topk, idx_bits, s_len):
    nkb = qb + 1
    kf = float(topk)
    valid_diag = _chunk_causal()

    def cols(kb):
        return kb * KEY_BLOCK + lax.broadcasted_iota(I32, (Q_TILE, KEY_BLOCK), 1)

    qi = qi_ref[0]
    q_heads = [_split_heads(qi[:, pr * LANES:(pr + 1) * LANES]) for pr in range(IDX_HEADS // 2)]
    wi = wi_ref[0].astype(F32)
    for h in range(IDX_HEADS):
        w_scr[h] = jnp.broadcast_to(wi[:, h:h + 1], (Q_TILE, LANES))

    def score_block(kb, valid):
        kk = ki_ref[0, _key_rows(kb), :]
        sc = jnp.zeros((Q_TILE, KEY_BLOCK), F32)
        for pr in range(IDX_HEADS // 2):
            d = _nt_dot(q_heads[pr], kk)
            sc = sc + _wide(w_scr[2 * pr]) * jnp.maximum(d[:Q_TILE], 0.0)
            sc = sc + _wide(w_scr[2 * pr + 1]) * jnp.maximum(d[Q_TILE:], 0.0)
        sc = jnp.where(sc == 0.0, 0.0, sc)
        if valid is not None:
            sc = jnp.where(valid, sc, -jnp.inf)
        key_scr[kb] = _sortable(sc)
        key_t = _sortable(sc.T)
        keyt_scr[kb] = key_t
        hi_scr[kb] = (key_t >> 16).astype(I16)
        lo_scr[kb] = ((key_t & 0xFFFF) - 2 ** 15).astype(I16)

    _for_blocks(0, qb, lambda kbs: [score_block(kb, None) for kb in kbs], sizes=(4, 2, 1))
    score_block(qb, valid_diag)

    def per_query(x):
        return jnp.broadcast_to(x[None], (KEY_BLOCK // 8, 8, Q_TILE)).reshape(KEY_BLOCK, Q_TILE)

    def key_index(kb):
        return kb * KEY_BLOCK + lax.broadcasted_iota(I32, (KEY_BLOCK, Q_TILE), 0)

    def count(pred):
        def one(kb, acc):
            hit = jnp.where(pred(keyt_scr[kb], kb), 1.0, 0.0)
            return acc + jnp.sum(hit.reshape(KEY_BLOCK // 8, 8, Q_TILE), axis=0)

        acc = _fold_blocks(nkb, one, jnp.zeros((8, Q_TILE), F32))
        return jnp.broadcast_to(jnp.sum(acc, axis=0, keepdims=True), (8, Q_TILE))

    def per_query16(x):
        x16 = x.astype(I16)
        return jnp.broadcast_to(x16[None], (KEY_BLOCK // 16, 16, Q_TILE)).reshape(KEY_BLOCK, Q_TILE)

    def count16(ref, pred):
        def one(kb, acc):
            hit = jnp.where(pred(ref[kb]), jnp.int16(1), jnp.int16(0))
            parts = [hit[g:g + 16] for g in range(0, KEY_BLOCK, 16)]
            while len(parts) > 1:
                parts = [parts[i] + parts[i + 1] for i in range(0, len(parts), 2)]
            return acc + parts[0]

        acc = _fold_blocks(nkb, one, jnp.zeros((16, Q_TILE), I16))
        return jnp.broadcast_to(jnp.sum(acc.astype(F32), axis=0, keepdims=True), (16, Q_TILE))

    hi_thr = jnp.where(count16(hi_scr, lambda h: h >= 0) >= kf,
                       jnp.zeros((16, Q_TILE), I32), jnp.full((16, Q_TILE), -(2 ** 15), I32))

    def hi_body(j, t):
        cand = t | lax.shift_left(jnp.int32(1), 14 - j)
        cw = per_query16(cand)
        return jnp.where(count16(hi_scr, lambda h: h >= cw) >= kf, cand, t)

    hi_thr = lax.fori_loop(0, 15, hi_body, hi_thr)
    hi_w = per_query16(hi_thr)
    above = count16(hi_scr, lambda h: h > hi_w)

    def keep_ties(kb, carry):
        lo_scr[kb] = jnp.where(hi_scr[kb] == hi_w, lo_scr[kb], jnp.int16(-(2 ** 15)))
        return carry

    lax.fori_loop(0, nkb, keep_ties, 0)

    def lo_body(j, t):
        cand = t | lax.shift_left(jnp.int32(1), 15 - j)
        cw = per_query16(cand - 2 ** 15)
        return jnp.where(above + count16(lo_scr, lambda lo: lo >= cw) >= kf, cand, t)

    lo_thr = lax.fori_loop(0, 16, lo_body, jnp.zeros((16, Q_TILE), I32))
    thr = (lax.shift_left(hi_thr, 16) | lo_thr)[0:8]
    thr_t = per_query(thr)

    need = kf - count(lambda key, kb: key > thr_t)
    ties = count(lambda key, kb: key == thr_t)
    last_scr[...] = jnp.full(last_scr.shape, s_len, I32)

    @pl.when(jnp.max(ties - need) > 0.0)
    def _():
        def tie_body(j, last):
            cand = last | lax.shift_left(jnp.int32(1), idx_bits - 1 - j)
            cw = per_query(cand)
            taken = count(lambda key, kb: (key == thr_t) & (key_index(kb) < cw))
            return jnp.where(taken < need, cand, last)

        last_scr[...] = lax.fori_loop(0, idx_bits, tie_body, jnp.zeros((8, Q_TILE), I32))

    def per_row(x):
        return _wide(jnp.broadcast_to(x[0:1], (LANES, Q_TILE)).T)

    tw = per_row(thr)
    lastw = per_row(last_scr[...])

    def mask_block(kb, valid):
        key = key_scr[kb]
        tie_neg = jnp.where(key == tw, jnp.where(cols(kb) <= lastw, 0.0, NEG), NEG)
        neg = jnp.where(key > tw, 0.0, tie_neg)
        if valid is not None:
            neg = jnp.where(valid, neg, NEG)
        neg_scr[kb] = neg

    def mask_body(kb, carry):
        mask_block(kb, None)
        return carry

    lax.fori_loop(0, qb, mask_body, 0)
    mask_block(qb, valid_diag)


def _dsa(proj, bias_tiles, bsz, s_len):
    topk = min(TOPK_MAX, s_len // 4)
    idx_bits = max(1, (s_len - 1).bit_length())
    n_blocks = s_len // KEY_BLOCK
    assert 2 * n_blocks <= 256
    kern = functools.partial(_dsa_kernel, topk=topk, idx_bits=idx_bits, s_len=s_len)
    lane_blk = lambda off: off // LANES
    return pl.pallas_call(
        kern,
        grid=(bsz, s_len // Q_TILE, A_HEADS // 2),
        in_specs=[pl.BlockSpec((1, Q_TILE, LANES), lambda b, i, p: (b, i, lane_blk(OFF_QA) + p)),
                  pl.BlockSpec((1, Q_TILE, 256), lambda b, i, p: (b, i, OFF_QI // 256)),
                  pl.BlockSpec((1, Q_TILE, LANES), lambda b, i, p: (b, i, lane_blk(OFF_WI))),
                  pl.BlockSpec((1, s_len, W_A), lambda b, i, p: (b, 0, OFF_KA // W_A),
                               pipeline_mode=pl.Buffered(1)),
                  pl.BlockSpec((1, s_len, W_A), lambda b, i, p: (b, 0, OFF_VA // W_A),
                               pipeline_mode=pl.Buffered(1)),
                  pl.BlockSpec((1, s_len, LANES), lambda b, i, p: (b, 0, lane_blk(OFF_KI))),
                  pl.BlockSpec(bias_tiles.shape, lambda b, i, p: (0, 0, 0, 0),
                               pipeline_mode=pl.Buffered(1))],
        out_specs=pl.BlockSpec((1, Q_TILE, LANES), lambda b, i, p: (b, i, p)),
        out_shape=jax.ShapeDtypeStruct((bsz, s_len, W_A), BF16),
        scratch_shapes=[pltpu.VMEM((n_blocks, Q_TILE, KEY_BLOCK), I32),
                        pltpu.VMEM((n_blocks, KEY_BLOCK, Q_TILE), I32),
                        pltpu.VMEM((n_blocks, KEY_BLOCK, Q_TILE), I16),
                        pltpu.VMEM((n_blocks, KEY_BLOCK, Q_TILE), I16),
                        pltpu.VMEM((n_blocks, Q_TILE, KEY_BLOCK), F32),
                        pltpu.VMEM((IDX_HEADS, Q_TILE, LANES), F32),
                        pltpu.VMEM((8, Q_TILE), I32)] + _softmax_scratch(s_len),
        compiler_params=_cparams("parallel", "arbitrary", "arbitrary"),
        name="dsa",
    )(proj, proj, proj, proj, proj, proj, bias_tiles)


def _diff_kernel(q_ref, k_ref, v_ref, bias_ref, lp_ref, g_ref, o_ref,
                 s_scr, mx_scr, l_scr, acc_scr, *, lam_init):
    qb = pl.program_id(2)
    qs = _split_heads(q_ref[0])

    def add(kb, s, t):
        return s + _tall(bias_ref[0, t])

    o = _softmax_pv(qs, lambda kb: k_ref[0, _key_rows(kb), :], lambda kb: v_ref[0, _key_rows(kb), :],
                    qb, add, s_scr, mx_scr, l_scr, acc_scr)
    lp = lp_ref[0]
    lam = (jnp.exp(jnp.sum(lp[0:1] * lp[1:2], axis=-1, keepdims=True))
           - jnp.exp(jnp.sum(lp[2:3] * lp[3:4], axis=-1, keepdims=True)) + lam_init)
    o = o[:Q_TILE] - lam * o[Q_TILE:]
    y = o * lax.rsqrt(jnp.mean(o * o, axis=-1, keepdims=True) + SUBLN_EPS)
    o_ref[0] = (y * g_ref[0] * (1.0 - lam_init)).astype(o_ref.dtype)


def _diff(proj, bias_tiles, lam_params, subln_g, layer, bsz, s_len, lam_init):
    kern = functools.partial(_diff_kernel, lam_init=lam_init)
    return pl.pallas_call(
        kern,
        grid=(bsz, B_HEADS, s_len // Q_TILE),
        in_specs=[pl.BlockSpec((1, Q_TILE, LANES), lambda b, h, i: (b, i, OFF_QB // LANES + h)),
                  pl.BlockSpec((1, s_len, LANES), lambda b, h, i: (b, 0, OFF_KB // LANES + h)),
                  pl.BlockSpec((1, s_len, LANES), lambda b, h, i: (b, 0, OFF_VB // LANES + h)),
                  pl.BlockSpec((1, 3, Q_TILE, KEY_BLOCK), lambda b, h, i: (h, 0, 0, 0)),
                  pl.BlockSpec((1, 4, HEAD_DIM), lambda b, h, i: (layer, 0, 0)),
                  pl.BlockSpec((1, 1, 2 * HEAD_DIM), lambda b, h, i: (layer, 0, 0))],
        out_specs=pl.BlockSpec((1, Q_TILE, LANES), lambda b, h, i: (b, i, h)),
        out_shape=jax.ShapeDtypeStruct((bsz, s_len, B_HEADS * 2 * HEAD_DIM), BF16),
        scratch_shapes=_softmax_scratch(s_len),
        compiler_params=_cparams("parallel", "parallel", "arbitrary"),
        name="diff_attn",
    )(proj, proj, proj, bias_tiles, lam_params, subln_g.reshape(-1, 1, 2 * HEAD_DIM))


def _stick_kernel(q_ref, k_ref, v_ref, o_ref, later_scr, run_scr, acc_scr):
    qb = pl.program_id(2)
    qs = _split_heads(q_ref[0])
    jj = lax.broadcasted_iota(I32, (2 * KEY_BLOCK, KEY_BLOCK), 0) & (KEY_BLOCK - 1)
    ss = lax.broadcasted_iota(I32, (2 * KEY_BLOCK, KEY_BLOCK), 1)
    later_scr[...] = jnp.where(jj > ss, 1.0, 0.0).astype(BF16)
    run_scr[...] = jnp.zeros_like(run_scr)
    acc_scr[...] = jnp.zeros_like(acc_scr)

    def sweep(kbs, diagonal):
        parts = []
        for j, kb in enumerate(kbs):
            z = _nt_dot(qs, k_ref[0, _key_rows(kb), :])
            soft = jnp.log(1.0 + jnp.exp2(jnp.abs(z) * (-LOG2E)))
            log_beta = jnp.minimum(z, 0.0) - soft
            log_1mb = log_beta - z
            strict = None
            if diagonal and j == 0:
                t = lax.broadcasted_iota(I32, z.shape, 0) & (Q_TILE - 1)
                strict = lax.broadcasted_iota(I32, z.shape, 1) < t
                log_1mb = jnp.where(strict, log_1mb, 0.0)
            hi = log_1mb.astype(BF16)
            lo = (log_1mb - hi.astype(F32)).astype(BF16)
            between = _dot(jnp.concatenate([hi, lo], axis=1), later_scr[...])
            total = between[:, 0:1] + log_1mb[:, 0:1]
            parts.append((kb, log_beta + between, total, strict))
        run = run_scr[...]
        acc = None
        for kb, logit, total, strict in parts:
            w = jnp.exp(logit + _wide(run))
            if strict is not None:
                w = jnp.where(strict, w, 0.0)
            pv = _dot(w.astype(BF16), v_ref[0, _key_rows(kb), :])
            acc = pv if acc is None else acc + pv
            run = run + jnp.broadcast_to(total, run.shape)
        acc_scr[...] += acc
        run_scr[...] = run

    first = jnp.where(qb + 1 >= GROUP, GROUP, jnp.where(qb + 1 >= 2, 2, 1))
    for size in (GROUP, 2, 1):
        @pl.when(first == size)
        def _(size=size):
            sweep([qb - u for u in range(size)], True)
    top = qb - first
    for size in BLOCK_GROUPS:
        trips = (top + 1) // size

        def body(i, carry, top=top, size=size):
            sweep([top - i * size - u for u in range(size)], False)
            return carry

        lax.fori_loop(0, trips, body, 0)
        top = top - trips * size
    o_ref[0] = _merge_heads(acc_scr[...], Q_TILE).astype(o_ref.dtype)


def _stick(proj, bsz, s_len):
    return pl.pallas_call(
        _stick_kernel,
        grid=(bsz, C_HEADS // 2, s_len // Q_TILE),
        in_specs=[pl.BlockSpec((1, Q_TILE, LANES), lambda b, h, i: (b, i, OFF_QC // LANES + h)),
                  pl.BlockSpec((1, s_len, LANES), lambda b, h, i: (b, 0, OFF_KC // LANES + h)),
                  pl.BlockSpec((1, s_len, LANES), lambda b, h, i: (b, 0, OFF_VC // LANES + h))],
        out_specs=pl.BlockSpec((1, Q_TILE, LANES), lambda b, h, i: (b, i, h)),
        out_shape=jax.ShapeDtypeStruct((bsz, s_len, C_HEADS * HEAD_DIM), BF16),
        scratch_shapes=[pltpu.VMEM((2 * KEY_BLOCK, KEY_BLOCK), BF16),
                        pltpu.VMEM((2 * Q_TILE, LANES), F32),
                        pltpu.VMEM((2 * Q_TILE, LANES), F32)],
        compiler_params=_cparams("parallel", "parallel", "arbitrary"),
        name="stick_attn",
    )(proj, proj, proj)


def _merge_kernel(oa_ref, ob_ref, oc_ref, gate_ref, x_ref, g1_ref, wbr_ref, wout_ref, o_ref):
    d = x_ref.shape[-1]
    merged = jnp.zeros(x_ref.shape, F32)
    for j, o_br in enumerate((oa_ref, ob_ref, oc_ref)):
        w = o_br.shape[-1]
        gate = jax.nn.sigmoid(gate_ref[:, j * d:(j + 1) * d].astype(F32))
        merged = merged + gate * _dot(o_br[...], wbr_ref[j * w:(j + 1) * w, :])
    o_ref[...] = x_ref[...] + g1_ref[0] * _dot(merged.astype(BF16), wout_ref[...])


def _merge(o_a, o_b, o_c, proj, x2, mod, w_br, w_out, s_len):
    t, d = x2.shape
    tm = 512
    per_b = s_len // tm
    w = o_a.shape[-1]
    return pl.pallas_call(
        _merge_kernel,
        grid=(t // tm,),
        in_specs=[pl.BlockSpec((tm, w), lambda i: (i, 0)),
                  pl.BlockSpec((tm, w), lambda i: (i, 0)),
                  pl.BlockSpec((tm, w), lambda i: (i, 0)),
                  pl.BlockSpec((tm, 3 * d), lambda i: (i, OFF_G // (3 * d))),
                  pl.BlockSpec((tm, d), lambda i: (i, 0)),
                  pl.BlockSpec((1, 1, d), lambda i: (i // per_b, 0, 2)),
                  pl.BlockSpec(w_br.shape, lambda i: (0, 0)),
                  pl.BlockSpec(w_out.shape, lambda i: (0, 0))],
        out_specs=pl.BlockSpec((tm, d), lambda i: (i, 0)),
        out_shape=jax.ShapeDtypeStruct((t, d), F32),
        compiler_params=_cparams("parallel"),
        name="merge",
    )(o_a, o_b, o_c, proj, x2, mod, w_br, w_out)


def _finish(x, gate, acc, fg_ref, final):
    y = x + gate * acc
    if final:
        y = y * lax.rsqrt(jnp.mean(y * y, axis=-1, keepdims=True) + EPS) * fg_ref[...]
    return y


def _swiglu_partial(h, w1, w3, w2):
    a = _dot(h, w1)
    act = (a * jax.nn.sigmoid(a)) * _dot(h, w3)
    return _dot(act.astype(BF16), w2)


def _ffn_kernel(x_ref, g_ref, sc_ref, sh_ref, gate_ref, w1_ref, w3_ref, w2_ref, fg_ref, o_ref,
                h_scr, acc_scr, *, final):
    f = pl.program_id(1)

    @pl.when(f == 0)
    def _():
        h_scr[...] = _modulated_norm(x_ref[...], g_ref[...], sc_ref[0], sh_ref[0]).astype(BF16)
        acc_scr[...] = jnp.zeros_like(acc_scr)

    acc_scr[...] += _swiglu_partial(h_scr[...], w1_ref[...], w3_ref[...], w2_ref[...])

    @pl.when(f == pl.num_programs(1) - 1)
    def _():
        o_ref[...] = _finish(x_ref[...], gate_ref[0], acc_scr[...], fg_ref, final)


def _ffn(x2, g, mod, w1, w3, w2, final_g, s_len, final):
    t, d = x2.shape
    d_ff = w1.shape[1]
    tm, tf = 512, d_ff
    once = dict(pipeline_mode=pl.Buffered(1))
    per_b = s_len // tm
    kern = functools.partial(_ffn_kernel, final=final)
    return pl.pallas_call(
        kern,
        grid=(t // tm, d_ff // tf),
        in_specs=[pl.BlockSpec((tm, d), lambda i, f: (i, 0)),
                  pl.BlockSpec((1, d), lambda i, f: (0, 0)),
                  pl.BlockSpec((1, 1, d), lambda i, f: (i // per_b, 0, 4)),
                  pl.BlockSpec((1, 1, d), lambda i, f: (i // per_b, 0, 3)),
                  pl.BlockSpec((1, 1, d), lambda i, f: (i // per_b, 0, 5)),
                  pl.BlockSpec((d, tf), lambda i, f: (0, f), **once),
                  pl.BlockSpec((d, tf), lambda i, f: (0, f), **once),
                  pl.BlockSpec((tf, d), lambda i, f: (f, 0), **once),
                  pl.BlockSpec((1, d), lambda i, f: (0, 0))],
        out_specs=pl.BlockSpec((tm, d), lambda i, f: (i, 0)),
        out_shape=jax.ShapeDtypeStruct((t, d), F32),
        scratch_shapes=[pltpu.VMEM((tm, d), BF16), pltpu.VMEM((tm, d), F32)],
        compiler_params=_cparams("parallel", "arbitrary"),
        name="ffn",
    )(x2, g, mod, mod, mod, w1, w3, w2, final_g)


MOE_TM = 512
MOE_ROWS = 512
R_E1, R_E2, R_RANK1, R_RANK2, R_W1, R_W2 = 0, 1, 2, 3, 4, 5


def _lane_pick(tile, lane, idx):
    return jnp.sum(jnp.where(lane == idx, tile, 0.0), axis=-1, keepdims=True)


def _route_kernel(x_ref, g_ref, sc_ref, sh_ref, wr_ref, br_ref, h_ref, rec_ref, cnt_ref, cnt_scr):
    @pl.when(pl.program_id(0) == 0)
    def _():
        cnt_scr[...] = jnp.zeros_like(cnt_scr)

    h = _modulated_norm(x_ref[...], g_ref[...], sc_ref[0], sh_ref[0])
    half = h.shape[1] // 2
    bits = pltpu.bitcast(h.astype(BF16).astype(F32), jnp.uint32)
    h_ref[...] = (bits[:, half:] & jnp.uint32(0xFFFF0000)) | (bits[:, :half] >> 16)

    h_hi = h.astype(BF16)
    h_lo = (h - h_hi.astype(F32)).astype(BF16)
    wr = wr_ref[...]
    w_hi = wr.astype(BF16)
    w_lo = (wr - w_hi.astype(F32)).astype(BF16)
    logits = _dot(h_hi, w_hi) + _dot(h_hi, w_lo) + _dot(h_lo, w_hi) + br_ref[...]
    lane = lax.broadcasted_iota(I32, logits.shape, 1).astype(F32)
    lg = jnp.where(lane < N_EXPERTS, logits, -jnp.inf)
    m1 = jnp.max(lg, axis=-1, keepdims=True)
    i1 = jnp.min(jnp.where(lg == m1, lane, float(LANES)), axis=-1, keepdims=True)
    lg2 = jnp.where(lane == i1, -jnp.inf, lg)
    m2 = jnp.max(lg2, axis=-1, keepdims=True)
    i2 = jnp.min(jnp.where(lg2 == m2, lane, float(LANES)), axis=-1, keepdims=True)
    e2 = jnp.exp(m2 - m1)
    w_top = 1.0 / (1.0 + e2)

    chosen = jnp.where(lane == i1, 1.0, 0.0) + jnp.where(lane == i2, 1.0, 0.0)
    tm = chosen.shape[0]
    earlier = jnp.where(lax.broadcasted_iota(I32, (tm, tm), 1) < lax.broadcasted_iota(I32, (tm, tm), 0),
                        1.0, 0.0).astype(BF16)
    rank = _dot(earlier, chosen.astype(BF16)) + cnt_scr[0:1, :]
    rec = jnp.zeros_like(logits)
    for slot, val in ((R_E1, i1), (R_E2, i2), (R_RANK1, _lane_pick(rank, lane, i1)),
                      (R_RANK2, _lane_pick(rank, lane, i2)), (R_W1, w_top), (R_W2, e2 * w_top)):
        rec = jnp.where(lane == float(slot), val, rec)
    rec_ref[...] = rec
    cnt_scr[0:1, :] = cnt_scr[0:1, :] + jnp.sum(chosen, axis=0, keepdims=True)
    cnt_ref[...] = cnt_scr[...]


def _route(x2, g, mod, wr_pad, br_pad, s_len):
    t, d = x2.shape
    tm = MOE_TM
    per_b = s_len // tm
    return pl.pallas_call(
        _route_kernel,
        grid=(t // tm,),
        in_specs=[pl.BlockSpec((tm, d), lambda i: (i, 0)),
                  pl.BlockSpec((1, d), lambda i: (0, 0)),
                  pl.BlockSpec((1, 1, d), lambda i: (i // per_b, 0, 4)),
                  pl.BlockSpec((1, 1, d), lambda i: (i // per_b, 0, 3)),
                  pl.BlockSpec((d, LANES), lambda i: (0, 0)),
                  pl.BlockSpec((1, LANES), lambda i: (0, 0))],
        out_specs=[pl.BlockSpec((tm, d // 2), lambda i: (i, 0)),
                   pl.BlockSpec((tm, LANES), lambda i: (i, 0)),
                   pl.BlockSpec((8, LANES), lambda i: (0, 0))],
        out_shape=[jax.ShapeDtypeStruct((t, d // 2), jnp.uint32),
                   jax.ShapeDtypeStruct((t, LANES), F32),
                   jax.ShapeDtypeStruct((8, LANES), F32)],
        scratch_shapes=[pltpu.VMEM((8, LANES), F32)],
        compiler_params=_cparams("arbitrary"),
        name="moe_route",
    )(x2, g, mod, mod, wr_pad, br_pad)


def _row_copy(src_ref, src_row, dst_ref, dst_row, sem):
    return pltpu.make_async_copy(src_ref.at[pl.ds(src_row, 1)], dst_ref.at[pl.ds(dst_row, 1)], sem)


def _dispatch_kernel(pos_ref, h_ref, xs_in_ref, xs_ref, sem):
    del xs_in_ref
    tm = h_ref.shape[0]

    def start(t, carry):
        _row_copy(h_ref, t, xs_ref, pos_ref[0, 0, t], sem).start(priority=0)
        _row_copy(h_ref, t, xs_ref, pos_ref[0, 0, tm + t], sem).start(priority=1)
        return carry

    lax.fori_loop(0, tm, start, 0, unroll=4)
    for _ in range(2):
        pltpu.make_async_copy(h_ref, xs_ref.at[pl.ds(0, tm)], sem).wait()


def _dispatch(pos, h_packed, n_rows):
    t, w = h_packed.shape
    tm = MOE_TM
    xs0 = jnp.zeros((n_rows, w), h_packed.dtype)
    return pl.pallas_call(
        _dispatch_kernel,
        grid=(t // tm,),
        in_specs=[pl.BlockSpec((1, 1, 2 * tm), lambda i: (i, 0, 0), memory_space=pltpu.SMEM),
                  pl.BlockSpec((tm, w), lambda i: (i, 0)),
                  pl.BlockSpec(memory_space=pl.ANY)],
        out_specs=pl.BlockSpec(memory_space=pl.ANY),
        out_shape=jax.ShapeDtypeStruct((n_rows, w), h_packed.dtype),
        scratch_shapes=[pltpu.SemaphoreType.DMA(())],
        input_output_aliases={2: 0},
        compiler_params=_cparams("arbitrary"),
        name="moe_dispatch",
    )(pos, h_packed, xs0)


def _expert_kernel(te_ref, nu_ref, xs_ref, w1_ref, w3_ref, w2_ref, o_ref, h_scr, acc_scr):
    i = pl.program_id(0)
    f = pl.program_id(1)

    @pl.when(i < nu_ref[0])
    def _():
        @pl.when(f == 0)
        def _():
            word = xs_ref[...]
            lo = pltpu.bitcast(word << 16, F32)
            hi = pltpu.bitcast(word & jnp.uint32(0xFFFF0000), F32)
            h_scr[...] = jnp.concatenate([lo, hi], axis=1).astype(BF16)
            acc_scr[...] = jnp.zeros_like(acc_scr)

        acc_scr[...] += _swiglu_partial(h_scr[...], w1_ref[0], w3_ref[0], w2_ref[0])

        @pl.when(f == pl.num_programs(1) - 1)
        def _():
            o_ref[...] = acc_scr[...]

    @pl.when((i >= nu_ref[0]) & (f == pl.num_programs(1) - 1))
    def _():
        o_ref[...] = jnp.zeros_like(o_ref)


def _experts(tile_expert, n_used, xs, w1, w3, w2):
    n_rows, half = xs.shape
    d = 2 * half
    d_ff = w1.shape[2]
    tf = 1792
    nf = d_ff // tf
    tile = lambda i, nu: jnp.minimum(i, nu[0] - 1)
    ff = lambda i, f, nu: jnp.where(i < nu[0], f, nf - 1)
    return pl.pallas_call(
        _expert_kernel,
        grid_spec=pltpu.PrefetchScalarGridSpec(
            num_scalar_prefetch=2,
            grid=(n_rows // MOE_ROWS, nf),
            in_specs=[pl.BlockSpec((MOE_ROWS, half), lambda i, f, te, nu: (tile(i, nu), 0)),
                      pl.BlockSpec((1, d, tf), lambda i, f, te, nu: (te[i], 0, ff(i, f, nu))),
                      pl.BlockSpec((1, d, tf), lambda i, f, te, nu: (te[i], 0, ff(i, f, nu))),
                      pl.BlockSpec((1, tf, d), lambda i, f, te, nu: (te[i], ff(i, f, nu), 0))],
            out_specs=pl.BlockSpec((MOE_ROWS, d), lambda i, f, te, nu: (i, 0)),
            scratch_shapes=[pltpu.VMEM((MOE_ROWS, d), BF16), pltpu.VMEM((MOE_ROWS, d), F32)]),
        out_shape=jax.ShapeDtypeStruct((n_rows, d), F32),
        compiler_params=_cparams("arbitrary", "arbitrary"),
        name="moe_experts",
    )(tile_expert, n_used, xs, w1, w3, w2)


def _combine_kernel(pos_ref, x_ref, gate_ref, rec_ref, fg_ref, ys_ref, o_ref, y_scr, sem, *, final):
    tm = x_ref.shape[0]

    def start(t, carry):
        _row_copy(ys_ref, pos_ref[0, 0, t], y_scr.at[0], t, sem).start(priority=0)
        _row_copy(ys_ref, pos_ref[0, 0, tm + t], y_scr.at[1], t, sem).start(priority=1)
        return carry

    lax.fori_loop(0, tm, start, 0, unroll=4)
    for slot in range(2):
        pltpu.make_async_copy(ys_ref.at[pl.ds(0, tm)], y_scr.at[slot], sem).wait()
    rec = rec_ref[...]
    lane = lax.broadcasted_iota(I32, rec.shape, 1)
    w_first = jnp.sum(jnp.where(lane == R_W1, rec, 0.0), axis=-1, keepdims=True)
    w_second = jnp.sum(jnp.where(lane == R_W2, rec, 0.0), axis=-1, keepdims=True)
    f = w_first * y_scr[0] + w_second * y_scr[1]
    o_ref[...] = _finish(x_ref[...], gate_ref[0], f, fg_ref, final)


def _combine(pos, x2, mod, rec, final_g, ys, s_len, final):
    t, d = x2.shape
    tm = MOE_TM
    per_b = s_len // tm
    kern = functools.partial(_combine_kernel, final=final)
    return pl.pallas_call(
        kern,
        grid=(t // tm,),
        in_specs=[pl.BlockSpec((1, 1, 2 * tm), lambda i: (i, 0, 0), memory_space=pltpu.SMEM),
                  pl.BlockSpec((tm, d), lambda i: (i, 0)),
                  pl.BlockSpec((1, 1, d), lambda i: (i // per_b, 0, 5)),
                  pl.BlockSpec((tm, LANES), lambda i: (i, 0)),
                  pl.BlockSpec((1, d), lambda i: (0, 0)),
                  pl.BlockSpec(memory_space=pl.ANY)],
        out_specs=pl.BlockSpec((tm, d), lambda i: (i, 0)),
        out_shape=jax.ShapeDtypeStruct((t, d), F32),
        scratch_shapes=[pltpu.VMEM((2, tm, d), F32), pltpu.SemaphoreType.DMA(())],
        compiler_params=_cparams("arbitrary"),
        name="moe_combine",
    )(pos, x2, mod, rec, final_g, ys)


def _moe(x2, g, mod, wr_pad, br_pad, w1, w3, w2, final_g, s_len, final):
    t, d = x2.shape
    n_e = w1.shape[0]
    h_packed, rec, cnt = _route(x2, g, mod, wr_pad, br_pad, s_len)

    n_tiles = 2 * t // MOE_ROWS + n_e
    counts = cnt[0, :n_e].astype(I32)
    padded = (counts + MOE_ROWS - 1) // MOE_ROWS * MOE_ROWS
    ends = jnp.cumsum(padded)
    starts = ends - padded
    e1, e2 = rec[:, R_E1].astype(I32), rec[:, R_E2].astype(I32)
    pos1 = starts[e1] + rec[:, R_RANK1].astype(I32)
    pos2 = starts[e2] + rec[:, R_RANK2].astype(I32)
    pos = jnp.concatenate([pos1.reshape(-1, 1, MOE_TM), pos2.reshape(-1, 1, MOE_TM)], axis=2)
    n_used = (ends[-1] // MOE_ROWS).reshape(1)
    tile_start = jnp.minimum(jnp.arange(n_tiles, dtype=I32), n_used[0] - 1) * MOE_ROWS
    tile_expert = jnp.sum(tile_start[:, None] >= ends[None, :], axis=1).astype(I32)

    xs = _dispatch(pos, h_packed, n_tiles * MOE_ROWS)
    ys = _experts(tile_expert, n_used, xs, w1, w3, w2)
    return _combine(pos, x2, mod, rec, final_g, ys, s_len, final)


def _pack_w_in(w):
    d = w.shape[0]
    sizes = (W_A, W_A, W_A, IDX_HEADS * IDX_DIM, IDX_DIM, IDX_HEADS,
             512, 512, 512, 512, 512, 512, 3 * d)
    offs = [0]
    for s in sizes:
        offs.append(offs[-1] + s)
    (qa, ka, va, qi, ki, wi, qb, kb, vb, qc, kc, vc, gl) = [
        w[:, offs[j]:offs[j + 1]] for j in range(len(sizes))]
    scale = HEAD_DIM ** -0.5
    pad_wi = jnp.zeros((d, LANES - IDX_HEADS), w.dtype)
    packed = jnp.concatenate(
        [gl, qa * scale, ka, va,
         qi * (IDX_DIM ** -0.5), ki, ki, wi * (IDX_HEADS ** -0.5), pad_wi,
         qb * scale, kb, vb, qc * scale, kc, vc], axis=1)
    assert packed.shape[1] == PACKED
    return packed.astype(BF16)


def kernel(x, c, w_ada, b_ada, norm1_g, norm2_g, w_in, w_br, w_out, rel_bias, lam_params,
           subln_g, ffn_w1, ffn_w3, ffn_w2, router_w, router_b, moe_w1, moe_w3, moe_w2, final_g):
    bsz, s_len, d = x.shape
    depth = w_ada.shape[0]
    assert s_len % 1024 == 0 and d == 1024 and OFF_G + 3 * d == OFF_QA

    c_pad = jnp.concatenate([c, jnp.zeros((8 - bsz % 8 if bsz % 8 else 0, d), c.dtype)], axis=0)
    mod_all = _ada(c_pad, w_ada, b_ada)
    dsa_tiles, diff_tiles = _bias_tiles(rel_bias)
    fg = final_g.reshape(1, d)

    x2 = x.reshape(bsz * s_len, d)
    for l in range(depth):
        mod = mod_all[l, :bsz].reshape(bsz, 1, 6 * d)
        proj = _inproj(x2, norm1_g[l].reshape(1, d), mod, _pack_w_in(w_in[l]), s_len)
        proj3 = proj.reshape(bsz, s_len, PACKED)
        lam_init = 0.8 - 0.6 * math.exp(-0.3 * l)
        o_a = _dsa(proj3, dsa_tiles, bsz, s_len)
        o_b = _diff(proj3, diff_tiles, lam_params, subln_g, l, bsz, s_len, lam_init)
        o_c = _stick(proj3, bsz, s_len)
        x2 = _merge(o_a.reshape(-1, o_a.shape[-1]), o_b.reshape(-1, o_b.shape[-1]),
                    o_c.reshape(-1, o_c.shape[-1]), proj, x2, mod,
                    w_br[l].astype(BF16), w_out[l].astype(BF16), s_len)
        g2 = norm2_g[l].reshape(1, d)
        final = l == depth - 1
        j = l // 2
        if l % 2 == 0:
            x2 = _ffn(x2, g2, mod, ffn_w1[j].astype(BF16), ffn_w3[j].astype(BF16),
                      ffn_w2[j].astype(BF16), fg, s_len, final)
        else:
            wr_pad = jnp.pad(router_w[j], ((0, 0), (0, LANES - N_EXPERTS)))
            br_pad = jnp.pad(router_b[j], (0, LANES - N_EXPERTS)).reshape(1, LANES)
            x2 = _moe(x2, g2, mod, wr_pad, br_pad, moe_w1[j].astype(BF16), moe_w3[j].astype(BF16),
                      moe_w2[j].astype(BF16), fg, s_len, final)
    return x2.reshape(bsz, s_len, d)
```

```python
import functools
import math

import jax
import jax.numpy as jnp
from jax import lax
from jax.experimental import pallas as pl
from jax.experimental.pallas import tpu as pltpu

F32 = jnp.float32
BF16 = jnp.bfloat16
I32 = jnp.int32
I16 = jnp.int16

LANES = 128
VMEM_LIMIT_BYTES = 56 * 1024 * 1024

CHUNK = 64
A_HEADS = 8
IDX_HEADS = 4
IDX_DIM = 64
TOPK_MAX = 256
B_HEADS = 4
C_HEADS = 8
HEAD_DIM = 64
REL_BUCKETS = 32
FAR_BUCKET = REL_BUCKETS // 2 - 1
N_EXPERTS = 8
EPS = 1e-6
SUBLN_EPS = 1e-5
NEG = -1e30
LOG2E = 1.4426950408889634
INT_MIN = -(2 ** 31)

KEY_BLOCK = 256
Q_TILE = 256
GROUP = 4
BLOCK_GROUPS = (8, 4, 2, 1)

W_A = A_HEADS * HEAD_DIM
OFF_G = 0
OFF_QA, OFF_KA, OFF_VA = 3072, 3584, 4096
OFF_QI, OFF_KI, OFF_WI = 4608, 4864, 4992
OFF_QB, OFF_KB, OFF_VB = 5120, 5632, 6144
OFF_QC, OFF_KC, OFF_VC = 6656, 7168, 7680
PACKED = 8192

LOG_BUCKET_STEPS = (12, 16, 23, 32, 46, 64, 91)


def _nt_dot(a, b):
    return lax.dot_general(a, b, (((1,), (1,)), ((), ())), preferred_element_type=F32)


def _dot(a, b):
    return jnp.dot(a, b, preferred_element_type=F32)


def _cparams(*sem):
    return pltpu.CompilerParams(dimension_semantics=sem, vmem_limit_bytes=VMEM_LIMIT_BYTES)


def _split_heads(x):
    lane = lax.broadcasted_iota(I32, x.shape, 1)
    keep_a = jnp.where(lane < HEAD_DIM, 1.0, 0.0).astype(x.dtype)
    keep_b = jnp.where(lane < HEAD_DIM, 0.0, 1.0).astype(x.dtype)
    return jnp.concatenate([x * keep_a, x * keep_b], axis=0)


def _merge_heads(o, m):
    lane = lax.broadcasted_iota(I32, (m, LANES), 1)
    return jnp.where(lane < HEAD_DIM, o[:m], o[m:])


def _wide(x):
    return jnp.concatenate([x, x], axis=1)


def _tall(x):
    return jnp.concatenate([x, x], axis=0)


def _key_rows(kb):
    return pl.ds(pl.multiple_of(kb * KEY_BLOCK, KEY_BLOCK), KEY_BLOCK)


def _sortable(x):
    bits = pltpu.bitcast(x, I32)
    return bits ^ ((bits >> 31) & 0x7FFFFFFF)


def _chunk_causal():
    r = lax.broadcasted_iota(I32, (Q_TILE, KEY_BLOCK), 0)
    c = lax.broadcasted_iota(I32, (Q_TILE, KEY_BLOCK), 1)
    return (c // CHUNK) <= (r // CHUNK)


def _ada_kernel(c_ref, w_ref, b_ref, o_ref):
    c = c_ref[...]
    a = c * jax.nn.sigmoid(c)
    o_ref[0] = jnp.dot(a, w_ref[0], preferred_element_type=F32,
                       precision=lax.Precision.HIGHEST) + b_ref[0]


def _ada(c_pad, w_ada, b_ada):
    depth, d, n = w_ada.shape
    tn = 1024
    return pl.pallas_call(
        _ada_kernel,
        grid=(depth, n // tn),
        in_specs=[pl.BlockSpec(c_pad.shape, lambda l, j: (0, 0)),
                  pl.BlockSpec((1, d, tn), lambda l, j: (l, 0, j)),
                  pl.BlockSpec((1, 1, tn), lambda l, j: (l, 0, j))],
        out_specs=pl.BlockSpec((1, c_pad.shape[0], tn), lambda l, j: (l, 0, j)),
        out_shape=jax.ShapeDtypeStruct((depth, c_pad.shape[0], n), F32),
        compiler_params=_cparams("parallel", "parallel"),
        name="ada",
    )(c_pad, w_ada, b_ada.reshape(depth, 1, n))


def _rel_bias_tile(tab_ref, head, d0, n_heads_total):
    r = lax.broadcasted_iota(I32, (Q_TILE, KEY_BLOCK), 0)
    c = lax.broadcasted_iota(I32, (Q_TILE, KEY_BLOCK), 1)
    d = c - r + d0
    n = jnp.abs(d)
    large = jnp.full(d.shape, REL_BUCKETS // 4, I32)
    for step in LOG_BUCKET_STEPS:
        large = large + jnp.where(n >= step, 1, 0)
    bucket = jnp.where(d > 0, REL_BUCKETS // 2, 0) + jnp.where(n < REL_BUCKETS // 4, n, large)
    out = jnp.zeros(d.shape, F32)
    for b in range(REL_BUCKETS):
        out = jnp.where(bucket == b, tab_ref[b * n_heads_total + head], out)
    return out - tab_ref[FAR_BUCKET * n_heads_total + head]


def _bias_kernel(tab_ref, dsa_ref, diff_ref):
    p = pl.program_id(0)
    n_heads = A_HEADS + B_HEADS
    dsa_ref[0, 0] = jnp.zeros(dsa_ref.shape[2:], F32)
    diff_ref[0, 0] = jnp.zeros(diff_ref.shape[2:], F32)
    for t, d0 in ((1, -KEY_BLOCK), (2, 0)):
        dsa_ref[0, t, 0:Q_TILE, :] = _rel_bias_tile(tab_ref, 2 * p, d0, n_heads)
        dsa_ref[0, t, Q_TILE:2 * Q_TILE, :] = _rel_bias_tile(tab_ref, 2 * p + 1, d0, n_heads)
        tile = _rel_bias_tile(tab_ref, A_HEADS + p, d0, n_heads)
        if t == 2:
            tile = jnp.where(_chunk_causal(), tile, NEG)
        diff_ref[0, t] = tile


def _bias_tiles(rel_bias):
    tab = rel_bias.reshape(-1)
    return pl.pallas_call(
        _bias_kernel,
        grid=(4,),
        in_specs=[pl.BlockSpec(memory_space=pltpu.SMEM)],
        out_specs=[pl.BlockSpec((1, 3, 2 * Q_TILE, KEY_BLOCK), lambda p: (p, 0, 0, 0)),
                   pl.BlockSpec((1, 3, Q_TILE, KEY_BLOCK), lambda p: (p, 0, 0, 0))],
        out_shape=[jax.ShapeDtypeStruct((A_HEADS // 2, 3, 2 * Q_TILE, KEY_BLOCK), F32),
                   jax.ShapeDtypeStruct((B_HEADS, 3, Q_TILE, KEY_BLOCK), F32)],
        compiler_params=_cparams("parallel"),
        name="rel_bias_tiles",
    )(tab)


def _modulated_norm(x, g, sc, sh):
    y = x * lax.rsqrt(jnp.mean(x * x, axis=-1, keepdims=True) + EPS)
    return y * g * (1.0 + sc) + sh


def _inproj_kernel(x_ref, g_ref, sc_ref, sh_ref, w_ref, o_ref, h_scr):
    @pl.when(pl.program_id(1) == 0)
    def _():
        h_scr[...] = _modulated_norm(x_ref[...], g_ref[...], sc_ref[0], sh_ref[0]).astype(BF16)

    o_ref[...] = _dot(h_scr[...], w_ref[...]).astype(o_ref.dtype)


def _inproj(x2, g, mod, w_packed, s_len):
    t, d = x2.shape
    n = w_packed.shape[1]
    tm, tn = 1024, 2048
    per_b = s_len // tm
    return pl.pallas_call(
        _inproj_kernel,
        grid=(t // tm, n // tn),
        in_specs=[pl.BlockSpec((tm, d), lambda i, j: (i, 0)),
                  pl.BlockSpec((1, d), lambda i, j: (0, 0)),
                  pl.BlockSpec((1, 1, d), lambda i, j: (i // per_b, 0, 1)),
                  pl.BlockSpec((1, 1, d), lambda i, j: (i // per_b, 0, 0)),
                  pl.BlockSpec((d, tn), lambda i, j: (0, j))],
        out_specs=pl.BlockSpec((tm, tn), lambda i, j: (i, j)),
        out_shape=jax.ShapeDtypeStruct((t, n), BF16),
        scratch_shapes=[pltpu.VMEM((tm, d), BF16)],
        compiler_params=_cparams("parallel", "arbitrary"),
        name="inproj",
    )(x2, g, mod, mod, w_packed)


def _for_blocks(lo, hi, fn, sizes=BLOCK_GROUPS):
    pos = lo
    for size in sizes:
        trips = jnp.maximum(hi - pos, 0) // size

        def body(i, carry, pos=pos, size=size):
            fn([pos + i * size + u for u in range(size)])
            return carry

        lax.fori_loop(0, trips, body, 0)
        pos = pos + trips * size


def _fold_blocks(n, fn, acc):
    pos = 0
    for size in BLOCK_GROUPS:
        trips = (n - pos) // size

        def body(i, acc, pos=pos, size=size):
            for u in range(size):
                acc = fn(pos + i * size + u, acc)
            return acc

        acc = lax.fori_loop(0, trips, body, acc)
        pos = pos + trips * size
    return acc


def _softmax_pv(qs, k_at, v_at, qb, add, s_scr, mx_scr, l_scr, acc_scr):
    def pass1(blocks):
        mx = None
        for kb in blocks:
            s = add(kb, _nt_dot(qs, k_at(kb)), jnp.clip(kb - qb + 2, 0, 2))
            s = s * LOG2E
            s_scr[kb] = s
            fold = jnp.maximum(s[:, :LANES], s[:, LANES:])
            mx = fold if mx is None else jnp.maximum(mx, fold)
        mx_scr[...] = jnp.maximum(mx_scr[...], mx)

    mx_scr[...] = jnp.full(mx_scr.shape, NEG, F32)
    _for_blocks(0, qb + 1, pass1)

    m = jnp.max(mx_scr[...], axis=-1, keepdims=True)
    mx_scr[...] = jnp.broadcast_to(m, mx_scr.shape)
    l_scr[...] = jnp.zeros_like(l_scr)
    acc_scr[...] = jnp.zeros_like(acc_scr)

    def pass2(kbs):
        mw = _wide(mx_scr[...])
        l_add = acc_add = None
        for kb in kbs:
            p = jnp.exp2(s_scr[kb] - mw)
            fold = p[:, :LANES] + p[:, LANES:]
            pv = _dot(p.astype(BF16), v_at(kb))
            l_add = fold if l_add is None else l_add + fold
            acc_add = pv if acc_add is None else acc_add + pv
        l_scr[...] += l_add
        acc_scr[...] += acc_add

    _for_blocks(0, qb + 1, pass2)
    return acc_scr[...] / jnp.sum(l_scr[...], axis=-1, keepdims=True)


def _softmax_scratch(s_len):
    return [pltpu.VMEM((s_len // KEY_BLOCK, 2 * Q_TILE, KEY_BLOCK), F32),
            pltpu.VMEM((2 * Q_TILE, LANES), F32),
            pltpu.VMEM((2 * Q_TILE, LANES), F32),
            pltpu.VMEM((2 * Q_TILE, LANES), F32)]


def _dsa_kernel(qa_ref, qi_ref, wi_ref, ka_ref, va_ref, ki_ref, bias_ref, o_ref,
                key_scr, keyt_scr, hi_scr, lo_scr, neg_scr, w_scr, last_scr, s_scr, mx_scr, l_scr, acc_scr,
                *, topk, idx_bits, s_len):
    qb = pl.program_id(1)

    @pl.when(pl.program_id(2) == 0)
    def _():
        _dsa_select(qb, qi_ref, wi_ref, ki_ref, key_scr, keyt_scr, hi_scr, lo_scr, neg_scr, w_scr, last_scr,
                    topk=topk, idx_bits=idx_bits, s_len=s_len)

    pr = pl.program_id(2)
    lanes = pl.ds(pl.multiple_of(pr * LANES, LANES), LANES)

    def add(kb, s, t):
        return s + _tall(neg_scr[kb]) + bias_ref[pr, t]

    o = _softmax_pv(_split_heads(qa_ref[0]),
                    lambda kb: ka_ref[0, _key_rows(kb), lanes],
                    lambda kb: va_ref[0, _key_rows(kb), lanes],
                    qb, add, s_scr, mx_scr, l_scr, acc_scr)
    o_ref[0] = _merge_heads(o, Q_TILE).astype(o_ref.dtype)


def _dsa_select(qb, qi_ref, wi_ref, ki_ref, key_scr, keyt_scr, hi_scr, lo_scr, neg_scr, w_scr, last_scr,
                *, topk, idx_bits, s_len):
    nkb = qb + 1
    kf = float(topk)
    valid_diag = _chunk_causal()

    def cols(kb):
        return kb * KEY_BLOCK + lax.broadcasted_iota(I32, (Q_TILE, KEY_BLOCK), 1)

    qi = qi_ref[0]
    q_heads = [_split_heads(qi[:, pr * LANES:(pr + 1) * LANES]) for pr in range(IDX_HEADS // 2)]
    wi = wi_ref[0].astype(F32)
    for h in range(IDX_HEADS):
        w_scr[h] = jnp.broadcast_to(wi[:, h:h + 1], (Q_TILE, LANES))

    def score_block(kb, valid):
        kk = ki_ref[0, _key_rows(kb), :]
        sc = jnp.zeros((Q_TILE, KEY_BLOCK), F32)
        for pr in range(IDX_HEADS // 2):
            d = _nt_dot(q_heads[pr], kk)
            sc = sc + _wide(w_scr[2 * pr]) * jnp.maximum(d[:Q_TILE], 0.0)
            sc = sc + _wide(w_scr[2 * pr + 1]) * jnp.maximum(d[Q_TILE:], 0.0)
        sc = jnp.where(sc == 0.0, 0.0, sc)
        if valid is not None:
            sc = jnp.where(valid, sc, -jnp.inf)
        key_scr[kb] = _sortable(sc)
        key_t = _sortable(sc.T)
        keyt_scr[kb] = key_t
        hi_scr[kb] = (key_t >> 16).astype(I16)
        lo_scr[kb] = ((key_t & 0xFFFF) - 2 ** 15).astype(I16)

    _for_blocks(0, qb, lambda kbs: [score_block(kb, None) for kb in kbs], sizes=(4, 2, 1))
    score_block(qb, valid_diag)

    def per_query(x):
        return jnp.broadcast_to(x[None], (KEY_BLOCK // 8, 8, Q_TILE)).reshape(KEY_BLOCK, Q_TILE)

    def key_index(kb):
        return kb * KEY_BLOCK + lax.broadcasted_iota(I32, (KEY_BLOCK, Q_TILE), 0)

    def count(pred):
        def one(kb, acc):
            hit = jnp.where(pred(keyt_scr[kb], kb), 1.0, 0.0)
            return acc + jnp.sum(hit.reshape(KEY_BLOCK // 8, 8, Q_TILE), axis=0)

        acc = _fold_blocks(nkb, one, jnp.zeros((8, Q_TILE), F32))
        return jnp.broadcast_to(jnp.sum(acc, axis=0, keepdims=True), (8, Q_TILE))

    def per_query16(x):
        x16 = x.astype(I16)
        return jnp.broadcast_to(x16[None], (KEY_BLOCK // 16, 16, Q_TILE)).reshape(KEY_BLOCK, Q_TILE)

    def count16(ref, pred):
        def one(kb, acc):
            hit = jnp.where(pred(ref[kb]), jnp.int16(1), jnp.int16(0))
            parts = [hit[g:g + 16] for g in range(0, KEY_BLOCK, 16)]
            while len(parts) > 1:
                parts = [parts[i] + parts[i + 1] for i in range(0, len(parts), 2)]
            return acc + parts[0]

        acc = _fold_blocks(nkb, one, jnp.zeros((16, Q_TILE), I16))
        return jnp.broadcast_to(jnp.sum(acc.astype(F32), axis=0, keepdims=True), (16, Q_TILE))

    hi_thr = jnp.where(count16(hi_scr, lambda h: h >= 0) >= kf,
                       jnp.zeros((16, Q_TILE), I32), jnp.full((16, Q_TILE), -(2 ** 15), I32))

    def hi_body(j, t):
        cand = t | lax.shift_left(jnp.int32(1), 14 - j)
        cw = per_query16(cand)
        return jnp.where(count16(hi_scr, lambda h: h >= cw) >= kf, cand, t)

    hi_thr = lax.fori_loop(0, 15, hi_body, hi_thr)
    hi_w = per_query16(hi_thr)
    above = count16(hi_scr, lambda h: h > hi_w)

    def keep_ties(kb, carry):
        lo_scr[kb] = jnp.where(hi_scr[kb] == hi_w, lo_scr[kb], jnp.int16(-(2 ** 15)))
        return carry

    lax.fori_loop(0, nkb, keep_ties, 0)

    def lo_body(j, t):
        cand = t | lax.shift_left(jnp.int32(1), 15 - j)
        cw = per_query16(cand - 2 ** 15)
        return jnp.where(above + count16(lo_scr, lambda lo: lo >= cw) >= kf, cand, t)

    lo_thr = lax.fori_loop(0, 16, lo_body, jnp.zeros((16, Q_TILE), I32))
    thr = (lax.shift_left(hi_thr, 16) | lo_thr)[0:8]
    thr_t = per_query(thr)

    need = kf - count(lambda key, kb: key > thr_t)
    ties = count(lambda key, kb: key == thr_t)
    last_scr[...] = jnp.full(last_scr.shape, s_len, I32)

    @pl.when(jnp.max(ties - need) > 0.0)
    def _():
        def tie_body(j, last):
            cand = last | lax.shift_left(jnp.int32(1), idx_bits - 1 - j)
            cw = per_query(cand)
            taken = count(lambda key, kb: (key == thr_t) & (key_index(kb) < cw))
            return jnp.where(taken < need, cand, last)

        last_scr[...] = lax.fori_loop(0, idx_bits, tie_body, jnp.zeros((8, Q_TILE), I32))

    def per_row(x):
        return _wide(jnp.broadcast_to(x[0:1], (LANES, Q_TILE)).T)

    tw = per_row(thr)
    lastw = per_row(last_scr[...])

    def mask_block(kb, valid):
        key = key_scr[kb]
        tie_neg = jnp.where(key == tw, jnp.where(cols(kb) <= lastw, 0.0, NEG), NEG)
        neg = jnp.where(key > tw, 0.0, tie_neg)
        if valid is not None:
            neg = jnp.where(valid, neg, NEG)
        neg_scr[kb] = neg

    def mask_body(kb, carry):
        mask_block(kb, None)
        return carry

    lax.fori_loop(0, qb, mask_body, 0)
    mask_block(qb, valid_diag)


def _dsa(proj, bias_tiles, bsz, s_len):
    topk = min(TOPK_MAX, s_len // 4)
    idx_bits = max(1, (s_len - 1).bit_length())
    n_blocks = s_len // KEY_BLOCK
    assert 2 * n_blocks <= 256
    kern = functools.partial(_dsa_kernel, topk=topk, idx_bits=idx_bits, s_len=s_len)
    lane_blk = lambda off: off // LANES
    return pl.pallas_call(
        kern,
        grid=(bsz, s_len // Q_TILE, A_HEADS // 2),
        in_specs=[pl.BlockSpec((1, Q_TILE, LANES), lambda b, i, p: (b, i, lane_blk(OFF_QA) + p)),
                  pl.BlockSpec((1, Q_TILE, 256), lambda b, i, p: (b, i, OFF_QI // 256)),
                  pl.BlockSpec((1, Q_TILE, LANES), lambda b, i, p: (b, i, lane_blk(OFF_WI))),
                  pl.BlockSpec((1, s_len, W_A), lambda b, i, p: (b, 0, OFF_KA // W_A),
                               pipeline_mode=pl.Buffered(1)),
                  pl.BlockSpec((1, s_len, W_A), lambda b, i, p: (b, 0, OFF_VA // W_A),
                               pipeline_mode=pl.Buffered(1)),
                  pl.BlockSpec((1, s_len, LANES), lambda b, i, p: (b, 0, lane_blk(OFF_KI))),
                  pl.BlockSpec(bias_tiles.shape, lambda b, i, p: (0, 0, 0, 0),
                               pipeline_mode=pl.Buffered(1))],
        out_specs=pl.BlockSpec((1, Q_TILE, LANES), lambda b, i, p: (b, i, p)),
        out_shape=jax.ShapeDtypeStruct((bsz, s_len, W_A), BF16),
        scratch_shapes=[pltpu.VMEM((n_blocks, Q_TILE, KEY_BLOCK), I32),
                        pltpu.VMEM((n_blocks, KEY_BLOCK, Q_TILE), I32),
                        pltpu.VMEM((n_blocks, KEY_BLOCK, Q_TILE), I16),
                        pltpu.VMEM((n_blocks, KEY_BLOCK, Q_TILE), I16),
                        pltpu.VMEM((n_blocks, Q_TILE, KEY_BLOCK), F32),
                        pltpu.VMEM((IDX_HEADS, Q_TILE, LANES), F32),
                        pltpu.VMEM((8, Q_TILE), I32)] + _softmax_scratch(s_len),
        compiler_params=_cparams("parallel", "arbitrary", "arbitrary"),
        name="dsa",
    )(proj, proj, proj, proj, proj, proj, bias_tiles)


def _diff_kernel(q_ref, k_ref, v_ref, bias_ref, lp_ref, g_ref, o_ref,
                 s_scr, mx_scr, l_scr, acc_scr, *, lam_init):
    qb = pl.program_id(2)
    qs = _split_heads(q_ref[0])

    def add(kb, s, t):
        return s + _tall(bias_ref[0, t])

    o = _softmax_pv(qs, lambda kb: k_ref[0, _key_rows(kb), :], lambda kb: v_ref[0, _key_rows(kb), :],
                    qb, add, s_scr, mx_scr, l_scr, acc_scr)
    lp = lp_ref[0]
    lam = (jnp.exp(jnp.sum(lp[0:1] * lp[1:2], axis=-1, keepdims=True))
           - jnp.exp(jnp.sum(lp[2:3] * lp[3:4], axis=-1, keepdims=True)) + lam_init)
    o = o[:Q_TILE] - lam * o[Q_TILE:]
    y = o * lax.rsqrt(jnp.mean(o * o, axis=-1, keepdims=True) + SUBLN_EPS)
    o_ref[0] = (y * g_ref[0] * (1.0 - lam_init)).astype(o_ref.dtype)


def _diff(proj, bias_tiles, lam_params, subln_g, layer, bsz, s_len, lam_init):
    kern = functools.partial(_diff_kernel, lam_init=lam_init)
    return pl.pallas_call(
        kern,
        grid=(bsz, B_HEADS, s_len // Q_TILE),
        in_specs=[pl.BlockSpec((1, Q_TILE, LANES), lambda b, h, i: (b, i, OFF_QB // LANES + h)),
                  pl.BlockSpec((1, s_len, LANES), lambda b, h, i: (b, 0, OFF_KB // LANES + h)),
                  pl.BlockSpec((1, s_len, LANES), lambda b, h, i: (b, 0, OFF_VB // LANES + h)),
                  pl.BlockSpec((1, 3, Q_TILE, KEY_BLOCK), lambda b, h, i: (h, 0, 0, 0)),
                  pl.BlockSpec((1, 4, HEAD_DIM), lambda b, h, i: (layer, 0, 0)),
                  pl.BlockSpec((1, 1, 2 * HEAD_DIM), lambda b, h, i: (layer, 0, 0))],
        out_specs=pl.BlockSpec((1, Q_TILE, LANES), lambda b, h, i: (b, i, h)),
        out_shape=jax.ShapeDtypeStruct((bsz, s_len, B_HEADS * 2 * HEAD_DIM), BF16),
        scratch_shapes=_softmax_scratch(s_len),
        compiler_params=_cparams("parallel", "parallel", "arbitrary"),
        name="diff_attn",
    )(proj, proj, proj, bias_tiles, lam_params, subln_g.reshape(-1, 1, 2 * HEAD_DIM))


def _stick_kernel(q_ref, k_ref, v_ref, o_ref, later_scr, run_scr, acc_scr):
    qb = pl.program_id(2)
    qs = _split_heads(q_ref[0])
    jj = lax.broadcasted_iota(I32, (2 * KEY_BLOCK, KEY_BLOCK), 0) & (KEY_BLOCK - 1)
    ss = lax.broadcasted_iota(I32, (2 * KEY_BLOCK, KEY_BLOCK), 1)
    later_scr[...] = jnp.where(jj > ss, 1.0, 0.0).astype(BF16)
    run_scr[...] = jnp.zeros_like(run_scr)
    acc_scr[...] = jnp.zeros_like(acc_scr)

    def sweep(kbs, diagonal):
        parts = []
        for j, kb in enumerate(kbs):
            z = _nt_dot(qs, k_ref[0, _key_rows(kb), :])
            soft = jnp.log(1.0 + jnp.exp2(jnp.abs(z) * (-LOG2E)))
            log_beta = jnp.minimum(z, 0.0) - soft
            log_1mb = log_beta - z
            strict = None
            if diagonal and j == 0:
                t = lax.broadcasted_iota(I32, z.shape, 0) & (Q_TILE - 1)
                strict = lax.broadcasted_iota(I32, z.shape, 1) < t
                log_1mb = jnp.where(strict, log_1mb, 0.0)
            hi = log_1mb.astype(BF16)
            lo = (log_1mb - hi.astype(F32)).astype(BF16)
            between = _dot(jnp.concatenate([hi, lo], axis=1), later_scr[...])
            total = between[:, 0:1] + log_1mb[:, 0:1]
            parts.append((kb, log_beta + between, total, strict))
        run = run_scr[...]
        acc = None
        for kb, logit, total, strict in parts:
            w = jnp.exp(logit + _wide(run))
            if strict is not None:
                w = jnp.where(strict, w, 0.0)
            pv = _dot(w.astype(BF16), v_ref[0, _key_rows(kb), :])
            acc = pv if acc is None else acc + pv
            run = run + jnp.broadcast_to(total, run.shape)
        acc_scr[...] += acc
        run_scr[...] = run

    first = jnp.where(qb + 1 >= GROUP, GROUP, jnp.where(qb + 1 >= 2, 2, 1))
    for size in (GROUP, 2, 1):
        @pl.when(first == size)
        def _(size=size):
            sweep([qb - u for u in range(size)], True)
    top = qb - first
    for size in BLOCK_GROUPS:
        trips = (top + 1) // size

        def body(i, carry, top=top, size=size):
            sweep([top - i * size - u for u in range(size)], False)
            return carry

        lax.fori_loop(0, trips, body, 0)
        top = top - trips * size
    o_ref[0] = _merge_heads(acc_scr[...], Q_TILE).astype(o_ref.dtype)


def _stick(proj, bsz, s_len):
    return pl.pallas_call(
        _stick_kernel,
        grid=(bsz, C_HEADS // 2, s_len // Q_TILE),
        in_specs=[pl.BlockSpec((1, Q_TILE, LANES), lambda b, h, i: (b, i, OFF_QC // LANES + h)),
                  pl.BlockSpec((1, s_len, LANES), lambda b, h, i: (b, 0, OFF_KC // LANES + h)),
                  pl.BlockSpec((1, s_len, LANES), lambda b, h, i: (b, 0, OFF_VC // LANES + h))],
        out_specs=pl.BlockSpec((1, Q_TILE, LANES), lambda b, h, i: (b, i, h)),
        out_shape=jax.ShapeDtypeStruct((bsz, s_len, C_HEADS * HEAD_DIM), BF16),
        scratch_shapes=[pltpu.VMEM((2 * KEY_BLOCK, KEY_BLOCK), BF16),
                        pltpu.VMEM((2 * Q_TILE, LANES), F32),
                        pltpu.VMEM((2 * Q_TILE, LANES), F32)],
        compiler_params=_cparams("parallel", "parallel", "arbitrary"),
        name="stick_attn",
    )(proj, proj, proj)


def _merge_kernel(oa_ref, ob_ref, oc_ref, gate_ref, x_ref, g1_ref, wbr_ref, wout_ref, o_ref):
    d = x_ref.shape[-1]
    merged = jnp.zeros(x_ref.shape, F32)
    for j, o_br in enumerate((oa_ref, ob_ref, oc_ref)):
        w = o_br.shape[-1]
        gate = jax.nn.sigmoid(gate_ref[:, j * d:(j + 1) * d].astype(F32))
        merged = merged + gate * _dot(o_br[...], wbr_ref[j * w:(j + 1) * w, :])
    o_ref[...] = x_ref[...] + g1_ref[0] * _dot(merged.astype(BF16), wout_ref[...])


def _merge(o_a, o_b, o_c, proj, x2, mod, w_br, w_out, s_len):
    t, d = x2.shape
    tm = 1024
    per_b = s_len // tm
    w = o_a.shape[-1]
    once = dict(pipeline_mode=pl.Buffered(1))
    return pl.pallas_call(
        _merge_kernel,
        grid=(t // tm,),
        in_specs=[pl.BlockSpec((tm, w), lambda i: (i, 0)),
                  pl.BlockSpec((tm, w), lambda i: (i, 0)),
                  pl.BlockSpec((tm, w), lambda i: (i, 0)),
                  pl.BlockSpec((tm, 3 * d), lambda i: (i, OFF_G // (3 * d))),
                  pl.BlockSpec((tm, d), lambda i: (i, 0)),
                  pl.BlockSpec((1, 1, d), lambda i: (i // per_b, 0, 2)),
                  pl.BlockSpec(w_br.shape, lambda i: (0, 0), **once),
                  pl.BlockSpec(w_out.shape, lambda i: (0, 0), **once)],
        out_specs=pl.BlockSpec((tm, d), lambda i: (i, 0)),
        out_shape=jax.ShapeDtypeStruct((t, d), F32),
        compiler_params=_cparams("parallel"),
        name="merge",
    )(o_a, o_b, o_c, proj, x2, mod, w_br, w_out)


def _finish(x, gate, acc, fg_ref, final):
    y = x + gate * acc
    if final:
        y = y * lax.rsqrt(jnp.mean(y * y, axis=-1, keepdims=True) + EPS) * fg_ref[...]
    return y


def _swiglu_partial(h, w1, w3, w2):
    a = _dot(h, w1)
    act = (a * jax.nn.sigmoid(a)) * _dot(h, w3)
    return _dot(act.astype(BF16), w2)


def _ffn_kernel(x_ref, g_ref, sc_ref, sh_ref, gate_ref, w1_ref, w3_ref, w2_ref, fg_ref, o_ref,
                h_scr, acc_scr, *, final):
    f = pl.program_id(1)

    @pl.when(f == 0)
    def _():
        h_scr[...] = _modulated_norm(x_ref[...], g_ref[...], sc_ref[0], sh_ref[0]).astype(BF16)
        acc_scr[...] = jnp.zeros_like(acc_scr)

    acc_scr[...] += _swiglu_partial(h_scr[...], w1_ref[...], w3_ref[...], w2_ref[...])

    @pl.when(f == pl.num_programs(1) - 1)
    def _():
        o_ref[...] = _finish(x_ref[...], gate_ref[0], acc_scr[...], fg_ref, final)


def _ffn(x2, g, mod, w1, w3, w2, final_g, s_len, final):
    t, d = x2.shape
    d_ff = w1.shape[1]
    tm, tf = 512, d_ff
    once = dict(pipeline_mode=pl.Buffered(1))
    per_b = s_len // tm
    kern = functools.partial(_ffn_kernel, final=final)
    return pl.pallas_call(
        kern,
        grid=(t // tm, d_ff // tf),
        in_specs=[pl.BlockSpec((tm, d), lambda i, f: (i, 0)),
                  pl.BlockSpec((1, d), lambda i, f: (0, 0)),
                  pl.BlockSpec((1, 1, d), lambda i, f: (i // per_b, 0, 4)),
                  pl.BlockSpec((1, 1, d), lambda i, f: (i // per_b, 0, 3)),
                  pl.BlockSpec((1, 1, d), lambda i, f: (i // per_b, 0, 5)),
                  pl.BlockSpec((d, tf), lambda i, f: (0, f), **once),
                  pl.BlockSpec((d, tf), lambda i, f: (0, f), **once),
                  pl.BlockSpec((tf, d), lambda i, f: (f, 0), **once),
                  pl.BlockSpec((1, d), lambda i, f: (0, 0))],
        out_specs=pl.BlockSpec((tm, d), lambda i, f: (i, 0)),
        out_shape=jax.ShapeDtypeStruct((t, d), F32),
        scratch_shapes=[pltpu.VMEM((tm, d), BF16), pltpu.VMEM((tm, d), F32)],
        compiler_params=_cparams("parallel", "arbitrary"),
        name="ffn",
    )(x2, g, mod, mod, mod, w1, w3, w2, final_g)


MOE_TM = 512
MOE_ROWS = 512
R_E1, R_E2, R_RANK1, R_RANK2, R_W1, R_W2 = 0, 1, 2, 3, 4, 5


def _lane_pick(tile, lane, idx):
    return jnp.sum(jnp.where(lane == idx, tile, 0.0), axis=-1, keepdims=True)


def _route_kernel(x_ref, g_ref, sc_ref, sh_ref, wr_ref, br_ref, h_ref, rec_ref, cnt_ref, cnt_scr):
    @pl.when(pl.program_id(0) == 0)
    def _():
        cnt_scr[...] = jnp.zeros_like(cnt_scr)

    h = _modulated_norm(x_ref[...], g_ref[...], sc_ref[0], sh_ref[0])
    half = h.shape[1] // 2
    bits = pltpu.bitcast(h.astype(BF16).astype(F32), jnp.uint32)
    h_ref[...] = (bits[:, half:] & jnp.uint32(0xFFFF0000)) | (bits[:, :half] >> 16)

    h_hi = h.astype(BF16)
    h_lo = (h - h_hi.astype(F32)).astype(BF16)
    wr = wr_ref[...]
    w_hi = wr.astype(BF16)
    w_lo = (wr - w_hi.astype(F32)).astype(BF16)
    logits = _dot(h_hi, w_hi) + _dot(h_hi, w_lo) + _dot(h_lo, w_hi) + br_ref[...]
    lane = lax.broadcasted_iota(I32, logits.shape, 1).astype(F32)
    lg = jnp.where(lane < N_EXPERTS, logits, -jnp.inf)
    m1 = jnp.max(lg, axis=-1, keepdims=True)
    i1 = jnp.min(jnp.where(lg == m1, lane, float(LANES)), axis=-1, keepdims=True)
    lg2 = jnp.where(lane == i1, -jnp.inf, lg)
    m2 = jnp.max(lg2, axis=-1, keepdims=True)
    i2 = jnp.min(jnp.where(lg2 == m2, lane, float(LANES)), axis=-1, keepdims=True)
    e2 = jnp.exp(m2 - m1)
    w_top = 1.0 / (1.0 + e2)

    chosen = jnp.where(lane == i1, 1.0, 0.0) + jnp.where(lane == i2, 1.0, 0.0)
    tm = chosen.shape[0]
    earlier = jnp.where(lax.broadcasted_iota(I32, (tm, tm), 1) < lax.broadcasted_iota(I32, (tm, tm), 0),
                        1.0, 0.0).astype(BF16)
    rank = _dot(earlier, chosen.astype(BF16)) + cnt_scr[0:1, :]
    rec = jnp.zeros_like(logits)
    for slot, val in ((R_E1, i1), (R_E2, i2), (R_RANK1, _lane_pick(rank, lane, i1)),
                      (R_RANK2, _lane_pick(rank, lane, i2)), (R_W1, w_top), (R_W2, e2 * w_top)):
        rec = jnp.where(lane == float(slot), val, rec)
    rec_ref[...] = rec
    cnt_scr[0:1, :] = cnt_scr[0:1, :] + jnp.sum(chosen, axis=0, keepdims=True)
    cnt_ref[...] = cnt_scr[...]


def _route(x2, g, mod, wr_pad, br_pad, s_len):
    t, d = x2.shape
    tm = MOE_TM
    per_b = s_len // tm
    return pl.pallas_call(
        _route_kernel,
        grid=(t // tm,),
        in_specs=[pl.BlockSpec((tm, d), lambda i: (i, 0)),
                  pl.BlockSpec((1, d), lambda i: (0, 0)),
                  pl.BlockSpec((1, 1, d), lambda i: (i // per_b, 0, 4)),
                  pl.BlockSpec((1, 1, d), lambda i: (i // per_b, 0, 3)),
                  pl.BlockSpec((d, LANES), lambda i: (0, 0)),
                  pl.BlockSpec((1, LANES), lambda i: (0, 0))],
        out_specs=[pl.BlockSpec((tm, d // 2), lambda i: (i, 0)),
                   pl.BlockSpec((tm, LANES), lambda i: (i, 0)),
                   pl.BlockSpec((8, LANES), lambda i: (0, 0))],
        out_shape=[jax.ShapeDtypeStruct((t, d // 2), jnp.uint32),
                   jax.ShapeDtypeStruct((t, LANES), F32),
                   jax.ShapeDtypeStruct((8, LANES), F32)],
        scratch_shapes=[pltpu.VMEM((8, LANES), F32)],
        compiler_params=_cparams("arbitrary"),
        name="moe_route",
    )(x2, g, mod, mod, wr_pad, br_pad)


def _row_copy(src_ref, src_row, dst_ref, dst_row, sem):
    return pltpu.make_async_copy(src_ref.at[pl.ds(src_row, 1)], dst_ref.at[pl.ds(dst_row, 1)], sem)


def _dispatch_kernel(pos_ref, h_ref, xs_in_ref, xs_ref, sem):
    del xs_in_ref
    tm = h_ref.shape[0]

    def start(t, carry):
        _row_copy(h_ref, t, xs_ref, pos_ref[0, 0, t], sem).start(priority=0)
        _row_copy(h_ref, t, xs_ref, pos_ref[0, 0, tm + t], sem).start(priority=1)
        return carry

    lax.fori_loop(0, tm, start, 0, unroll=4)
    for _ in range(2):
        pltpu.make_async_copy(h_ref, xs_ref.at[pl.ds(0, tm)], sem).wait()


def _dispatch(pos, h_packed, n_rows):
    t, w = h_packed.shape
    tm = MOE_TM
    xs0 = jnp.zeros((n_rows, w), h_packed.dtype)
    return pl.pallas_call(
        _dispatch_kernel,
        grid=(t // tm,),
        in_specs=[pl.BlockSpec((1, 1, 2 * tm), lambda i: (i, 0, 0), memory_space=pltpu.SMEM),
                  pl.BlockSpec((tm, w), lambda i: (i, 0)),
                  pl.BlockSpec(memory_space=pl.ANY)],
        out_specs=pl.BlockSpec(memory_space=pl.ANY),
        out_shape=jax.ShapeDtypeStruct((n_rows, w), h_packed.dtype),
        scratch_shapes=[pltpu.SemaphoreType.DMA(())],
        input_output_aliases={2: 0},
        compiler_params=_cparams("arbitrary"),
        name="moe_dispatch",
    )(pos, h_packed, xs0)


def _expert_kernel(te_ref, nu_ref, xs_ref, w1_ref, w3_ref, w2_ref, o_ref, h_scr, acc_scr):
    i = pl.program_id(0)
    f = pl.program_id(1)

    @pl.when(i < nu_ref[0])
    def _():
        @pl.when(f == 0)
        def _():
            word = xs_ref[...]
            lo = pltpu.bitcast(word << 16, F32)
            hi = pltpu.bitcast(word & jnp.uint32(0xFFFF0000), F32)
            h_scr[...] = jnp.concatenate([lo, hi], axis=1).astype(BF16)
            acc_scr[...] = jnp.zeros_like(acc_scr)

        acc_scr[...] += _swiglu_partial(h_scr[...], w1_ref[0], w3_ref[0], w2_ref[0])

        @pl.when(f == pl.num_programs(1) - 1)
        def _():
            o_ref[...] = acc_scr[...]

    @pl.when((i >= nu_ref[0]) & (f == pl.num_programs(1) - 1))
    def _():
        o_ref[...] = jnp.zeros_like(o_ref)


def _experts(tile_expert, n_used, xs, w1, w3, w2):
    n_rows, half = xs.shape
    d = 2 * half
    d_ff = w1.shape[2]
    tf = 1792
    nf = d_ff // tf
    tile = lambda i, nu: jnp.minimum(i, nu[0] - 1)
    ff = lambda i, f, nu: jnp.where(i < nu[0], f, nf - 1)
    return pl.pallas_call(
        _expert_kernel,
        grid_spec=pltpu.PrefetchScalarGridSpec(
            num_scalar_prefetch=2,
            grid=(n_rows // MOE_ROWS, nf),
            in_specs=[pl.BlockSpec((MOE_ROWS, half), lambda i, f, te, nu: (tile(i, nu), 0)),
                      pl.BlockSpec((1, d, tf), lambda i, f, te, nu: (te[i], 0, ff(i, f, nu))),
                      pl.BlockSpec((1, d, tf), lambda i, f, te, nu: (te[i], 0, ff(i, f, nu))),
                      pl.BlockSpec((1, tf, d), lambda i, f, te, nu: (te[i], ff(i, f, nu), 0))],
            out_specs=pl.BlockSpec((MOE_ROWS, d), lambda i, f, te, nu: (i, 0)),
            scratch_shapes=[pltpu.VMEM((MOE_ROWS, d), BF16), pltpu.VMEM((MOE_ROWS, d), F32)]),
        out_shape=jax.ShapeDtypeStruct((n_rows, d), F32),
        compiler_params=_cparams("arbitrary", "arbitrary"),
        name="moe_experts",
    )(tile_expert, n_used, xs, w1, w3, w2)


def _combine_kernel(pos_ref, x_ref, gate_ref, rec_ref, fg_ref, ys_ref, o_ref, y_scr, sem, *, final):
    tm = x_ref.shape[0]

    def start(t, carry):
        _row_copy(ys_ref, pos_ref[0, 0, t], y_scr.at[0], t, sem).start(priority=0)
        _row_copy(ys_ref, pos_ref[0, 0, tm + t], y_scr.at[1], t, sem).start(priority=1)
        return carry

    lax.fori_loop(0, tm, start, 0, unroll=4)
    for slot in range(2):
        pltpu.make_async_copy(ys_ref.at[pl.ds(0, tm)], y_scr.at[slot], sem).wait()
    rec = rec_ref[...]
    lane = lax.broadcasted_iota(I32, rec.shape, 1)
    w_first = jnp.sum(jnp.where(lane == R_W1, rec, 0.0), axis=-1, keepdims=True)
    w_second = jnp.sum(jnp.where(lane == R_W2, rec, 0.0), axis=-1, keepdims=True)
    f = w_first * y_scr[0] + w_second * y_scr[1]
    o_ref[...] = _finish(x_ref[...], gate_ref[0], f, fg_ref, final)


def _combine(pos, x2, mod, rec, final_g, ys, s_len, final):
    t, d = x2.shape
    tm = MOE_TM
    per_b = s_len // tm
    kern = functools.partial(_combine_kernel, final=final)
    return pl.pallas_call(
        kern,
        grid=(t // tm,),
        in_specs=[pl.BlockSpec((1, 1, 2 * tm), lambda i: (i, 0, 0), memory_space=pltpu.SMEM),
                  pl.BlockSpec((tm, d), lambda i: (i, 0)),
                  pl.BlockSpec((1, 1, d), lambda i: (i // per_b, 0, 5)),
                  pl.BlockSpec((tm, LANES), lambda i: (i, 0)),
                  pl.BlockSpec((1, d), lambda i: (0, 0)),
                  pl.BlockSpec(memory_space=pl.ANY)],
        out_specs=pl.BlockSpec((tm, d), lambda i: (i, 0)),
        out_shape=jax.ShapeDtypeStruct((t, d), F32),
        scratch_shapes=[pltpu.VMEM((2, tm, d), F32), pltpu.SemaphoreType.DMA(())],
        compiler_params=_cparams("arbitrary"),
        name="moe_combine",
    )(pos, x2, mod, rec, final_g, ys)


def _moe(x2, g, mod, wr_pad, br_pad, w1, w3, w2, final_g, s_len, final):
    t, d = x2.shape
    n_e = w1.shape[0]
    h_packed, rec, cnt = _route(x2, g, mod, wr_pad, br_pad, s_len)

    n_tiles = 2 * t // MOE_ROWS + n_e
    counts = cnt[0, :n_e].astype(I32)
    padded = (counts + MOE_ROWS - 1) // MOE_ROWS * MOE_ROWS
    ends = jnp.cumsum(padded)
    starts = ends - padded
    e1, e2 = rec[:, R_E1].astype(I32), rec[:, R_E2].astype(I32)
    pos1 = starts[e1] + rec[:, R_RANK1].astype(I32)
    pos2 = starts[e2] + rec[:, R_RANK2].astype(I32)
    pos = jnp.concatenate([pos1.reshape(-1, 1, MOE_TM), pos2.reshape(-1, 1, MOE_TM)], axis=2)
    n_used = (ends[-1] // MOE_ROWS).reshape(1)
    tile_start = jnp.minimum(jnp.arange(n_tiles, dtype=I32), n_used[0] - 1) * MOE_ROWS
    tile_expert = jnp.sum(tile_start[:, None] >= ends[None, :], axis=1).astype(I32)

    xs = _dispatch(pos, h_packed, n_tiles * MOE_ROWS)
    ys = _experts(tile_expert, n_used, xs, w1, w3, w2)
    return _combine(pos, x2, mod, rec, final_g, ys, s_len, final)


def _pack_w_in(w):
    d = w.shape[0]
    sizes = (W_A, W_A, W_A, IDX_HEADS * IDX_DIM, IDX_DIM, IDX_HEADS,
             512, 512, 512, 512, 512, 512, 3 * d)
    offs = [0]
    for s in sizes:
        offs.append(offs[-1] + s)
    (qa, ka, va, qi, ki, wi, qb, kb, vb, qc, kc, vc, gl) = [
        w[:, offs[j]:offs[j + 1]] for j in range(len(sizes))]
    scale = HEAD_DIM ** -0.5
    pad_wi = jnp.zeros((d, LANES - IDX_HEADS), w.dtype)
    packed = jnp.concatenate(
        [gl, qa * scale, ka, va,
         qi * (IDX_DIM ** -0.5), ki, ki, wi * (IDX_HEADS ** -0.5), pad_wi,
         qb * scale, kb, vb, qc * scale, kc, vc], axis=1)
    assert packed.shape[1] == PACKED
    return packed.astype(BF16)


def kernel(x, c, w_ada, b_ada, norm1_g, norm2_g, w_in, w_br, w_out, rel_bias, lam_params,
           subln_g, ffn_w1, ffn_w3, ffn_w2, router_w, router_b, moe_w1, moe_w3, moe_w2, final_g):
    bsz, s_len, d = x.shape
    depth = w_ada.shape[0]
    assert s_len % 1024 == 0 and d == 1024 and OFF_G + 3 * d == OFF_QA

    c_pad = jnp.concatenate([c, jnp.zeros((8 - bsz % 8 if bsz % 8 else 0, d), c.dtype)], axis=0)
    mod_all = _ada(c_pad, w_ada, b_ada)
    dsa_tiles, diff_tiles = _bias_tiles(rel_bias)
    fg = final_g.reshape(1, d)

    x2 = x.reshape(bsz * s_len, d)
    for l in range(depth):
        mod = mod_all[l, :bsz].reshape(bsz, 1, 6 * d)
        proj = _inproj(x2, norm1_g[l].reshape(1, d), mod, _pack_w_in(w_in[l]), s_len)
        proj3 = proj.reshape(bsz, s_len, PACKED)
        lam_init = 0.8 - 0.6 * math.exp(-0.3 * l)
        o_a = _dsa(proj3, dsa_tiles, bsz, s_len)
        o_b = _diff(proj3, diff_tiles, lam_params, subln_g, l, bsz, s_len, lam_init)
        o_c = _stick(proj3, bsz, s_len)
        x2 = _merge(o_a.reshape(-1, o_a.shape[-1]), o_b.reshape(-1, o_b.shape[-1]),
                    o_c.reshape(-1, o_c.shape[-1]), proj, x2, mod,
                    w_br[l].astype(BF16), w_out[l].astype(BF16), s_len)
        g2 = norm2_g[l].reshape(1, d)
        final = l == depth - 1
        j = l // 2
        if l % 2 == 0:
            x2 = _ffn(x2, g2, mod, ffn_w1[j].astype(BF16), ffn_w3[j].astype(BF16),
                      ffn_w2[j].astype(BF16), fg, s_len, final)
        else:
            wr_pad = jnp.pad(router_w[j], ((0, 0), (0, LANES - N_EXPERTS)))
            br_pad = jnp.pad(router_b[j], (0, LANES - N_EXPERTS)).reshape(1, LANES)
            x2 = _moe(x2, g2, mod, wr_pad, br_pad, moe_w1[j].astype(BF16), moe_w3[j].astype(BF16),
                      moe_w2[j].astype(BF16), fg, s_len, final)
    return x2.reshape(bsz, s_len, d)
```

```python
import functools
import math

import jax
import jax.numpy as jnp
from jax import lax
from jax.experimental import pallas as pl
from jax.experimental.pallas import tpu as pltpu

F32 = jnp.float32
BF16 = jnp.bfloat16
I32 = jnp.int32
I16 = jnp.int16

LANES = 128
VMEM_LIMIT_BYTES = 56 * 1024 * 1024

CHUNK = 64
A_HEADS = 8
IDX_HEADS = 4
IDX_DIM = 64
TOPK_MAX = 256
B_HEADS = 4
C_HEADS = 8
HEAD_DIM = 64
REL_BUCKETS = 32
FAR_BUCKET = REL_BUCKETS // 2 - 1
N_EXPERTS = 8
EPS = 1e-6
SUBLN_EPS = 1e-5
NEG = -1e30
LOG2E = 1.4426950408889634
INT_MIN = -(2 ** 31)

KEY_BLOCK = 256
Q_TILE = 256
GROUP = 4
BLOCK_GROUPS = (8, 4, 2, 1)

W_A = A_HEADS * HEAD_DIM
OFF_G = 0
OFF_QA, OFF_KA, OFF_VA = 3072, 3584, 4096
OFF_QI, OFF_KI, OFF_WI = 4608, 4864, 4992
OFF_QB, OFF_KB, OFF_VB = 5120, 5632, 6144
OFF_QC, OFF_KC, OFF_VC = 6656, 7168, 7680
PACKED = 8192

LOG_BUCKET_STEPS = (12, 16, 23, 32, 46, 64, 91)


def _nt_dot(a, b):
    return lax.dot_general(a, b, (((1,), (1,)), ((), ())), preferred_element_type=F32)


def _dot(a, b):
    return jnp.dot(a, b, preferred_element_type=F32)


def _cparams(*sem):
    return pltpu.CompilerParams(dimension_semantics=sem, vmem_limit_bytes=VMEM_LIMIT_BYTES)


def _split_heads(x):
    lane = lax.broadcasted_iota(I32, x.shape, 1)
    keep_a = jnp.where(lane < HEAD_DIM, 1.0, 0.0).astype(x.dtype)
    keep_b = jnp.where(lane < HEAD_DIM, 0.0, 1.0).astype(x.dtype)
    return jnp.concatenate([x * keep_a, x * keep_b], axis=0)


def _merge_heads(o, m):
    lane = lax.broadcasted_iota(I32, (m, LANES), 1)
    return jnp.where(lane < HEAD_DIM, o[:m], o[m:])


def _wide(x):
    return jnp.concatenate([x, x], axis=1)


def _tall(x):
    return jnp.concatenate([x, x], axis=0)


def _key_rows(kb):
    return pl.ds(pl.multiple_of(kb * KEY_BLOCK, KEY_BLOCK), KEY_BLOCK)


def _sortable(x):
    bits = pltpu.bitcast(x, I32)
    return bits ^ ((bits >> 31) & 0x7FFFFFFF)


def _chunk_causal():
    r = lax.broadcasted_iota(I32, (Q_TILE, KEY_BLOCK), 0)
    c = lax.broadcasted_iota(I32, (Q_TILE, KEY_BLOCK), 1)
    return (c // CHUNK) <= (r // CHUNK)


def _ada_kernel(c_ref, w_ref, b_ref, o_ref):
    c = c_ref[...]
    a = c * jax.nn.sigmoid(c)
    o_ref[0] = jnp.dot(a, w_ref[0], preferred_element_type=F32,
                       precision=lax.Precision.HIGHEST) + b_ref[0]


def _ada(c_pad, w_ada, b_ada):
    depth, d, n = w_ada.shape
    tn = 1024
    return pl.pallas_call(
        _ada_kernel,
        grid=(depth, n // tn),
        in_specs=[pl.BlockSpec(c_pad.shape, lambda l, j: (0, 0)),
                  pl.BlockSpec((1, d, tn), lambda l, j: (l, 0, j)),
                  pl.BlockSpec((1, 1, tn), lambda l, j: (l, 0, j))],
        out_specs=pl.BlockSpec((1, c_pad.shape[0], tn), lambda l, j: (l, 0, j)),
        out_shape=jax.ShapeDtypeStruct((depth, c_pad.shape[0], n), F32),
        compiler_params=_cparams("parallel", "parallel"),
        name="ada",
    )(c_pad, w_ada, b_ada.reshape(depth, 1, n))


def _rel_bias_tile(tab_ref, head, d0, n_heads_total):
    r = lax.broadcasted_iota(I32, (Q_TILE, KEY_BLOCK), 0)
    c = lax.broadcasted_iota(I32, (Q_TILE, KEY_BLOCK), 1)
    d = c - r + d0
    n = jnp.abs(d)
    large = jnp.full(d.shape, REL_BUCKETS // 4, I32)
    for step in LOG_BUCKET_STEPS:
        large = large + jnp.where(n >= step, 1, 0)
    bucket = jnp.where(d > 0, REL_BUCKETS // 2, 0) + jnp.where(n < REL_BUCKETS // 4, n, large)
    out = jnp.zeros(d.shape, F32)
    for b in range(REL_BUCKETS):
        out = jnp.where(bucket == b, tab_ref[b * n_heads_total + head], out)
    return out - tab_ref[FAR_BUCKET * n_heads_total + head]


def _bias_kernel(tab_ref, dsa_ref, diff_ref):
    p = pl.program_id(0)
    n_heads = A_HEADS + B_HEADS
    dsa_ref[0, 0] = jnp.zeros(dsa_ref.shape[2:], F32)
    diff_ref[0, 0] = jnp.zeros(diff_ref.shape[2:], F32)
    for t, d0 in ((1, -KEY_BLOCK), (2, 0)):
        dsa_ref[0, t, 0:Q_TILE, :] = _rel_bias_tile(tab_ref, 2 * p, d0, n_heads)
        dsa_ref[0, t, Q_TILE:2 * Q_TILE, :] = _rel_bias_tile(tab_ref, 2 * p + 1, d0, n_heads)
        tile = _rel_bias_tile(tab_ref, A_HEADS + p, d0, n_heads)
        if t == 2:
            tile = jnp.where(_chunk_causal(), tile, NEG)
        diff_ref[0, t] = tile


def _bias_tiles(rel_bias):
    tab = rel_bias.reshape(-1)
    return pl.pallas_call(
        _bias_kernel,
        grid=(4,),
        in_specs=[pl.BlockSpec(memory_space=pltpu.SMEM)],
        out_specs=[pl.BlockSpec((1, 3, 2 * Q_TILE, KEY_BLOCK), lambda p: (p, 0, 0, 0)),
                   pl.BlockSpec((1, 3, Q_TILE, KEY_BLOCK), lambda p: (p, 0, 0, 0))],
        out_shape=[jax.ShapeDtypeStruct((A_HEADS // 2, 3, 2 * Q_TILE, KEY_BLOCK), F32),
                   jax.ShapeDtypeStruct((B_HEADS, 3, Q_TILE, KEY_BLOCK), F32)],
        compiler_params=_cparams("parallel"),
        name="rel_bias_tiles",
    )(tab)


def _modulated_norm(x, g, sc, sh):
    y = x * lax.rsqrt(jnp.mean(x * x, axis=-1, keepdims=True) + EPS)
    return y * g * (1.0 + sc) + sh


def _inproj_kernel(x_ref, g_ref, sc_ref, sh_ref, w_ref, o_ref, h_scr):
    @pl.when(pl.program_id(1) == 0)
    def _():
        h_scr[...] = _modulated_norm(x_ref[...], g_ref[...], sc_ref[0], sh_ref[0]).astype(BF16)

    o_ref[...] = _dot(h_scr[...], w_ref[...]).astype(o_ref.dtype)


def _inproj(x2, g, mod, w_packed, s_len):
    t, d = x2.shape
    n = w_packed.shape[1]
    tm, tn = 1024, 2048
    per_b = s_len // tm
    return pl.pallas_call(
        _inproj_kernel,
        grid=(t // tm, n // tn),
        in_specs=[pl.BlockSpec((tm, d), lambda i, j: (i, 0)),
                  pl.BlockSpec((1, d), lambda i, j: (0, 0)),
                  pl.BlockSpec((1, 1, d), lambda i, j: (i // per_b, 0, 1)),
                  pl.BlockSpec((1, 1, d), lambda i, j: (i // per_b, 0, 0)),
                  pl.BlockSpec((d, tn), lambda i, j: (0, j))],
        out_specs=pl.BlockSpec((tm, tn), lambda i, j: (i, j)),
        out_shape=jax.ShapeDtypeStruct((t, n), BF16),
        scratch_shapes=[pltpu.VMEM((tm, d), BF16)],
        compiler_params=_cparams("parallel", "arbitrary"),
        name="inproj",
    )(x2, g, mod, mod, w_packed)


def _for_blocks(lo, hi, fn, sizes=BLOCK_GROUPS):
    pos = lo
    for size in sizes:
        trips = jnp.maximum(hi - pos, 0) // size

        def body(i, carry, pos=pos, size=size):
            fn([pos + i * size + u for u in range(size)])
            return carry

        lax.fori_loop(0, trips, body, 0)
        pos = pos + trips * size


def _fold_blocks(n, fn, acc):
    pos = 0
    for size in BLOCK_GROUPS:
        trips = (n - pos) // size

        def body(i, acc, pos=pos, size=size):
            for u in range(size):
                acc = fn(pos + i * size + u, acc)
            return acc

        acc = lax.fori_loop(0, trips, body, acc)
        pos = pos + trips * size
    return acc


def _softmax_phases(qs, k_at, v_at, qb, add, s_scr, mx_scr, l_scr, acc_scr):
    def pass1(blocks):
        mx = None
        for kb in blocks:
            s = add(kb, _nt_dot(qs, k_at(kb)), jnp.clip(kb - qb + 2, 0, 2))
            s = s * LOG2E
            s_scr[kb] = s
            fold = jnp.maximum(s[:, :LANES], s[:, LANES:])
            mx = fold if mx is None else jnp.maximum(mx, fold)
        mx_scr[...] = jnp.maximum(mx_scr[...], mx)

    def logits():
        mx_scr[...] = jnp.full(mx_scr.shape, NEG, F32)
        _for_blocks(0, qb + 1, pass1)

    def row_max():
        m = jnp.max(mx_scr[...], axis=-1, keepdims=True)
        mx_scr[...] = jnp.broadcast_to(m, mx_scr.shape)
        l_scr[...] = jnp.zeros_like(l_scr)
        acc_scr[...] = jnp.zeros_like(acc_scr)

    def pass2(kbs):
        mw = _wide(mx_scr[...])
        l_add = acc_add = None
        for kb in kbs:
            p = jnp.exp2(s_scr[kb] - mw)
            fold = p[:, :LANES] + p[:, LANES:]
            pv = _dot(p.astype(BF16), v_at(kb))
            l_add = fold if l_add is None else l_add + fold
            acc_add = pv if acc_add is None else acc_add + pv
        l_scr[...] += l_add
        acc_scr[...] += acc_add

    def weights():
        _for_blocks(0, qb + 1, pass2)

    def result():
        return acc_scr[...] / jnp.sum(l_scr[...], axis=-1, keepdims=True)

    return logits, row_max, weights, result


def _softmax_pv(*args):
    logits, row_max, weights, result = _softmax_phases(*args)
    logits()
    row_max()
    weights()
    return result()


def _softmax_scratch(s_len):
    return [pltpu.VMEM((s_len // KEY_BLOCK, 2 * Q_TILE, KEY_BLOCK), F32),
            pltpu.VMEM((2 * Q_TILE, LANES), F32),
            pltpu.VMEM((2 * Q_TILE, LANES), F32),
            pltpu.VMEM((2 * Q_TILE, LANES), F32)]


DSA_PAIRS_PER_STEP = 2


def _dsa_kernel(qa_ref, qi_ref, wi_ref, ka_ref, va_ref, ki_ref, bias_ref, o_ref,
                key_scr, keyt_scr, hi_scr, lo_scr, neg_scr, w_scr, last_scr, *softmax_scr,
                topk, idx_bits, s_len):
    qb = pl.program_id(1)

    @pl.when(pl.program_id(2) == 0)
    def _():
        _dsa_select(qb, qi_ref, wi_ref, ki_ref, key_scr, keyt_scr, hi_scr, lo_scr, neg_scr, w_scr, last_scr,
                    topk=topk, idx_bits=idx_bits, s_len=s_len)

    pairs = []
    for j in range(DSA_PAIRS_PER_STEP):
        pr = pl.program_id(2) * DSA_PAIRS_PER_STEP + j
        lanes = pl.ds(pl.multiple_of(pr * LANES, LANES), LANES)
        pairs.append(_softmax_phases(
            _split_heads(qa_ref[0, :, j * LANES:(j + 1) * LANES]),
            lambda kb, lanes=lanes: ka_ref[0, _key_rows(kb), lanes],
            lambda kb, lanes=lanes: va_ref[0, _key_rows(kb), lanes], qb,
            lambda kb, s, t, pr=pr: s + _tall(neg_scr[kb]) + bias_ref[pr, t],
            *softmax_scr[4 * j:4 * j + 4]))
    for step in range(3):
        for phases in pairs:
            phases[step]()
    for j, phases in enumerate(pairs):
        o_ref[0, :, j * LANES:(j + 1) * LANES] = _merge_heads(phases[3](), Q_TILE).astype(o_ref.dtype)


def _dsa_select(qb, qi_ref, wi_ref, ki_ref, key_scr, keyt_scr, hi_scr, lo_scr, neg_scr, w_scr, last_scr,
                *, topk, idx_bits, s_len):
    nkb = qb + 1
    kf = float(topk)
    valid_diag = _chunk_causal()

    def cols(kb):
        return kb * KEY_BLOCK + lax.broadcasted_iota(I32, (Q_TILE, KEY_BLOCK), 1)

    qi = qi_ref[0]
    q_heads = [_split_heads(qi[:, pr * LANES:(pr + 1) * LANES]) for pr in range(IDX_HEADS // 2)]
    wi = wi_ref[0].astype(F32)
    for h in range(IDX_HEADS):
        w_scr[h] = jnp.broadcast_to(wi[:, h:h + 1], (Q_TILE, LANES))

    def score_block(kb, valid):
        kk = ki_ref[0, _key_rows(kb), :]
        sc = jnp.zeros((Q_TILE, KEY_BLOCK), F32)
        for pr in range(IDX_HEADS // 2):
            d = _nt_dot(q_heads[pr], kk)
            sc = sc + _wide(w_scr[2 * pr]) * jnp.maximum(d[:Q_TILE], 0.0)
            sc = sc + _wide(w_scr[2 * pr + 1]) * jnp.maximum(d[Q_TILE:], 0.0)
        sc = jnp.where(sc == 0.0, 0.0, sc)
        if valid is not None:
            sc = jnp.where(valid, sc, -jnp.inf)
        key_scr[kb] = _sortable(sc)
        key_t = _sortable(sc.T)
        keyt_scr[kb] = key_t
        hi_scr[kb] = (key_t >> 16).astype(I16)
        lo_scr[kb] = ((key_t & 0xFFFF) - 2 ** 15).astype(I16)

    _for_blocks(0, qb, lambda kbs: [score_block(kb, None) for kb in kbs], sizes=(4, 2, 1))
    score_block(qb, valid_diag)

    def per_query(x):
        return jnp.broadcast_to(x[None], (KEY_BLOCK // 8, 8, Q_TILE)).reshape(KEY_BLOCK, Q_TILE)

    def key_index(kb):
        return kb * KEY_BLOCK + lax.broadcasted_iota(I32, (KEY_BLOCK, Q_TILE), 0)

    def count(pred):
        def one(kb, acc):
            hit = jnp.where(pred(keyt_scr[kb], kb), 1.0, 0.0)
            return acc + jnp.sum(hit.reshape(KEY_BLOCK // 8, 8, Q_TILE), axis=0)

        acc = _fold_blocks(nkb, one, jnp.zeros((8, Q_TILE), F32))
        return jnp.broadcast_to(jnp.sum(acc, axis=0, keepdims=True), (8, Q_TILE))

    def per_query16(x):
        x16 = x.astype(I16)
        return jnp.broadcast_to(x16[None], (KEY_BLOCK // 16, 16, Q_TILE)).reshape(KEY_BLOCK, Q_TILE)

    def count16(ref, pred):
        def one(kb, acc):
            hit = jnp.where(pred(ref[kb]), jnp.int16(1), jnp.int16(0))
            parts = [hit[g:g + 16] for g in range(0, KEY_BLOCK, 16)]
            while len(parts) > 1:
                parts = [parts[i] + parts[i + 1] for i in range(0, len(parts), 2)]
            return acc + parts[0]

        acc = _fold_blocks(nkb, one, jnp.zeros((16, Q_TILE), I16))
        return jnp.broadcast_to(jnp.sum(acc.astype(F32), axis=0, keepdims=True), (16, Q_TILE))

    hi_thr = jnp.where(count16(hi_scr, lambda h: h >= 0) >= kf,
                       jnp.zeros((16, Q_TILE), I32), jnp.full((16, Q_TILE), -(2 ** 15), I32))

    def hi_body(j, t):
        cand = t | lax.shift_left(jnp.int32(1), 14 - j)
        cw = per_query16(cand)
        return jnp.where(count16(hi_scr, lambda h: h >= cw) >= kf, cand, t)

    hi_thr = lax.fori_loop(0, 15, hi_body, hi_thr)
    hi_w = per_query16(hi_thr)
    above = count16(hi_scr, lambda h: h > hi_w)

    def keep_ties(kb, carry):
        lo_scr[kb] = jnp.where(hi_scr[kb] == hi_w, lo_scr[kb], jnp.int16(-(2 ** 15)))
        return carry

    lax.fori_loop(0, nkb, keep_ties, 0)

    def lo_body(j, t):
        cand = t | lax.shift_left(jnp.int32(1), 15 - j)
        cw = per_query16(cand - 2 ** 15)
        return jnp.where(above + count16(lo_scr, lambda lo: lo >= cw) >= kf, cand, t)

    lo_thr = lax.fori_loop(0, 16, lo_body, jnp.zeros((16, Q_TILE), I32))
    thr = (lax.shift_left(hi_thr, 16) | lo_thr)[0:8]
    thr_t = per_query(thr)

    need = kf - count(lambda key, kb: key > thr_t)
    ties = count(lambda key, kb: key == thr_t)
    last_scr[...] = jnp.full(last_scr.shape, s_len, I32)

    @pl.when(jnp.max(ties - need) > 0.0)
    def _():
        def tie_body(j, last):
            cand = last | lax.shift_left(jnp.int32(1), idx_bits - 1 - j)
            cw = per_query(cand)
            taken = count(lambda key, kb: (key == thr_t) & (key_index(kb) < cw))
            return jnp.where(taken < need, cand, last)

        last_scr[...] = lax.fori_loop(0, idx_bits, tie_body, jnp.zeros((8, Q_TILE), I32))

    def per_row(x):
        return _wide(jnp.broadcast_to(x[0:1], (LANES, Q_TILE)).T)

    tw = per_row(thr)
    lastw = per_row(last_scr[...])

    def mask_block(kb, valid):
        key = key_scr[kb]
        tie_neg = jnp.where(key == tw, jnp.where(cols(kb) <= lastw, 0.0, NEG), NEG)
        neg = jnp.where(key > tw, 0.0, tie_neg)
        if valid is not None:
            neg = jnp.where(valid, neg, NEG)
        neg_scr[kb] = neg

    def mask_body(kb, carry):
        mask_block(kb, None)
        return carry

    lax.fori_loop(0, qb, mask_body, 0)
    mask_block(qb, valid_diag)


def _dsa(proj, bias_tiles, bsz, s_len):
    topk = min(TOPK_MAX, s_len // 4)
    idx_bits = max(1, (s_len - 1).bit_length())
    n_blocks = s_len // KEY_BLOCK
    assert 2 * n_blocks <= 256
    kern = functools.partial(_dsa_kernel, topk=topk, idx_bits=idx_bits, s_len=s_len)
    lane_blk = lambda off: off // LANES
    n, w = DSA_PAIRS_PER_STEP, DSA_PAIRS_PER_STEP * LANES
    return pl.pallas_call(
        kern,
        grid=(bsz, s_len // Q_TILE, A_HEADS // 2 // n),
        in_specs=[pl.BlockSpec((1, Q_TILE, w), lambda b, i, p: (b, i, OFF_QA // w + p)),
                  pl.BlockSpec((1, Q_TILE, 256), lambda b, i, p: (b, i, OFF_QI // 256)),
                  pl.BlockSpec((1, Q_TILE, LANES), lambda b, i, p: (b, i, lane_blk(OFF_WI))),
                  pl.BlockSpec((1, s_len, W_A), lambda b, i, p: (b, 0, OFF_KA // W_A),
                               pipeline_mode=pl.Buffered(1)),
                  pl.BlockSpec((1, s_len, W_A), lambda b, i, p: (b, 0, OFF_VA // W_A),
                               pipeline_mode=pl.Buffered(1)),
                  pl.BlockSpec((1, s_len, LANES), lambda b, i, p: (b, 0, lane_blk(OFF_KI))),
                  pl.BlockSpec(bias_tiles.shape, lambda b, i, p: (0, 0, 0, 0),
                               pipeline_mode=pl.Buffered(1))],
        out_specs=pl.BlockSpec((1, Q_TILE, w), lambda b, i, p: (b, i, p)),
        out_shape=jax.ShapeDtypeStruct((bsz, s_len, W_A), BF16),
        scratch_shapes=[pltpu.VMEM((n_blocks, Q_TILE, KEY_BLOCK), I32),
                        pltpu.VMEM((n_blocks, KEY_BLOCK, Q_TILE), I32),
                        pltpu.VMEM((n_blocks, KEY_BLOCK, Q_TILE), I16),
                        pltpu.VMEM((n_blocks, KEY_BLOCK, Q_TILE), I16),
                        pltpu.VMEM((n_blocks, Q_TILE, KEY_BLOCK), F32),
                        pltpu.VMEM((IDX_HEADS, Q_TILE, LANES), F32),
                        pltpu.VMEM((8, Q_TILE), I32)] + _softmax_scratch(s_len) * n,
        compiler_params=_cparams("parallel", "arbitrary", "arbitrary"),
        name="dsa",
    )(proj, proj, proj, proj, proj, proj, bias_tiles)


DIFF_HEADS_PER_STEP = 2


def _diff_kernel(q_ref, k_ref, v_ref, bias_ref, lp_ref, g_ref, o_ref, *scratch, lam_init):
    qb = pl.program_id(2)
    heads = []
    for j in range(DIFF_HEADS_PER_STEP):
        lanes = slice(j * LANES, (j + 1) * LANES)
        qs = _split_heads(q_ref[0, :, lanes])
        heads.append(_softmax_phases(
            qs, lambda kb, lanes=lanes: k_ref[0, _key_rows(kb), lanes],
            lambda kb, lanes=lanes: v_ref[0, _key_rows(kb), lanes], qb,
            lambda kb, s, t, j=j: s + _tall(bias_ref[j, t]), *scratch[4 * j:4 * j + 4]))
    for step in range(3):
        for phases in heads:
            phases[step]()
    lp = lp_ref[0]
    lam = (jnp.exp(jnp.sum(lp[0:1] * lp[1:2], axis=-1, keepdims=True))
           - jnp.exp(jnp.sum(lp[2:3] * lp[3:4], axis=-1, keepdims=True)) + lam_init)
    for j, phases in enumerate(heads):
        o = phases[3]()
        o = o[:Q_TILE] - lam * o[Q_TILE:]
        y = o * lax.rsqrt(jnp.mean(o * o, axis=-1, keepdims=True) + SUBLN_EPS)
        o_ref[0, :, j * LANES:(j + 1) * LANES] = (y * g_ref[0] * (1.0 - lam_init)).astype(o_ref.dtype)


def _diff(proj, bias_tiles, lam_params, subln_g, layer, bsz, s_len, lam_init):
    kern = functools.partial(_diff_kernel, lam_init=lam_init)
    n, w = DIFF_HEADS_PER_STEP, DIFF_HEADS_PER_STEP * LANES
    return pl.pallas_call(
        kern,
        grid=(bsz, B_HEADS // n, s_len // Q_TILE),
        in_specs=[pl.BlockSpec((1, Q_TILE, w), lambda b, h, i: (b, i, OFF_QB // w + h)),
                  pl.BlockSpec((1, s_len, w), lambda b, h, i: (b, 0, OFF_KB // w + h)),
                  pl.BlockSpec((1, s_len, w), lambda b, h, i: (b, 0, OFF_VB // w + h)),
                  pl.BlockSpec((n, 3, Q_TILE, KEY_BLOCK), lambda b, h, i: (h, 0, 0, 0)),
                  pl.BlockSpec((1, 4, HEAD_DIM), lambda b, h, i: (layer, 0, 0)),
                  pl.BlockSpec((1, 1, 2 * HEAD_DIM), lambda b, h, i: (layer, 0, 0))],
        out_specs=pl.BlockSpec((1, Q_TILE, w), lambda b, h, i: (b, i, h)),
        out_shape=jax.ShapeDtypeStruct((bsz, s_len, B_HEADS * 2 * HEAD_DIM), BF16),
        scratch_shapes=_softmax_scratch(s_len) * n,
        compiler_params=_cparams("parallel", "parallel", "arbitrary"),
        name="diff_attn",
    )(proj, proj, proj, bias_tiles, lam_params, subln_g.reshape(-1, 1, 2 * HEAD_DIM))


def _stick_kernel(q_ref, k_ref, v_ref, o_ref, later_scr, run_scr, acc_scr):
    qb = pl.program_id(2)
    qs = _split_heads(q_ref[0])
    jj = lax.broadcasted_iota(I32, (2 * KEY_BLOCK, KEY_BLOCK), 0) & (KEY_BLOCK - 1)
    ss = lax.broadcasted_iota(I32, (2 * KEY_BLOCK, KEY_BLOCK), 1)
    later_scr[...] = jnp.where(jj > ss, 1.0, 0.0).astype(BF16)
    run_scr[...] = jnp.zeros_like(run_scr)
    acc_scr[...] = jnp.zeros_like(acc_scr)

    def sweep(kbs, diagonal):
        parts = []
        for j, kb in enumerate(kbs):
            z = _nt_dot(qs, k_ref[0, _key_rows(kb), :])
            soft = jnp.log(1.0 + jnp.exp2(jnp.abs(z) * (-LOG2E)))
            log_beta = jnp.minimum(z, 0.0) - soft
            log_1mb = log_beta - z
            strict = None
            if diagonal and j == 0:
                t = lax.broadcasted_iota(I32, z.shape, 0) & (Q_TILE - 1)
                strict = lax.broadcasted_iota(I32, z.shape, 1) < t
                log_1mb = jnp.where(strict, log_1mb, 0.0)
            hi = log_1mb.astype(BF16)
            lo = (log_1mb - hi.astype(F32)).astype(BF16)
            between = _dot(jnp.concatenate([hi, lo], axis=1), later_scr[...])
            total = between[:, 0:1] + log_1mb[:, 0:1]
            parts.append((kb, log_beta + between, total, strict))
        run = run_scr[...]
        acc = None
        for kb, logit, total, strict in parts:
            w = jnp.exp(logit + _wide(run))
            if strict is not None:
                w = jnp.where(strict, w, 0.0)
            pv = _dot(w.astype(BF16), v_ref[0, _key_rows(kb), :])
            acc = pv if acc is None else acc + pv
            run = run + jnp.broadcast_to(total, run.shape)
        acc_scr[...] += acc
        run_scr[...] = run

    first = jnp.where(qb + 1 >= GROUP, GROUP, jnp.where(qb + 1 >= 2, 2, 1))
    for size in (GROUP, 2, 1):
        @pl.when(first == size)
        def _(size=size):
            sweep([qb - u for u in range(size)], True)
    top = qb - first
    for size in BLOCK_GROUPS:
        trips = (top + 1) // size

        def body(i, carry, top=top, size=size):
            sweep([top - i * size - u for u in range(size)], False)
            return carry

        lax.fori_loop(0, trips, body, 0)
        top = top - trips * size
    o_ref[0] = _merge_heads(acc_scr[...], Q_TILE).astype(o_ref.dtype)


def _stick(proj, bsz, s_len):
    return pl.pallas_call(
        _stick_kernel,
        grid=(bsz, C_HEADS // 2, s_len // Q_TILE),
        in_specs=[pl.BlockSpec((1, Q_TILE, LANES), lambda b, h, i: (b, i, OFF_QC // LANES + h)),
                  pl.BlockSpec((1, s_len, LANES), lambda b, h, i: (b, 0, OFF_KC // LANES + h)),
                  pl.BlockSpec((1, s_len, LANES), lambda b, h, i: (b, 0, OFF_VC // LANES + h))],
        out_specs=pl.BlockSpec((1, Q_TILE, LANES), lambda b, h, i: (b, i, h)),
        out_shape=jax.ShapeDtypeStruct((bsz, s_len, C_HEADS * HEAD_DIM), BF16),
        scratch_shapes=[pltpu.VMEM((2 * KEY_BLOCK, KEY_BLOCK), BF16),
                        pltpu.VMEM((2 * Q_TILE, LANES), F32),
                        pltpu.VMEM((2 * Q_TILE, LANES), F32)],
        compiler_params=_cparams("parallel", "parallel", "arbitrary"),
        name="stick_attn",
    )(proj, proj, proj)


def _merge_kernel(oa_ref, ob_ref, oc_ref, gate_ref, x_ref, g1_ref, wbr_ref, wout_ref, o_ref):
    d = x_ref.shape[-1]
    merged = jnp.zeros(x_ref.shape, F32)
    for j, o_br in enumerate((oa_ref, ob_ref, oc_ref)):
        w = o_br.shape[-1]
        gate = jax.nn.sigmoid(gate_ref[:, j * d:(j + 1) * d].astype(F32))
        merged = merged + gate * _dot(o_br[...], wbr_ref[j * w:(j + 1) * w, :])
    o_ref[...] = x_ref[...] + g1_ref[0] * _dot(merged.astype(BF16), wout_ref[...])


def _merge(o_a, o_b, o_c, proj, x2, mod, w_br, w_out, s_len):
    t, d = x2.shape
    tm = 1024
    per_b = s_len // tm
    w = o_a.shape[-1]
    once = dict(pipeline_mode=pl.Buffered(1))
    return pl.pallas_call(
        _merge_kernel,
        grid=(t // tm,),
        in_specs=[pl.BlockSpec((tm, w), lambda i: (i, 0)),
                  pl.BlockSpec((tm, w), lambda i: (i, 0)),
                  pl.BlockSpec((tm, w), lambda i: (i, 0)),
                  pl.BlockSpec((tm, 3 * d), lambda i: (i, OFF_G // (3 * d))),
                  pl.BlockSpec((tm, d), lambda i: (i, 0)),
                  pl.BlockSpec((1, 1, d), lambda i: (i // per_b, 0, 2)),
                  pl.BlockSpec(w_br.shape, lambda i: (0, 0), **once),
                  pl.BlockSpec(w_out.shape, lambda i: (0, 0), **once)],
        out_specs=pl.BlockSpec((tm, d), lambda i: (i, 0)),
        out_shape=jax.ShapeDtypeStruct((t, d), F32),
        compiler_params=_cparams("parallel"),
        name="merge",
    )(o_a, o_b, o_c, proj, x2, mod, w_br, w_out)


def _finish(x, gate, acc, fg_ref, final):
    y = x + gate * acc
    if final:
        y = y * lax.rsqrt(jnp.mean(y * y, axis=-1, keepdims=True) + EPS) * fg_ref[...]
    return y


def _swiglu_partial(h, w1, w3, w2):
    a = _dot(h, w1)
    act = (a * jax.nn.sigmoid(a)) * _dot(h, w3)
    return _dot(act.astype(BF16), w2)


def _ffn_kernel(x_ref, g_ref, sc_ref, sh_ref, gate_ref, w1_ref, w3_ref, w2_ref, fg_ref, o_ref,
                h_scr, acc_scr, *, final):
    f = pl.program_id(1)

    @pl.when(f == 0)
    def _():
        h_scr[...] = _modulated_norm(x_ref[...], g_ref[...], sc_ref[0], sh_ref[0]).astype(BF16)
        acc_scr[...] = jnp.zeros_like(acc_scr)

    acc_scr[...] += _swiglu_partial(h_scr[...], w1_ref[...], w3_ref[...], w2_ref[...])

    @pl.when(f == pl.num_programs(1) - 1)
    def _():
        o_ref[...] = _finish(x_ref[...], gate_ref[0], acc_scr[...], fg_ref, final)


def _ffn(x2, g, mod, w1, w3, w2, final_g, s_len, final):
    t, d = x2.shape
    d_ff = w1.shape[1]
    tm, tf = 512, d_ff
    once = dict(pipeline_mode=pl.Buffered(1))
    per_b = s_len // tm
    kern = functools.partial(_ffn_kernel, final=final)
    return pl.pallas_call(
        kern,
        grid=(t // tm, d_ff // tf),
        in_specs=[pl.BlockSpec((tm, d), lambda i, f: (i, 0)),
                  pl.BlockSpec((1, d), lambda i, f: (0, 0)),
                  pl.BlockSpec((1, 1, d), lambda i, f: (i // per_b, 0, 4)),
                  pl.BlockSpec((1, 1, d), lambda i, f: (i // per_b, 0, 3)),
                  pl.BlockSpec((1, 1, d), lambda i, f: (i // per_b, 0, 5)),
                  pl.BlockSpec((d, tf), lambda i, f: (0, f), **once),
                  pl.BlockSpec((d, tf), lambda i, f: (0, f), **once),
                  pl.BlockSpec((tf, d), lambda i, f: (f, 0), **once),
                  pl.BlockSpec((1, d), lambda i, f: (0, 0))],
        out_specs=pl.BlockSpec((tm, d), lambda i, f: (i, 0)),
        out_shape=jax.ShapeDtypeStruct((t, d), F32),
        scratch_shapes=[pltpu.VMEM((tm, d), BF16), pltpu.VMEM((tm, d), F32)],
        compiler_params=_cparams("parallel", "arbitrary"),
        name="ffn",
    )(x2, g, mod, mod, mod, w1, w3, w2, final_g)


MOE_TM = 512
MOE_ROWS = 512
R_E1, R_E2, R_RANK1, R_RANK2, R_W1, R_W2 = 0, 1, 2, 3, 4, 5


def _lane_pick(tile, lane, idx):
    return jnp.sum(jnp.where(lane == idx, tile, 0.0), axis=-1, keepdims=True)


def _route_kernel(x_ref, g_ref, sc_ref, sh_ref, wr_ref, br_ref, h_ref, rec_ref, cnt_ref, cnt_scr):
    @pl.when(pl.program_id(0) == 0)
    def _():
        cnt_scr[...] = jnp.zeros_like(cnt_scr)

    h = _modulated_norm(x_ref[...], g_ref[...], sc_ref[0], sh_ref[0])
    half = h.shape[1] // 2
    bits = pltpu.bitcast(h.astype(BF16).astype(F32), jnp.uint32)
    h_ref[...] = (bits[:, half:] & jnp.uint32(0xFFFF0000)) | (bits[:, :half] >> 16)

    h_hi = h.astype(BF16)
    h_lo = (h - h_hi.astype(F32)).astype(BF16)
    wr = wr_ref[...]
    w_hi = wr.astype(BF16)
    w_lo = (wr - w_hi.astype(F32)).astype(BF16)
    logits = _dot(h_hi, w_hi) + _dot(h_hi, w_lo) + _dot(h_lo, w_hi) + br_ref[...]
    lane = lax.broadcasted_iota(I32, logits.shape, 1).astype(F32)
    lg = jnp.where(lane < N_EXPERTS, logits, -jnp.inf)
    m1 = jnp.max(lg, axis=-1, keepdims=True)
    i1 = jnp.min(jnp.where(lg == m1, lane, float(LANES)), axis=-1, keepdims=True)
    lg2 = jnp.where(lane == i1, -jnp.inf, lg)
    m2 = jnp.max(lg2, axis=-1, keepdims=True)
    i2 = jnp.min(jnp.where(lg2 == m2, lane, float(LANES)), axis=-1, keepdims=True)
    e2 = jnp.exp(m2 - m1)
    w_top = 1.0 / (1.0 + e2)

    chosen = jnp.where(lane == i1, 1.0, 0.0) + jnp.where(lane == i2, 1.0, 0.0)
    tm = chosen.shape[0]
    earlier = jnp.where(lax.broadcasted_iota(I32, (tm, tm), 1) < lax.broadcasted_iota(I32, (tm, tm), 0),
                        1.0, 0.0).astype(BF16)
    rank = _dot(earlier, chosen.astype(BF16)) + cnt_scr[0:1, :]
    rec = jnp.zeros_like(logits)
    for slot, val in ((R_E1, i1), (R_E2, i2), (R_RANK1, _lane_pick(rank, lane, i1)),
                      (R_RANK2, _lane_pick(rank, lane, i2)), (R_W1, w_top), (R_W2, e2 * w_top)):
        rec = jnp.where(lane == float(slot), val, rec)
    rec_ref[...] = rec
    cnt_scr[0:1, :] = cnt_scr[0:1, :] + jnp.sum(chosen, axis=0, keepdims=True)
    cnt_ref[...] = cnt_scr[...]


def _route(x2, g, mod, wr_pad, br_pad, s_len):
    t, d = x2.shape
    tm = MOE_TM
    per_b = s_len // tm
    return pl.pallas_call(
        _route_kernel,
        grid=(t // tm,),
        in_specs=[pl.BlockSpec((tm, d), lambda i: (i, 0)),
                  pl.BlockSpec((1, d), lambda i: (0, 0)),
                  pl.BlockSpec((1, 1, d), lambda i: (i // per_b, 0, 4)),
                  pl.BlockSpec((1, 1, d), lambda i: (i // per_b, 0, 3)),
                  pl.BlockSpec((d, LANES), lambda i: (0, 0)),
                  pl.BlockSpec((1, LANES), lambda i: (0, 0))],
        out_specs=[pl.BlockSpec((tm, d // 2), lambda i: (i, 0)),
                   pl.BlockSpec((tm, LANES), lambda i: (i, 0)),
                   pl.BlockSpec((8, LANES), lambda i: (0, 0))],
        out_shape=[jax.ShapeDtypeStruct((t, d // 2), jnp.uint32),
                   jax.ShapeDtypeStruct((t, LANES), F32),
                   jax.ShapeDtypeStruct((8, LANES), F32)],
        scratch_shapes=[pltpu.VMEM((8, LANES), F32)],
        compiler_params=_cparams("arbitrary"),
        name="moe_route",
    )(x2, g, mod, mod, wr_pad, br_pad)


def _row_copy(src_ref, src_row, dst_ref, dst_row, sem):
    return pltpu.make_async_copy(src_ref.at[pl.ds(src_row, 1)], dst_ref.at[pl.ds(dst_row, 1)], sem)


def _dispatch_kernel(pos_ref, h_ref, xs_in_ref, xs_ref, sem):
    del xs_in_ref
    tm = h_ref.shape[0]

    def start(t, carry):
        _row_copy(h_ref, t, xs_ref, pos_ref[0, 0, t], sem).start(priority=0)
        _row_copy(h_ref, t, xs_ref, pos_ref[0, 0, tm + t], sem).start(priority=1)
        return carry

    lax.fori_loop(0, tm, start, 0, unroll=4)
    for _ in range(2):
        pltpu.make_async_copy(h_ref, xs_ref.at[pl.ds(0, tm)], sem).wait()


def _dispatch(pos, h_packed, n_rows):
    t, w = h_packed.shape
    tm = MOE_TM
    xs0 = jnp.zeros((n_rows, w), h_packed.dtype)
    return pl.pallas_call(
        _dispatch_kernel,
        grid=(t // tm,),
        in_specs=[pl.BlockSpec((1, 1, 2 * tm), lambda i: (i, 0, 0), memory_space=pltpu.SMEM),
                  pl.BlockSpec((tm, w), lambda i: (i, 0)),
                  pl.BlockSpec(memory_space=pl.ANY)],
        out_specs=pl.BlockSpec(memory_space=pl.ANY),
        out_shape=jax.ShapeDtypeStruct((n_rows, w), h_packed.dtype),
        scratch_shapes=[pltpu.SemaphoreType.DMA(())],
        input_output_aliases={2: 0},
        compiler_params=_cparams("arbitrary"),
        name="moe_dispatch",
    )(pos, h_packed, xs0)


def _expert_kernel(te_ref, nu_ref, xs_ref, w1_ref, w3_ref, w2_ref, o_ref, h_scr, acc_scr):
    i = pl.program_id(0)
    f = pl.program_id(1)

    @pl.when(i < nu_ref[0])
    def _():
        @pl.when(f == 0)
        def _():
            word = xs_ref[...]
            lo = pltpu.bitcast(word << 16, F32)
            hi = pltpu.bitcast(word & jnp.uint32(0xFFFF0000), F32)
            h_scr[...] = jnp.concatenate([lo, hi], axis=1).astype(BF16)
            acc_scr[...] = jnp.zeros_like(acc_scr)

        acc_scr[...] += _swiglu_partial(h_scr[...], w1_ref[0], w3_ref[0], w2_ref[0])

        @pl.when(f == pl.num_programs(1) - 1)
        def _():
            o_ref[...] = acc_scr[...]

    @pl.when((i >= nu_ref[0]) & (f == pl.num_programs(1) - 1))
    def _():
        o_ref[...] = jnp.zeros_like(o_ref)


def _experts(tile_expert, n_used, xs, w1, w3, w2):
    n_rows, half = xs.shape
    d = 2 * half
    d_ff = w1.shape[2]
    tf = 1792
    nf = d_ff // tf
    tile = lambda i, nu: jnp.minimum(i, nu[0] - 1)
    ff = lambda i, f, nu: jnp.where(i < nu[0], f, nf - 1)
    return pl.pallas_call(
        _expert_kernel,
        grid_spec=pltpu.PrefetchScalarGridSpec(
            num_scalar_prefetch=2,
            grid=(n_rows // MOE_ROWS, nf),
            in_specs=[pl.BlockSpec((MOE_ROWS, half), lambda i, f, te, nu: (tile(i, nu), 0)),
                      pl.BlockSpec((1, d, tf), lambda i, f, te, nu: (te[i], 0, ff(i, f, nu))),
                      pl.BlockSpec((1, d, tf), lambda i, f, te, nu: (te[i], 0, ff(i, f, nu))),
                      pl.BlockSpec((1, tf, d), lambda i, f, te, nu: (te[i], ff(i, f, nu), 0))],
            out_specs=pl.BlockSpec((MOE_ROWS, d), lambda i, f, te, nu: (i, 0)),
            scratch_shapes=[pltpu.VMEM((MOE_ROWS, d), BF16), pltpu.VMEM((MOE_ROWS, d), F32)]),
        out_shape=jax.ShapeDtypeStruct((n_rows, d), F32),
        compiler_params=_cparams("arbitrary", "arbitrary"),
        name="moe_experts",
    )(tile_expert, n_used, xs, w1, w3, w2)


def _combine_kernel(pos_ref, x_ref, gate_ref, rec_ref, fg_ref, ys_ref, o_ref, y_scr, sem, *, final):
    tm = x_ref.shape[0]

    def start(t, carry):
        _row_copy(ys_ref, pos_ref[0, 0, t], y_scr.at[0], t, sem).start(priority=0)
        _row_copy(ys_ref, pos_ref[0, 0, tm + t], y_scr.at[1], t, sem).start(priority=1)
        return carry

    lax.fori_loop(0, tm, start, 0, unroll=4)
    for slot in range(2):
        pltpu.make_async_copy(ys_ref.at[pl.ds(0, tm)], y_scr.at[slot], sem).wait()
    rec = rec_ref[...]
    lane = lax.broadcasted_iota(I32, rec.shape, 1)
    w_first = jnp.sum(jnp.where(lane == R_W1, rec, 0.0), axis=-1, keepdims=True)
    w_second = jnp.sum(jnp.where(lane == R_W2, rec, 0.0), axis=-1, keepdims=True)
    f = w_first * y_scr[0] + w_second * y_scr[1]
    o_ref[...] = _finish(x_ref[...], gate_ref[0], f, fg_ref, final)


def _combine(pos, x2, mod, rec, final_g, ys, s_len, final):
    t, d = x2.shape
    tm = MOE_TM
    per_b = s_len // tm
    kern = functools.partial(_combine_kernel, final=final)
    return pl.pallas_call(
        kern,
        grid=(t // tm,),
        in_specs=[pl.BlockSpec((1, 1, 2 * tm), lambda i: (i, 0, 0), memory_space=pltpu.SMEM),
                  pl.BlockSpec((tm, d), lambda i: (i, 0)),
                  pl.BlockSpec((1, 1, d), lambda i: (i // per_b, 0, 5)),
                  pl.BlockSpec((tm, LANES), lambda i: (i, 0)),
                  pl.BlockSpec((1, d), lambda i: (0, 0)),
                  pl.BlockSpec(memory_space=pl.ANY)],
        out_specs=pl.BlockSpec((tm, d), lambda i: (i, 0)),
        out_shape=jax.ShapeDtypeStruct((t, d), F32),
        scratch_shapes=[pltpu.VMEM((2, tm, d), F32), pltpu.SemaphoreType.DMA(())],
        compiler_params=_cparams("arbitrary"),
        name="moe_combine",
    )(pos, x2, mod, rec, final_g, ys)


def _moe(x2, g, mod, wr_pad, br_pad, w1, w3, w2, final_g, s_len, final):
    t, d = x2.shape
    n_e = w1.shape[0]
    h_packed, rec, cnt = _route(x2, g, mod, wr_pad, br_pad, s_len)

    n_tiles = 2 * t // MOE_ROWS + n_e
    counts = cnt[0, :n_e].astype(I32)
    padded = (counts + MOE_ROWS - 1) // MOE_ROWS * MOE_ROWS
    ends = jnp.cumsum(padded)
    starts = ends - padded
    e1, e2 = rec[:, R_E1].astype(I32), rec[:, R_E2].astype(I32)
    pos1 = starts[e1] + rec[:, R_RANK1].astype(I32)
    pos2 = starts[e2] + rec[:, R_RANK2].astype(I32)
    pos = jnp.concatenate([pos1.reshape(-1, 1, MOE_TM), pos2.reshape(-1, 1, MOE_TM)], axis=2)
    n_used = (ends[-1] // MOE_ROWS).reshape(1)
    tile_start = jnp.minimum(jnp.arange(n_tiles, dtype=I32), n_used[0] - 1) * MOE_ROWS
    tile_expert = jnp.sum(tile_start[:, None] >= ends[None, :], axis=1).astype(I32)

    xs = _dispatch(pos, h_packed, n_tiles * MOE_ROWS)
    ys = _experts(tile_expert, n_used, xs, w1, w3, w2)
    return _combine(pos, x2, mod, rec, final_g, ys, s_len, final)


def _pack_w_in(w):
    d = w.shape[0]
    sizes = (W_A, W_A, W_A, IDX_HEADS * IDX_DIM, IDX_DIM, IDX_HEADS,
             512, 512, 512, 512, 512, 512, 3 * d)
    offs = [0]
    for s in sizes:
        offs.append(offs[-1] + s)
    (qa, ka, va, qi, ki, wi, qb, kb, vb, qc, kc, vc, gl) = [
        w[:, offs[j]:offs[j + 1]] for j in range(len(sizes))]
    scale = HEAD_DIM ** -0.5
    pad_wi = jnp.zeros((d, LANES - IDX_HEADS), w.dtype)
    packed = jnp.concatenate(
        [gl, qa * scale, ka, va,
         qi * (IDX_DIM ** -0.5), ki, ki, wi * (IDX_HEADS ** -0.5), pad_wi,
         qb * scale, kb, vb, qc * scale, kc, vc], axis=1)
    assert packed.shape[1] == PACKED
    return packed.astype(BF16)


def kernel(x, c, w_ada, b_ada, norm1_g, norm2_g, w_in, w_br, w_out, rel_bias, lam_params,
           subln_g, ffn_w1, ffn_w3, ffn_w2, router_w, router_b, moe_w1, moe_w3, moe_w2, final_g):
    bsz, s_len, d = x.shape
    depth = w_ada.shape[0]
    assert s_len % 1024 == 0 and d == 1024 and OFF_G + 3 * d == OFF_QA

    c_pad = jnp.concatenate([c, jnp.zeros((8 - bsz % 8 if bsz % 8 else 0, d), c.dtype)], axis=0)
    mod_all = _ada(c_pad, w_ada, b_ada)
    dsa_tiles, diff_tiles = _bias_tiles(rel_bias)
    fg = final_g.reshape(1, d)

    x2 = x.reshape(bsz * s_len, d)
    for l in range(depth):
        mod = mod_all[l, :bsz].reshape(bsz, 1, 6 * d)
        proj = _inproj(x2, norm1_g[l].reshape(1, d), mod, _pack_w_in(w_in[l]), s_len)
        proj3 = proj.reshape(bsz, s_len, PACKED)
        lam_init = 0.8 - 0.6 * math.exp(-0.3 * l)
        o_a = _dsa(proj3, dsa_tiles, bsz, s_len)
        o_b = _diff(proj3, diff_tiles, lam_params, subln_g, l, bsz, s_len, lam_init)
        o_c = _stick(proj3, bsz, s_len)
        x2 = _merge(o_a.reshape(-1, o_a.shape[-1]), o_b.reshape(-1, o_b.shape[-1]),
                    o_c.reshape(-1, o_c.shape[-1]), proj, x2, mod,
                    w_br[l].astype(BF16), w_out[l].astype(BF16), s_len)
        g2 = norm2_g[l].reshape(1, d)
        final = l == depth - 1
        j = l // 2
        if l % 2 == 0:
            x2 = _ffn(x2, g2, mod, ffn_w1[j].astype(BF16), ffn_w3[j].astype(BF16),
                      ffn_w2[j].astype(BF16), fg, s_len, final)
        else:
            wr_pad = jnp.pad(router_w[j], ((0, 0), (0, LANES - N_EXPERTS)))
            br_pad = jnp.pad(router_b[j], (0, LANES - N_EXPERTS)).reshape(1, LANES)
            x2 = _moe(x2, g2, mod, wr_pad, br_pad, moe_w1[j].astype(BF16), moe_w3[j].astype(BF16),
                      moe_w2[j].astype(BF16), fg, s_len, final)
    return x2.reshape(bsz, s_len, d)
```

```python
import functools
import math

import jax
import jax.numpy as jnp
from jax import lax
from jax.experimental import pallas as pl
from jax.experimental.pallas import tpu as pltpu

F32 = jnp.float32
BF16 = jnp.bfloat16
I32 = jnp.int32
I16 = jnp.int16

LANES = 128
VMEM_LIMIT_BYTES = 56 * 1024 * 1024

CHUNK = 64
A_HEADS = 8
IDX_HEADS = 4
IDX_DIM = 64
TOPK_MAX = 256
B_HEADS = 4
C_HEADS = 8
HEAD_DIM = 64
REL_BUCKETS = 32
FAR_BUCKET = REL_BUCKETS // 2 - 1
N_EXPERTS = 8
EPS = 1e-6
SUBLN_EPS = 1e-5
NEG = -1e30
LOG2E = 1.4426950408889634
INT_MIN = -(2 ** 31)

KEY_BLOCK = 256
Q_TILE = 256
GROUP = 4
BLOCK_GROUPS = (8, 4, 2, 1)

W_A = A_HEADS * HEAD_DIM
OFF_G = 0
OFF_QA, OFF_KA, OFF_VA = 3072, 3584, 4096
OFF_QI, OFF_KI, OFF_WI = 4608, 4864, 4992
OFF_QB, OFF_KB, OFF_VB = 5120, 5632, 6144
OFF_QC, OFF_KC, OFF_VC = 6656, 7168, 7680
PACKED = 8192

LOG_BUCKET_STEPS = (12, 16, 23, 32, 46, 64, 91)


def _nt_dot(a, b):
    return lax.dot_general(a, b, (((1,), (1,)), ((), ())), preferred_element_type=F32)


def _dot(a, b):
    return jnp.dot(a, b, preferred_element_type=F32)


def _cparams(*sem):
    return pltpu.CompilerParams(dimension_semantics=sem, vmem_limit_bytes=VMEM_LIMIT_BYTES)


def _split_heads(x):
    lane = lax.broadcasted_iota(I32, x.shape, 1)
    keep_a = jnp.where(lane < HEAD_DIM, 1.0, 0.0).astype(x.dtype)
    keep_b = jnp.where(lane < HEAD_DIM, 0.0, 1.0).astype(x.dtype)
    return jnp.concatenate([x * keep_a, x * keep_b], axis=0)


def _merge_heads(o, m):
    lane = lax.broadcasted_iota(I32, (m, LANES), 1)
    return jnp.where(lane < HEAD_DIM, o[:m], o[m:])


def _wide(x):
    return jnp.concatenate([x, x], axis=1)


def _tall(x):
    return jnp.concatenate([x, x], axis=0)


def _key_rows(kb):
    return pl.ds(pl.multiple_of(kb * KEY_BLOCK, KEY_BLOCK), KEY_BLOCK)


def _sortable(x):
    bits = pltpu.bitcast(x, I32)
    return bits ^ ((bits >> 31) & 0x7FFFFFFF)


def _chunk_causal():
    r = lax.broadcasted_iota(I32, (Q_TILE, KEY_BLOCK), 0)
    c = lax.broadcasted_iota(I32, (Q_TILE, KEY_BLOCK), 1)
    return (c // CHUNK) <= (r // CHUNK)


def _ada_kernel(c_ref, w_ref, b_ref, o_ref):
    c = c_ref[...]
    a = c * jax.nn.sigmoid(c)
    o_ref[0] = jnp.dot(a, w_ref[0], preferred_element_type=F32,
                       precision=lax.Precision.HIGHEST) + b_ref[0]


def _ada(c_pad, w_ada, b_ada):
    depth, d, n = w_ada.shape
    tn = 1024
    return pl.pallas_call(
        _ada_kernel,
        grid=(depth, n // tn),
        in_specs=[pl.BlockSpec(c_pad.shape, lambda l, j: (0, 0)),
                  pl.BlockSpec((1, d, tn), lambda l, j: (l, 0, j)),
                  pl.BlockSpec((1, 1, tn), lambda l, j: (l, 0, j))],
        out_specs=pl.BlockSpec((1, c_pad.shape[0], tn), lambda l, j: (l, 0, j)),
        out_shape=jax.ShapeDtypeStruct((depth, c_pad.shape[0], n), F32),
        compiler_params=_cparams("parallel", "parallel"),
        name="ada",
    )(c_pad, w_ada, b_ada.reshape(depth, 1, n))


def _rel_bias_tile(tab_ref, head, d0, n_heads_total):
    r = lax.broadcasted_iota(I32, (Q_TILE, KEY_BLOCK), 0)
    c = lax.broadcasted_iota(I32, (Q_TILE, KEY_BLOCK), 1)
    d = c - r + d0
    n = jnp.abs(d)
    large = jnp.full(d.shape, REL_BUCKETS // 4, I32)
    for step in LOG_BUCKET_STEPS:
        large = large + jnp.where(n >= step, 1, 0)
    bucket = jnp.where(d > 0, REL_BUCKETS // 2, 0) + jnp.where(n < REL_BUCKETS // 4, n, large)
    out = jnp.zeros(d.shape, F32)
    for b in range(REL_BUCKETS):
        out = jnp.where(bucket == b, tab_ref[b * n_heads_total + head], out)
    return out - tab_ref[FAR_BUCKET * n_heads_total + head]


def _bias_kernel(tab_ref, dsa_ref, diff_ref):
    p = pl.program_id(0)
    n_heads = A_HEADS + B_HEADS
    dsa_ref[0, 0] = jnp.zeros(dsa_ref.shape[2:], F32)
    diff_ref[0, 0] = jnp.zeros(diff_ref.shape[2:], F32)
    for t, d0 in ((1, -KEY_BLOCK), (2, 0)):
        dsa_ref[0, t, 0:Q_TILE, :] = _rel_bias_tile(tab_ref, 2 * p, d0, n_heads)
        dsa_ref[0, t, Q_TILE:2 * Q_TILE, :] = _rel_bias_tile(tab_ref, 2 * p + 1, d0, n_heads)
        tile = _rel_bias_tile(tab_ref, A_HEADS + p, d0, n_heads)
        if t == 2:
            tile = jnp.where(_chunk_causal(), tile, NEG)
        diff_ref[0, t] = tile


def _bias_tiles(rel_bias):
    tab = rel_bias.reshape(-1)
    return pl.pallas_call(
        _bias_kernel,
        grid=(4,),
        in_specs=[pl.BlockSpec(memory_space=pltpu.SMEM)],
        out_specs=[pl.BlockSpec((1, 3, 2 * Q_TILE, KEY_BLOCK), lambda p: (p, 0, 0, 0)),
                   pl.BlockSpec((1, 3, Q_TILE, KEY_BLOCK), lambda p: (p, 0, 0, 0))],
        out_shape=[jax.ShapeDtypeStruct((A_HEADS // 2, 3, 2 * Q_TILE, KEY_BLOCK), F32),
                   jax.ShapeDtypeStruct((B_HEADS, 3, Q_TILE, KEY_BLOCK), F32)],
        compiler_params=_cparams("parallel"),
        name="rel_bias_tiles",
    )(tab)


def _modulated_norm(x, g, sc, sh):
    y = x * lax.rsqrt(jnp.mean(x * x, axis=-1, keepdims=True) + EPS)
    return y * g * (1.0 + sc) + sh


def _inproj_kernel(x_ref, g_ref, sc_ref, sh_ref, w_ref, o_ref, h_scr):
    @pl.when(pl.program_id(1) == 0)
    def _():
        h_scr[...] = _modulated_norm(x_ref[...], g_ref[...], sc_ref[0], sh_ref[0]).astype(BF16)

    o_ref[...] = _dot(h_scr[...], w_ref[...]).astype(o_ref.dtype)


def _inproj(x2, g, mod, w_packed, s_len):
    t, d = x2.shape
    n = w_packed.shape[1]
    tm, tn = 1024, 2048
    per_b = s_len // tm
    return pl.pallas_call(
        _inproj_kernel,
        grid=(t // tm, n // tn),
        in_specs=[pl.BlockSpec((tm, d), lambda i, j: (i, 0)),
                  pl.BlockSpec((1, d), lambda i, j: (0, 0)),
                  pl.BlockSpec((1, 1, d), lambda i, j: (i // per_b, 0, 1)),
                  pl.BlockSpec((1, 1, d), lambda i, j: (i // per_b, 0, 0)),
                  pl.BlockSpec((d, tn), lambda i, j: (0, j))],
        out_specs=pl.BlockSpec((tm, tn), lambda i, j: (i, j)),
        out_shape=jax.ShapeDtypeStruct((t, n), BF16),
        scratch_shapes=[pltpu.VMEM((tm, d), BF16)],
        compiler_params=_cparams("parallel", "arbitrary"),
        name="inproj",
    )(x2, g, mod, mod, w_packed)


def _for_blocks(lo, hi, fn, sizes=BLOCK_GROUPS):
    pos = lo
    for size in sizes:
        trips = jnp.maximum(hi - pos, 0) // size

        def body(i, carry, pos=pos, size=size):
            fn([pos + i * size + u for u in range(size)])
            return carry

        lax.fori_loop(0, trips, body, 0)
        pos = pos + trips * size


def _fold_blocks(n, fn, acc):
    pos = 0
    for size in BLOCK_GROUPS:
        trips = (n - pos) // size

        def body(i, acc, pos=pos, size=size):
            for u in range(size):
                acc = fn(pos + i * size + u, acc)
            return acc

        acc = lax.fori_loop(0, trips, body, acc)
        pos = pos + trips * size
    return acc


def _softmax_phases(qs, k_at, v_at, qb, add, s_scr, mx_scr, l_scr, acc_scr):
    def pass1(blocks):
        mx = None
        for kb in blocks:
            s = add(kb, _nt_dot(qs, k_at(kb)), jnp.clip(kb - qb + 2, 0, 2))
            s = s * LOG2E
            s_scr[kb] = s
            fold = jnp.maximum(s[:, :LANES], s[:, LANES:])
            mx = fold if mx is None else jnp.maximum(mx, fold)
        mx_scr[...] = jnp.maximum(mx_scr[...], mx)

    def logits():
        mx_scr[...] = jnp.full(mx_scr.shape, NEG, F32)
        _for_blocks(0, qb + 1, pass1)

    def row_max():
        m = jnp.max(mx_scr[...], axis=-1, keepdims=True)
        mx_scr[...] = jnp.broadcast_to(m, mx_scr.shape)
        l_scr[...] = jnp.zeros_like(l_scr)
        acc_scr[...] = jnp.zeros_like(acc_scr)

    def pass2(kbs):
        mw = _wide(mx_scr[...])
        l_add = acc_add = None
        for kb in kbs:
            p = jnp.exp2(s_scr[kb] - mw)
            fold = p[:, :LANES] + p[:, LANES:]
            pv = _dot(p.astype(BF16), v_at(kb))
            l_add = fold if l_add is None else l_add + fold
            acc_add = pv if acc_add is None else acc_add + pv
        l_scr[...] += l_add
        acc_scr[...] += acc_add

    def weights():
        _for_blocks(0, qb + 1, pass2)

    def result():
        return acc_scr[...] / jnp.sum(l_scr[...], axis=-1, keepdims=True)

    return logits, row_max, weights, result


def _softmax_pv(*args):
    logits, row_max, weights, result = _softmax_phases(*args)
    logits()
    row_max()
    weights()
    return result()


def _softmax_scratch(s_len):
    return [pltpu.VMEM((s_len // KEY_BLOCK, 2 * Q_TILE, KEY_BLOCK), F32),
            pltpu.VMEM((2 * Q_TILE, LANES), F32),
            pltpu.VMEM((2 * Q_TILE, LANES), F32),
            pltpu.VMEM((2 * Q_TILE, LANES), F32)]


DSA_PAIRS_PER_STEP = 2


def _dsa_kernel(qa_ref, qi_ref, wi_ref, ka_ref, va_ref, ki_ref, bias_ref, o_ref,
                key_scr, keyt_scr, hi_scr, lo_scr, neg_scr, w_scr, last_scr, *softmax_scr,
                topk, idx_bits, s_len):
    qb = pl.program_id(1)

    @pl.when(pl.program_id(2) == 0)
    def _():
        _dsa_select(qb, qi_ref, wi_ref, ki_ref, key_scr, keyt_scr, hi_scr, lo_scr, neg_scr, w_scr, last_scr,
                    topk=topk, idx_bits=idx_bits, s_len=s_len)

    pairs = []
    for j in range(DSA_PAIRS_PER_STEP):
        pr = pl.program_id(2) * DSA_PAIRS_PER_STEP + j
        lanes = pl.ds(pl.multiple_of(pr * LANES, LANES), LANES)
        pairs.append(_softmax_phases(
            _split_heads(qa_ref[0, :, j * LANES:(j + 1) * LANES]),
            lambda kb, lanes=lanes: ka_ref[0, _key_rows(kb), lanes],
            lambda kb, lanes=lanes: va_ref[0, _key_rows(kb), lanes], qb,
            lambda kb, s, t, pr=pr: s + _tall(neg_scr[kb]) + bias_ref[pr, t],
            *softmax_scr[4 * j:4 * j + 4]))
    for step in range(3):
        for phases in pairs:
            phases[step]()
    for j, phases in enumerate(pairs):
        o_ref[0, :, j * LANES:(j + 1) * LANES] = _merge_heads(phases[3](), Q_TILE).astype(o_ref.dtype)


def _dsa_select(qb, qi_ref, wi_ref, ki_ref, key_scr, keyt_scr, hi_scr, lo_scr, neg_scr, w_scr, last_scr,
                *, topk, idx_bits, s_len):
    nkb = qb + 1
    kf = float(topk)
    valid_diag = _chunk_causal()

    def cols(kb):
        return kb * KEY_BLOCK + lax.broadcasted_iota(I32, (Q_TILE, KEY_BLOCK), 1)

    qi = qi_ref[0]
    q_heads = [_split_heads(qi[:, pr * LANES:(pr + 1) * LANES]) for pr in range(IDX_HEADS // 2)]
    wi = wi_ref[0].astype(F32)
    for h in range(IDX_HEADS):
        w_scr[h] = jnp.broadcast_to(wi[:, h:h + 1], (Q_TILE, LANES))

    def score_block(kb, valid):
        kk = ki_ref[0, _key_rows(kb), :]
        sc = jnp.zeros((Q_TILE, KEY_BLOCK), F32)
        for pr in range(IDX_HEADS // 2):
            d = _nt_dot(q_heads[pr], kk)
            sc = sc + _wide(w_scr[2 * pr]) * jnp.maximum(d[:Q_TILE], 0.0)
            sc = sc + _wide(w_scr[2 * pr + 1]) * jnp.maximum(d[Q_TILE:], 0.0)
        sc = jnp.where(sc == 0.0, 0.0, sc)
        if valid is not None:
            sc = jnp.where(valid, sc, -jnp.inf)
        key_scr[kb] = _sortable(sc)
        key_t = _sortable(sc.T)
        keyt_scr[kb] = key_t
        hi_scr[kb] = (key_t >> 16).astype(I16)
        lo_scr[kb] = ((key_t & 0xFFFF) - 2 ** 15).astype(I16)

    _for_blocks(0, qb, lambda kbs: [score_block(kb, None) for kb in kbs], sizes=(4, 2, 1))
    score_block(qb, valid_diag)

    def per_query(x):
        return jnp.broadcast_to(x[None], (KEY_BLOCK // 8, 8, Q_TILE)).reshape(KEY_BLOCK, Q_TILE)

    def key_index(kb):
        return kb * KEY_BLOCK + lax.broadcasted_iota(I32, (KEY_BLOCK, Q_TILE), 0)

    def count(pred):
        def one(kb, acc):
            hit = jnp.where(pred(keyt_scr[kb], kb), 1.0, 0.0)
            return acc + jnp.sum(hit.reshape(KEY_BLOCK // 8, 8, Q_TILE), axis=0)

        acc = _fold_blocks(nkb, one, jnp.zeros((8, Q_TILE), F32))
        return jnp.broadcast_to(jnp.sum(acc, axis=0, keepdims=True), (8, Q_TILE))

    def per_query16(x):
        x16 = x.astype(I16)
        return jnp.broadcast_to(x16[None], (KEY_BLOCK // 16, 16, Q_TILE)).reshape(KEY_BLOCK, Q_TILE)

    def count16(ref, pred):
        def one(kb, acc):
            hit = jnp.where(pred(ref[kb]), jnp.int16(1), jnp.int16(0))
            parts = [hit[g:g + 16] for g in range(0, KEY_BLOCK, 16)]
            while len(parts) > 1:
                parts = [parts[i] + parts[i + 1] for i in range(0, len(parts), 2)]
            return acc + parts[0]

        acc = _fold_blocks(nkb, one, jnp.zeros((16, Q_TILE), I16))
        return jnp.broadcast_to(jnp.sum(acc.astype(F32), axis=0, keepdims=True), (16, Q_TILE))

    hi_thr = jnp.where(count16(hi_scr, lambda h: h >= 0) >= kf,
                       jnp.zeros((16, Q_TILE), I32), jnp.full((16, Q_TILE), -(2 ** 15), I32))

    def hi_body(j, t):
        cand = t | lax.shift_left(jnp.int32(1), 14 - j)
        cw = per_query16(cand)
        return jnp.where(count16(hi_scr, lambda h: h >= cw) >= kf, cand, t)

    hi_thr = lax.fori_loop(0, 15, hi_body, hi_thr)
    hi_w = per_query16(hi_thr)
    above = count16(hi_scr, lambda h: h > hi_w)

    def keep_ties(kb, carry):
        lo_scr[kb] = jnp.where(hi_scr[kb] == hi_w, lo_scr[kb], jnp.int16(-(2 ** 15)))
        return carry

    lax.fori_loop(0, nkb, keep_ties, 0)

    def lo_body(j, t):
        cand = t | lax.shift_left(jnp.int32(1), 15 - j)
        cw = per_query16(cand - 2 ** 15)
        return jnp.where(above + count16(lo_scr, lambda lo: lo >= cw) >= kf, cand, t)

    lo_thr = lax.fori_loop(0, 16, lo_body, jnp.zeros((16, Q_TILE), I32))
    thr = (lax.shift_left(hi_thr, 16) | lo_thr)[0:8]
    thr_t = per_query(thr)

    need = kf - count(lambda key, kb: key > thr_t)
    ties = count(lambda key, kb: key == thr_t)
    last_scr[...] = jnp.full(last_scr.shape, s_len, I32)

    @pl.when(jnp.max(ties - need) > 0.0)
    def _():
        def tie_body(j, last):
            cand = last | lax.shift_left(jnp.int32(1), idx_bits - 1 - j)
            cw = per_query(cand)
            taken = count(lambda key, kb: (key == thr_t) & (key_index(kb) < cw))
            return jnp.where(taken < need, cand, last)

        last_scr[...] = lax.fori_loop(0, idx_bits, tie_body, jnp.zeros((8, Q_TILE), I32))

    def per_row(x):
        return _wide(jnp.broadcast_to(x[0:1], (LANES, Q_TILE)).T)

    tw = per_row(thr)
    lastw = per_row(last_scr[...])

    def mask_block(kb, valid):
        key = key_scr[kb]
        tie_neg = jnp.where(key == tw, jnp.where(cols(kb) <= lastw, 0.0, NEG), NEG)
        neg = jnp.where(key > tw, 0.0, tie_neg)
        if valid is not None:
            neg = jnp.where(valid, neg, NEG)
        neg_scr[kb] = neg

    def mask_body(kb, carry):
        mask_block(kb, None)
        return carry

    lax.fori_loop(0, qb, mask_body, 0)
    mask_block(qb, valid_diag)


def _dsa(proj, bias_tiles, bsz, s_len):
    topk = min(TOPK_MAX, s_len // 4)
    idx_bits = max(1, (s_len - 1).bit_length())
    n_blocks = s_len // KEY_BLOCK
    assert 16 * n_blocks < 2 ** 15
    kern = functools.partial(_dsa_kernel, topk=topk, idx_bits=idx_bits, s_len=s_len)
    lane_blk = lambda off: off // LANES
    n, w = DSA_PAIRS_PER_STEP, DSA_PAIRS_PER_STEP * LANES
    return pl.pallas_call(
        kern,
        grid=(bsz, s_len // Q_TILE, A_HEADS // 2 // n),
        in_specs=[pl.BlockSpec((1, Q_TILE, w), lambda b, i, p: (b, i, OFF_QA // w + p)),
                  pl.BlockSpec((1, Q_TILE, 256), lambda b, i, p: (b, i, OFF_QI // 256)),
                  pl.BlockSpec((1, Q_TILE, LANES), lambda b, i, p: (b, i, lane_blk(OFF_WI))),
                  pl.BlockSpec((1, s_len, W_A), lambda b, i, p: (b, 0, OFF_KA // W_A),
                               pipeline_mode=pl.Buffered(1)),
                  pl.BlockSpec((1, s_len, W_A), lambda b, i, p: (b, 0, OFF_VA // W_A),
                               pipeline_mode=pl.Buffered(1)),
                  pl.BlockSpec((1, s_len, LANES), lambda b, i, p: (b, 0, lane_blk(OFF_KI))),
                  pl.BlockSpec(bias_tiles.shape, lambda b, i, p: (0, 0, 0, 0),
                               pipeline_mode=pl.Buffered(1))],
        out_specs=pl.BlockSpec((1, Q_TILE, w), lambda b, i, p: (b, i, p)),
        out_shape=jax.ShapeDtypeStruct((bsz, s_len, W_A), BF16),
        scratch_shapes=[pltpu.VMEM((n_blocks, Q_TILE, KEY_BLOCK), I32),
                        pltpu.VMEM((n_blocks, KEY_BLOCK, Q_TILE), I32),
                        pltpu.VMEM((n_blocks, KEY_BLOCK, Q_TILE), I16),
                        pltpu.VMEM((n_blocks, KEY_BLOCK, Q_TILE), I16),
                        pltpu.VMEM((n_blocks, Q_TILE, KEY_BLOCK), F32),
                        pltpu.VMEM((IDX_HEADS, Q_TILE, LANES), F32),
                        pltpu.VMEM((8, Q_TILE), I32)] + _softmax_scratch(s_len) * n,
        compiler_params=_cparams("parallel", "arbitrary", "arbitrary"),
        name="dsa",
    )(proj, proj, proj, proj, proj, proj, bias_tiles)


DIFF_HEADS_PER_STEP = 2


def _diff_kernel(q_ref, k_ref, v_ref, bias_ref, lp_ref, g_ref, o_ref, *scratch, lam_init):
    qb = pl.program_id(2)
    heads = []
    for j in range(DIFF_HEADS_PER_STEP):
        lanes = slice(j * LANES, (j + 1) * LANES)
        qs = _split_heads(q_ref[0, :, lanes])
        heads.append(_softmax_phases(
            qs, lambda kb, lanes=lanes: k_ref[0, _key_rows(kb), lanes],
            lambda kb, lanes=lanes: v_ref[0, _key_rows(kb), lanes], qb,
            lambda kb, s, t, j=j: s + _tall(bias_ref[j, t]), *scratch[4 * j:4 * j + 4]))
    for step in range(3):
        for phases in heads:
            phases[step]()
    lp = lp_ref[0]
    lam = (jnp.exp(jnp.sum(lp[0:1] * lp[1:2], axis=-1, keepdims=True))
           - jnp.exp(jnp.sum(lp[2:3] * lp[3:4], axis=-1, keepdims=True)) + lam_init)
    for j, phases in enumerate(heads):
        o = phases[3]()
        o = o[:Q_TILE] - lam * o[Q_TILE:]
        y = o * lax.rsqrt(jnp.mean(o * o, axis=-1, keepdims=True) + SUBLN_EPS)
        o_ref[0, :, j * LANES:(j + 1) * LANES] = (y * g_ref[0] * (1.0 - lam_init)).astype(o_ref.dtype)


def _diff(proj, bias_tiles, lam_params, subln_g, layer, bsz, s_len, lam_init):
    kern = functools.partial(_diff_kernel, lam_init=lam_init)
    n, w = DIFF_HEADS_PER_STEP, DIFF_HEADS_PER_STEP * LANES
    return pl.pallas_call(
        kern,
        grid=(bsz, B_HEADS // n, s_len // Q_TILE),
        in_specs=[pl.BlockSpec((1, Q_TILE, w), lambda b, h, i: (b, i, OFF_QB // w + h)),
                  pl.BlockSpec((1, s_len, w), lambda b, h, i: (b, 0, OFF_KB // w + h)),
                  pl.BlockSpec((1, s_len, w), lambda b, h, i: (b, 0, OFF_VB // w + h)),
                  pl.BlockSpec((n, 3, Q_TILE, KEY_BLOCK), lambda b, h, i: (h, 0, 0, 0)),
                  pl.BlockSpec((1, 4, HEAD_DIM), lambda b, h, i: (layer, 0, 0)),
                  pl.BlockSpec((1, 1, 2 * HEAD_DIM), lambda b, h, i: (layer, 0, 0))],
        out_specs=pl.BlockSpec((1, Q_TILE, w), lambda b, h, i: (b, i, h)),
        out_shape=jax.ShapeDtypeStruct((bsz, s_len, B_HEADS * 2 * HEAD_DIM), BF16),
        scratch_shapes=_softmax_scratch(s_len) * n,
        compiler_params=_cparams("parallel", "parallel", "arbitrary"),
        name="diff_attn",
    )(proj, proj, proj, bias_tiles, lam_params, subln_g.reshape(-1, 1, 2 * HEAD_DIM))


def _stick_kernel(q_ref, k_ref, v_ref, o_ref, later_scr, run_scr, acc_scr):
    qb = pl.program_id(2)
    qs = _split_heads(q_ref[0])
    @pl.when((pl.program_id(0) == 0) & (pl.program_id(1) == 0) & (qb == 0))
    def _():
        jj = lax.broadcasted_iota(I32, (2 * KEY_BLOCK, KEY_BLOCK), 0) & (KEY_BLOCK - 1)
        ss = lax.broadcasted_iota(I32, (2 * KEY_BLOCK, KEY_BLOCK), 1)
        later_scr[...] = jnp.where(jj > ss, 1.0, 0.0).astype(BF16)

    run_scr[...] = jnp.zeros_like(run_scr)
    acc_scr[...] = jnp.zeros_like(acc_scr)

    def sweep(kbs, diagonal):
        parts = []
        for j, kb in enumerate(kbs):
            z = _nt_dot(qs, k_ref[0, _key_rows(kb), :])
            soft = jnp.log(1.0 + jnp.exp2(jnp.abs(z) * (-LOG2E)))
            log_beta = jnp.minimum(z, 0.0) - soft
            log_1mb = log_beta - z
            strict = None
            if diagonal and j == 0:
                t = lax.broadcasted_iota(I32, z.shape, 0) & (Q_TILE - 1)
                strict = lax.broadcasted_iota(I32, z.shape, 1) < t
                log_1mb = jnp.where(strict, log_1mb, 0.0)
            hi = log_1mb.astype(BF16)
            lo = (log_1mb - hi.astype(F32)).astype(BF16)
            between = _dot(jnp.concatenate([hi, lo], axis=1), later_scr[...])
            total = between[:, 0:1] + log_1mb[:, 0:1]
            parts.append((kb, log_beta + between, total, strict))
        run = run_scr[...]
        acc = None
        for kb, logit, total, strict in parts:
            w = jnp.exp(logit + _wide(run))
            if strict is not None:
                w = jnp.where(strict, w, 0.0)
            pv = _dot(w.astype(BF16), v_ref[0, _key_rows(kb), :])
            acc = pv if acc is None else acc + pv
            run = run + jnp.broadcast_to(total, run.shape)
        acc_scr[...] += acc
        run_scr[...] = run

    first = jnp.int32(1)
    for size in BLOCK_GROUPS[::-1]:
        first = jnp.where(qb + 1 >= size, size, first)
    for size in BLOCK_GROUPS:
        @pl.when(first == size)
        def _(size=size):
            sweep([qb - u for u in range(size)], True)
    top = qb - first
    for size in BLOCK_GROUPS:
        trips = (top + 1) // size

        def body(i, carry, top=top, size=size):
            sweep([top - i * size - u for u in range(size)], False)
            return carry

        lax.fori_loop(0, trips, body, 0)
        top = top - trips * size
    o_ref[0] = _merge_heads(acc_scr[...], Q_TILE).astype(o_ref.dtype)


def _stick(proj, bsz, s_len):
    return pl.pallas_call(
        _stick_kernel,
        grid=(bsz, C_HEADS // 2, s_len // Q_TILE),
        in_specs=[pl.BlockSpec((1, Q_TILE, LANES), lambda b, h, i: (b, i, OFF_QC // LANES + h)),
                  pl.BlockSpec((1, s_len, LANES), lambda b, h, i: (b, 0, OFF_KC // LANES + h)),
                  pl.BlockSpec((1, s_len, LANES), lambda b, h, i: (b, 0, OFF_VC // LANES + h))],
        out_specs=pl.BlockSpec((1, Q_TILE, LANES), lambda b, h, i: (b, i, h)),
        out_shape=jax.ShapeDtypeStruct((bsz, s_len, C_HEADS * HEAD_DIM), BF16),
        scratch_shapes=[pltpu.VMEM((2 * KEY_BLOCK, KEY_BLOCK), BF16),
                        pltpu.VMEM((2 * Q_TILE, LANES), F32),
                        pltpu.VMEM((2 * Q_TILE, LANES), F32)],
        compiler_params=_cparams("arbitrary", "arbitrary", "arbitrary"),
        name="stick_attn",
    )(proj, proj, proj)


def _merge_kernel(oa_ref, ob_ref, oc_ref, gate_ref, x_ref, g1_ref, wbr_ref, wout_ref, o_ref):
    d = x_ref.shape[-1]
    merged = jnp.zeros(x_ref.shape, F32)
    for j, o_br in enumerate((oa_ref, ob_ref, oc_ref)):
        w = o_br.shape[-1]
        gate = jax.nn.sigmoid(gate_ref[:, j * d:(j + 1) * d].astype(F32))
        merged = merged + gate * _dot(o_br[...], wbr_ref[j * w:(j + 1) * w, :])
    o_ref[...] = x_ref[...] + g1_ref[0] * _dot(merged.astype(BF16), wout_ref[...])


def _merge(o_a, o_b, o_c, proj, x2, mod, w_br, w_out, s_len):
    t, d = x2.shape
    tm = 1024
    per_b = s_len // tm
    w = o_a.shape[-1]
    once = dict(pipeline_mode=pl.Buffered(1))
    return pl.pallas_call(
        _merge_kernel,
        grid=(t // tm,),
        in_specs=[pl.BlockSpec((tm, w), lambda i: (i, 0)),
                  pl.BlockSpec((tm, w), lambda i: (i, 0)),
                  pl.BlockSpec((tm, w), lambda i: (i, 0)),
                  pl.BlockSpec((tm, 3 * d), lambda i: (i, OFF_G // (3 * d))),
                  pl.BlockSpec((tm, d), lambda i: (i, 0)),
                  pl.BlockSpec((1, 1, d), lambda i: (i // per_b, 0, 2)),
                  pl.BlockSpec(w_br.shape, lambda i: (0, 0), **once),
                  pl.BlockSpec(w_out.shape, lambda i: (0, 0), **once)],
        out_specs=pl.BlockSpec((tm, d), lambda i: (i, 0)),
        out_shape=jax.ShapeDtypeStruct((t, d), F32),
        compiler_params=_cparams("parallel"),
        name="merge",
    )(o_a, o_b, o_c, proj, x2, mod, w_br, w_out)


def _finish(x, gate, acc, fg_ref, final):
    y = x + gate * acc
    if final:
        y = y * lax.rsqrt(jnp.mean(y * y, axis=-1, keepdims=True) + EPS) * fg_ref[...]
    return y


def _swiglu_partial(h, w1, w3, w2):
    a = _dot(h, w1)
    act = (a * jax.nn.sigmoid(a)) * _dot(h, w3)
    return _dot(act.astype(BF16), w2)


def _ffn_kernel(x_ref, g_ref, sc_ref, sh_ref, gate_ref, w1_ref, w3_ref, w2_ref, fg_ref, o_ref,
                h_scr, acc_scr, *, final):
    f = pl.program_id(1)

    @pl.when(f == 0)
    def _():
        h_scr[...] = _modulated_norm(x_ref[...], g_ref[...], sc_ref[0], sh_ref[0]).astype(BF16)
        acc_scr[...] = jnp.zeros_like(acc_scr)

    acc_scr[...] += _swiglu_partial(h_scr[...], w1_ref[...], w3_ref[...], w2_ref[...])

    @pl.when(f == pl.num_programs(1) - 1)
    def _():
        o_ref[...] = _finish(x_ref[...], gate_ref[0], acc_scr[...], fg_ref, final)


def _ffn(x2, g, mod, w1, w3, w2, final_g, s_len, final):
    t, d = x2.shape
    d_ff = w1.shape[1]
    tm, tf = 512, d_ff
    once = dict(pipeline_mode=pl.Buffered(1))
    per_b = s_len // tm
    kern = functools.partial(_ffn_kernel, final=final)
    return pl.pallas_call(
        kern,
        grid=(t // tm, d_ff // tf),
        in_specs=[pl.BlockSpec((tm, d), lambda i, f: (i, 0)),
                  pl.BlockSpec((1, d), lambda i, f: (0, 0)),
                  pl.BlockSpec((1, 1, d), lambda i, f: (i // per_b, 0, 4)),
                  pl.BlockSpec((1, 1, d), lambda i, f: (i // per_b, 0, 3)),
                  pl.BlockSpec((1, 1, d), lambda i, f: (i // per_b, 0, 5)),
                  pl.BlockSpec((d, tf), lambda i, f: (0, f), **once),
                  pl.BlockSpec((d, tf), lambda i, f: (0, f), **once),
                  pl.BlockSpec((tf, d), lambda i, f: (f, 0), **once),
                  pl.BlockSpec((1, d), lambda i, f: (0, 0))],
        out_specs=pl.BlockSpec((tm, d), lambda i, f: (i, 0)),
        out_shape=jax.ShapeDtypeStruct((t, d), F32),
        scratch_shapes=[pltpu.VMEM((tm, d), BF16), pltpu.VMEM((tm, d), F32)],
        compiler_params=_cparams("parallel", "arbitrary"),
        name="ffn",
    )(x2, g, mod, mod, mod, w1, w3, w2, final_g)


MOE_TM = 512
MOE_ROWS = 512
R_E1, R_E2, R_RANK1, R_RANK2, R_W1, R_W2 = 0, 1, 2, 3, 4, 5


def _lane_pick(tile, lane, idx):
    return jnp.sum(jnp.where(lane == idx, tile, 0.0), axis=-1, keepdims=True)


def _route_kernel(x_ref, g_ref, sc_ref, sh_ref, wr_ref, br_ref, h_ref, rec_ref, cnt_ref, cnt_scr):
    @pl.when(pl.program_id(0) == 0)
    def _():
        cnt_scr[...] = jnp.zeros_like(cnt_scr)

    h = _modulated_norm(x_ref[...], g_ref[...], sc_ref[0], sh_ref[0])
    half = h.shape[1] // 2
    bits = pltpu.bitcast(h.astype(BF16).astype(F32), jnp.uint32)
    h_ref[...] = (bits[:, half:] & jnp.uint32(0xFFFF0000)) | (bits[:, :half] >> 16)

    h_hi = h.astype(BF16)
    h_lo = (h - h_hi.astype(F32)).astype(BF16)
    wr = wr_ref[...]
    w_hi = wr.astype(BF16)
    w_lo = (wr - w_hi.astype(F32)).astype(BF16)
    logits = _dot(h_hi, w_hi) + _dot(h_hi, w_lo) + _dot(h_lo, w_hi) + br_ref[...]
    lane = lax.broadcasted_iota(I32, logits.shape, 1).astype(F32)
    lg = jnp.where(lane < N_EXPERTS, logits, -jnp.inf)
    m1 = jnp.max(lg, axis=-1, keepdims=True)
    i1 = jnp.min(jnp.where(lg == m1, lane, float(LANES)), axis=-1, keepdims=True)
    lg2 = jnp.where(lane == i1, -jnp.inf, lg)
    m2 = jnp.max(lg2, axis=-1, keepdims=True)
    i2 = jnp.min(jnp.where(lg2 == m2, lane, float(LANES)), axis=-1, keepdims=True)
    e2 = jnp.exp(m2 - m1)
    w_top = 1.0 / (1.0 + e2)

    chosen = jnp.where(lane == i1, 1.0, 0.0) + jnp.where(lane == i2, 1.0, 0.0)
    tm = chosen.shape[0]
    earlier = jnp.where(lax.broadcasted_iota(I32, (tm, tm), 1) < lax.broadcasted_iota(I32, (tm, tm), 0),
                        1.0, 0.0).astype(BF16)
    rank = _dot(earlier, chosen.astype(BF16)) + cnt_scr[0:1, :]
    rec = jnp.zeros_like(logits)
    for slot, val in ((R_E1, i1), (R_E2, i2), (R_RANK1, _lane_pick(rank, lane, i1)),
                      (R_RANK2, _lane_pick(rank, lane, i2)), (R_W1, w_top), (R_W2, e2 * w_top)):
        rec = jnp.where(lane == float(slot), val, rec)
    rec_ref[...] = rec
    cnt_scr[0:1, :] = cnt_scr[0:1, :] + jnp.sum(chosen, axis=0, keepdims=True)
    cnt_ref[...] = cnt_scr[...]


def _route(x2, g, mod, wr_pad, br_pad, s_len):
    t, d = x2.shape
    tm = MOE_TM
    per_b = s_len // tm
    return pl.pallas_call(
        _route_kernel,
        grid=(t // tm,),
        in_specs=[pl.BlockSpec((tm, d), lambda i: (i, 0)),
                  pl.BlockSpec((1, d), lambda i: (0, 0)),
                  pl.BlockSpec((1, 1, d), lambda i: (i // per_b, 0, 4)),
                  pl.BlockSpec((1, 1, d), lambda i: (i // per_b, 0, 3)),
                  pl.BlockSpec((d, LANES), lambda i: (0, 0)),
                  pl.BlockSpec((1, LANES), lambda i: (0, 0))],
        out_specs=[pl.BlockSpec((tm, d // 2), lambda i: (i, 0)),
                   pl.BlockSpec((tm, LANES), lambda i: (i, 0)),
                   pl.BlockSpec((8, LANES), lambda i: (0, 0))],
        out_shape=[jax.ShapeDtypeStruct((t, d // 2), jnp.uint32),
                   jax.ShapeDtypeStruct((t, LANES), F32),
                   jax.ShapeDtypeStruct((8, LANES), F32)],
        scratch_shapes=[pltpu.VMEM((8, LANES), F32)],
        compiler_params=_cparams("arbitrary"),
        name="moe_route",
    )(x2, g, mod, mod, wr_pad, br_pad)


def _row_copy(src_ref, src_row, dst_ref, dst_row, sem):
    return pltpu.make_async_copy(src_ref.at[pl.ds(src_row, 1)], dst_ref.at[pl.ds(dst_row, 1)], sem)


def _dispatch_kernel(pos_ref, h_ref, xs_in_ref, xs_ref, sem):
    del xs_in_ref
    tm = h_ref.shape[0]

    def start(t, carry):
        _row_copy(h_ref, t, xs_ref, pos_ref[0, 0, t], sem).start(priority=0)
        _row_copy(h_ref, t, xs_ref, pos_ref[0, 0, tm + t], sem).start(priority=1)
        return carry

    lax.fori_loop(0, tm, start, 0, unroll=8)
    for _ in range(2):
        pltpu.make_async_copy(h_ref, xs_ref.at[pl.ds(0, tm)], sem).wait()


def _dispatch(pos, h_packed, n_rows):
    t, w = h_packed.shape
    tm = MOE_TM
    xs0 = jnp.zeros((n_rows, w), h_packed.dtype)
    return pl.pallas_call(
        _dispatch_kernel,
        grid=(t // tm,),
        in_specs=[pl.BlockSpec((1, 1, 2 * tm), lambda i: (i, 0, 0), memory_space=pltpu.SMEM),
                  pl.BlockSpec((tm, w), lambda i: (i, 0)),
                  pl.BlockSpec(memory_space=pl.ANY)],
        out_specs=pl.BlockSpec(memory_space=pl.ANY),
        out_shape=jax.ShapeDtypeStruct((n_rows, w), h_packed.dtype),
        scratch_shapes=[pltpu.SemaphoreType.DMA(())],
        input_output_aliases={2: 0},
        compiler_params=_cparams("arbitrary"),
        name="moe_dispatch",
    )(pos, h_packed, xs0)


def _expert_kernel(te_ref, nu_ref, xs_ref, w1_ref, w3_ref, w2_ref, o_ref, h_scr, acc_scr):
    i = pl.program_id(0)
    f = pl.program_id(1)

    @pl.when(i < nu_ref[0])
    def _():
        @pl.when(f == 0)
        def _():
            word = xs_ref[...]
            lo = pltpu.bitcast(word << 16, F32)
            hi = pltpu.bitcast(word & jnp.uint32(0xFFFF0000), F32)
            h_scr[...] = jnp.concatenate([lo, hi], axis=1).astype(BF16)
            acc_scr[...] = jnp.zeros_like(acc_scr)

        acc_scr[...] += _swiglu_partial(h_scr[...], w1_ref[0], w3_ref[0], w2_ref[0])

        @pl.when(f == pl.num_programs(1) - 1)
        def _():
            o_ref[...] = acc_scr[...]

    @pl.when((i >= nu_ref[0]) & (f == pl.num_programs(1) - 1))
    def _():
        o_ref[...] = jnp.zeros_like(o_ref)


def _experts(tile_expert, n_used, xs, w1, w3, w2):
    n_rows, half = xs.shape
    d = 2 * half
    d_ff = w1.shape[2]
    tf = 1792
    nf = d_ff // tf
    tile = lambda i, nu: jnp.minimum(i, nu[0] - 1)
    ff = lambda i, f, nu: jnp.where(i < nu[0], f, nf - 1)
    return pl.pallas_call(
        _expert_kernel,
        grid_spec=pltpu.PrefetchScalarGridSpec(
            num_scalar_prefetch=2,
            grid=(n_rows // MOE_ROWS, nf),
            in_specs=[pl.BlockSpec((MOE_ROWS, half), lambda i, f, te, nu: (tile(i, nu), 0)),
                      pl.BlockSpec((1, d, tf), lambda i, f, te, nu: (te[i], 0, ff(i, f, nu))),
                      pl.BlockSpec((1, d, tf), lambda i, f, te, nu: (te[i], 0, ff(i, f, nu))),
                      pl.BlockSpec((1, tf, d), lambda i, f, te, nu: (te[i], ff(i, f, nu), 0))],
            out_specs=pl.BlockSpec((MOE_ROWS, d), lambda i, f, te, nu: (i, 0)),
            scratch_shapes=[pltpu.VMEM((MOE_ROWS, d), BF16), pltpu.VMEM((MOE_ROWS, d), F32)]),
        out_shape=jax.ShapeDtypeStruct((n_rows, d), F32),
        compiler_params=_cparams("arbitrary", "arbitrary"),
        name="moe_experts",
    )(tile_expert, n_used, xs, w1, w3, w2)


def _combine_kernel(pos_ref, x_ref, gate_ref, rec_ref, fg_ref, ys_ref, o_ref, y_scr, sem, *, final):
    tm = x_ref.shape[0]

    def start(t, carry):
        _row_copy(ys_ref, pos_ref[0, 0, t], y_scr.at[0], t, sem).start(priority=0)
        _row_copy(ys_ref, pos_ref[0, 0, tm + t], y_scr.at[1], t, sem).start(priority=1)
        return carry

    lax.fori_loop(0, tm, start, 0, unroll=8)
    for slot in range(2):
        pltpu.make_async_copy(ys_ref.at[pl.ds(0, tm)], y_scr.at[slot], sem).wait()
    rec = rec_ref[...]
    lane = lax.broadcasted_iota(I32, rec.shape, 1)
    w_first = jnp.sum(jnp.where(lane == R_W1, rec, 0.0), axis=-1, keepdims=True)
    w_second = jnp.sum(jnp.where(lane == R_W2, rec, 0.0), axis=-1, keepdims=True)
    f = w_first * y_scr[0] + w_second * y_scr[1]
    o_ref[...] = _finish(x_ref[...], gate_ref[0], f, fg_ref, final)


def _combine(pos, x2, mod, rec, final_g, ys, s_len, final):
    t, d = x2.shape
    tm = MOE_TM
    per_b = s_len // tm
    kern = functools.partial(_combine_kernel, final=final)
    return pl.pallas_call(
        kern,
        grid=(t // tm,),
        in_specs=[pl.BlockSpec((1, 1, 2 * tm), lambda i: (i, 0, 0), memory_space=pltpu.SMEM),
                  pl.BlockSpec((tm, d), lambda i: (i, 0)),
                  pl.BlockSpec((1, 1, d), lambda i: (i // per_b, 0, 5)),
                  pl.BlockSpec((tm, LANES), lambda i: (i, 0)),
                  pl.BlockSpec((1, d), lambda i: (0, 0)),
                  pl.BlockSpec(memory_space=pl.ANY)],
        out_specs=pl.BlockSpec((tm, d), lambda i: (i, 0)),
        out_shape=jax.ShapeDtypeStruct((t, d), F32),
        scratch_shapes=[pltpu.VMEM((2, tm, d), F32), pltpu.SemaphoreType.DMA(())],
        compiler_params=_cparams("arbitrary"),
        name="moe_combine",
    )(pos, x2, mod, rec, final_g, ys)


def _moe(x2, g, mod, wr_pad, br_pad, w1, w3, w2, final_g, s_len, final):
    t, d = x2.shape
    n_e = w1.shape[0]
    h_packed, rec, cnt = _route(x2, g, mod, wr_pad, br_pad, s_len)

    n_tiles = 2 * t // MOE_ROWS + n_e
    counts = cnt[0, :n_e].astype(I32)
    padded = (counts + MOE_ROWS - 1) // MOE_ROWS * MOE_ROWS
    ends = jnp.cumsum(padded)
    starts = ends - padded
    e1, e2 = rec[:, R_E1].astype(I32), rec[:, R_E2].astype(I32)
    pos1 = starts[e1] + rec[:, R_RANK1].astype(I32)
    pos2 = starts[e2] + rec[:, R_RANK2].astype(I32)
    pos = jnp.concatenate([pos1.reshape(-1, 1, MOE_TM), pos2.reshape(-1, 1, MOE_TM)], axis=2)
    n_used = (ends[-1] // MOE_ROWS).reshape(1)
    tile_start = jnp.minimum(jnp.arange(n_tiles, dtype=I32), n_used[0] - 1) * MOE_ROWS
    tile_expert = jnp.sum(tile_start[:, None] >= ends[None, :], axis=1).astype(I32)

    xs = _dispatch(pos, h_packed, n_tiles * MOE_ROWS)
    ys = _experts(tile_expert, n_used, xs, w1, w3, w2)
    return _combine(pos, x2, mod, rec, final_g, ys, s_len, final)


def _pack_w_in(w):
    d = w.shape[0]
    sizes = (W_A, W_A, W_A, IDX_HEADS * IDX_DIM, IDX_DIM, IDX_HEADS,
             512, 512, 512, 512, 512, 512, 3 * d)
    offs = [0]
    for s in sizes:
        offs.append(offs[-1] + s)
    (qa, ka, va, qi, ki, wi, qb, kb, vb, qc, kc, vc, gl) = [
        w[:, offs[j]:offs[j + 1]] for j in range(len(sizes))]
    scale = HEAD_DIM ** -0.5
    pad_wi = jnp.zeros((d, LANES - IDX_HEADS), w.dtype)
    packed = jnp.concatenate(
        [gl, qa * scale, ka, va,
         qi * (IDX_DIM ** -0.5), ki, ki, wi * (IDX_HEADS ** -0.5), pad_wi,
         qb * scale, kb, vb, qc * scale, kc, vc], axis=1)
    assert packed.shape[1] == PACKED
    return packed.astype(BF16)


def kernel(x, c, w_ada, b_ada, norm1_g, norm2_g, w_in, w_br, w_out, rel_bias, lam_params,
           subln_g, ffn_w1, ffn_w3, ffn_w2, router_w, router_b, moe_w1, moe_w3, moe_w2, final_g):
    bsz, s_len, d = x.shape
    depth = w_ada.shape[0]
    assert s_len % 1024 == 0 and d == 1024 and OFF_G + 3 * d == OFF_QA

    c_pad = jnp.concatenate([c, jnp.zeros((8 - bsz % 8 if bsz % 8 else 0, d), c.dtype)], axis=0)
    mod_all = _ada(c_pad, w_ada, b_ada)
    dsa_tiles, diff_tiles = _bias_tiles(rel_bias)
    fg = final_g.reshape(1, d)

    x2 = x.reshape(bsz * s_len, d)
    for l in range(depth):
        mod = mod_all[l, :bsz].reshape(bsz, 1, 6 * d)
        proj = _inproj(x2, norm1_g[l].reshape(1, d), mod, _pack_w_in(w_in[l]), s_len)
        proj3 = proj.reshape(bsz, s_len, PACKED)
        lam_init = 0.8 - 0.6 * math.exp(-0.3 * l)
        o_a = _dsa(proj3, dsa_tiles, bsz, s_len)
        o_b = _diff(proj3, diff_tiles, lam_params, subln_g, l, bsz, s_len, lam_init)
        o_c = _stick(proj3, bsz, s_len)
        x2 = _merge(o_a.reshape(-1, o_a.shape[-1]), o_b.reshape(-1, o_b.shape[-1]),
                    o_c.reshape(-1, o_c.shape[-1]), proj, x2, mod,
                    w_br[l].astype(BF16), w_out[l].astype(BF16), s_len)
        g2 = norm2_g[l].reshape(1, d)
        final = l == depth - 1
        j = l // 2
        if l % 2 == 0:
            x2 = _ffn(x2, g2, mod, ffn_w1[j].astype(BF16), ffn_w3[j].astype(BF16),
                      ffn_w2[j].astype(BF16), fg, s_len, final)
        else:
            wr_pad = jnp.pad(router_w[j], ((0, 0), (0, LANES - N_EXPERTS)))
            br_pad = jnp.pad(router_b[j], (0, LANES - N_EXPERTS)).reshape(1, LANES)
            x2 = _moe(x2, g2, mod, wr_pad, br_pad, moe_w1[j].astype(BF16), moe_w3[j].astype(BF16),
                      moe_w2[j].astype(BF16), fg, s_len, final)
    return x2.reshape(bsz, s_len, d)
```
